```python
import math
import jax, jax.numpy as jnp
from jax import lax
import numpy as np

D_MODEL = 2048
BATCH = 4
SEQ = 2048
DEPTH = 4
DEC_BATCH = 8
DEC_SEQ = 8
PAST_LEN = 16384
PAGE_SIZE = 128

N_MIXERS = 3
N_RWKV = (DEPTH + 2) // 3
N_S5 = (DEPTH + 1) // 3
N_MOBA = DEPTH // 3
RWKV_N = 64
RWKV_H = D_MODEL // RWKV_N
DECAY_LORA = max(32, int(round(1.8 * D_MODEL ** 0.5 / 32)) * 32)
AAA_LORA = max(32, int(round(1.8 * D_MODEL ** 0.5 / 32)) * 32)
MV_LORA = max(32, int(round(1.3 * D_MODEL ** 0.5 / 32)) * 32)
GATE_LORA = max(32, int(round(0.6 * D_MODEL ** 0.8 / 32)) * 32)
RWKV_GN_EPS = 64e-5
S5_GROUP = 16
S5_G = D_MODEL // S5_GROUP
S5_P = 64
MOBA_H = 16
MOBA_DH = D_MODEL // MOBA_H
MOBA_BLOCK = 256
MOBA_TOPK = 3
Q_BLOCK = 128
D_FF = 11 * D_MODEL // 4
CONV_W = 3
LN_EPS = 1e-5
DN_ALPHA = (2.0 * DEPTH) ** 0.25
DN_BETA = (8.0 * DEPTH) ** -0.25
NEG = -1e30

kernel_name = 'hybrid_rwkv7_s5_moba_convffn_step'


def layer_norm(x, g, b):
    xf = x.astype(jnp.float32)
    mu = jnp.mean(xf, axis=-1, keepdims=True)
    var = jnp.mean(jnp.square(xf - mu), axis=-1, keepdims=True)
    return ((xf - mu) * lax.rsqrt(var + LN_EPS) * g + b).astype(x.dtype)


def alibi_slopes(n_heads):
    return 2.0 ** (-8.0 * jnp.arange(1, n_heads + 1, dtype=jnp.float32) / n_heads)


def wkv_scan(S0, r, w, k, v, kk, a):
    def step(S, inp):
        r_t, w_t, k_t, v_t, kk_t, a_t = inp
        sa = jnp.einsum('bhij,bhj->bhi', S, -kk_t)
        S = S * w_t[:, :, None, :] + sa[..., None] * (kk_t * a_t)[:, :, None, :] + v_t[..., None] * k_t[:, :, None, :]
        return S, jnp.einsum('bhij,bhj->bhi', S, r_t)
    xs = tuple(jnp.swapaxes(t.astype(jnp.float32), 0, 1) for t in (r, w, k, v, kk, a))
    S, ys = lax.scan(step, S0.astype(jnp.float32), xs)
    return S, jnp.swapaxes(ys, 0, 1)


def rwkv_mixer(x, shift0, S0, v_first, p, j):
    B, T, D = x.shape
    f32 = jnp.float32
    x_prev = jnp.concatenate([shift0[:, None, :].astype(x.dtype), x[:, :-1]], axis=1)
    xm = x[:, :, None, :] + (x_prev - x)[:, :, None, :] * p['rwkv_mu'][j]
    xr, xw, xk, xv, xa, xg = (xm[:, :, n] for n in range(6))
    w_rkv = p['rwkv_w_rkv'][j]
    r = xr @ w_rkv[0]
    k = xk @ w_rkv[1]
    v = xv @ w_rkv[2]
    w_log = -jax.nn.softplus(-(p['rwkv_w0'][j] + jnp.tanh(xw @ p['rwkv_w1'][j]) @ p['rwkv_w2'][j])) - 0.5
    decay = jnp.exp(-jnp.exp(w_log.astype(f32)))
    a = jax.nn.sigmoid(p['rwkv_a0'][j] + (xa @ p['rwkv_a1'][j]) @ p['rwkv_a2'][j])
    g = jax.nn.sigmoid(xg @ p['rwkv_g1'][j]) @ p['rwkv_g2'][j]
    if j == 0:
        v_first = v
    else:
        v = v + (v_first - v) * jax.nn.sigmoid(p['rwkv_v0'][j - 1] + (xv @ p['rwkv_v1'][j - 1]) @ p['rwkv_v2'][j - 1])
    heads = lambda t: t.reshape(B, T, RWKV_H, RWKV_N)
    kk = heads(k * p['rwkv_k_k'][j]).astype(f32)
    kk = kk / jnp.maximum(jnp.linalg.norm(kk, axis=-1, keepdims=True), 1e-12)
    k = k * (1.0 + (a - 1.0) * p['rwkv_k_a'][j])
    S, y = wkv_scan(S0, heads(r), heads(decay), heads(k), heads(v), kk, heads(a))
    mu = jnp.mean(y, axis=-1, keepdims=True)
    var = jnp.mean(jnp.square(y - mu), axis=-1, keepdims=True)
    y = ((y - mu) * lax.rsqrt(var + RWKV_GN_EPS)).reshape(B, T, D) * p['rwkv_lnx_g'][j] + p['rwkv_lnx_b'][j]
    bonus = (jnp.sum(heads(r * k).astype(f32) * p['rwkv_r_k'][j], axis=-1, keepdims=True) * heads(v).astype(f32)).reshape(B, T, D)
    out = ((y + bonus).astype(x.dtype) * g) @ p['rwkv_wo'][j]
    return out, S, x[:, -1], v_first


def _complex_affine_combine(e1, e2):
    a1r, a1i, b1r, b1i = e1
    a2r, a2i, b2r, b2i = e2
    return (a2r * a1r - a2i * a1i, a2r * a1i + a2i * a1r,
            a2r * b1r - a2i * b1i + b2r, a2r * b1i + a2i * b1r + b2i)


def s5_mixer(x, h0_re, h0_im, p, j):
    B, T, D = x.shape
    f32 = jnp.float32
    dt = jnp.exp(p['s5_log_dt'][j].astype(f32))[:, None]
    lr = p['s5_lam_re'][j].astype(f32)
    li = p['s5_lam_im'][j].astype(f32)
    mag = jnp.exp(lr * dt)
    ar, ai = mag * jnp.cos(li * dt), mag * jnp.sin(li * dt)
    den = lr * lr + li * li
    cr = ((ar - 1.0) * lr + ai * li) / den
    ci = (ai * lr - (ar - 1.0) * li) / den
    u = x.reshape(B, T, S5_G, S5_GROUP).astype(f32)
    bu_re = jnp.einsum('btgc,gpc->tbgp', u, p['s5_b_re'][j].astype(f32))
    bu_im = jnp.einsum('btgc,gpc->tbgp', u, p['s5_b_im'][j].astype(f32))
    b_re = cr * bu_re - ci * bu_im
    b_im = cr * bu_im + ci * bu_re
    a_re = jnp.broadcast_to(ar, (T, 1, S5_G, S5_P))
    a_im = jnp.broadcast_to(ai, (T, 1, S5_G, S5_P))
    A_re, A_im, X_re, X_im = lax.associative_scan(_complex_affine_combine, (a_re, a_im, b_re, b_im), axis=0)
    h0r = h0_re.astype(f32)
    h0i = h0_im.astype(f32)
    X_re, X_im = (X_re + A_re * h0r - A_im * h0i, X_im + A_re * h0i + A_im * h0r)
    y = (jnp.einsum('tbgp,gcp->btgc', X_re, p['s5_c_re'][j].astype(f32))
         - jnp.einsum('tbgp,gcp->btgc', X_im, p['s5_c_im'][j].astype(f32)))
    y = y.reshape(B, T, D).astype(x.dtype) + p['s5_d'][j] * x
    z = jax.nn.gelu(y)
    out = (z @ p['s5_w_val'][j]) * jax.nn.sigmoid(z @ p['s5_w_gate'][j])
    return out, X_re[-1], X_im[-1]


def moba_attend(q, k, v, q_pos):
    B, L, H, dh = k.shape
    Q = q.shape[1]
    nb = -(-L // MOBA_BLOCK)
    pad = nb * MOBA_BLOCK - L
    kp = jnp.pad(k, ((0, 0), (0, pad), (0, 0), (0, 0))).reshape(B, nb, MOBA_BLOCK, H, dh)
    vp = jnp.pad(v, ((0, 0), (0, pad), (0, 0), (0, 0))).reshape(B, nb, MOBA_BLOCK, H, dh)
    k_mean = jnp.mean(kp.astype(jnp.float32), axis=2)
    k_top = min(MOBA_TOPK, nb)
    slopes = alibi_slopes(H)
    scale = dh ** -0.5
    qb = Q if Q <= Q_BLOCK else Q_BLOCK
    n_qb = Q // qb
    bi = jnp.arange(B)[:, None, None, None]
    hi = jnp.arange(H)[None, None, :, None]

    def one_block(i):
        qi = lax.dynamic_slice_in_dim(q, i * qb, qb, axis=1)
        pos = lax.dynamic_slice_in_dim(q_pos, i * qb, qb)
        own = pos // MOBA_BLOCK
        gate = jnp.einsum('bqhd,bnhd->bqhn', qi.astype(jnp.float32), k_mean)
        past = jnp.arange(nb)[None, :] < own[:, None]
        gate = jnp.where(past[None, :, None, :], gate, NEG)
        _, top = lax.top_k(gate, k_top)
        own_b = jnp.broadcast_to(own[None, :, None, None], (B, qb, H, 1))
        sel = jnp.concatenate([top, own_b], axis=-1)
        slot_ok = jnp.concatenate([top < own[None, :, None, None], jnp.ones((B, qb, H, 1), bool)], axis=-1)
        kg = kp[bi, sel, :, hi, :]
        vg = vp[bi, sel, :, hi, :]
        s = jnp.einsum('bqhd,bqhnkd->bqhnk', qi, kg).astype(jnp.float32) * scale
        key_pos = sel[..., None] * MOBA_BLOCK + jnp.arange(MOBA_BLOCK)
        dist = pos[None, :, None, None, None] - key_pos
        s = s - slopes[None, None, :, None, None] * dist.astype(jnp.float32)
        s = jnp.where(slot_ok[..., None] & (dist >= 0), s, NEG)
        pr = jax.nn.softmax(s.reshape(B, qb, H, -1), axis=-1).reshape(s.shape)
        return jnp.einsum('bqhnk,bqhnkd->bqhd', pr.astype(vg.dtype), vg)

    out = lax.map(one_block, jnp.arange(n_qb))
    return jnp.transpose(out, (1, 0, 2, 3, 4)).reshape(B, Q, H, dh)


def moba_mixer(x, cache_k, cache_v, page_table, p, j):
    B, T, D = x.shape
    qkv = (x @ p['moba_w_qkv'][j]).reshape(B, T, 3, MOBA_H, MOBA_DH)
    q, k, v = qkv[:, :, 0], qkv[:, :, 1], qkv[:, :, 2]
    if cache_k is None:
        k_all, v_all = k, v
    else:
        k_all = jnp.concatenate([cache_k[j, page_table].reshape(B, -1, MOBA_H, MOBA_DH).astype(k.dtype), k], axis=1)
        v_all = jnp.concatenate([cache_v[j, page_table].reshape(B, -1, MOBA_H, MOBA_DH).astype(v.dtype), v], axis=1)
    pos0 = k_all.shape[1] - T
    q_pos = pos0 + jnp.arange(T, dtype=jnp.int32)
    o = moba_attend(q, k_all, v_all, q_pos)
    return o.reshape(B, T, D) @ p['moba_wo'][j], k, v


def conv_ffn(x, conv0, p, i):
    B, T, D = x.shape
    hgv = x @ p['ffn_w_in'][i]
    hg, hv = hgv[..., :D_FF], hgv[..., D_FF:]
    hpad = jnp.concatenate([conv0.astype(hg.dtype), hg], axis=1)
    w = p['ffn_conv_w'][i]
    c = p['ffn_conv_b'][i] + sum(hpad[:, n:n + T] * w[n] for n in range(CONV_W))
    h = jax.nn.silu(c) * hv
    return h @ p['ffn_w_down'][i], hpad[:, T:]


def trunk(x, wkv0, shift0, s5re0, s5im0, conv0, cache_k, cache_v, page_table, p):
    wkv_new, shift_new, s5re_new, s5im_new, k_new, v_new, conv_new = [], [], [], [], [], [], []
    v_first = None
    for i in range(DEPTH):
        kind, j = i % N_MIXERS, i // N_MIXERS
        if kind == 0:
            h, S, last, v_first = rwkv_mixer(x, shift0[j], wkv0[j], v_first, p, j)
            wkv_new.append(S)
            shift_new.append(last)
        elif kind == 1:
            h, hr, him = s5_mixer(x, s5re0[j], s5im0[j], p, j)
            s5re_new.append(hr)
            s5im_new.append(him)
        else:
            h, kr, vr = moba_mixer(x, cache_k, cache_v, page_table, p, j)
            k_new.append(kr)
            v_new.append(vr)
        x = layer_norm(DN_ALPHA * x + h, p['ln_g'][i, 0], p['ln_b'][i, 0])
        f, cst = conv_ffn(x, conv0[i], p, i)
        conv_new.append(cst)
        x = layer_norm(DN_ALPHA * x + f, p['ln_g'][i, 1], p['ln_b'][i, 1])
    return (x, jnp.stack(wkv_new), jnp.stack(shift_new), jnp.stack(s5re_new), jnp.stack(s5im_new),
            jnp.stack(conv_new), jnp.stack(k_new), jnp.stack(v_new))


def setup_inputs(seed: int = 0) -> dict:
    key = jax.random.key(seed)
    ks = iter(jax.random.split(key, 64))
    f32 = jnp.float32

    def nrm(shape, scale=1.0):
        return jax.random.normal(next(ks), shape, f32) * scale

    def unif(shape, lo, hi):
        return jax.random.uniform(next(ks), shape, f32, lo, hi)

    D, F = D_MODEL, D_FF
    n_pages = PAST_LEN // PAGE_SIZE
    n_used = DEC_BATCH * n_pages
    n_pool = n_used + max(1, n_used // 4)
    page_table = jax.random.permutation(next(ks), n_pool)[:n_used].reshape(DEC_BATCH, n_pages).astype(jnp.int32)
    n_rv = max(N_RWKV - 1, 0)
    return {
        'x_prompt': nrm((BATCH, SEQ, D)),
        'x_sample': nrm((DEC_BATCH, DEC_SEQ, D)),
        'state_rwkv_wkv': nrm((N_RWKV, DEC_BATCH, RWKV_H, RWKV_N, RWKV_N), 0.1),
        'state_rwkv_shift': nrm((N_RWKV, DEC_BATCH, D)),
        'state_s5_re': nrm((N_S5, DEC_BATCH, S5_G, S5_P), 0.5),
        'state_s5_im': nrm((N_S5, DEC_BATCH, S5_G, S5_P), 0.5),
        'state_ffn_conv': nrm((DEPTH, DEC_BATCH, CONV_W - 1, F)),
        'cache_k': nrm((N_MOBA, n_pool, PAGE_SIZE, MOBA_H, MOBA_DH)),
        'cache_v': nrm((N_MOBA, n_pool, PAGE_SIZE, MOBA_H, MOBA_DH)),
        'page_table': page_table,
        'ln_g': 1.0 + nrm((DEPTH, 2, D), 0.01),
        'ln_b': nrm((DEPTH, 2, D), 0.01),
        'rwkv_mu': unif((N_RWKV, 6, D), 0.0, 1.0),
        'rwkv_w_rkv': nrm((N_RWKV, 3, D, D), D ** -0.5),
        'rwkv_w0': unif((N_RWKV, D), -6.0, 1.0),
        'rwkv_w1': nrm((N_RWKV, D, DECAY_LORA), D ** -0.5),
        'rwkv_w2': nrm((N_RWKV, DECAY_LORA, D), 0.3 * DECAY_LORA ** -0.5),
        'rwkv_a0': nrm((N_RWKV, D), 0.1),
        'rwkv_a1': nrm((N_RWKV, D, AAA_LORA), D ** -0.5),
        'rwkv_a2': nrm((N_RWKV, AAA_LORA, D), 0.3 * AAA_LORA ** -0.5),
        'rwkv_v0': nrm((n_rv, D), 0.1),
        'rwkv_v1': nrm((n_rv, D, MV_LORA), D ** -0.5),
        'rwkv_v2': nrm((n_rv, MV_LORA, D), 0.3 * MV_LORA ** -0.5),
        'rwkv_g1': nrm((N_RWKV, D, GATE_LORA), D ** -0.5),
        'rwkv_g2': nrm((N_RWKV, GATE_LORA, D), GATE_LORA ** -0.5),
        'rwkv_k_k': 0.85 + nrm((N_RWKV, D), 0.02),
        'rwkv_k_a': 1.0 + nrm((N_RWKV, D), 0.02),
        'rwkv_r_k': nrm((N_RWKV, RWKV_H, RWKV_N), 0.1),
        'rwkv_lnx_g': 1.0 + nrm((N_RWKV, D), 0.01),
        'rwkv_lnx_b': nrm((N_RWKV, D), 0.01),
        'rwkv_wo': nrm((N_RWKV, D, D), DN_BETA * D ** -0.5),
        's5_log_dt': unif((N_S5, S5_G), math.log(0.001), math.log(0.1)),
        's5_lam_re': -0.5 + nrm((N_S5, S5_G, S5_P), 0.01),
        's5_lam_im': math.pi * jnp.arange(S5_P, dtype=f32) + nrm((N_S5, S5_G, S5_P), 0.01),
        's5_b_re': nrm((N_S5, S5_G, S5_P, S5_GROUP), (2.0 * S5_GROUP) ** -0.5),
        's5_b_im': nrm((N_S5, S5_G, S5_P, S5_GROUP), (2.0 * S5_GROUP) ** -0.5),
        's5_c_re': nrm((N_S5, S5_G, S5_GROUP, S5_P), (2.0 * S5_P) ** -0.5),
        's5_c_im': nrm((N_S5, S5_G, S5_GROUP, S5_P), (2.0 * S5_P) ** -0.5),
        's5_d': nrm((N_S5, D), 0.5),
        's5_w_val': nrm((N_S5, D, D), DN_BETA * D ** -0.5),
        's5_w_gate': nrm((N_S5, D, D), D ** -0.5),
        'moba_w_qkv': nrm((N_MOBA, D, 3 * D), D ** -0.5),
        'moba_wo': nrm((N_MOBA, D, D), DN_BETA * D ** -0.5),
        'ffn_w_in': nrm((DEPTH, D, 2 * F), D ** -0.5),
        'ffn_conv_w': nrm((DEPTH, CONV_W, F), CONV_W ** -0.5),
        'ffn_conv_b': nrm((DEPTH, F), 0.01),
        'ffn_w_down': nrm((DEPTH, F, D), DN_BETA * F ** -0.5),
    }


def reference(x_prompt, x_sample, state_rwkv_wkv, state_rwkv_shift, state_s5_re, state_s5_im, state_ffn_conv,
              cache_k, cache_v, page_table, ln_g, ln_b, rwkv_mu, rwkv_w_rkv, rwkv_w0, rwkv_w1, rwkv_w2,
              rwkv_a0, rwkv_a1, rwkv_a2, rwkv_v0, rwkv_v1, rwkv_v2, rwkv_g1, rwkv_g2, rwkv_k_k, rwkv_k_a,
              rwkv_r_k, rwkv_lnx_g, rwkv_lnx_b, rwkv_wo, s5_log_dt, s5_lam_re, s5_lam_im, s5_b_re, s5_b_im,
              s5_c_re, s5_c_im, s5_d, s5_w_val, s5_w_gate, moba_w_qkv, moba_wo, ffn_w_in, ffn_conv_w,
              ffn_conv_b, ffn_w_down):
    p = dict(ln_g=ln_g, ln_b=ln_b, rwkv_mu=rwkv_mu, rwkv_w_rkv=rwkv_w_rkv, rwkv_w0=rwkv_w0, rwkv_w1=rwkv_w1,
             rwkv_w2=rwkv_w2, rwkv_a0=rwkv_a0, rwkv_a1=rwkv_a1, rwkv_a2=rwkv_a2, rwkv_v0=rwkv_v0,
             rwkv_v1=rwkv_v1, rwkv_v2=rwkv_v2, rwkv_g1=rwkv_g1, rwkv_g2=rwkv_g2, rwkv_k_k=rwkv_k_k,
             rwkv_k_a=rwkv_k_a, rwkv_r_k=rwkv_r_k, rwkv_lnx_g=rwkv_lnx_g, rwkv_lnx_b=rwkv_lnx_b,
             rwkv_wo=rwkv_wo, s5_log_dt=s5_log_dt, s5_lam_re=s5_lam_re, s5_lam_im=s5_lam_im,
             s5_b_re=s5_b_re, s5_b_im=s5_b_im, s5_c_re=s5_c_re, s5_c_im=s5_c_im, s5_d=s5_d,
             s5_w_val=s5_w_val, s5_w_gate=s5_w_gate, moba_w_qkv=moba_w_qkv, moba_wo=moba_wo,
             ffn_w_in=ffn_w_in, ffn_conv_w=ffn_conv_w, ffn_conv_b=ffn_conv_b, ffn_w_down=ffn_w_down)
    B = x_prompt.shape[0]
    y_prompt, p_wkv, p_shift, p_s5_re, p_s5_im, p_conv, p_k, p_v = trunk(
        x_prompt,
        jnp.zeros((N_RWKV, B, RWKV_H, RWKV_N, RWKV_N), jnp.float32),
        jnp.zeros((N_RWKV, B, D_MODEL), x_prompt.dtype),
        jnp.zeros((N_S5, B, S5_G, S5_P), jnp.float32),
        jnp.zeros((N_S5, B, S5_G, S5_P), jnp.float32),
        jnp.zeros((DEPTH, B, CONV_W - 1, D_FF), x_prompt.dtype),
        None, None, None, p)
    y_sample, s_wkv, s_shift, s_s5_re, s_s5_im, s_conv, s_k, s_v = trunk(
        x_sample, state_rwkv_wkv, state_rwkv_shift, state_s5_re, state_s5_im, state_ffn_conv,
        cache_k, cache_v, page_table, p)
    return (y_prompt, y_sample, p_wkv, p_shift, p_s5_re, p_s5_im, p_conv, p_k, p_v,
            s_wkv, s_shift, s_s5_re, s_s5_im, s_conv, s_k, s_v)
```

```python
import functools
import math

import jax
import jax.numpy as jnp
from jax import lax
from jax.experimental import pallas as pl
from jax.experimental.pallas import tpu as pltpu

F32 = jnp.float32
BF16 = jnp.bfloat16

D_MODEL = 2048
DEPTH = 4
RWKV_N = 64
RWKV_H = D_MODEL // RWKV_N
RWKV_GN_EPS = 64e-5
S5_GROUP = 16
S5_G = D_MODEL // S5_GROUP
S5_P = 64
MOBA_H = 16
MOBA_DH = D_MODEL // MOBA_H
MOBA_BLOCK = 256
MOBA_TOPK = 3
PAGE_SIZE = 128
D_FF = 11 * D_MODEL // 4
CONV_W = 3
LN_EPS = 1e-5
DN_ALPHA = (2.0 * DEPTH) ** 0.25
NEG = -1e30

LANES = 128
SUBLANES = 8
VMEM_LIMIT_BYTES = 52 * 1024 * 1024

WKV_CHUNK = 64
S5_SLAB = LANES
S5_STATES = (S5_SLAB // S5_GROUP) * S5_P


def _params(n_axes):
    return pltpu.CompilerParams(dimension_semantics=("arbitrary",) * n_axes,
                                vmem_limit_bytes=VMEM_LIMIT_BYTES)


def _sigmoid(x):
    return 1.0 / (1.0 + jnp.exp(-x))


def _softplus(x):
    return jnp.maximum(x, 0.0) + jnp.log1p(jnp.exp(-jnp.abs(x)))


def _dg(a, b, dims):
    return lax.dot_general(a, b, (dims, ((), ())), preferred_element_type=F32)


_NN = ((1,), (0,))
_NT = ((1,), (1,))
_TN = ((0,), (0,))


def _split(a):
    hi = a.astype(BF16)
    lo = (a - hi.astype(F32)).astype(BF16)
    return hi, lo


def _dot3(a, b, dims=_NN):
    ah, al = _split(a)
    bh, bl = _split(b)
    return _dg(ah, bh, dims) + (_dg(al, bh, dims) + _dg(ah, bl, dims))


def _dot1(a, b, dims=_NN):
    return _dg(a.astype(BF16), b.astype(BF16), dims)


def _dot_exact_lhs(lhs_bf16, x):
    x1 = x.astype(BF16)
    r1 = x - x1.astype(F32)
    x2 = r1.astype(BF16)
    x3 = (r1 - x2.astype(F32)).astype(BF16)
    return _dg(lhs_bf16, x1, _NN) + (_dg(lhs_bf16, x2, _NN) + _dg(lhs_bf16, x3, _NN))


def _layer_norm(y, g, b):
    mu = jnp.mean(y, axis=-1, keepdims=True)
    yc = y - mu
    var = jnp.mean(yc * yc, axis=-1, keepdims=True)
    return yc * lax.rsqrt(var + LN_EPS) * g + b


def _mm_kernel(*refs, n_row, n_vec, n_w, n_erow, n_evec, n_out, prologue, epilogue):
    it = iter(refs)
    rows = [next(it) for _ in range(n_row)]
    vecs = [next(it) for _ in range(n_vec)]
    ws = [next(it) for _ in range(n_w)]
    erows = [next(it) for _ in range(n_erow)]
    evecs = [next(it) for _ in range(n_evec)]
    outs = [next(it) for _ in range(n_out)]
    a_scr = next(it)

    @pl.when(pl.program_id(1) == 0)
    def _():
        a_scr[...] = prologue(*[r[...] for r in rows], *[v[...] for v in vecs]).astype(BF16)

    a = a_scr[...]
    accs = [jnp.dot(a, w[...], preferred_element_type=F32) for w in ws]
    res = epilogue(*accs, *[e[...] for e in erows], *[e[...] for e in evecs])
    for o_ref, o in zip(outs, res):
        o_ref[...] = o.astype(o_ref.dtype)


def _mm(name, rows, vecs, ws, erows, evecs, prologue, epilogue, out_dtypes):
    M, K = rows[0].shape
    N = ws[0].shape[1]
    tm = min(512, M)
    tn = 512 if N % 512 == 0 else N
    assert M % tm == 0 and N % tn == 0
    in_specs = ([pl.BlockSpec((tm, K), lambda i, j: (i, 0)) for _ in rows]
                + [pl.BlockSpec((1, K), lambda i, j: (0, 0)) for _ in vecs]
                + [pl.BlockSpec((K, tn), lambda i, j: (0, j)) for _ in ws]
                + [pl.BlockSpec((tm, tn), lambda i, j: (i, j)) for _ in erows]
                + [pl.BlockSpec((1, tn), lambda i, j: (0, j)) for _ in evecs])
    out_specs = [pl.BlockSpec((tm, tn), lambda i, j: (i, j)) for _ in out_dtypes]
    out_shape = [jax.ShapeDtypeStruct((M, N), dt) for dt in out_dtypes]
    kern = functools.partial(_mm_kernel, n_row=len(rows), n_vec=len(vecs), n_w=len(ws), n_erow=len(erows),
                             n_evec=len(evecs), n_out=len(out_dtypes), prologue=prologue, epilogue=epilogue)
    return pl.pallas_call(
        kern, grid=(M // tm, N // tn), in_specs=in_specs, out_specs=out_specs, out_shape=out_shape,
        scratch_shapes=[pltpu.VMEM((tm, K), BF16)], compiler_params=_params(2), name=name,
    )(*rows, *vecs, *ws, *erows, *evecs)


def _mm_ln_kernel(*refs, n_row, prologue, n_k):
    it = iter(refs)
    rows = [next(it) for _ in range(n_row)]
    w_ref, x_ref, g_ref, b_ref, o_ref, acc = (next(it) for _ in range(6))
    k = pl.program_id(1)
    a = prologue(*[r[...] for r in rows]).astype(BF16)
    p = jnp.dot(a, w_ref[...], preferred_element_type=F32)

    @pl.when(k == 0)
    def _():
        acc[...] = p

    @pl.when(k > 0)
    def _():
        acc[...] += p

    @pl.when(k == n_k - 1)
    def _():
        o_ref[...] = _layer_norm(DN_ALPHA * x_ref[...] + acc[...], g_ref[...], b_ref[...])


def _mm_ln(name, rows, w, x, g, b, prologue):
    M, K = rows[0].shape
    N = w.shape[1]
    tm = min(512, M)
    tk = 512
    assert M % tm == 0 and K % tk == 0
    n_k = K // tk
    in_specs = ([pl.BlockSpec((tm, tk), lambda i, k: (i, k)) for _ in rows]
                + [pl.BlockSpec((tk, N), lambda i, k: (k, 0)),
                   pl.BlockSpec((tm, N), lambda i, k: (i, 0)),
                   pl.BlockSpec((1, N), lambda i, k: (0, 0)),
                   pl.BlockSpec((1, N), lambda i, k: (0, 0))])
    kern = functools.partial(_mm_ln_kernel, n_row=len(rows), prologue=prologue, n_k=n_k)
    return pl.pallas_call(
        kern, grid=(M // tm, n_k), in_specs=in_specs,
        out_specs=pl.BlockSpec((tm, N), lambda i, k: (i, 0)),
        out_shape=jax.ShapeDtypeStruct((M, N), F32),
        scratch_shapes=[pltpu.VMEM((tm, N), F32)], compiler_params=_params(2), name=name,
    )(*rows, w, x, g, b)


def _ln_kernel(x_ref, h_ref, g_ref, b_ref, o_ref):
    o_ref[...] = _layer_norm(DN_ALPHA * x_ref[...] + h_ref[...], g_ref[...], b_ref[...])


def _ln(name, x, h, g, b):
    M, N = x.shape
    tm = min(512, M)
    row = pl.BlockSpec((tm, N), lambda i: (i, 0))
    vec = pl.BlockSpec((1, N), lambda i: (0, 0))
    return pl.pallas_call(_ln_kernel, grid=(M // tm,), in_specs=[row, row, vec, vec], out_specs=row,
                          out_shape=jax.ShapeDtypeStruct((M, N), F32), compiler_params=_params(1), name=name)(x, h, g, b)


def _ffn_in_kernel(x_ref, wg_ref, wv_ref, c0_ref, cw_ref, cb_ref, h_ref, cs_ref, a_scr, carry, *, bt, tt):
    t = pl.program_id(1)
    j = pl.program_id(2)
    rows = bt * tt
    tn = wg_ref.shape[1]

    @pl.when(j == 0)
    def _():
        a_scr[...] = x_ref[...].reshape(rows, x_ref.shape[2]).astype(BF16)

    @pl.when(t == 0)
    def _():
        carry[j] = c0_ref[...]

    a = a_scr[...]
    hg = jnp.dot(a, wg_ref[...], preferred_element_type=F32)
    hv = jnp.dot(a, wv_ref[...], preferred_element_type=F32)
    prev = carry[j]
    p2 = jnp.broadcast_to(prev[:, 0:1, :], (bt, tt, tn)).reshape(rows, tn)
    p1 = jnp.broadcast_to(prev[:, 1:2, :], (bt, tt, tn)).reshape(rows, tn)
    r = lax.broadcasted_iota(jnp.int32, (rows, tn), 0) % tt
    s1 = jnp.where(r == 0, p1, pltpu.roll(hg, 1, axis=0))
    s2 = jnp.where(r == 0, p2, jnp.where(r == 1, p1, pltpu.roll(hg, 2, axis=0)))
    cw = cw_ref[...]
    c = cb_ref[...] + cw[0:1, :] * s2 + cw[1:2, :] * s1 + cw[2:3, :] * hg
    h_ref[...] = (c * _sigmoid(c) * hv).astype(h_ref.dtype)
    last = hg.reshape(bt, tt, tn)[:, tt - 2:, :]
    carry[j] = last
    cs_ref[0] = last


def _ffn_in(name, x3, wg, wv, conv0, cw, cb):
    B, T, K = x3.shape
    F = wg.shape[1]
    bt, tt = (1, 512) if T >= 512 else (B, T)
    tn = 512
    assert B % bt == 0 and T % tt == 0 and F % tn == 0 and tt >= CONV_W - 1
    n_j = F // tn
    kern = functools.partial(_ffn_in_kernel, bt=bt, tt=tt)
    n_t = T // tt
    h, cst = pl.pallas_call(
        kern, grid=(B // bt, n_t, n_j),
        in_specs=[pl.BlockSpec((bt, tt, K), lambda b, t, j: (b, t, 0)),
                  pl.BlockSpec((K, tn), lambda b, t, j: (0, j)),
                  pl.BlockSpec((K, tn), lambda b, t, j: (0, j)),
                  pl.BlockSpec((bt, CONV_W - 1, tn), lambda b, t, j: (b, 0, j)),
                  pl.BlockSpec((CONV_W, tn), lambda b, t, j: (0, j)),
                  pl.BlockSpec((1, tn), lambda b, t, j: (0, j))],
        out_specs=[pl.BlockSpec((bt * tt, tn), lambda b, t, j: (b * n_t + t, j)),
                   pl.BlockSpec((1, bt, CONV_W - 1, tn), lambda b, t, j: (t, b, 0, j))],
        out_shape=[jax.ShapeDtypeStruct((B * T, F), BF16),
                   jax.ShapeDtypeStruct((n_t, B, CONV_W - 1, F), F32)],
        scratch_shapes=[pltpu.VMEM((bt * tt, K), BF16), pltpu.VMEM((n_j, bt, CONV_W - 1, tn), F32)],
        compiler_params=_params(3), name=name,
    )(x3, wg, wv, conv0, cw, cb)
    return h, cst[n_t - 1]


def _wkv_head(r, lw, k, v, a, kkp, kap, rkp, lng, lnb, S, consts):
    C = WKV_CHUNK
    l_incl, strict, incl, eye = consts
    kkn = k * kkp
    nrm = jnp.sqrt(jnp.sum(kkn * kkn, axis=-1, keepdims=True))
    kk = kkn / jnp.maximum(nrm, 1e-12)
    kp = k * (1.0 + (a - 1.0) * kap)
    b = kk * a
    cum = _dot_exact_lhs(l_incl, lw)
    cum_last = cum[C - 1:C, :]
    e_neg = jnp.exp(-cum)
    x = jnp.concatenate([kk * jnp.exp(cum - lw), r * jnp.exp(cum)], axis=0)
    wcat = jnp.concatenate([b * e_neg, kp * e_neg], axis=0)
    gm = _dot3(x, wcat, _NT)
    xs = _dot3(x, S, _NT)
    am = jnp.where(strict, gm[:C, :C], 0.0)
    ak = jnp.where(strict, gm[:C, C:], 0.0)
    mb = jnp.where(incl, gm[C:, :C], 0.0)
    mk = jnp.where(incl, gm[C:, C:], 0.0)
    kv = _dot3(jnp.concatenate([ak, mk], axis=0), v)
    rhs = -(xs[:C] + kv[:C])
    npow = -am
    tinv = eye + npow
    for _ in range(int(math.log2(C)) - 1):
        npow = _dot3(npow, npow)
        tinv = tinv + _dot3(tinv, npow)
    u = _dot3(tinv, rhs)
    y = xs[C:] + kv[C:] + _dot3(mb, u)
    e_rest = jnp.exp(cum_last - cum)
    s_new = S * jnp.exp(cum_last) + _dot3(jnp.concatenate([u, v], axis=0),
                                          jnp.concatenate([b * e_rest, kp * e_rest], axis=0), _TN)
    mu = jnp.mean(y, axis=-1, keepdims=True)
    yc = y - mu
    var = jnp.mean(yc * yc, axis=-1, keepdims=True)
    yn = yc * lax.rsqrt(var + RWKV_GN_EPS) * lng + lnb
    bonus = jnp.sum(r * kp * rkp, axis=-1, keepdims=True) * v
    return yn + bonus, s_new


def _wkv_kernel(r_ref, lw_ref, k_ref, v_ref, a_ref, kk_ref, ka_ref, rk_ref, g_ref, b_ref, s0_ref,
                z_ref, so_ref, s_scr, *, n_t):
    t = pl.program_id(2)
    C = WKV_CHUNK
    N = RWKV_N

    @pl.when(t == 0)
    def _():
        s_scr[...] = s0_ref[0]

    row = lax.broadcasted_iota(jnp.int32, (C, C), 0)
    col = lax.broadcasted_iota(jnp.int32, (C, C), 1)
    consts = ((row >= col).astype(BF16), row > col, row >= col, (row == col).astype(F32))
    zs = []
    for h in range(LANES // N):
        sl = slice(h * N, (h + 1) * N)
        z, s_new = _wkv_head(r_ref[0, :, sl], lw_ref[0, :, sl], k_ref[0, :, sl], v_ref[0, :, sl], a_ref[0, :, sl],
                             kk_ref[:, sl], ka_ref[:, sl], rk_ref[:, sl], g_ref[:, sl], b_ref[:, sl],
                             s_scr[h], consts)
        s_scr[h] = s_new
        zs.append(z)
    z_ref[0] = jnp.concatenate(zs, axis=-1)

    @pl.when(t == n_t - 1)
    def _():
        so_ref[0] = s_scr[...]


def _wkv(name, r, lw, k, v, a, kkp, kap, rkp, lng, lnb, s0):
    B, T, D = r.shape
    C = WKV_CHUNK
    assert T % C == 0
    n_t = T // C
    hp = LANES // RWKV_N
    tok = pl.BlockSpec((1, C, LANES), lambda b, h, t: (b, t, h))
    vec = pl.BlockSpec((1, LANES), lambda b, h, t: (0, h))
    st = pl.BlockSpec((1, hp, RWKV_N, RWKV_N), lambda b, h, t: (b, h, 0, 0))
    return pl.pallas_call(
        functools.partial(_wkv_kernel, n_t=n_t), grid=(B, D // LANES, n_t),
        in_specs=[tok] * 5 + [vec] * 5 + [st], out_specs=[tok, st],
        out_shape=[jax.ShapeDtypeStruct((B, T, D), F32), jax.ShapeDtypeStruct(s0.shape, F32)],
        scratch_shapes=[pltpu.VMEM((hp, RWKV_N, RWKV_N), F32)], compiler_params=_params(3), name=name,
    )(r, lw, k, v, a, kkp, kap, rkp, lng, lnb, s0)


def _cmul(ar, ai, br, bi):
    return ar * br - ai * bi, ar * bi + ai * br


def _s5_kernel(x_ref, wb_ref, wc_ref, lr_ref, li_ref, dt_ref, d_ref, h0r_ref, h0i_ref,
               z_ref, hr_ref, hi_ref, pw_re, pw_im, cf, hc, xr_s, xi_s, *, n_t, tt):
    t = pl.program_id(2)
    R = SUBLANES

    @pl.when(t == 0)
    def _():
        lr, li, dt = lr_ref[0], li_ref[0], dt_ref[0]
        n = (lax.broadcasted_iota(jnp.int32, (R, S5_STATES), 0) + 1).astype(F32)
        mag = jnp.exp(lr * dt * n)
        ang = li * dt * n
        pw_re[...] = mag * jnp.cos(ang)
        pw_im[...] = mag * jnp.sin(ang)
        ar, ai = pw_re[0:1, :], pw_im[0:1, :]
        den = lr * lr + li * li
        cf[0:1, :] = ((ar - 1.0) * lr + ai * li) / den
        cf[1:2, :] = (ai * lr - (ar - 1.0) * li) / den
        hc[0:1, :] = h0r_ref[0]
        hc[1:2, :] = h0i_ref[0]

    x = x_ref[0]
    bu = _dot3(x, wb_ref[0])
    cr, ci = cf[0:1, :], cf[1:2, :]
    br, bi = _cmul(cr, ci, bu[:, :S5_STATES], bu[:, S5_STATES:])
    xr_s[...] = br
    xi_s[...] = bi
    rid = lax.broadcasted_iota(jnp.int32, (R, S5_STATES), 0)
    pr, pi = pw_re[...], pw_im[...]

    def body(i, carry):
        cr_, ci_ = carry
        o = pl.multiple_of(i * R, R)
        vr, vi = xr_s[pl.ds(o, R), :], xi_s[pl.ds(o, R), :]
        for sh in (1, 2, 4):
            mr, mi = pr[sh - 1:sh, :], pi[sh - 1:sh, :]
            sr, si = pltpu.roll(vr, sh, axis=0), pltpu.roll(vi, sh, axis=0)
            ur, ui = _cmul(mr, mi, sr, si)
            keep = rid >= sh
            vr, vi = vr + jnp.where(keep, ur, 0.0), vi + jnp.where(keep, ui, 0.0)
        ur, ui = _cmul(pr, pi, cr_, ci_)
        vr, vi = vr + ur, vi + ui
        xr_s[pl.ds(o, R), :] = vr
        xi_s[pl.ds(o, R), :] = vi
        return vr[R - 1:R, :], vi[R - 1:R, :]

    hr, hi = lax.fori_loop(0, tt // R, body, (hc[0:1, :], hc[1:2, :]))
    hc[0:1, :] = hr
    hc[1:2, :] = hi
    y = _dot3(jnp.concatenate([xr_s[...], xi_s[...]], axis=-1), wc_ref[0]) + d_ref[...] * x
    z_ref[0] = 0.5 * y * (1.0 + jnp.tanh(math.sqrt(2.0 / math.pi) * (y + 0.044715 * (y * y * y))))

    @pl.when(t == n_t - 1)
    def _():
        hr_ref[0] = hr
        hi_ref[0] = hi


def _s5(name, x3, wb, wc, lr, li, dt, d, h0r, h0i):
    B, T, D = x3.shape
    tt = min(256, T)
    assert T % tt == 0 and tt % SUBLANES == 0
    n_t = T // tt
    n_s = D // S5_SLAB
    S = S5_STATES
    st = pl.BlockSpec((1, 1, S), lambda b, s, t: (b, 0, s))
    pv = pl.BlockSpec((1, 1, S), lambda b, s, t: (0, 0, s))
    return pl.pallas_call(
        functools.partial(_s5_kernel, n_t=n_t, tt=tt), grid=(B, n_s, n_t),
        in_specs=[pl.BlockSpec((1, tt, S5_SLAB), lambda b, s, t: (b, t, s)),
                  pl.BlockSpec((1, S5_SLAB, 2 * S), lambda b, s, t: (s, 0, 0)),
                  pl.BlockSpec((1, 2 * S, S5_SLAB), lambda b, s, t: (s, 0, 0)),
                  pv, pv, pv,
                  pl.BlockSpec((1, S5_SLAB), lambda b, s, t: (0, s)),
                  st, st],
        out_specs=[pl.BlockSpec((1, tt, S5_SLAB), lambda b, s, t: (b, t, s)), st, st],
        out_shape=[jax.ShapeDtypeStruct((B, T, D), F32),
                   jax.ShapeDtypeStruct((B, 1, n_s * S), F32), jax.ShapeDtypeStruct((B, 1, n_s * S), F32)],
        scratch_shapes=[pltpu.VMEM((SUBLANES, S), F32), pltpu.VMEM((SUBLANES, S), F32),
                        pltpu.VMEM((SUBLANES, S), F32), pltpu.VMEM((SUBLANES, S), F32),
                        pltpu.VMEM((tt, S), F32), pltpu.VMEM((tt, S), F32)],
        compiler_params=_params(3), name=name,
    )(x3, wb, wc, lr, li, dt, d, h0r, h0i)


def _select_topk(gate, idx, valid, axis):
    g = jnp.where(valid, gate, NEG)
    big = jnp.int32(1 << 30)
    sel = jnp.zeros(gate.shape, jnp.bool_)
    for _ in range(MOBA_TOPK):
        m = jnp.max(g, axis=axis, keepdims=True)
        first = jnp.min(jnp.where(g == m, idx, big), axis=axis, keepdims=True)
        pick = idx == first
        sel = jnp.logical_or(sel, pick)
        g = jnp.where(pick, -jnp.inf, g)
    return jnp.logical_and(sel, valid)


def _moba_prompt_kernel(q_ref, k_ref, v_ref, sl_ref, o_ref, *, n_blk):
    i = pl.program_id(2)
    BK = MOBA_BLOCK
    q = q_ref[0]
    K = k_ref[0]
    V = v_ref[0]
    kmean = jnp.sum(K.reshape(n_blk, BK, MOBA_DH), axis=1) * (1.0 / BK)
    gate = _dot3(q, kmean, _NT)
    nidx = lax.broadcasted_iota(jnp.int32, (BK, n_blk), 1)
    sel = _select_topk(gate, nidx, nidx < i, axis=1).astype(F32)
    slope = sl_ref[0][:, 0:1]
    qi = lax.broadcasted_iota(jnp.int32, (BK, BK), 0)
    ki = lax.broadcasted_iota(jnp.int32, (BK, BK), 1)
    qb = q.astype(BF16)
    scale = MOBA_DH ** -0.5
    ss = []
    for n in range(n_blk):
        s = _dg(qb, K[n * BK:(n + 1) * BK].astype(BF16), _NT) * scale
        dist = qi - ki + (i - n) * BK
        s = s - slope * dist.astype(F32)
        ok = jnp.logical_or(jnp.logical_and(i == n, dist >= 0), sel[:, n:n + 1] > 0.5)
        ss.append(jnp.where(ok, s, NEG))
    m = functools.reduce(jnp.maximum, [jnp.max(s, axis=-1, keepdims=True) for s in ss])
    ps = [jnp.exp(s - m) for s in ss]
    l = functools.reduce(lambda x, y: x + y, [jnp.sum(p, axis=-1, keepdims=True) for p in ps])
    acc = jnp.zeros((BK, MOBA_DH), F32)
    for n in range(n_blk):
        acc = acc + _dg((ps[n] / l).astype(BF16), V[n * BK:(n + 1) * BK].astype(BF16), _NN)
    o_ref[0] = acc


def _moba_prompt(name, q, k, v, slopes):
    B, T, D = q.shape
    n_blk = T // MOBA_BLOCK
    assert T % MOBA_BLOCK == 0
    qs = pl.BlockSpec((1, MOBA_BLOCK, MOBA_DH), lambda b, h, i: (b, i, h))
    kv = pl.BlockSpec((1, T, MOBA_DH), lambda b, h, i: (b, 0, h))
    return pl.pallas_call(
        functools.partial(_moba_prompt_kernel, n_blk=n_blk), grid=(B, MOBA_H, n_blk),
        in_specs=[qs, kv, kv, pl.BlockSpec((1, 1, LANES), lambda b, h, i: (h, 0, 0))], out_specs=qs,
        out_shape=jax.ShapeDtypeStruct((B, T, D), F32), compiler_params=_params(3), name=name,
    )(q, k, v, slopes)


def _q_blockdiag(q, nq):
    rows = MOBA_H * nq
    qt = jnp.concatenate([q] * MOBA_H, axis=0)
    rh = lax.broadcasted_iota(jnp.int32, (rows, D_MODEL), 0) // nq
    ch = lax.broadcasted_iota(jnp.int32, (rows, D_MODEL), 1) // MOBA_DH
    return jnp.where(rh == ch, qt, 0.0)


def _moba_scores_kernel(pt_ref, q_ref, k_ref, s_ref, ks_ref, qbd, *, nq):
    @pl.when(pl.program_id(1) == 0)
    def _():
        qbd[...] = _q_blockdiag(q_ref[0], nq).astype(BF16)

    kp = k_ref[0]
    s_ref[0, 0] = _dg(kp.astype(BF16), qbd[...], _NT)
    ks_ref[0, 0] = jnp.sum(kp, axis=0, keepdims=True)


def _moba_scores(name, page_table, q, cache_k):
    B, nq, D = q.shape
    n_pg = page_table.shape[1]
    gs = pltpu.PrefetchScalarGridSpec(
        num_scalar_prefetch=1, grid=(B, n_pg),
        in_specs=[pl.BlockSpec((1, nq, D), lambda b, p, pt: (b, 0, 0)),
                  pl.BlockSpec((1, PAGE_SIZE, D), lambda b, p, pt: (pt[b, p], 0, 0))],
        out_specs=[pl.BlockSpec((1, 1, PAGE_SIZE, MOBA_H * nq), lambda b, p, pt: (b, p, 0, 0)),
                   pl.BlockSpec((1, 1, 1, D), lambda b, p, pt: (b, p, 0, 0))],
        scratch_shapes=[pltpu.VMEM((MOBA_H * nq, D), BF16)])
    return pl.pallas_call(
        functools.partial(_moba_scores_kernel, nq=nq), grid_spec=gs,
        out_shape=[jax.ShapeDtypeStruct((B, n_pg, PAGE_SIZE, MOBA_H * nq), F32),
                   jax.ShapeDtypeStruct((B, n_pg, 1, D), F32)],
        compiler_params=_params(2), name=name,
    )(page_table, q, cache_k)


def _moba_probs_kernel(s_ref, ks_ref, q_ref, kn_ref, sl_ref, p_ref, pn_ref, sel_scr, *, nq, n_blk, past):
    BK = MOBA_BLOCK
    HQ = MOBA_H * nq
    qbd = _q_blockdiag(q_ref[0], nq)
    ksum = ks_ref[0]
    ppb = BK // PAGE_SIZE
    kmean = jnp.sum(ksum.reshape(n_blk, ppb, D_MODEL), axis=1) * (1.0 / BK)
    gate = _dot3(kmean, qbd, _NT)
    nidx = lax.broadcasted_iota(jnp.int32, (n_blk, HQ), 0)
    own = past // BK
    sel = _select_topk(gate, nidx, nidx < own, axis=0)
    sel_scr[...] = sel.astype(F32)
    slope = sl_ref[...]
    scale = MOBA_DH ** -0.5
    qpos = past + lax.broadcasted_iota(jnp.int32, (1, HQ), 1) % nq
    tok = lax.broadcasted_iota(jnp.int32, (BK, HQ), 0)

    def scores(n):
        s = s_ref[0, n] * scale - slope * (qpos - (n * BK + tok)).astype(F32)
        return jnp.where(sel_scr[pl.ds(n, 1), :] > 0.5, s, NEG)

    sn = _dg(kn_ref[0].astype(BF16), qbd.astype(BF16), _NT) * scale
    tn = lax.broadcasted_iota(jnp.int32, (PAGE_SIZE, HQ), 0)
    dn = qpos - (past + tn)
    sn = jnp.where(jnp.logical_and(dn >= 0, tn < nq), sn - slope * dn.astype(F32), NEG)
    m = lax.fori_loop(0, n_blk, lambda n, m_: jnp.maximum(m_, jnp.max(scores(n), axis=0, keepdims=True)),
                      jnp.max(sn, axis=0, keepdims=True))
    pn = jnp.exp(sn - m)
    l = lax.fori_loop(0, n_blk, lambda n, l_: l_ + jnp.sum(jnp.exp(scores(n) - m), axis=0, keepdims=True),
                      jnp.sum(pn, axis=0, keepdims=True))
    pn_ref[0] = (pn / l).astype(pn_ref.dtype)

    def write(n, c):
        p_ref[0, n] = (jnp.exp(scores(n) - m) / l).astype(p_ref.dtype)
        return c

    lax.fori_loop(0, n_blk, write, 0)


def _moba_probs(name, s, ksum, q, k_new_pad, slopes_hq, past):
    B, n_blk, BK, HQ = s.shape
    nq = q.shape[1]
    blk = pl.BlockSpec((1, n_blk, BK, HQ), lambda b: (b, 0, 0, 0))
    return pl.pallas_call(
        functools.partial(_moba_probs_kernel, nq=nq, n_blk=n_blk, past=past), grid=(B,),
        in_specs=[blk,
                  pl.BlockSpec((1, ksum.shape[1], D_MODEL), lambda b: (b, 0, 0)),
                  pl.BlockSpec((1, nq, D_MODEL), lambda b: (b, 0, 0)),
                  pl.BlockSpec((1, PAGE_SIZE, D_MODEL), lambda b: (b, 0, 0)),
                  pl.BlockSpec((1, HQ), lambda b: (0, 0))],
        out_specs=[blk, pl.BlockSpec((1, PAGE_SIZE, HQ), lambda b: (b, 0, 0))],
        out_shape=[jax.ShapeDtypeStruct((B, n_blk, BK, HQ), BF16), jax.ShapeDtypeStruct((B, PAGE_SIZE, HQ), BF16)],
        scratch_shapes=[pltpu.VMEM((n_blk, HQ), F32)],
        compiler_params=_params(1), name=name,
    )(s, ksum, q, k_new_pad, slopes_hq)


def _moba_pv_kernel(pt_ref, p_ref, v_ref, pn_ref, vn_ref, o_ref, acc, *, nq, n_pg):
    pg = pl.program_id(1)

    @pl.when(pg == 0)
    def _():
        acc[...] = _dg(pn_ref[0], vn_ref[0].astype(BF16), _TN)

    acc[...] += _dg(p_ref[0, 0], v_ref[0].astype(BF16), _TN)

    @pl.when(pg == n_pg - 1)
    def _():
        a = acc[...]
        o_ref[0] = jnp.concatenate(
            [a[h * nq:(h + 1) * nq, h * MOBA_DH:(h + 1) * MOBA_DH] for h in range(MOBA_H)], axis=-1)


def _moba_pv(name, page_table, p, cache_v, pn, v_new_pad, nq):
    B, n_pg, _, HQ = p.shape
    D = D_MODEL
    gs = pltpu.PrefetchScalarGridSpec(
        num_scalar_prefetch=1, grid=(B, n_pg),
        in_specs=[pl.BlockSpec((1, 1, PAGE_SIZE, HQ), lambda b, g, pt: (b, g, 0, 0)),
                  pl.BlockSpec((1, PAGE_SIZE, D), lambda b, g, pt: (pt[b, g], 0, 0)),
                  pl.BlockSpec((1, PAGE_SIZE, HQ), lambda b, g, pt: (b, 0, 0)),
                  pl.BlockSpec((1, PAGE_SIZE, D), lambda b, g, pt: (b, 0, 0))],
        out_specs=pl.BlockSpec((1, nq, D), lambda b, g, pt: (b, 0, 0)),
        scratch_shapes=[pltpu.VMEM((HQ, D), F32)])
    return pl.pallas_call(
        functools.partial(_moba_pv_kernel, nq=nq, n_pg=n_pg), grid_spec=gs,
        out_shape=jax.ShapeDtypeStruct((B, nq, D), F32), compiler_params=_params(2), name=name,
    )(page_table, p, cache_v, pn, v_new_pad)


def _ident(x):
    return x


def _mix(x, xp, mu):
    return x + (xp - x) * mu


def _pad_cols(w, n):
    return jnp.pad(w, ((0, 0), (0, n - w.shape[1])))


def _pad_rows(w, n):
    return jnp.pad(w, ((0, n - w.shape[0]), (0, 0)))


def _rwkv_layer(tag, x3, shift0, s0, v_first, P, j, ln_g, ln_b):
    B, T, D = x3.shape
    M = B * T
    x = x3.reshape(M, D)
    xp = jnp.concatenate([shift0[:, None, :], x3[:, :-1]], axis=1).reshape(M, D)
    mu = P['rwkv_mu'][j]
    mus = [mu[n][None, :] for n in range(6)]
    wb = lambda w: w.astype(BF16)
    one = lambda acc: (acc,)
    rows = [x, xp]
    r, = _mm(tag + 'r', rows, [mus[0]], [wb(P['rwkv_w_rkv'][j, 0])], [], [], _mix, one, [F32])
    k, = _mm(tag + 'k', rows, [mus[2]], [wb(P['rwkv_w_rkv'][j, 1])], [], [], _mix, one, [F32])
    v, = _mm(tag + 'v', rows, [mus[3]], [wb(P['rwkv_w_rkv'][j, 2])], [], [], _mix, one, [F32])
    lp = LANES
    hw, = _mm(tag + 'w1', rows, [mus[1]], [wb(_pad_cols(P['rwkv_w1'][j], lp))], [], [], _mix,
              lambda acc: (jnp.tanh(acc),), [F32])
    lw, = _mm(tag + 'w2', [hw], [], [wb(_pad_rows(P['rwkv_w2'][j], lp))], [], [P['rwkv_w0'][j][None, :]], _ident,
              lambda acc, w0: (-jnp.exp(-_softplus(-(w0 + acc)) - 0.5),), [F32])
    ha, = _mm(tag + 'a1', rows, [mus[4]], [wb(_pad_cols(P['rwkv_a1'][j], lp))], [], [], _mix, one, [F32])
    a, = _mm(tag + 'a2', [ha], [], [wb(_pad_rows(P['rwkv_a2'][j], lp))], [], [P['rwkv_a0'][j][None, :]], _ident,
             lambda acc, a0: (_sigmoid(a0 + acc),), [F32])
    hg, = _mm(tag + 'g1', rows, [mus[5]], [wb(P['rwkv_g1'][j])], [], [], _mix,
              lambda acc: (_sigmoid(acc),), [F32])
    g, = _mm(tag + 'g2', [hg], [], [wb(P['rwkv_g2'][j])], [], [], _ident, one, [F32])
    if j == 0:
        v_first = v
    else:
        hv, = _mm(tag + 'v1', rows, [mus[3]], [wb(_pad_cols(P['rwkv_v1'][j - 1], lp))], [], [], _mix, one, [F32])
        v, = _mm(tag + 'v2', [hv], [], [wb(_pad_rows(P['rwkv_v2'][j - 1], lp))], [v, v_first],
                 [P['rwkv_v0'][j - 1][None, :]], _ident,
                 lambda acc, v_, vf, v0: (v_ + (vf - v_) * _sigmoid(v0 + acc),), [F32])
    Tp = -(-T // WKV_CHUNK) * WKV_CHUNK
    seq = lambda t: jnp.pad(t.reshape(B, T, D), ((0, 0), (0, Tp - T), (0, 0)))
    vecp = lambda name: P[name][j].reshape(1, D)
    z, s_new = _wkv(tag + 'wkv', seq(r), seq(lw), seq(k), seq(v), seq(a), vecp('rwkv_k_k'), vecp('rwkv_k_a'),
                    vecp('rwkv_r_k'), vecp('rwkv_lnx_g'), vecp('rwkv_lnx_b'), s0)
    z = z[:, :T].reshape(M, D)
    xn = _mm_ln(tag + 'wo', [z, g], wb(P['rwkv_wo'][j]), x, ln_g, ln_b, lambda z_, g_: z_ * g_)
    return xn, s_new, x3[:, -1], v_first


def _s5_weights(P, j):
    ns, gs = D_MODEL // S5_SLAB, S5_SLAB // S5_GROUP
    eye = jnp.eye(gs, dtype=F32)

    def bd_in(b):
        bt = jnp.swapaxes(b.reshape(ns, gs, S5_P, S5_GROUP), 2, 3)
        return jnp.einsum('sgcp,gh->sgchp', bt, eye).reshape(ns, S5_SLAB, S5_STATES)

    def bd_out(c):
        ct = jnp.swapaxes(c.reshape(ns, gs, S5_GROUP, S5_P), 2, 3)
        return jnp.einsum('sgpc,gh->sgphc', ct, eye).reshape(ns, S5_STATES, S5_SLAB)

    wb = jnp.concatenate([bd_in(P['s5_b_re'][j]), bd_in(P['s5_b_im'][j])], axis=2)
    wc = jnp.concatenate([bd_out(P['s5_c_re'][j]), -bd_out(P['s5_c_im'][j])], axis=1)
    flat = lambda t: t.reshape(1, 1, S5_G * S5_P)
    dt = jnp.broadcast_to(jnp.exp(P['s5_log_dt'][j])[:, None], (S5_G, S5_P))
    return wb, wc, flat(P['s5_lam_re'][j]), flat(P['s5_lam_im'][j]), flat(dt)


def _s5_layer(tag, x3, h0r, h0i, P, j, ln_g, ln_b):
    B, T, D = x3.shape
    M = B * T
    x = x3.reshape(M, D)
    wb, wc, lr, li, dt = _s5_weights(P, j)
    st = lambda h: h.reshape(B, 1, S5_G * S5_P)
    z, hr, hi = _s5(tag + 'scan', x3, wb, wc, lr, li, dt, P['s5_d'][j][None, :], st(h0r), st(h0i))
    out, = _mm(tag + 'gate', [z.reshape(M, D)], [], [P['s5_w_val'][j].astype(BF16), P['s5_w_gate'][j].astype(BF16)],
               [], [], _ident, lambda val, gate: (val * _sigmoid(gate),), [F32])
    xn = _ln(tag + 'ln', x, out, ln_g, ln_b)
    return xn, hr.reshape(B, S5_G, S5_P), hi.reshape(B, S5_G, S5_P)


def _alibi_slopes():
    return 2.0 ** (-8.0 * jnp.arange(1, MOBA_H + 1, dtype=F32) / MOBA_H)


def _moba_layer(tag, x3, cache_k, cache_v, page_table, P, j, ln_g, ln_b):
    B, T, D = x3.shape
    M = B * T
    x = x3.reshape(M, D)
    wqkv = P['moba_w_qkv'][j]
    ws = [wqkv[:, n * D:(n + 1) * D].astype(BF16) for n in range(3)]
    q, k, v = _mm(tag + 'qkv', [x], [], ws, [], [], _ident, lambda a, b, c: (a, b, c), [F32, F32, F32])
    q3, k3, v3 = (t.reshape(B, T, D) for t in (q, k, v))
    slopes = _alibi_slopes()
    if cache_k is None:
        sl = jnp.broadcast_to(slopes[:, None, None], (MOBA_H, 1, LANES))
        o = _moba_prompt(tag + 'attn', q3, k3, v3, sl)
    else:
        n_pool = cache_k.shape[1]
        ck = cache_k[j].reshape(n_pool, PAGE_SIZE, D)
        cv = cache_v[j].reshape(n_pool, PAGE_SIZE, D)
        n_pg = page_table.shape[1]
        past = n_pg * PAGE_SIZE
        assert past % MOBA_BLOCK == 0 and T <= PAGE_SIZE and T <= MOBA_BLOCK
        HQ = MOBA_H * T
        s, ksum = _moba_scores(tag + 'scores', page_table, q3, ck)
        sl = jnp.repeat(slopes, T)[None, :]
        pad = lambda t: jnp.pad(t, ((0, 0), (0, PAGE_SIZE - T), (0, 0)))
        p, pn = _moba_probs(tag + 'probs', s.reshape(B, past // MOBA_BLOCK, MOBA_BLOCK, HQ),
                            ksum.reshape(B, n_pg, D), q3, pad(k3), sl, past)
        o = _moba_pv(tag + 'pv', page_table, p.reshape(B, n_pg, PAGE_SIZE, HQ), cv, pn, pad(v3), T)
    xn = _mm_ln(tag + 'wo', [o.reshape(M, D)], P['moba_wo'][j].astype(BF16), x, ln_g, ln_b, _ident)
    return xn, k3.reshape(B, T, MOBA_H, MOBA_DH), v3.reshape(B, T, MOBA_H, MOBA_DH)


def _ffn_layer(tag, x3, conv0, P, i, ln_g, ln_b):
    B, T, D = x3.shape
    w_in = P['ffn_w_in'][i]
    h, cst = _ffn_in(tag + 'in', x3, w_in[:, :D_FF].astype(BF16), w_in[:, D_FF:].astype(BF16), conv0,
                     P['ffn_conv_w'][i], P['ffn_conv_b'][i][None, :])
    xn = _mm_ln(tag + 'down', [h], P['ffn_w_down'][i].astype(BF16), x3.reshape(B * T, D), ln_g, ln_b, _ident)
    return xn.reshape(B, T, D), cst


def _trunk(grp, x3, wkv0, shift0, s5re0, s5im0, conv0, cache_k, cache_v, page_table, P):
    B, T, D = x3.shape
    wkv_new, shift_new, s5re_new, s5im_new, k_new, v_new, conv_new = [], [], [], [], [], [], []
    v_first = None
    for i in range(DEPTH):
        kind, j = i % 3, i // 3
        tag = f'{grp}{i}_'
        g0, b0 = P['ln_g'][i, 0][None, :], P['ln_b'][i, 0][None, :]
        g1, b1 = P['ln_g'][i, 1][None, :], P['ln_b'][i, 1][None, :]
        if kind == 0:
            xn, S, last, v_first = _rwkv_layer(tag, x3, shift0[j], wkv0[j], v_first, P, j, g0, b0)
            wkv_new.append(S)
            shift_new.append(last)
        elif kind == 1:
            xn, hr, hi = _s5_layer(tag, x3, s5re0[j], s5im0[j], P, j, g0, b0)
            s5re_new.append(hr)
            s5im_new.append(hi)
        else:
            xn, kr, vr = _moba_layer(tag, x3, cache_k, cache_v, page_table, P, j, g0, b0)
            k_new.append(kr)
            v_new.append(vr)
        x3, cst = _ffn_layer(tag + 'ffn_', xn.reshape(B, T, D), conv0[i], P, i, g1, b1)
        conv_new.append(cst)
    return (x3, jnp.stack(wkv_new), jnp.stack(shift_new), jnp.stack(s5re_new), jnp.stack(s5im_new),
            jnp.stack(conv_new), jnp.stack(k_new), jnp.stack(v_new))


def kernel(x_prompt, x_sample, state_rwkv_wkv, state_rwkv_shift, state_s5_re, state_s5_im, state_ffn_conv, cache_k, cache_v, page_table, ln_g, ln_b, rwkv_mu, rwkv_w_rkv, rwkv_w0, rwkv_w1, rwkv_w2, rwkv_a0, rwkv_a1, rwkv_a2, rwkv_v0, rwkv_v1, rwkv_v2, rwkv_g1, rwkv_g2, rwkv_k_k, rwkv_k_a, rwkv_r_k, rwkv_lnx_g, rwkv_lnx_b, rwkv_wo, s5_log_dt, s5_lam_re, s5_lam_im, s5_b_re, s5_b_im, s5_c_re, s5_c_im, s5_d, s5_w_val, s5_w_gate, moba_w_qkv, moba_wo, ffn_w_in, ffn_conv_w, ffn_conv_b, ffn_w_down):
    P = dict(ln_g=ln_g, ln_b=ln_b, rwkv_mu=rwkv_mu, rwkv_w_rkv=rwkv_w_rkv, rwkv_w0=rwkv_w0, rwkv_w1=rwkv_w1,
             rwkv_w2=rwkv_w2, rwkv_a0=rwkv_a0, rwkv_a1=rwkv_a1, rwkv_a2=rwkv_a2, rwkv_v0=rwkv_v0,
             rwkv_v1=rwkv_v1, rwkv_v2=rwkv_v2, rwkv_g1=rwkv_g1, rwkv_g2=rwkv_g2, rwkv_k_k=rwkv_k_k,
             rwkv_k_a=rwkv_k_a, rwkv_r_k=rwkv_r_k.reshape(rwkv_r_k.shape[0], D_MODEL), rwkv_lnx_g=rwkv_lnx_g,
             rwkv_lnx_b=rwkv_lnx_b, rwkv_wo=rwkv_wo, s5_log_dt=s5_log_dt, s5_lam_re=s5_lam_re,
             s5_lam_im=s5_lam_im, s5_b_re=s5_b_re, s5_b_im=s5_b_im, s5_c_re=s5_c_re, s5_c_im=s5_c_im, s5_d=s5_d,
             s5_w_val=s5_w_val, s5_w_gate=s5_w_gate, moba_w_qkv=moba_w_qkv, moba_wo=moba_wo,
             ffn_w_in=ffn_w_in, ffn_conv_w=ffn_conv_w, ffn_conv_b=ffn_conv_b, ffn_w_down=ffn_w_down)
    B = x_prompt.shape[0]
    n_rwkv, n_s5 = state_rwkv_wkv.shape[0], state_s5_re.shape[0]
    zeros = lambda *s: jnp.zeros(s, F32)
    outs_p = _trunk('p', x_prompt, zeros(n_rwkv, B, RWKV_H, RWKV_N, RWKV_N), zeros(n_rwkv, B, D_MODEL),
                    zeros(n_s5, B, S5_G, S5_P), zeros(n_s5, B, S5_G, S5_P), zeros(DEPTH, B, CONV_W - 1, D_FF),
                    None, None, None, P)
    outs_s = _trunk('s', x_sample, state_rwkv_wkv, state_rwkv_shift, state_s5_re, state_s5_im, state_ffn_conv,
                    cache_k, cache_v, page_table, P)
    return (outs_p[0], outs_s[0]) + tuple(outs_p[1:]) + tuple(outs_s[1:])
```

```python
import functools
import math

import jax
import jax.numpy as jnp
from jax import lax
from jax.experimental import pallas as pl
from jax.experimental.pallas import tpu as pltpu

F32 = jnp.float32
BF16 = jnp.bfloat16

D_MODEL = 2048
DEPTH = 4
RWKV_N = 64
RWKV_H = D_MODEL // RWKV_N
RWKV_GN_EPS = 64e-5
S5_GROUP = 16
S5_G = D_MODEL // S5_GROUP
S5_P = 64
MOBA_H = 16
MOBA_DH = D_MODEL // MOBA_H
MOBA_BLOCK = 256
MOBA_TOPK = 3
PAGE_SIZE = 128
D_FF = 11 * D_MODEL // 4
CONV_W = 3
LN_EPS = 1e-5
DN_ALPHA = (2.0 * DEPTH) ** 0.25
NEG = -1e30

LANES = 128
SUBLANES = 8
VMEM_LIMIT_BYTES = 52 * 1024 * 1024

WKV_CHUNK = 64
WKV_HEADS = 4
S5_SLAB = LANES
S5_STATES = (S5_SLAB // S5_GROUP) * S5_P


def _params(n_axes):
    return pltpu.CompilerParams(dimension_semantics=("arbitrary",) * n_axes,
                                vmem_limit_bytes=VMEM_LIMIT_BYTES)


def _sigmoid(x):
    return 1.0 / (1.0 + jnp.exp(-x))


def _softplus(x):
    return jnp.maximum(x, 0.0) + jnp.log1p(jnp.exp(-jnp.abs(x)))


def _dg(a, b, dims):
    return lax.dot_general(a, b, (dims, ((), ())), preferred_element_type=F32)


_NN = ((1,), (0,))
_NT = ((1,), (1,))
_TN = ((0,), (0,))


def _split(a):
    hi = a.astype(BF16)
    lo = (a - hi.astype(F32)).astype(BF16)
    return hi, lo


def _dot3(a, b, dims=_NN):
    ah, al = _split(a)
    bh, bl = _split(b)
    return _dg(ah, bh, dims) + (_dg(al, bh, dims) + _dg(ah, bl, dims))


def _dot1(a, b, dims=_NN):
    return _dg(a.astype(BF16), b.astype(BF16), dims)


def _dot_exact_lhs(lhs_bf16, x):
    x1 = x.astype(BF16)
    r1 = x - x1.astype(F32)
    x2 = r1.astype(BF16)
    x3 = (r1 - x2.astype(F32)).astype(BF16)
    return _dg(lhs_bf16, x1, _NN) + (_dg(lhs_bf16, x2, _NN) + _dg(lhs_bf16, x3, _NN))


def _layer_norm(y, g, b):
    mu = jnp.mean(y, axis=-1, keepdims=True)
    yc = y - mu
    var = jnp.mean(yc * yc, axis=-1, keepdims=True)
    return yc * lax.rsqrt(var + LN_EPS) * g + b


def _mm_kernel(*refs, n_row, n_vec, n_w, n_erow, n_evec, n_out, prologue, epilogue):
    it = iter(refs)
    rows = [next(it) for _ in range(n_row)]
    vecs = [next(it) for _ in range(n_vec)]
    ws = [next(it) for _ in range(n_w)]
    erows = [next(it) for _ in range(n_erow)]
    evecs = [next(it) for _ in range(n_evec)]
    outs = [next(it) for _ in range(n_out)]
    a_scr = next(it)

    @pl.when(pl.program_id(1) == 0)
    def _():
        a_scr[...] = prologue(*[r[...] for r in rows], *[v[...] for v in vecs]).astype(BF16)

    a = a_scr[...]
    accs = [jnp.dot(a, w[...], preferred_element_type=F32) for w in ws]
    res = epilogue(*accs, *[e[...] for e in erows], *[e[...] for e in evecs])
    for o_ref, o in zip(outs, res):
        o_ref[...] = o.astype(o_ref.dtype)


def _mm(name, rows, vecs, ws, erows, evecs, prologue, epilogue, out_dtypes):
    M, K = rows[0].shape
    N = ws[0].shape[1]
    tm = min(512, M)
    tn = 512 if N % 512 == 0 else N
    assert M % tm == 0 and N % tn == 0
    in_specs = ([pl.BlockSpec((tm, K), lambda i, j: (i, 0)) for _ in rows]
                + [pl.BlockSpec((1, K), lambda i, j: (0, 0)) for _ in vecs]
                + [pl.BlockSpec((K, tn), lambda i, j: (0, j)) for _ in ws]
                + [pl.BlockSpec((tm, tn), lambda i, j: (i, j)) for _ in erows]
                + [pl.BlockSpec((1, tn), lambda i, j: (0, j)) for _ in evecs])
    out_specs = [pl.BlockSpec((tm, tn), lambda i, j: (i, j)) for _ in out_dtypes]
    out_shape = [jax.ShapeDtypeStruct((M, N), dt) for dt in out_dtypes]
    kern = functools.partial(_mm_kernel, n_row=len(rows), n_vec=len(vecs), n_w=len(ws), n_erow=len(erows),
                             n_evec=len(evecs), n_out=len(out_dtypes), prologue=prologue, epilogue=epilogue)
    return pl.pallas_call(
        kern, grid=(M // tm, N // tn), in_specs=in_specs, out_specs=out_specs, out_shape=out_shape,
        scratch_shapes=[pltpu.VMEM((tm, K), BF16)], compiler_params=_params(2), name=name,
    )(*rows, *vecs, *ws, *erows, *evecs)


def _mm_ln_kernel(*refs, n_row, prologue, n_k):
    it = iter(refs)
    rows = [next(it) for _ in range(n_row)]
    w_ref, x_ref, g_ref, b_ref, o_ref, acc = (next(it) for _ in range(6))
    k = pl.program_id(1)
    a = prologue(*[r[...] for r in rows]).astype(BF16)
    p = jnp.dot(a, w_ref[...], preferred_element_type=F32)

    @pl.when(k == 0)
    def _():
        acc[...] = p

    @pl.when(k > 0)
    def _():
        acc[...] += p

    @pl.when(k == n_k - 1)
    def _():
        o_ref[...] = _layer_norm(DN_ALPHA * x_ref[...] + acc[...], g_ref[...], b_ref[...])


def _mm_ln(name, rows, w, x, g, b, prologue):
    M, K = rows[0].shape
    N = w.shape[1]
    tm = min(512, M)
    tk = 512
    assert M % tm == 0 and K % tk == 0
    n_k = K // tk
    in_specs = ([pl.BlockSpec((tm, tk), lambda i, k: (i, k)) for _ in rows]
                + [pl.BlockSpec((tk, N), lambda i, k: (k, 0)),
                   pl.BlockSpec((tm, N), lambda i, k: (i, 0)),
                   pl.BlockSpec((1, N), lambda i, k: (0, 0)),
                   pl.BlockSpec((1, N), lambda i, k: (0, 0))])
    kern = functools.partial(_mm_ln_kernel, n_row=len(rows), prologue=prologue, n_k=n_k)
    return pl.pallas_call(
        kern, grid=(M // tm, n_k), in_specs=in_specs,
        out_specs=pl.BlockSpec((tm, N), lambda i, k: (i, 0)),
        out_shape=jax.ShapeDtypeStruct((M, N), F32),
        scratch_shapes=[pltpu.VMEM((tm, N), F32)], compiler_params=_params(2), name=name,
    )(*rows, w, x, g, b)


def _ln_kernel(x_ref, h_ref, g_ref, b_ref, o_ref):
    o_ref[...] = _layer_norm(DN_ALPHA * x_ref[...] + h_ref[...], g_ref[...], b_ref[...])


def _ln(name, x, h, g, b):
    M, N = x.shape
    tm = min(512, M)
    row = pl.BlockSpec((tm, N), lambda i: (i, 0))
    vec = pl.BlockSpec((1, N), lambda i: (0, 0))
    return pl.pallas_call(_ln_kernel, grid=(M // tm,), in_specs=[row, row, vec, vec], out_specs=row,
                          out_shape=jax.ShapeDtypeStruct((M, N), F32), compiler_params=_params(1), name=name)(x, h, g, b)


def _ffn_in_kernel(x_ref, wg_ref, wv_ref, c0_ref, cw_ref, cb_ref, h_ref, cs_ref, a_scr, carry, *, bt, tt):
    t = pl.program_id(1)
    j = pl.program_id(2)
    rows = bt * tt
    tn = wg_ref.shape[1]

    @pl.when(j == 0)
    def _():
        a_scr[...] = x_ref[...].reshape(rows, x_ref.shape[2]).astype(BF16)

    @pl.when(t == 0)
    def _():
        carry[j] = c0_ref[...]

    a = a_scr[...]
    hg = jnp.dot(a, wg_ref[...], preferred_element_type=F32)
    hv = jnp.dot(a, wv_ref[...], preferred_element_type=F32)
    prev = carry[j]
    p2 = jnp.broadcast_to(prev[:, 0:1, :], (bt, tt, tn)).reshape(rows, tn)
    p1 = jnp.broadcast_to(prev[:, 1:2, :], (bt, tt, tn)).reshape(rows, tn)
    r = lax.broadcasted_iota(jnp.int32, (rows, tn), 0) % tt
    s1 = jnp.where(r == 0, p1, pltpu.roll(hg, 1, axis=0))
    s2 = jnp.where(r == 0, p2, jnp.where(r == 1, p1, pltpu.roll(hg, 2, axis=0)))
    cw = cw_ref[...]
    c = cb_ref[...] + cw[0:1, :] * s2 + cw[1:2, :] * s1 + cw[2:3, :] * hg
    h_ref[...] = (c * _sigmoid(c) * hv).astype(h_ref.dtype)
    last = hg.reshape(bt, tt, tn)[:, tt - 2:, :]
    carry[j] = last
    cs_ref[0] = last


def _ffn_in(name, x3, wg, wv, conv0, cw, cb):
    B, T, K = x3.shape
    F = wg.shape[1]
    bt, tt = (1, 512) if T >= 512 else (B, T)
    tn = 512
    assert B % bt == 0 and T % tt == 0 and F % tn == 0 and tt >= CONV_W - 1
    n_j = F // tn
    kern = functools.partial(_ffn_in_kernel, bt=bt, tt=tt)
    n_t = T // tt
    h, cst = pl.pallas_call(
        kern, grid=(B // bt, n_t, n_j),
        in_specs=[pl.BlockSpec((bt, tt, K), lambda b, t, j: (b, t, 0)),
                  pl.BlockSpec((K, tn), lambda b, t, j: (0, j)),
                  pl.BlockSpec((K, tn), lambda b, t, j: (0, j)),
                  pl.BlockSpec((bt, CONV_W - 1, tn), lambda b, t, j: (b, 0, j)),
                  pl.BlockSpec((CONV_W, tn), lambda b, t, j: (0, j)),
                  pl.BlockSpec((1, tn), lambda b, t, j: (0, j))],
        out_specs=[pl.BlockSpec((bt * tt, tn), lambda b, t, j: (b * n_t + t, j)),
                   pl.BlockSpec((1, bt, CONV_W - 1, tn), lambda b, t, j: (t, b, 0, j))],
        out_shape=[jax.ShapeDtypeStruct((B * T, F), BF16),
                   jax.ShapeDtypeStruct((n_t, B, CONV_W - 1, F), F32)],
        scratch_shapes=[pltpu.VMEM((bt * tt, K), BF16), pltpu.VMEM((n_j, bt, CONV_W - 1, tn), F32)],
        compiler_params=_params(3), name=name,
    )(x3, wg, wv, conv0, cw, cb)
    return h, cst[n_t - 1]


def _wkv_chunk(r, lw, k, v, a, kkp, kap, rkp, lng, lnb, S, consts):
    C = WKV_CHUNK
    bdmask, l_incl, strict, incl, eye = consts

    def bd(xp):
        return jnp.concatenate([xp.astype(BF16)] * WKV_HEADS, axis=0) * bdmask

    def headsum(xp):
        hi, lo = _split(xp)
        return _dg(hi, bdmask, _NN) + _dg(lo, bdmask, _NN)

    kkn = k * kkp
    kp = k * (1.0 + (a - 1.0) * kap)
    sums = headsum(jnp.concatenate([kkn * kkn, r * kp * rkp], axis=0))
    kk = kkn / jnp.maximum(jnp.sqrt(sums[:C]), 1e-12)
    bonus = sums[C:] * v
    b = kk * a
    cum = _dot_exact_lhs(l_incl, lw)
    cum_last = cum[C - 1:C, :]
    e_neg = jnp.exp(-cum)
    x = jnp.concatenate([kk * jnp.exp(cum - lw), r * jnp.exp(cum)], axis=0).astype(BF16)
    g_b = _dg(x, bd(b * e_neg), _NT)
    g_k = _dg(x, bd(kp * e_neg), _NT)
    xs = _dg(x, S.astype(BF16), _NT)
    am = jnp.where(strict, g_b[:C], 0.0)
    mb = jnp.where(incl, g_b[C:], 0.0)
    akmk = jnp.concatenate([jnp.where(strict, g_k[:C], 0.0), jnp.where(incl, g_k[C:], 0.0)], axis=0)
    kv = _dg(akmk.astype(BF16), bd(v), _NN)
    rhs = -(xs[:C] + kv[:C])
    npow = -am
    tinv = eye + npow
    npow = _dg(npow.astype(BF16), bd(npow), _NN)
    for _ in range(int(math.log2(C)) - 2):
        both = _dg(jnp.concatenate([tinv, npow], axis=0).astype(BF16), bd(npow), _NN)
        tinv = tinv + both[:C]
        npow = both[C:]
    tinv = tinv + _dg(tinv.astype(BF16), bd(npow), _NN)
    u = _dg(tinv.astype(BF16), bd(rhs), _NN)
    y = xs[C:] + kv[C:] + _dg(mb.astype(BF16), bd(u), _NN)
    e_rest = jnp.exp(cum_last - cum)
    upd = _dg(jnp.concatenate([u, v], axis=0).astype(BF16),
              jnp.concatenate([b * e_rest, kp * e_rest], axis=0).astype(BF16), _TN)
    s_new = S * jnp.exp(cum_last) + upd * bdmask.astype(F32)
    mu = headsum(y) * (1.0 / RWKV_N)
    yc = y - mu
    var = headsum(yc * yc) * (1.0 / RWKV_N)
    yn = yc * lax.rsqrt(var + RWKV_GN_EPS) * lng + lnb
    return yn + bonus, s_new


def _wkv_kernel(r_ref, lw_ref, k_ref, v_ref, a_ref, kk_ref, ka_ref, rk_ref, g_ref, b_ref, s0_ref,
                z_ref, so_ref, s_scr, *, n_t, nb):
    t = pl.program_id(1)
    C, N, HW = WKV_CHUNK, RWKV_N, WKV_HEADS * RWKV_N
    ri = lax.broadcasted_iota(jnp.int32, (HW, HW), 0) // N
    ci = lax.broadcasted_iota(jnp.int32, (HW, HW), 1) // N
    bdmask_f = (ri == ci).astype(F32)

    @pl.when(t == 0)
    def _():
        for bi in range(nb):
            rows = s0_ref[bi].reshape(HW, N)
            s_scr[bi] = jnp.concatenate([rows] * WKV_HEADS, axis=1) * bdmask_f

    tt = lax.broadcasted_iota(jnp.int32, (C, C), 0)
    ss = lax.broadcasted_iota(jnp.int32, (C, C), 1)
    tp = lax.broadcasted_iota(jnp.int32, (C, HW), 0)
    sp = lax.broadcasted_iota(jnp.int32, (C, HW), 1) % N
    consts = (bdmask_f.astype(BF16), (tt >= ss).astype(BF16), tp > sp, tp >= sp, (tp == sp).astype(F32))
    for bi in range(nb):
        z, s_new = _wkv_chunk(r_ref[bi], lw_ref[bi], k_ref[bi], v_ref[bi], a_ref[bi], kk_ref[...], ka_ref[...],
                              rk_ref[...], g_ref[...], b_ref[...], s_scr[bi], consts)
        s_scr[bi] = s_new
        z_ref[bi] = z

    @pl.when(t == n_t - 1)
    def _():
        for bi in range(nb):
            for h in range(WKV_HEADS):
                so_ref[bi, h] = s_scr[bi, h * N:(h + 1) * N, h * N:(h + 1) * N]


def _wkv(name, r, lw, k, v, a, kkp, kap, rkp, lng, lnb, s0):
    B, T, D = r.shape
    C, HW = WKV_CHUNK, WKV_HEADS * RWKV_N
    assert T % C == 0 and D % HW == 0
    n_t = T // C
    tok = pl.BlockSpec((B, C, HW), lambda h, t: (0, t, h))
    vec = pl.BlockSpec((1, HW), lambda h, t: (0, h))
    st = pl.BlockSpec((B, WKV_HEADS, RWKV_N, RWKV_N), lambda h, t: (0, h, 0, 0))
    return pl.pallas_call(
        functools.partial(_wkv_kernel, n_t=n_t, nb=B), grid=(D // HW, n_t),
        in_specs=[tok] * 5 + [vec] * 5 + [st], out_specs=[tok, st],
        out_shape=[jax.ShapeDtypeStruct((B, T, D), F32), jax.ShapeDtypeStruct(s0.shape, F32)],
        scratch_shapes=[pltpu.VMEM((B, HW, HW), F32)], compiler_params=_params(2), name=name,
    )(r, lw, k, v, a, kkp, kap, rkp, lng, lnb, s0)


def _cmul(ar, ai, br, bi):
    return ar * br - ai * bi, ar * bi + ai * br


def _s5_kernel(x_ref, wb_ref, wc_ref, lr_ref, li_ref, dt_ref, d_ref, h0r_ref, h0i_ref,
               z_ref, hr_ref, hi_ref, pw_re, pw_im, cf, hc, xr_s, xi_s, *, n_t, tt):
    t = pl.program_id(2)
    R = SUBLANES

    @pl.when(t == 0)
    def _():
        lr, li, dt = lr_ref[0], li_ref[0], dt_ref[0]
        n = (lax.broadcasted_iota(jnp.int32, (R, S5_STATES), 0) + 1).astype(F32)
        mag = jnp.exp(lr * dt * n)
        ang = li * dt * n
        pw_re[...] = mag * jnp.cos(ang)
        pw_im[...] = mag * jnp.sin(ang)
        ar, ai = pw_re[0:1, :], pw_im[0:1, :]
        den = lr * lr + li * li
        cf[0:1, :] = ((ar - 1.0) * lr + ai * li) / den
        cf[1:2, :] = (ai * lr - (ar - 1.0) * li) / den
        hc[0:1, :] = h0r_ref[0]
        hc[1:2, :] = h0i_ref[0]

    x = x_ref[0]
    bu = _dot3(x, wb_ref[0])
    cr, ci = cf[0:1, :], cf[1:2, :]
    br, bi = _cmul(cr, ci, bu[:, :S5_STATES], bu[:, S5_STATES:])
    xr_s[...] = br
    xi_s[...] = bi
    rid = lax.broadcasted_iota(jnp.int32, (R, S5_STATES), 0)
    pr, pi = pw_re[...], pw_im[...]

    def body(i, carry):
        cr_, ci_ = carry
        o = pl.multiple_of(i * R, R)
        vr, vi = xr_s[pl.ds(o, R), :], xi_s[pl.ds(o, R), :]
        for sh in (1, 2, 4):
            mr, mi = pr[sh - 1:sh, :], pi[sh - 1:sh, :]
            sr, si = pltpu.roll(vr, sh, axis=0), pltpu.roll(vi, sh, axis=0)
            ur, ui = _cmul(mr, mi, sr, si)
            keep = rid >= sh
            vr, vi = vr + jnp.where(keep, ur, 0.0), vi + jnp.where(keep, ui, 0.0)
        ur, ui = _cmul(pr, pi, cr_, ci_)
        vr, vi = vr + ur, vi + ui
        xr_s[pl.ds(o, R), :] = vr
        xi_s[pl.ds(o, R), :] = vi
        return vr[R - 1:R, :], vi[R - 1:R, :]

    hr, hi = lax.fori_loop(0, tt // R, body, (hc[0:1, :], hc[1:2, :]))
    hc[0:1, :] = hr
    hc[1:2, :] = hi
    y = _dot3(jnp.concatenate([xr_s[...], xi_s[...]], axis=-1), wc_ref[0]) + d_ref[...] * x
    z_ref[0] = 0.5 * y * (1.0 + jnp.tanh(math.sqrt(2.0 / math.pi) * (y + 0.044715 * (y * y * y))))

    @pl.when(t == n_t - 1)
    def _():
        hr_ref[0] = hr
        hi_ref[0] = hi


def _s5(name, x3, wb, wc, lr, li, dt, d, h0r, h0i):
    B, T, D = x3.shape
    tt = min(256, T)
    assert T % tt == 0 and tt % SUBLANES == 0
    n_t = T // tt
    n_s = D // S5_SLAB
    S = S5_STATES
    st = pl.BlockSpec((1, 1, S), lambda b, s, t: (b, 0, s))
    pv = pl.BlockSpec((1, 1, S), lambda b, s, t: (0, 0, s))
    return pl.pallas_call(
        functools.partial(_s5_kernel, n_t=n_t, tt=tt), grid=(B, n_s, n_t),
        in_specs=[pl.BlockSpec((1, tt, S5_SLAB), lambda b, s, t: (b, t, s)),
                  pl.BlockSpec((1, S5_SLAB, 2 * S), lambda b, s, t: (s, 0, 0)),
                  pl.BlockSpec((1, 2 * S, S5_SLAB), lambda b, s, t: (s, 0, 0)),
                  pv, pv, pv,
                  pl.BlockSpec((1, S5_SLAB), lambda b, s, t: (0, s)),
                  st, st],
        out_specs=[pl.BlockSpec((1, tt, S5_SLAB), lambda b, s, t: (b, t, s)), st, st],
        out_shape=[jax.ShapeDtypeStruct((B, T, D), F32),
                   jax.ShapeDtypeStruct((B, 1, n_s * S), F32), jax.ShapeDtypeStruct((B, 1, n_s * S), F32)],
        scratch_shapes=[pltpu.VMEM((SUBLANES, S), F32), pltpu.VMEM((SUBLANES, S), F32),
                        pltpu.VMEM((SUBLANES, S), F32), pltpu.VMEM((SUBLANES, S), F32),
                        pltpu.VMEM((tt, S), F32), pltpu.VMEM((tt, S), F32)],
        compiler_params=_params(3), name=name,
    )(x3, wb, wc, lr, li, dt, d, h0r, h0i)


def _select_topk(gate, idx, valid, axis):
    g = jnp.where(valid, gate, NEG)
    big = jnp.int32(1 << 30)
    sel = jnp.zeros(gate.shape, jnp.bool_)
    for _ in range(MOBA_TOPK):
        m = jnp.max(g, axis=axis, keepdims=True)
        first = jnp.min(jnp.where(g == m, idx, big), axis=axis, keepdims=True)
        pick = idx == first
        sel = jnp.logical_or(sel, pick)
        g = jnp.where(pick, -jnp.inf, g)
    return jnp.logical_and(sel, valid)


def _moba_prompt_kernel(q_ref, k_ref, v_ref, sl_ref, o_ref, m_scr, l_scr, acc_scr, *, n_blk):
    i = pl.program_id(2)
    BK = MOBA_BLOCK
    q = q_ref[0]
    kmean = jnp.concatenate([jnp.sum(k_ref[0, n * BK:(n + 1) * BK, :], axis=0, keepdims=True)
                             for n in range(n_blk)], axis=0) * (1.0 / BK)
    gate = _dot3(q, kmean, _NT)
    nidx = lax.broadcasted_iota(jnp.int32, (BK, n_blk), 1)
    sel = _select_topk(gate, nidx, nidx < i, axis=1).astype(F32)
    slope = sl_ref[0][:, 0:1]
    qi = lax.broadcasted_iota(jnp.int32, (BK, BK), 0)
    ki = lax.broadcasted_iota(jnp.int32, (BK, BK), 1)
    qb = q.astype(BF16)
    scale = MOBA_DH ** -0.5
    o0 = pl.multiple_of(i * BK, BK)
    dist = qi - ki
    s = _dg(qb, k_ref[0, pl.ds(o0, BK), :].astype(BF16), _NT) * scale - slope * dist.astype(F32)
    s = jnp.where(dist >= 0, s, NEG)
    m = jnp.max(s, axis=-1, keepdims=True)
    p = jnp.exp(s - m)
    m_scr[...] = m
    l_scr[...] = jnp.sum(p, axis=-1, keepdims=True)
    acc_scr[...] = _dg(p.astype(BF16), v_ref[0, pl.ds(o0, BK), :].astype(BF16), _NN)
    for n in range(n_blk - 1):
        @pl.when(n < i)
        def _(n=n):
            dist = qi - ki + (i - n) * BK
            s = _dg(qb, k_ref[0, n * BK:(n + 1) * BK, :].astype(BF16), _NT) * scale - slope * dist.astype(F32)
            s = jnp.where(sel[:, n:n + 1] > 0.5, s, NEG)
            m_old = m_scr[...]
            m_new = jnp.maximum(m_old, jnp.max(s, axis=-1, keepdims=True))
            alpha = jnp.exp(m_old - m_new)
            p = jnp.exp(s - m_new)
            m_scr[...] = m_new
            l_scr[...] = alpha * l_scr[...] + jnp.sum(p, axis=-1, keepdims=True)
            acc_scr[...] = alpha * acc_scr[...] + _dg(p.astype(BF16), v_ref[0, n * BK:(n + 1) * BK, :].astype(BF16),
                                                      _NN)
    o_ref[0] = acc_scr[...] / l_scr[...]


def _moba_prompt(name, q, k, v, slopes):
    B, T, D = q.shape
    n_blk = T // MOBA_BLOCK
    assert T % MOBA_BLOCK == 0
    qs = pl.BlockSpec((1, MOBA_BLOCK, MOBA_DH), lambda b, h, i: (b, i, h))
    kv = pl.BlockSpec((1, T, MOBA_DH), lambda b, h, i: (b, 0, h))
    return pl.pallas_call(
        functools.partial(_moba_prompt_kernel, n_blk=n_blk), grid=(B, MOBA_H, n_blk),
        in_specs=[qs, kv, kv, pl.BlockSpec((1, 1, LANES), lambda b, h, i: (h, 0, 0))], out_specs=qs,
        out_shape=jax.ShapeDtypeStruct((B, T, D), F32),
        scratch_shapes=[pltpu.VMEM((MOBA_BLOCK, 1), F32), pltpu.VMEM((MOBA_BLOCK, 1), F32),
                        pltpu.VMEM((MOBA_BLOCK, MOBA_DH), F32)],
        compiler_params=_params(3), name=name,
    )(q, k, v, slopes)


def _q_blockdiag(q, nq):
    rows = MOBA_H * nq
    qt = jnp.concatenate([q] * MOBA_H, axis=0)
    rh = lax.broadcasted_iota(jnp.int32, (rows, D_MODEL), 0) // nq
    ch = lax.broadcasted_iota(jnp.int32, (rows, D_MODEL), 1) // MOBA_DH
    return jnp.where(rh == ch, qt, 0.0)


def _moba_scores_kernel(pt_ref, q_ref, k_ref, s_ref, ks_ref, qbd, *, nq):
    @pl.when(pl.program_id(1) == 0)
    def _():
        qbd[...] = _q_blockdiag(q_ref[0], nq).astype(BF16)

    kp = k_ref[0]
    s_ref[0, 0] = _dg(kp.astype(BF16), qbd[...], _NT)
    ks_ref[0, 0] = jnp.sum(kp, axis=0, keepdims=True)


def _moba_scores(name, page_table, q, cache_k):
    B, nq, D = q.shape
    n_pg = page_table.shape[1]
    gs = pltpu.PrefetchScalarGridSpec(
        num_scalar_prefetch=1, grid=(B, n_pg),
        in_specs=[pl.BlockSpec((1, nq, D), lambda b, p, pt: (b, 0, 0)),
                  pl.BlockSpec((1, PAGE_SIZE, D), lambda b, p, pt: (pt[b, p], 0, 0))],
        out_specs=[pl.BlockSpec((1, 1, PAGE_SIZE, MOBA_H * nq), lambda b, p, pt: (b, p, 0, 0)),
                   pl.BlockSpec((1, 1, 1, D), lambda b, p, pt: (b, p, 0, 0))],
        scratch_shapes=[pltpu.VMEM((MOBA_H * nq, D), BF16)])
    return pl.pallas_call(
        functools.partial(_moba_scores_kernel, nq=nq), grid_spec=gs,
        out_shape=[jax.ShapeDtypeStruct((B, n_pg, PAGE_SIZE, MOBA_H * nq), F32),
                   jax.ShapeDtypeStruct((B, n_pg, 1, D), F32)],
        compiler_params=_params(2), name=name,
    )(page_table, q, cache_k)


def _moba_probs_kernel(s_ref, ks_ref, q_ref, kn_ref, sl_ref, p_ref, pn_ref, sel_scr, *, nq, n_blk, past):
    BK = MOBA_BLOCK
    HQ = MOBA_H * nq
    qbd = _q_blockdiag(q_ref[0], nq)
    ksum = ks_ref[0]
    ppb = BK // PAGE_SIZE
    kmean = jnp.sum(ksum.reshape(n_blk, ppb, D_MODEL), axis=1) * (1.0 / BK)
    gate = _dot3(kmean, qbd, _NT)
    nidx = lax.broadcasted_iota(jnp.int32, (n_blk, HQ), 0)
    own = past // BK
    sel = _select_topk(gate, nidx, nidx < own, axis=0)
    sel_scr[...] = sel.astype(F32)
    slope = sl_ref[...]
    scale = MOBA_DH ** -0.5
    qpos = past + lax.broadcasted_iota(jnp.int32, (1, HQ), 1) % nq
    tok = lax.broadcasted_iota(jnp.int32, (BK, HQ), 0)

    def scores(n):
        s = s_ref[0, n] * scale - slope * (qpos - (n * BK + tok)).astype(F32)
        return jnp.where(sel_scr[pl.ds(n, 1), :] > 0.5, s, NEG)

    sn = _dg(kn_ref[0].astype(BF16), qbd.astype(BF16), _NT) * scale
    tn = lax.broadcasted_iota(jnp.int32, (PAGE_SIZE, HQ), 0)
    dn = qpos - (past + tn)
    sn = jnp.where(jnp.logical_and(dn >= 0, tn < nq), sn - slope * dn.astype(F32), NEG)
    m = lax.fori_loop(0, n_blk, lambda n, m_: jnp.maximum(m_, jnp.max(scores(n), axis=0, keepdims=True)),
                      jnp.max(sn, axis=0, keepdims=True))
    pn = jnp.exp(sn - m)
    l = lax.fori_loop(0, n_blk, lambda n, l_: l_ + jnp.sum(jnp.exp(scores(n) - m), axis=0, keepdims=True),
                      jnp.sum(pn, axis=0, keepdims=True))
    pn_ref[0] = (pn / l).astype(pn_ref.dtype)

    def write(n, c):
        p_ref[0, n] = (jnp.exp(scores(n) - m) / l).astype(p_ref.dtype)
        return c

    lax.fori_loop(0, n_blk, write, 0)


def _moba_probs(name, s, ksum, q, k_new_pad, slopes_hq, past):
    B, n_blk, BK, HQ = s.shape
    nq = q.shape[1]
    blk = pl.BlockSpec((1, n_blk, BK, HQ), lambda b: (b, 0, 0, 0))
    return pl.pallas_call(
        functools.partial(_moba_probs_kernel, nq=nq, n_blk=n_blk, past=past), grid=(B,),
        in_specs=[blk,
                  pl.BlockSpec((1, ksum.shape[1], D_MODEL), lambda b: (b, 0, 0)),
                  pl.BlockSpec((1, nq, D_MODEL), lambda b: (b, 0, 0)),
                  pl.BlockSpec((1, PAGE_SIZE, D_MODEL), lambda b: (b, 0, 0)),
                  pl.BlockSpec((1, HQ), lambda b: (0, 0))],
        out_specs=[blk, pl.BlockSpec((1, PAGE_SIZE, HQ), lambda b: (b, 0, 0))],
        out_shape=[jax.ShapeDtypeStruct((B, n_blk, BK, HQ), BF16), jax.ShapeDtypeStruct((B, PAGE_SIZE, HQ), BF16)],
        scratch_shapes=[pltpu.VMEM((n_blk, HQ), F32)],
        compiler_params=_params(1), name=name,
    )(s, ksum, q, k_new_pad, slopes_hq)


def _moba_pv_kernel(pt_ref, p_ref, v_ref, pn_ref, vn_ref, o_ref, acc, *, nq, n_pg):
    pg = pl.program_id(1)

    @pl.when(pg == 0)
    def _():
        acc[...] = _dg(pn_ref[0], vn_ref[0].astype(BF16), _TN)

    acc[...] += _dg(p_ref[0, 0], v_ref[0].astype(BF16), _TN)

    @pl.when(pg == n_pg - 1)
    def _():
        a = acc[...]
        o_ref[0] = jnp.concatenate(
            [a[h * nq:(h + 1) * nq, h * MOBA_DH:(h + 1) * MOBA_DH] for h in range(MOBA_H)], axis=-1)


def _moba_pv(name, page_table, p, cache_v, pn, v_new_pad, nq):
    B, n_pg, _, HQ = p.shape
    D = D_MODEL
    gs = pltpu.PrefetchScalarGridSpec(
        num_scalar_prefetch=1, grid=(B, n_pg),
        in_specs=[pl.BlockSpec((1, 1, PAGE_SIZE, HQ), lambda b, g, pt: (b, g, 0, 0)),
                  pl.BlockSpec((1, PAGE_SIZE, D), lambda b, g, pt: (pt[b, g], 0, 0)),
                  pl.BlockSpec((1, PAGE_SIZE, HQ), lambda b, g, pt: (b, 0, 0)),
                  pl.BlockSpec((1, PAGE_SIZE, D), lambda b, g, pt: (b, 0, 0))],
        out_specs=pl.BlockSpec((1, nq, D), lambda b, g, pt: (b, 0, 0)),
        scratch_shapes=[pltpu.VMEM((HQ, D), F32)])
    return pl.pallas_call(
        functools.partial(_moba_pv_kernel, nq=nq, n_pg=n_pg), grid_spec=gs,
        out_shape=jax.ShapeDtypeStruct((B, nq, D), F32), compiler_params=_params(2), name=name,
    )(page_table, p, cache_v, pn, v_new_pad)


def _ident(x):
    return x


def _mix(x, xp, mu):
    return x + (xp - x) * mu


def _pad_cols(w, n):
    return jnp.pad(w, ((0, 0), (0, n - w.shape[1])))


def _pad_rows(w, n):
    return jnp.pad(w, ((0, n - w.shape[0]), (0, 0)))


def _rwkv_layer(tag, x3, shift0, s0, v_first, P, j, ln_g, ln_b):
    B, T, D = x3.shape
    M = B * T
    x = x3.reshape(M, D)
    xp = jnp.concatenate([shift0[:, None, :], x3[:, :-1]], axis=1).reshape(M, D)
    mu = P['rwkv_mu'][j]
    mus = [mu[n][None, :] for n in range(6)]
    wb = lambda w: w.astype(BF16)
    one = lambda acc: (acc,)
    rows = [x, xp]
    r, = _mm(tag + 'r', rows, [mus[0]], [wb(P['rwkv_w_rkv'][j, 0])], [], [], _mix, one, [F32])
    k, = _mm(tag + 'k', rows, [mus[2]], [wb(P['rwkv_w_rkv'][j, 1])], [], [], _mix, one, [F32])
    v, = _mm(tag + 'v', rows, [mus[3]], [wb(P['rwkv_w_rkv'][j, 2])], [], [], _mix, one, [F32])
    lp = LANES
    hw, = _mm(tag + 'w1', rows, [mus[1]], [wb(_pad_cols(P['rwkv_w1'][j], lp))], [], [], _mix,
              lambda acc: (jnp.tanh(acc),), [F32])
    lw, = _mm(tag + 'w2', [hw], [], [wb(_pad_rows(P['rwkv_w2'][j], lp))], [], [P['rwkv_w0'][j][None, :]], _ident,
              lambda acc, w0: (-jnp.exp(-_softplus(-(w0 + acc)) - 0.5),), [F32])
    ha, = _mm(tag + 'a1', rows, [mus[4]], [wb(_pad_cols(P['rwkv_a1'][j], lp))], [], [], _mix, one, [F32])
    a, = _mm(tag + 'a2', [ha], [], [wb(_pad_rows(P['rwkv_a2'][j], lp))], [], [P['rwkv_a0'][j][None, :]], _ident,
             lambda acc, a0: (_sigmoid(a0 + acc),), [F32])
    hg, = _mm(tag + 'g1', rows, [mus[5]], [wb(P['rwkv_g1'][j])], [], [], _mix,
              lambda acc: (_sigmoid(acc),), [F32])
    g, = _mm(tag + 'g2', [hg], [], [wb(P['rwkv_g2'][j])], [], [], _ident, one, [F32])
    if j == 0:
        v_first = v
    else:
        hv, = _mm(tag + 'v1', rows, [mus[3]], [wb(_pad_cols(P['rwkv_v1'][j - 1], lp))], [], [], _mix, one, [F32])
        v, = _mm(tag + 'v2', [hv], [], [wb(_pad_rows(P['rwkv_v2'][j - 1], lp))], [v, v_first],
                 [P['rwkv_v0'][j - 1][None, :]], _ident,
                 lambda acc, v_, vf, v0: (v_ + (vf - v_) * _sigmoid(v0 + acc),), [F32])
    Tp = -(-T // WKV_CHUNK) * WKV_CHUNK
    seq = lambda t: jnp.pad(t.reshape(B, T, D), ((0, 0), (0, Tp - T), (0, 0)))
    vecp = lambda name: P[name][j].reshape(1, D)
    z, s_new = _wkv(tag + 'wkv', seq(r), seq(lw), seq(k), seq(v), seq(a), vecp('rwkv_k_k'), vecp('rwkv_k_a'),
                    vecp('rwkv_r_k'), vecp('rwkv_lnx_g'), vecp('rwkv_lnx_b'), s0)
    z = z[:, :T].reshape(M, D)
    xn = _mm_ln(tag + 'wo', [z, g], wb(P['rwkv_wo'][j]), x, ln_g, ln_b, lambda z_, g_: z_ * g_)
    return xn, s_new, x3[:, -1], v_first


def _s5_weights(P, j):
    ns, gs = D_MODEL // S5_SLAB, S5_SLAB // S5_GROUP
    eye = jnp.eye(gs, dtype=F32)

    def bd_in(b):
        bt = jnp.swapaxes(b.reshape(ns, gs, S5_P, S5_GROUP), 2, 3)
        return jnp.einsum('sgcp,gh->sgchp', bt, eye).reshape(ns, S5_SLAB, S5_STATES)

    def bd_out(c):
        ct = jnp.swapaxes(c.reshape(ns, gs, S5_GROUP, S5_P), 2, 3)
        return jnp.einsum('sgpc,gh->sgphc', ct, eye).reshape(ns, S5_STATES, S5_SLAB)

    wb = jnp.concatenate([bd_in(P['s5_b_re'][j]), bd_in(P['s5_b_im'][j])], axis=2)
    wc = jnp.concatenate([bd_out(P['s5_c_re'][j]), -bd_out(P['s5_c_im'][j])], axis=1)
    flat = lambda t: t.reshape(1, 1, S5_G * S5_P)
    dt = jnp.broadcast_to(jnp.exp(P['s5_log_dt'][j])[:, None], (S5_G, S5_P))
    return wb, wc, flat(P['s5_lam_re'][j]), flat(P['s5_lam_im'][j]), flat(dt)


def _s5_layer(tag, x3, h0r, h0i, P, j, ln_g, ln_b):
    B, T, D = x3.shape
    M = B * T
    x = x3.reshape(M, D)
    wb, wc, lr, li, dt = _s5_weights(P, j)
    st = lambda h: h.reshape(B, 1, S5_G * S5_P)
    z, hr, hi = _s5(tag + 'scan', x3, wb, wc, lr, li, dt, P['s5_d'][j][None, :], st(h0r), st(h0i))
    out, = _mm(tag + 'gate', [z.reshape(M, D)], [], [P['s5_w_val'][j].astype(BF16), P['s5_w_gate'][j].astype(BF16)],
               [], [], _ident, lambda val, gate: (val * _sigmoid(gate),), [F32])
    xn = _ln(tag + 'ln', x, out, ln_g, ln_b)
    return xn, hr.reshape(B, S5_G, S5_P), hi.reshape(B, S5_G, S5_P)


def _alibi_slopes():
    return 2.0 ** (-8.0 * jnp.arange(1, MOBA_H + 1, dtype=F32) / MOBA_H)


def _moba_layer(tag, x3, cache_k, cache_v, page_table, P, j, ln_g, ln_b):
    B, T, D = x3.shape
    M = B * T
    x = x3.reshape(M, D)
    wqkv = P['moba_w_qkv'][j]
    ws = [wqkv[:, n * D:(n + 1) * D].astype(BF16) for n in range(3)]
    q, k, v = _mm(tag + 'qkv', [x], [], ws, [], [], _ident, lambda a, b, c: (a, b, c), [F32, F32, F32])
    q3, k3, v3 = (t.reshape(B, T, D) for t in (q, k, v))
    slopes = _alibi_slopes()
    if cache_k is None:
        sl = jnp.broadcast_to(slopes[:, None, None], (MOBA_H, 1, LANES))
        o = _moba_prompt(tag + 'attn', q3, k3, v3, sl)
    else:
        n_layers, n_pool = cache_k.shape[:2]
        ck = cache_k.reshape(n_layers * n_pool, PAGE_SIZE, D)
        cv = cache_v.reshape(n_layers * n_pool, PAGE_SIZE, D)
        page_table = page_table + j * n_pool
        n_pg = page_table.shape[1]
        past = n_pg * PAGE_SIZE
        assert past % MOBA_BLOCK == 0 and T <= PAGE_SIZE and T <= MOBA_BLOCK
        HQ = MOBA_H * T
        s, ksum = _moba_scores(tag + 'scores', page_table, q3, ck)
        sl = jnp.repeat(slopes, T)[None, :]
        pad = lambda t: jnp.pad(t, ((0, 0), (0, PAGE_SIZE - T), (0, 0)))
        p, pn = _moba_probs(tag + 'probs', s.reshape(B, past // MOBA_BLOCK, MOBA_BLOCK, HQ),
                            ksum.reshape(B, n_pg, D), q3, pad(k3), sl, past)
        o = _moba_pv(tag + 'pv', page_table, p.reshape(B, n_pg, PAGE_SIZE, HQ), cv, pn, pad(v3), T)
    xn = _mm_ln(tag + 'wo', [o.reshape(M, D)], P['moba_wo'][j].astype(BF16), x, ln_g, ln_b, _ident)
    return xn, k3.reshape(B, T, MOBA_H, MOBA_DH), v3.reshape(B, T, MOBA_H, MOBA_DH)


def _ffn_layer(tag, x3, conv0, P, i, ln_g, ln_b):
    B, T, D = x3.shape
    w_in = P['ffn_w_in'][i]
    h, cst = _ffn_in(tag + 'in', x3, w_in[:, :D_FF].astype(BF16), w_in[:, D_FF:].astype(BF16), conv0,
                     P['ffn_conv_w'][i], P['ffn_conv_b'][i][None, :])
    xn = _mm_ln(tag + 'down', [h], P['ffn_w_down'][i].astype(BF16), x3.reshape(B * T, D), ln_g, ln_b, _ident)
    return xn.reshape(B, T, D), cst


def _trunk(grp, x3, wkv0, shift0, s5re0, s5im0, conv0, cache_k, cache_v, page_table, P):
    B, T, D = x3.shape
    wkv_new, shift_new, s5re_new, s5im_new, k_new, v_new, conv_new = [], [], [], [], [], [], []
    v_first = None
    for i in range(DEPTH):
        kind, j = i % 3, i // 3
        tag = f'{grp}{i}_'
        g0, b0 = P['ln_g'][i, 0][None, :], P['ln_b'][i, 0][None, :]
        g1, b1 = P['ln_g'][i, 1][None, :], P['ln_b'][i, 1][None, :]
        if kind == 0:
            xn, S, last, v_first = _rwkv_layer(tag, x3, shift0[j], wkv0[j], v_first, P, j, g0, b0)
            wkv_new.append(S)
            shift_new.append(last)
        elif kind == 1:
            xn, hr, hi = _s5_layer(tag, x3, s5re0[j], s5im0[j], P, j, g0, b0)
            s5re_new.append(hr)
            s5im_new.append(hi)
        else:
            xn, kr, vr = _moba_layer(tag, x3, cache_k, cache_v, page_table, P, j, g0, b0)
            k_new.append(kr)
            v_new.append(vr)
        x3, cst = _ffn_layer(tag + 'ffn_', xn.reshape(B, T, D), conv0[i], P, i, g1, b1)
        conv_new.append(cst)
    return (x3, jnp.stack(wkv_new), jnp.stack(shift_new), jnp.stack(s5re_new), jnp.stack(s5im_new),
            jnp.stack(conv_new), jnp.stack(k_new), jnp.stack(v_new))


def kernel(x_prompt, x_sample, state_rwkv_wkv, state_rwkv_shift, state_s5_re, state_s5_im, state_ffn_conv, cache_k, cache_v, page_table, ln_g, ln_b, rwkv_mu, rwkv_w_rkv, rwkv_w0, rwkv_w1, rwkv_w2, rwkv_a0, rwkv_a1, rwkv_a2, rwkv_v0, rwkv_v1, rwkv_v2, rwkv_g1, rwkv_g2, rwkv_k_k, rwkv_k_a, rwkv_r_k, rwkv_lnx_g, rwkv_lnx_b, rwkv_wo, s5_log_dt, s5_lam_re, s5_lam_im, s5_b_re, s5_b_im, s5_c_re, s5_c_im, s5_d, s5_w_val, s5_w_gate, moba_w_qkv, moba_wo, ffn_w_in, ffn_conv_w, ffn_conv_b, ffn_w_down):
    P = dict(ln_g=ln_g, ln_b=ln_b, rwkv_mu=rwkv_mu, rwkv_w_rkv=rwkv_w_rkv, rwkv_w0=rwkv_w0, rwkv_w1=rwkv_w1,
             rwkv_w2=rwkv_w2, rwkv_a0=rwkv_a0, rwkv_a1=rwkv_a1, rwkv_a2=rwkv_a2, rwkv_v0=rwkv_v0,
             rwkv_v1=rwkv_v1, rwkv_v2=rwkv_v2, rwkv_g1=rwkv_g1, rwkv_g2=rwkv_g2, rwkv_k_k=rwkv_k_k,
             rwkv_k_a=rwkv_k_a, rwkv_r_k=rwkv_r_k.reshape(rwkv_r_k.shape[0], D_MODEL), rwkv_lnx_g=rwkv_lnx_g,
             rwkv_lnx_b=rwkv_lnx_b, rwkv_wo=rwkv_wo, s5_log_dt=s5_log_dt, s5_lam_re=s5_lam_re,
             s5_lam_im=s5_lam_im, s5_b_re=s5_b_re, s5_b_im=s5_b_im, s5_c_re=s5_c_re, s5_c_im=s5_c_im, s5_d=s5_d,
             s5_w_val=s5_w_val, s5_w_gate=s5_w_gate, moba_w_qkv=moba_w_qkv, moba_wo=moba_wo,
             ffn_w_in=ffn_w_in, ffn_conv_w=ffn_conv_w, ffn_conv_b=ffn_conv_b, ffn_w_down=ffn_w_down)
    B = x_prompt.shape[0]
    n_rwkv, n_s5 = state_rwkv_wkv.shape[0], state_s5_re.shape[0]
    zeros = lambda *s: jnp.zeros(s, F32)
    outs_p = _trunk('p', x_prompt, zeros(n_rwkv, B, RWKV_H, RWKV_N, RWKV_N), zeros(n_rwkv, B, D_MODEL),
                    zeros(n_s5, B, S5_G, S5_P), zeros(n_s5, B, S5_G, S5_P), zeros(DEPTH, B, CONV_W - 1, D_FF),
                    None, None, None, P)
    outs_s = _trunk('s', x_sample, state_rwkv_wkv, state_rwkv_shift, state_s5_re, state_s5_im, state_ffn_conv,
                    cache_k, cache_v, page_table, P)
    return (outs_p[0], outs_s[0]) + tuple(outs_p[1:]) + tuple(outs_s[1:])
```

```python
import functools
import math

import jax
import jax.numpy as jnp
from jax import lax
from jax.experimental import pallas as pl
from jax.experimental.pallas import tpu as pltpu

F32 = jnp.float32
BF16 = jnp.bfloat16

D_MODEL = 2048
DEPTH = 4
RWKV_N = 64
RWKV_H = D_MODEL // RWKV_N
RWKV_GN_EPS = 64e-5
S5_GROUP = 16
S5_G = D_MODEL // S5_GROUP
S5_P = 64
MOBA_H = 16
MOBA_DH = D_MODEL // MOBA_H
MOBA_BLOCK = 256
MOBA_TOPK = 3
PAGE_SIZE = 128
D_FF = 11 * D_MODEL // 4
CONV_W = 3
LN_EPS = 1e-5
DN_ALPHA = (2.0 * DEPTH) ** 0.25
NEG = -1e30

LANES = 128
SUBLANES = 8
VMEM_LIMIT_BYTES = 52 * 1024 * 1024

MM_LN_WEIGHT_TILE_BYTES = 12 * 1024 * 1024
WKV_CHUNK = 64
WKV_HEADS = 4
S5_SLAB = LANES
S5_STATES = (S5_SLAB // S5_GROUP) * S5_P


def _params(n_axes):
    return pltpu.CompilerParams(dimension_semantics=("arbitrary",) * n_axes,
                                vmem_limit_bytes=VMEM_LIMIT_BYTES)


def _sigmoid(x):
    return 1.0 / (1.0 + jnp.exp(-x))


def _softplus(x):
    return jnp.maximum(x, 0.0) + jnp.log1p(jnp.exp(-jnp.abs(x)))


def _dg(a, b, dims):
    return lax.dot_general(a, b, (dims, ((), ())), preferred_element_type=F32)


_NN = ((1,), (0,))
_NT = ((1,), (1,))
_TN = ((0,), (0,))


def _split(a):
    hi = a.astype(BF16)
    lo = (a - hi.astype(F32)).astype(BF16)
    return hi, lo


def _dot3(a, b, dims=_NN):
    ah, al = _split(a)
    bh, bl = _split(b)
    return _dg(ah, bh, dims) + (_dg(al, bh, dims) + _dg(ah, bl, dims))


def _dot1(a, b, dims=_NN):
    return _dg(a.astype(BF16), b.astype(BF16), dims)


def _dot_exact_lhs(lhs_bf16, x):
    x1 = x.astype(BF16)
    r1 = x - x1.astype(F32)
    x2 = r1.astype(BF16)
    x3 = (r1 - x2.astype(F32)).astype(BF16)
    return _dg(lhs_bf16, x1, _NN) + (_dg(lhs_bf16, x2, _NN) + _dg(lhs_bf16, x3, _NN))


def _layer_norm(y, g, b):
    mu = jnp.mean(y, axis=-1, keepdims=True)
    yc = y - mu
    var = jnp.mean(yc * yc, axis=-1, keepdims=True)
    return yc * lax.rsqrt(var + LN_EPS) * g + b


def _mm_kernel(*refs, n_row, n_vec, n_w, n_erow, n_evec, n_out, prologue, epilogue):
    it = iter(refs)
    rows = [next(it) for _ in range(n_row)]
    vecs = [next(it) for _ in range(n_vec)]
    ws = [next(it) for _ in range(n_w)]
    erows = [next(it) for _ in range(n_erow)]
    evecs = [next(it) for _ in range(n_evec)]
    outs = [next(it) for _ in range(n_out)]
    a_scr = next(it)

    @pl.when(pl.program_id(1) == 0)
    def _():
        a_scr[...] = prologue(*[r[...] for r in rows], *[v[...] for v in vecs]).astype(BF16)

    a = a_scr[...]
    accs = [jnp.dot(a, w[...], preferred_element_type=F32) for w in ws]
    res = epilogue(*accs, *[e[...] for e in erows], *[e[...] for e in evecs])
    for o_ref, o in zip(outs, res):
        o_ref[...] = o.astype(o_ref.dtype)


def _mm(name, rows, vecs, ws, erows, evecs, prologue, epilogue, out_dtypes):
    M, K = rows[0].shape
    N = ws[0].shape[1]
    tm = min(512, M)
    tn = 512 if N % 512 == 0 else N
    assert M % tm == 0 and N % tn == 0
    in_specs = ([pl.BlockSpec((tm, K), lambda i, j: (i, 0)) for _ in rows]
                + [pl.BlockSpec((1, K), lambda i, j: (0, 0)) for _ in vecs]
                + [pl.BlockSpec((K, tn), lambda i, j: (0, j)) for _ in ws]
                + [pl.BlockSpec((tm, tn), lambda i, j: (i, j)) for _ in erows]
                + [pl.BlockSpec((1, tn), lambda i, j: (0, j)) for _ in evecs])
    out_specs = [pl.BlockSpec((tm, tn), lambda i, j: (i, j)) for _ in out_dtypes]
    out_shape = [jax.ShapeDtypeStruct((M, N), dt) for dt in out_dtypes]
    kern = functools.partial(_mm_kernel, n_row=len(rows), n_vec=len(vecs), n_w=len(ws), n_erow=len(erows),
                             n_evec=len(evecs), n_out=len(out_dtypes), prologue=prologue, epilogue=epilogue)
    return pl.pallas_call(
        kern, grid=(M // tm, N // tn), in_specs=in_specs, out_specs=out_specs, out_shape=out_shape,
        scratch_shapes=[pltpu.VMEM((tm, K), BF16)], compiler_params=_params(2), name=name,
    )(*rows, *vecs, *ws, *erows, *evecs)


def _mm_ln_kernel(*refs, n_row, prologue, n_k):
    it = iter(refs)
    rows = [next(it) for _ in range(n_row)]
    w_ref, x_ref, g_ref, b_ref, o_ref, acc = (next(it) for _ in range(6))
    k = pl.program_id(1)
    a = prologue(*[r[...] for r in rows]).astype(BF16)
    p = jnp.dot(a, w_ref[...], preferred_element_type=F32)
    if n_k == 1:
        o_ref[...] = _layer_norm(DN_ALPHA * x_ref[...] + p, g_ref[...], b_ref[...])
        return

    @pl.when(k == 0)
    def _():
        acc[...] = p

    @pl.when(jnp.logical_and(k > 0, k < n_k - 1))
    def _():
        acc[...] += p

    @pl.when(k == n_k - 1)
    def _():
        o_ref[...] = _layer_norm(DN_ALPHA * x_ref[...] + (acc[...] + p), g_ref[...], b_ref[...])


def _mm_ln(name, rows, w, x, g, b, prologue):
    M, K = rows[0].shape
    N = w.shape[1]
    tm = min(256, M)
    n_k = -(-K * N * 2 // MM_LN_WEIGHT_TILE_BYTES)
    tk = K // n_k
    assert M % tm == 0 and K % n_k == 0 and tk % LANES == 0
    in_specs = ([pl.BlockSpec((tm, tk), lambda i, k: (i, k)) for _ in rows]
                + [pl.BlockSpec((tk, N), lambda i, k: (k, 0)),
                   pl.BlockSpec((tm, N), lambda i, k: (i, 0)),
                   pl.BlockSpec((1, N), lambda i, k: (0, 0)),
                   pl.BlockSpec((1, N), lambda i, k: (0, 0))])
    kern = functools.partial(_mm_ln_kernel, n_row=len(rows), prologue=prologue, n_k=n_k)
    return pl.pallas_call(
        kern, grid=(M // tm, n_k), in_specs=in_specs,
        out_specs=pl.BlockSpec((tm, N), lambda i, k: (i, 0)),
        out_shape=jax.ShapeDtypeStruct((M, N), F32),
        scratch_shapes=[pltpu.VMEM((tm, N), F32)], compiler_params=_params(2), name=name,
    )(*rows, w, x, g, b)


def _ln_kernel(x_ref, h_ref, g_ref, b_ref, o_ref):
    o_ref[...] = _layer_norm(DN_ALPHA * x_ref[...] + h_ref[...], g_ref[...], b_ref[...])


def _ln(name, x, h, g, b):
    M, N = x.shape
    tm = min(512, M)
    row = pl.BlockSpec((tm, N), lambda i: (i, 0))
    vec = pl.BlockSpec((1, N), lambda i: (0, 0))
    return pl.pallas_call(_ln_kernel, grid=(M // tm,), in_specs=[row, row, vec, vec], out_specs=row,
                          out_shape=jax.ShapeDtypeStruct((M, N), F32), compiler_params=_params(1), name=name)(x, h, g, b)


def _ffn_in_kernel(x_ref, wg_ref, wv_ref, c0_ref, cw_ref, cb_ref, h_ref, cs_ref, a_scr, carry, *, bt, tt):
    t = pl.program_id(1)
    j = pl.program_id(2)
    rows = bt * tt
    tn = wg_ref.shape[1]

    @pl.when(j == 0)
    def _():
        a_scr[...] = x_ref[...].reshape(rows, x_ref.shape[2]).astype(BF16)

    @pl.when(t == 0)
    def _():
        carry[j] = c0_ref[...]

    a = a_scr[...]
    hg = jnp.dot(a, wg_ref[...], preferred_element_type=F32)
    hv = jnp.dot(a, wv_ref[...], preferred_element_type=F32)
    prev = carry[j]
    p2 = jnp.broadcast_to(prev[:, 0:1, :], (bt, tt, tn)).reshape(rows, tn)
    p1 = jnp.broadcast_to(prev[:, 1:2, :], (bt, tt, tn)).reshape(rows, tn)
    r = lax.broadcasted_iota(jnp.int32, (rows, tn), 0) % tt
    s1 = jnp.where(r == 0, p1, pltpu.roll(hg, 1, axis=0))
    s2 = jnp.where(r == 0, p2, jnp.where(r == 1, p1, pltpu.roll(hg, 2, axis=0)))
    cw = cw_ref[...]
    c = cb_ref[...] + cw[0:1, :] * s2 + cw[1:2, :] * s1 + cw[2:3, :] * hg
    h_ref[...] = (c * _sigmoid(c) * hv).astype(h_ref.dtype)
    last = hg.reshape(bt, tt, tn)[:, tt - 2:, :]
    carry[j] = last
    cs_ref[0] = last


def _ffn_in(name, x3, wg, wv, conv0, cw, cb):
    B, T, K = x3.shape
    F = wg.shape[1]
    bt, tt = (1, 512) if T >= 512 else (B, T)
    tn = 512
    assert B % bt == 0 and T % tt == 0 and F % tn == 0 and tt >= CONV_W - 1
    n_j = F // tn
    kern = functools.partial(_ffn_in_kernel, bt=bt, tt=tt)
    n_t = T // tt
    h, cst = pl.pallas_call(
        kern, grid=(B // bt, n_t, n_j),
        in_specs=[pl.BlockSpec((bt, tt, K), lambda b, t, j: (b, t, 0)),
                  pl.BlockSpec((K, tn), lambda b, t, j: (0, j)),
                  pl.BlockSpec((K, tn), lambda b, t, j: (0, j)),
                  pl.BlockSpec((bt, CONV_W - 1, tn), lambda b, t, j: (b, 0, j)),
                  pl.BlockSpec((CONV_W, tn), lambda b, t, j: (0, j)),
                  pl.BlockSpec((1, tn), lambda b, t, j: (0, j))],
        out_specs=[pl.BlockSpec((bt * tt, tn), lambda b, t, j: (b * n_t + t, j)),
                   pl.BlockSpec((1, bt, CONV_W - 1, tn), lambda b, t, j: (t, b, 0, j))],
        out_shape=[jax.ShapeDtypeStruct((B * T, F), BF16),
                   jax.ShapeDtypeStruct((n_t, B, CONV_W - 1, F), F32)],
        scratch_shapes=[pltpu.VMEM((bt * tt, K), BF16), pltpu.VMEM((n_j, bt, CONV_W - 1, tn), F32)],
        compiler_params=_params(3), name=name,
    )(x3, wg, wv, conv0, cw, cb)
    return h, cst[n_t - 1]


def _wkv_chunk(r, lw, k, v, a, kkp, kap, rkp, lng, lnb, S, consts):
    C = WKV_CHUNK
    bdmask, l_incl, strict, incl, eye = consts

    def bd(xp):
        return jnp.concatenate([xp.astype(BF16)] * WKV_HEADS, axis=0) * bdmask

    def headsum(xp):
        hi, lo = _split(xp)
        return _dg(hi, bdmask, _NN) + _dg(lo, bdmask, _NN)

    kkn = k * kkp
    kp = k * (1.0 + (a - 1.0) * kap)
    sums = headsum(jnp.concatenate([kkn * kkn, r * kp * rkp], axis=0))
    kk = kkn / jnp.maximum(jnp.sqrt(sums[:C]), 1e-12)
    bonus = sums[C:] * v
    b = kk * a
    cum = _dot_exact_lhs(l_incl, lw)
    cum_last = cum[C - 1:C, :]
    e_neg = jnp.exp(-cum)
    x = jnp.concatenate([kk * jnp.exp(cum - lw), r * jnp.exp(cum)], axis=0).astype(BF16)
    g_b = _dg(x, bd(b * e_neg), _NT)
    g_k = _dg(x, bd(kp * e_neg), _NT)
    xs = _dg(x, S.astype(BF16), _NT)
    am = jnp.where(strict, g_b[:C], 0.0)
    mb = jnp.where(incl, g_b[C:], 0.0)
    akmk = jnp.concatenate([jnp.where(strict, g_k[:C], 0.0), jnp.where(incl, g_k[C:], 0.0)], axis=0)
    kv = _dg(akmk.astype(BF16), bd(v), _NN)
    rhs = -(xs[:C] + kv[:C])
    npow = -am
    tinv = eye + npow
    npow = _dg(npow.astype(BF16), bd(npow), _NN)
    for _ in range(int(math.log2(C)) - 2):
        both = _dg(jnp.concatenate([tinv, npow], axis=0).astype(BF16), bd(npow), _NN)
        tinv = tinv + both[:C]
        npow = both[C:]
    tinv = tinv + _dg(tinv.astype(BF16), bd(npow), _NN)
    u = _dg(tinv.astype(BF16), bd(rhs), _NN)
    y = xs[C:] + kv[C:] + _dg(mb.astype(BF16), bd(u), _NN)
    e_rest = jnp.exp(cum_last - cum)
    upd = _dg(jnp.concatenate([u, v], axis=0).astype(BF16),
              jnp.concatenate([b * e_rest, kp * e_rest], axis=0).astype(BF16), _TN)
    s_new = S * jnp.exp(cum_last) + upd * bdmask.astype(F32)
    mu = headsum(y) * (1.0 / RWKV_N)
    yc = y - mu
    var = headsum(yc * yc) * (1.0 / RWKV_N)
    yn = yc * lax.rsqrt(var + RWKV_GN_EPS) * lng + lnb
    return yn + bonus, s_new


def _wkv_kernel(r_ref, lw_ref, k_ref, v_ref, a_ref, kk_ref, ka_ref, rk_ref, g_ref, b_ref, s0_ref,
                z_ref, so_ref, s_scr, *, n_t, nb):
    t = pl.program_id(1)
    C, N, HW = WKV_CHUNK, RWKV_N, WKV_HEADS * RWKV_N
    ri = lax.broadcasted_iota(jnp.int32, (HW, HW), 0) // N
    ci = lax.broadcasted_iota(jnp.int32, (HW, HW), 1) // N
    bdmask_f = (ri == ci).astype(F32)

    @pl.when(t == 0)
    def _():
        for bi in range(nb):
            rows = s0_ref[bi].reshape(HW, N)
            s_scr[bi] = jnp.concatenate([rows] * WKV_HEADS, axis=1) * bdmask_f

    tt = lax.broadcasted_iota(jnp.int32, (C, C), 0)
    ss = lax.broadcasted_iota(jnp.int32, (C, C), 1)
    tp = lax.broadcasted_iota(jnp.int32, (C, HW), 0)
    sp = lax.broadcasted_iota(jnp.int32, (C, HW), 1) % N
    consts = (bdmask_f.astype(BF16), (tt >= ss).astype(BF16), tp > sp, tp >= sp, (tp == sp).astype(F32))
    for bi in range(nb):
        z, s_new = _wkv_chunk(r_ref[bi], lw_ref[bi], k_ref[bi], v_ref[bi], a_ref[bi], kk_ref[...], ka_ref[...],
                              rk_ref[...], g_ref[...], b_ref[...], s_scr[bi], consts)
        s_scr[bi] = s_new
        z_ref[bi] = z

    @pl.when(t == n_t - 1)
    def _():
        for bi in range(nb):
            for h in range(WKV_HEADS):
                so_ref[bi, h] = s_scr[bi, h * N:(h + 1) * N, h * N:(h + 1) * N]


def _wkv(name, r, lw, k, v, a, kkp, kap, rkp, lng, lnb, s0):
    B, T, D = r.shape
    C, HW = WKV_CHUNK, WKV_HEADS * RWKV_N
    assert T % C == 0 and D % HW == 0
    n_t = T // C
    tok = pl.BlockSpec((B, C, HW), lambda h, t: (0, t, h))
    vec = pl.BlockSpec((1, HW), lambda h, t: (0, h))
    st = pl.BlockSpec((B, WKV_HEADS, RWKV_N, RWKV_N), lambda h, t: (0, h, 0, 0))
    return pl.pallas_call(
        functools.partial(_wkv_kernel, n_t=n_t, nb=B), grid=(D // HW, n_t),
        in_specs=[tok] * 5 + [vec] * 5 + [st], out_specs=[tok, st],
        out_shape=[jax.ShapeDtypeStruct((B, T, D), F32), jax.ShapeDtypeStruct(s0.shape, F32)],
        scratch_shapes=[pltpu.VMEM((B, HW, HW), F32)], compiler_params=_params(2), name=name,
    )(r, lw, k, v, a, kkp, kap, rkp, lng, lnb, s0)


def _cmul(ar, ai, br, bi):
    return ar * br - ai * bi, ar * bi + ai * br


def _s5_kernel(x_ref, wb_ref, wc_ref, lr_ref, li_ref, dt_ref, d_ref, h0r_ref, h0i_ref,
               z_ref, hr_ref, hi_ref, pw_re, pw_im, cf, hc, xr_s, xi_s, *, n_t, tt):
    t = pl.program_id(2)
    R = SUBLANES

    @pl.when(t == 0)
    def _():
        lr, li, dt = lr_ref[0], li_ref[0], dt_ref[0]
        n = (lax.broadcasted_iota(jnp.int32, (R, S5_STATES), 0) + 1).astype(F32)
        mag = jnp.exp(lr * dt * n)
        ang = li * dt * n
        pw_re[...] = mag * jnp.cos(ang)
        pw_im[...] = mag * jnp.sin(ang)
        ar, ai = pw_re[0:1, :], pw_im[0:1, :]
        den = lr * lr + li * li
        cf[0:1, :] = ((ar - 1.0) * lr + ai * li) / den
        cf[1:2, :] = (ai * lr - (ar - 1.0) * li) / den
        hc[0:1, :] = h0r_ref[0]
        hc[1:2, :] = h0i_ref[0]

    x = x_ref[0]
    bu = _dot3(x, wb_ref[0])
    cr, ci = cf[0:1, :], cf[1:2, :]
    br, bi = _cmul(cr, ci, bu[:, :S5_STATES], bu[:, S5_STATES:])
    xr_s[...] = br
    xi_s[...] = bi
    rid = lax.broadcasted_iota(jnp.int32, (R, S5_STATES), 0)
    pr, pi = pw_re[...], pw_im[...]

    def body(i, carry):
        cr_, ci_ = carry
        o = pl.multiple_of(i * R, R)
        vr, vi = xr_s[pl.ds(o, R), :], xi_s[pl.ds(o, R), :]
        for sh in (1, 2, 4):
            mr, mi = pr[sh - 1:sh, :], pi[sh - 1:sh, :]
            sr, si = pltpu.roll(vr, sh, axis=0), pltpu.roll(vi, sh, axis=0)
            ur, ui = _cmul(mr, mi, sr, si)
            keep = rid >= sh
            vr, vi = vr + jnp.where(keep, ur, 0.0), vi + jnp.where(keep, ui, 0.0)
        ur, ui = _cmul(pr, pi, cr_, ci_)
        vr, vi = vr + ur, vi + ui
        xr_s[pl.ds(o, R), :] = vr
        xi_s[pl.ds(o, R), :] = vi
        return vr[R - 1:R, :], vi[R - 1:R, :]

    hr, hi = lax.fori_loop(0, tt // R, body, (hc[0:1, :], hc[1:2, :]))
    hc[0:1, :] = hr
    hc[1:2, :] = hi
    y = _dot3(jnp.concatenate([xr_s[...], xi_s[...]], axis=-1), wc_ref[0]) + d_ref[...] * x
    z_ref[0] = 0.5 * y * (1.0 + jnp.tanh(math.sqrt(2.0 / math.pi) * (y + 0.044715 * (y * y * y))))

    @pl.when(t == n_t - 1)
    def _():
        hr_ref[0] = hr
        hi_ref[0] = hi


def _s5(name, x3, wb, wc, lr, li, dt, d, h0r, h0i):
    B, T, D = x3.shape
    tt = min(256, T)
    assert T % tt == 0 and tt % SUBLANES == 0
    n_t = T // tt
    n_s = D // S5_SLAB
    S = S5_STATES
    st = pl.BlockSpec((1, 1, S), lambda b, s, t: (b, 0, s))
    pv = pl.BlockSpec((1, 1, S), lambda b, s, t: (0, 0, s))
    return pl.pallas_call(
        functools.partial(_s5_kernel, n_t=n_t, tt=tt), grid=(B, n_s, n_t),
        in_specs=[pl.BlockSpec((1, tt, S5_SLAB), lambda b, s, t: (b, t, s)),
                  pl.BlockSpec((1, S5_SLAB, 2 * S), lambda b, s, t: (s, 0, 0)),
                  pl.BlockSpec((1, 2 * S, S5_SLAB), lambda b, s, t: (s, 0, 0)),
                  pv, pv, pv,
                  pl.BlockSpec((1, S5_SLAB), lambda b, s, t: (0, s)),
                  st, st],
        out_specs=[pl.BlockSpec((1, tt, S5_SLAB), lambda b, s, t: (b, t, s)), st, st],
        out_shape=[jax.ShapeDtypeStruct((B, T, D), F32),
                   jax.ShapeDtypeStruct((B, 1, n_s * S), F32), jax.ShapeDtypeStruct((B, 1, n_s * S), F32)],
        scratch_shapes=[pltpu.VMEM((SUBLANES, S), F32), pltpu.VMEM((SUBLANES, S), F32),
                        pltpu.VMEM((SUBLANES, S), F32), pltpu.VMEM((SUBLANES, S), F32),
                        pltpu.VMEM((tt, S), F32), pltpu.VMEM((tt, S), F32)],
        compiler_params=_params(3), name=name,
    )(x3, wb, wc, lr, li, dt, d, h0r, h0i)


def _select_topk(gate, idx, valid, axis):
    g = jnp.where(valid, gate, NEG)
    big = jnp.int32(1 << 30)
    sel = jnp.zeros(gate.shape, jnp.bool_)
    for _ in range(MOBA_TOPK):
        m = jnp.max(g, axis=axis, keepdims=True)
        first = jnp.min(jnp.where(g == m, idx, big), axis=axis, keepdims=True)
        pick = idx == first
        sel = jnp.logical_or(sel, pick)
        g = jnp.where(pick, -jnp.inf, g)
    return jnp.logical_and(sel, valid)


def _moba_prompt_kernel(q_ref, k_ref, v_ref, sl_ref, o_ref, m_scr, l_scr, acc_scr, *, n_blk):
    i = pl.program_id(2)
    BK = MOBA_BLOCK
    q = q_ref[0]
    kmean = jnp.concatenate([jnp.sum(k_ref[0, n * BK:(n + 1) * BK, :], axis=0, keepdims=True)
                             for n in range(n_blk)], axis=0) * (1.0 / BK)
    gate = _dot3(q, kmean, _NT)
    nidx = lax.broadcasted_iota(jnp.int32, (BK, n_blk), 1)
    sel = _select_topk(gate, nidx, nidx < i, axis=1).astype(F32)
    slope = sl_ref[0][:, 0:1]
    qi = lax.broadcasted_iota(jnp.int32, (BK, BK), 0)
    ki = lax.broadcasted_iota(jnp.int32, (BK, BK), 1)
    qb = q.astype(BF16)
    scale = MOBA_DH ** -0.5
    o0 = pl.multiple_of(i * BK, BK)
    dist = qi - ki
    s = _dg(qb, k_ref[0, pl.ds(o0, BK), :].astype(BF16), _NT) * scale - slope * dist.astype(F32)
    s = jnp.where(dist >= 0, s, NEG)
    m = jnp.max(s, axis=-1, keepdims=True)
    p = jnp.exp(s - m)
    m_scr[...] = m
    l_scr[...] = jnp.sum(p, axis=-1, keepdims=True)
    acc_scr[...] = _dg(p.astype(BF16), v_ref[0, pl.ds(o0, BK), :].astype(BF16), _NN)
    for n in range(n_blk - 1):
        @pl.when(n < i)
        def _(n=n):
            dist = qi - ki + (i - n) * BK
            s = _dg(qb, k_ref[0, n * BK:(n + 1) * BK, :].astype(BF16), _NT) * scale - slope * dist.astype(F32)
            s = jnp.where(sel[:, n:n + 1] > 0.5, s, NEG)
            m_old = m_scr[...]
            m_new = jnp.maximum(m_old, jnp.max(s, axis=-1, keepdims=True))
            alpha = jnp.exp(m_old - m_new)
            p = jnp.exp(s - m_new)
            m_scr[...] = m_new
            l_scr[...] = alpha * l_scr[...] + jnp.sum(p, axis=-1, keepdims=True)
            acc_scr[...] = alpha * acc_scr[...] + _dg(p.astype(BF16), v_ref[0, n * BK:(n + 1) * BK, :].astype(BF16),
                                                      _NN)
    o_ref[0] = acc_scr[...] / l_scr[...]


def _moba_prompt(name, q, k, v, slopes):
    B, T, D = q.shape
    n_blk = T // MOBA_BLOCK
    assert T % MOBA_BLOCK == 0
    qs = pl.BlockSpec((1, MOBA_BLOCK, MOBA_DH), lambda b, h, i: (b, i, h))
    kv = pl.BlockSpec((1, T, MOBA_DH), lambda b, h, i: (b, 0, h))
    return pl.pallas_call(
        functools.partial(_moba_prompt_kernel, n_blk=n_blk), grid=(B, MOBA_H, n_blk),
        in_specs=[qs, kv, kv, pl.BlockSpec((1, 1, LANES), lambda b, h, i: (h, 0, 0))], out_specs=qs,
        out_shape=jax.ShapeDtypeStruct((B, T, D), F32),
        scratch_shapes=[pltpu.VMEM((MOBA_BLOCK, 1), F32), pltpu.VMEM((MOBA_BLOCK, 1), F32),
                        pltpu.VMEM((MOBA_BLOCK, MOBA_DH), F32)],
        compiler_params=_params(3), name=name,
    )(q, k, v, slopes)


def _q_rows(q, nq):
    return jnp.concatenate([q[:, h * MOBA_DH:(h + 1) * MOBA_DH] for h in range(MOBA_H)], axis=0)


def _head_match(n_rows, nq):
    shape = (n_rows * MOBA_H, MOBA_H * nq)
    return lax.broadcasted_iota(jnp.int32, shape, 0) % MOBA_H == lax.broadcasted_iota(jnp.int32, shape, 1) // nq


def _token_scores(k3, qr, nq, dot):
    n_rows = k3.shape[0]
    s2 = dot(k3.reshape(n_rows * MOBA_H, MOBA_DH), qr, _NT)
    s2 = jnp.where(_head_match(n_rows, nq), s2, 0.0)
    return jnp.sum(s2.reshape(n_rows, MOBA_H, MOBA_H * nq), axis=1)


def _moba_scores_kernel(pt_ref, q_ref, k_ref, s_ref, ks_ref, *, nq):
    kp = k_ref[0, 0]
    s_ref[0, 0] = _token_scores(kp, _q_rows(q_ref[0], nq), nq, _dot1)
    ks_ref[0, 0] = jnp.sum(kp, axis=0)


def _moba_scores(name, page_table, q, cache_k, layer):
    B, nq, D = q.shape
    n_pg = page_table.shape[1]
    gs = pltpu.PrefetchScalarGridSpec(
        num_scalar_prefetch=1, grid=(B, n_pg),
        in_specs=[pl.BlockSpec((1, nq, D), lambda b, p, pt: (b, 0, 0)),
                  pl.BlockSpec((1, 1, PAGE_SIZE, MOBA_H, MOBA_DH), lambda b, p, pt: (layer, pt[b, p], 0, 0, 0))],
        out_specs=[pl.BlockSpec((1, 1, PAGE_SIZE, MOBA_H * nq), lambda b, p, pt: (b, p, 0, 0)),
                   pl.BlockSpec((1, 1, MOBA_H, MOBA_DH), lambda b, p, pt: (b, p, 0, 0))])
    return pl.pallas_call(
        functools.partial(_moba_scores_kernel, nq=nq), grid_spec=gs,
        out_shape=[jax.ShapeDtypeStruct((B, n_pg, PAGE_SIZE, MOBA_H * nq), F32),
                   jax.ShapeDtypeStruct((B, n_pg, MOBA_H, MOBA_DH), F32)],
        compiler_params=_params(2), name=name,
    )(page_table, q, cache_k)


def _moba_probs_kernel(s_ref, ks_ref, q_ref, kn_ref, sl_ref, p_ref, pn_ref, sel_scr, *, nq, n_blk, past):
    BK = MOBA_BLOCK
    HQ = MOBA_H * nq
    qr = _q_rows(q_ref[0], nq)
    ppb = BK // PAGE_SIZE
    kmean = jnp.sum(ks_ref[0].reshape(n_blk, ppb, MOBA_H, MOBA_DH), axis=1) * (1.0 / BK)
    gate = _token_scores(kmean, qr, nq, _dot3)
    nidx = lax.broadcasted_iota(jnp.int32, (n_blk, HQ), 0)
    own = past // BK
    sel = _select_topk(gate, nidx, nidx < own, axis=0)
    sel_scr[...] = sel.astype(F32)
    slope = sl_ref[...]
    scale = MOBA_DH ** -0.5
    qpos = past + lax.broadcasted_iota(jnp.int32, (1, HQ), 1) % nq
    tok = lax.broadcasted_iota(jnp.int32, (BK, HQ), 0)

    def scores(n):
        s = s_ref[0, n] * scale - slope * (qpos - (n * BK + tok)).astype(F32)
        return jnp.where(sel_scr[pl.ds(n, 1), :] > 0.5, s, NEG)

    sn = _token_scores(kn_ref[0], qr, nq, _dot1) * scale
    tn = lax.broadcasted_iota(jnp.int32, (PAGE_SIZE, HQ), 0)
    dn = qpos - (past + tn)
    sn = jnp.where(jnp.logical_and(dn >= 0, tn < nq), sn - slope * dn.astype(F32), NEG)
    m = lax.fori_loop(0, n_blk, lambda n, m_: jnp.maximum(m_, jnp.max(scores(n), axis=0, keepdims=True)),
                      jnp.max(sn, axis=0, keepdims=True))
    pn = jnp.exp(sn - m)
    l = lax.fori_loop(0, n_blk, lambda n, l_: l_ + jnp.sum(jnp.exp(scores(n) - m), axis=0, keepdims=True),
                      jnp.sum(pn, axis=0, keepdims=True))
    pn_ref[0] = (pn / l).astype(pn_ref.dtype)

    def write(n, c):
        p_ref[0, n] = (jnp.exp(scores(n) - m) / l).astype(p_ref.dtype)
        return c

    lax.fori_loop(0, n_blk, write, 0)


def _moba_probs(name, s, ksum, q, k_new_pad, slopes_hq, past):
    B, n_blk, BK, HQ = s.shape
    nq = q.shape[1]
    blk = pl.BlockSpec((1, n_blk, BK, HQ), lambda b: (b, 0, 0, 0))
    return pl.pallas_call(
        functools.partial(_moba_probs_kernel, nq=nq, n_blk=n_blk, past=past), grid=(B,),
        in_specs=[blk,
                  pl.BlockSpec((1, ksum.shape[1], MOBA_H, MOBA_DH), lambda b: (b, 0, 0, 0)),
                  pl.BlockSpec((1, nq, D_MODEL), lambda b: (b, 0, 0)),
                  pl.BlockSpec((1, PAGE_SIZE, MOBA_H, MOBA_DH), lambda b: (b, 0, 0, 0)),
                  pl.BlockSpec((1, HQ), lambda b: (0, 0))],
        out_specs=[blk, pl.BlockSpec((1, PAGE_SIZE, HQ), lambda b: (b, 0, 0))],
        out_shape=[jax.ShapeDtypeStruct((B, n_blk, BK, HQ), BF16), jax.ShapeDtypeStruct((B, PAGE_SIZE, HQ), BF16)],
        scratch_shapes=[pltpu.VMEM((n_blk, HQ), F32)],
        compiler_params=_params(1), name=name,
    )(s, ksum, q, k_new_pad, slopes_hq)


def _moba_pv_kernel(pt_ref, p_ref, v_ref, pn_ref, vn_ref, o_ref, acc, *, nq, n_pg):
    pg = pl.program_id(1)

    match = _head_match(PAGE_SIZE, nq)

    def page_pv(p, v3):
        p2 = jnp.broadcast_to(p.astype(F32)[:, None, :], (PAGE_SIZE, MOBA_H, MOBA_H * nq))
        p2 = jnp.where(match, p2.reshape(PAGE_SIZE * MOBA_H, MOBA_H * nq), 0.0)
        return _dot1(p2, v3.reshape(PAGE_SIZE * MOBA_H, MOBA_DH), _TN)

    @pl.when(pg == 0)
    def _():
        acc[...] = page_pv(pn_ref[0], vn_ref[0])

    acc[...] += page_pv(p_ref[0, 0], v_ref[0, 0])

    @pl.when(pg == n_pg - 1)
    def _():
        a = acc[...]
        o_ref[0] = jnp.concatenate([a[h * nq:(h + 1) * nq, :] for h in range(MOBA_H)], axis=-1)


def _moba_pv(name, page_table, p, cache_v, pn, v_new_pad, nq, layer):
    B, n_pg, _, HQ = p.shape
    D = D_MODEL
    gs = pltpu.PrefetchScalarGridSpec(
        num_scalar_prefetch=1, grid=(B, n_pg),
        in_specs=[pl.BlockSpec((1, 1, PAGE_SIZE, HQ), lambda b, g, pt: (b, g, 0, 0)),
                  pl.BlockSpec((1, 1, PAGE_SIZE, MOBA_H, MOBA_DH), lambda b, g, pt: (layer, pt[b, g], 0, 0, 0)),
                  pl.BlockSpec((1, PAGE_SIZE, HQ), lambda b, g, pt: (b, 0, 0)),
                  pl.BlockSpec((1, PAGE_SIZE, MOBA_H, MOBA_DH), lambda b, g, pt: (b, 0, 0, 0))],
        out_specs=pl.BlockSpec((1, nq, D), lambda b, g, pt: (b, 0, 0)),
        scratch_shapes=[pltpu.VMEM((HQ, MOBA_DH), F32)])
    return pl.pallas_call(
        functools.partial(_moba_pv_kernel, nq=nq, n_pg=n_pg), grid_spec=gs,
        out_shape=jax.ShapeDtypeStruct((B, nq, D), F32), compiler_params=_params(2), name=name,
    )(page_table, p, cache_v, pn, v_new_pad)


def _ident(x):
    return x


def _mix(x, xp, mu):
    return x + (xp - x) * mu


def _pad_cols(w, n):
    return jnp.pad(w, ((0, 0), (0, n - w.shape[1])))


def _pad_rows(w, n):
    return jnp.pad(w, ((0, n - w.shape[0]), (0, 0)))


def _rwkv_layer(tag, x3, shift0, s0, v_first, P, j, ln_g, ln_b):
    B, T, D = x3.shape
    M = B * T
    x = x3.reshape(M, D)
    xp = jnp.concatenate([shift0[:, None, :], x3[:, :-1]], axis=1).reshape(M, D)
    mu = P['rwkv_mu'][j]
    mus = [mu[n][None, :] for n in range(6)]
    wb = lambda w: w.astype(BF16)
    one = lambda acc: (acc,)
    rows = [x, xp]
    r, = _mm(tag + 'r', rows, [mus[0]], [wb(P['rwkv_w_rkv'][j, 0])], [], [], _mix, one, [F32])
    k, = _mm(tag + 'k', rows, [mus[2]], [wb(P['rwkv_w_rkv'][j, 1])], [], [], _mix, one, [F32])
    v, = _mm(tag + 'v', rows, [mus[3]], [wb(P['rwkv_w_rkv'][j, 2])], [], [], _mix, one, [F32])
    lp = LANES
    hw, = _mm(tag + 'w1', rows, [mus[1]], [wb(_pad_cols(P['rwkv_w1'][j], lp))], [], [], _mix,
              lambda acc: (jnp.tanh(acc),), [F32])
    lw, = _mm(tag + 'w2', [hw], [], [wb(_pad_rows(P['rwkv_w2'][j], lp))], [], [P['rwkv_w0'][j][None, :]], _ident,
              lambda acc, w0: (-jnp.exp(-_softplus(-(w0 + acc)) - 0.5),), [F32])
    ha, = _mm(tag + 'a1', rows, [mus[4]], [wb(_pad_cols(P['rwkv_a1'][j], lp))], [], [], _mix, one, [F32])
    a, = _mm(tag + 'a2', [ha], [], [wb(_pad_rows(P['rwkv_a2'][j], lp))], [], [P['rwkv_a0'][j][None, :]], _ident,
             lambda acc, a0: (_sigmoid(a0 + acc),), [F32])
    hg, = _mm(tag + 'g1', rows, [mus[5]], [wb(P['rwkv_g1'][j])], [], [], _mix,
              lambda acc: (_sigmoid(acc),), [F32])
    g, = _mm(tag + 'g2', [hg], [], [wb(P['rwkv_g2'][j])], [], [], _ident, one, [F32])
    if j == 0:
        v_first = v
    else:
        hv, = _mm(tag + 'v1', rows, [mus[3]], [wb(_pad_cols(P['rwkv_v1'][j - 1], lp))], [], [], _mix, one, [F32])
        v, = _mm(tag + 'v2', [hv], [], [wb(_pad_rows(P['rwkv_v2'][j - 1], lp))], [v, v_first],
                 [P['rwkv_v0'][j - 1][None, :]], _ident,
                 lambda acc, v_, vf, v0: (v_ + (vf - v_) * _sigmoid(v0 + acc),), [F32])
    Tp = -(-T // WKV_CHUNK) * WKV_CHUNK
    seq = lambda t: jnp.pad(t.reshape(B, T, D), ((0, 0), (0, Tp - T), (0, 0)))
    vecp = lambda name: P[name][j].reshape(1, D)
    z, s_new = _wkv(tag + 'wkv', seq(r), seq(lw), seq(k), seq(v), seq(a), vecp('rwkv_k_k'), vecp('rwkv_k_a'),
                    vecp('rwkv_r_k'), vecp('rwkv_lnx_g'), vecp('rwkv_lnx_b'), s0)
    z = z[:, :T].reshape(M, D)
    xn = _mm_ln(tag + 'wo', [z, g], wb(P['rwkv_wo'][j]), x, ln_g, ln_b, lambda z_, g_: z_ * g_)
    return xn, s_new, x3[:, -1], v_first


def _s5_weights(P, j):
    ns, gs = D_MODEL // S5_SLAB, S5_SLAB // S5_GROUP
    eye = jnp.eye(gs, dtype=F32)

    def bd_in(b):
        bt = jnp.swapaxes(b.reshape(ns, gs, S5_P, S5_GROUP), 2, 3)
        return jnp.einsum('sgcp,gh->sgchp', bt, eye).reshape(ns, S5_SLAB, S5_STATES)

    def bd_out(c):
        ct = jnp.swapaxes(c.reshape(ns, gs, S5_GROUP, S5_P), 2, 3)
        return jnp.einsum('sgpc,gh->sgphc', ct, eye).reshape(ns, S5_STATES, S5_SLAB)

    wb = jnp.concatenate([bd_in(P['s5_b_re'][j]), bd_in(P['s5_b_im'][j])], axis=2)
    wc = jnp.concatenate([bd_out(P['s5_c_re'][j]), -bd_out(P['s5_c_im'][j])], axis=1)
    flat = lambda t: t.reshape(1, 1, S5_G * S5_P)
    dt = jnp.broadcast_to(jnp.exp(P['s5_log_dt'][j])[:, None], (S5_G, S5_P))
    return wb, wc, flat(P['s5_lam_re'][j]), flat(P['s5_lam_im'][j]), flat(dt)


def _s5_layer(tag, x3, h0r, h0i, P, j, ln_g, ln_b):
    B, T, D = x3.shape
    M = B * T
    x = x3.reshape(M, D)
    wb, wc, lr, li, dt = _s5_weights(P, j)
    st = lambda h: h.reshape(B, 1, S5_G * S5_P)
    z, hr, hi = _s5(tag + 'scan', x3, wb, wc, lr, li, dt, P['s5_d'][j][None, :], st(h0r), st(h0i))
    out, = _mm(tag + 'gate', [z.reshape(M, D)], [], [P['s5_w_val'][j].astype(BF16), P['s5_w_gate'][j].astype(BF16)],
               [], [], _ident, lambda val, gate: (val * _sigmoid(gate),), [F32])
    xn = _ln(tag + 'ln', x, out, ln_g, ln_b)
    return xn, hr.reshape(B, S5_G, S5_P), hi.reshape(B, S5_G, S5_P)


def _alibi_slopes():
    return 2.0 ** (-8.0 * jnp.arange(1, MOBA_H + 1, dtype=F32) / MOBA_H)


def _moba_layer(tag, x3, cache_k, cache_v, page_table, P, j, ln_g, ln_b):
    B, T, D = x3.shape
    M = B * T
    x = x3.reshape(M, D)
    wqkv = P['moba_w_qkv'][j]
    ws = [wqkv[:, n * D:(n + 1) * D].astype(BF16) for n in range(3)]
    q, k, v = _mm(tag + 'qkv', [x], [], ws, [], [], _ident, lambda a, b, c: (a, b, c), [F32, F32, F32])
    q3, k3, v3 = (t.reshape(B, T, D) for t in (q, k, v))
    slopes = _alibi_slopes()
    if cache_k is None:
        sl = jnp.broadcast_to(slopes[:, None, None], (MOBA_H, 1, LANES))
        o = _moba_prompt(tag + 'attn', q3, k3, v3, sl)
    else:
        n_pg = page_table.shape[1]
        past = n_pg * PAGE_SIZE
        assert past % MOBA_BLOCK == 0 and T <= PAGE_SIZE and T <= MOBA_BLOCK
        HQ = MOBA_H * T
        s, ksum = _moba_scores(tag + 'scores', page_table, q3, cache_k, j)
        sl = jnp.repeat(slopes, T)[None, :]
        pad = lambda t: jnp.pad(t.reshape(B, T, MOBA_H, MOBA_DH), ((0, 0), (0, PAGE_SIZE - T), (0, 0), (0, 0)))
        p, pn = _moba_probs(tag + 'probs', s.reshape(B, past // MOBA_BLOCK, MOBA_BLOCK, HQ), ksum, q3, pad(k3),
                            sl, past)
        o = _moba_pv(tag + 'pv', page_table, p.reshape(B, n_pg, PAGE_SIZE, HQ), cache_v, pn, pad(v3), T, j)
    xn = _mm_ln(tag + 'wo', [o.reshape(M, D)], P['moba_wo'][j].astype(BF16), x, ln_g, ln_b, _ident)
    return xn, k3.reshape(B, T, MOBA_H, MOBA_DH), v3.reshape(B, T, MOBA_H, MOBA_DH)


def _ffn_layer(tag, x3, conv0, P, i, ln_g, ln_b):
    B, T, D = x3.shape
    w_in = P['ffn_w_in'][i]
    h, cst = _ffn_in(tag + 'in', x3, w_in[:, :D_FF].astype(BF16), w_in[:, D_FF:].astype(BF16), conv0,
                     P['ffn_conv_w'][i], P['ffn_conv_b'][i][None, :])
    xn = _mm_ln(tag + 'down', [h], P['ffn_w_down'][i].astype(BF16), x3.reshape(B * T, D), ln_g, ln_b, _ident)
    return xn.reshape(B, T, D), cst


def _trunk(grp, x3, wkv0, shift0, s5re0, s5im0, conv0, cache_k, cache_v, page_table, P):
    B, T, D = x3.shape
    wkv_new, shift_new, s5re_new, s5im_new, k_new, v_new, conv_new = [], [], [], [], [], [], []
    v_first = None
    for i in range(DEPTH):
        kind, j = i % 3, i // 3
        tag = f'{grp}{i}_'
        g0, b0 = P['ln_g'][i, 0][None, :], P['ln_b'][i, 0][None, :]
        g1, b1 = P['ln_g'][i, 1][None, :], P['ln_b'][i, 1][None, :]
        if kind == 0:
            xn, S, last, v_first = _rwkv_layer(tag, x3, shift0[j], wkv0[j], v_first, P, j, g0, b0)
            wkv_new.append(S)
            shift_new.append(last)
        elif kind == 1:
            xn, hr, hi = _s5_layer(tag, x3, s5re0[j], s5im0[j], P, j, g0, b0)
            s5re_new.append(hr)
            s5im_new.append(hi)
        else:
            xn, kr, vr = _moba_layer(tag, x3, cache_k, cache_v, page_table, P, j, g0, b0)
            k_new.append(kr)
            v_new.append(vr)
        x3, cst = _ffn_layer(tag + 'ffn_', xn.reshape(B, T, D), conv0[i], P, i, g1, b1)
        conv_new.append(cst)
    return (x3, jnp.stack(wkv_new), jnp.stack(shift_new), jnp.stack(s5re_new), jnp.stack(s5im_new),
            jnp.stack(conv_new), jnp.stack(k_new), jnp.stack(v_new))


def kernel(x_prompt, x_sample, state_rwkv_wkv, state_rwkv_shift, state_s5_re, state_s5_im, state_ffn_conv, cache_k, cache_v, page_table, ln_g, ln_b, rwkv_mu, rwkv_w_rkv, rwkv_w0, rwkv_w1, rwkv_w2, rwkv_a0, rwkv_a1, rwkv_a2, rwkv_v0, rwkv_v1, rwkv_v2, rwkv_g1, rwkv_g2, rwkv_k_k, rwkv_k_a, rwkv_r_k, rwkv_lnx_g, rwkv_lnx_b, rwkv_wo, s5_log_dt, s5_lam_re, s5_lam_im, s5_b_re, s5_b_im, s5_c_re, s5_c_im, s5_d, s5_w_val, s5_w_gate, moba_w_qkv, moba_wo, ffn_w_in, ffn_conv_w, ffn_conv_b, ffn_w_down):
    P = dict(ln_g=ln_g, ln_b=ln_b, rwkv_mu=rwkv_mu, rwkv_w_rkv=rwkv_w_rkv, rwkv_w0=rwkv_w0, rwkv_w1=rwkv_w1,
             rwkv_w2=rwkv_w2, rwkv_a0=rwkv_a0, rwkv_a1=rwkv_a1, rwkv_a2=rwkv_a2, rwkv_v0=rwkv_v0,
             rwkv_v1=rwkv_v1, rwkv_v2=rwkv_v2, rwkv_g1=rwkv_g1, rwkv_g2=rwkv_g2, rwkv_k_k=rwkv_k_k,
             rwkv_k_a=rwkv_k_a, rwkv_r_k=rwkv_r_k.reshape(rwkv_r_k.shape[0], D_MODEL), rwkv_lnx_g=rwkv_lnx_g,
             rwkv_lnx_b=rwkv_lnx_b, rwkv_wo=rwkv_wo, s5_log_dt=s5_log_dt, s5_lam_re=s5_lam_re,
             s5_lam_im=s5_lam_im, s5_b_re=s5_b_re, s5_b_im=s5_b_im, s5_c_re=s5_c_re, s5_c_im=s5_c_im, s5_d=s5_d,
             s5_w_val=s5_w_val, s5_w_gate=s5_w_gate, moba_w_qkv=moba_w_qkv, moba_wo=moba_wo,
             ffn_w_in=ffn_w_in, ffn_conv_w=ffn_conv_w, ffn_conv_b=ffn_conv_b, ffn_w_down=ffn_w_down)
    B = x_prompt.shape[0]
    n_rwkv, n_s5 = state_rwkv_wkv.shape[0], state_s5_re.shape[0]
    zeros = lambda *s: jnp.zeros(s, F32)
    outs_p = _trunk('p', x_prompt, zeros(n_rwkv, B, RWKV_H, RWKV_N, RWKV_N), zeros(n_rwkv, B, D_MODEL),
                    zeros(n_s5, B, S5_G, S5_P), zeros(n_s5, B, S5_G, S5_P), zeros(DEPTH, B, CONV_W - 1, D_FF),
                    None, None, None, P)
    outs_s = _trunk('s', x_sample, state_rwkv_wkv, state_rwkv_shift, state_s5_re, state_s5_im, state_ffn_conv,
                    cache_k, cache_v, page_table, P)
    return (outs_p[0], outs_s[0]) + tuple(outs_p[1:]) + tuple(outs_s[1:])
```

```python
import functools
import math

import jax
import jax.numpy as jnp
from jax import lax
from jax.experimental import pallas as pl
from jax.experimental.pallas import tpu as pltpu

F32 = jnp.float32
BF16 = jnp.bfloat16

D_MODEL = 2048
DEPTH = 4
RWKV_N = 64
RWKV_H = D_MODEL // RWKV_N
RWKV_GN_EPS = 64e-5
S5_GROUP = 16
S5_G = D_MODEL // S5_GROUP
S5_P = 64
MOBA_H = 16
MOBA_DH = D_MODEL // MOBA_H
MOBA_BLOCK = 256
MOBA_TOPK = 3
PAGE_SIZE = 128
D_FF = 11 * D_MODEL // 4
CONV_W = 3
LN_EPS = 1e-5
DN_ALPHA = (2.0 * DEPTH) ** 0.25
NEG = -1e30

LANES = 128
SUBLANES = 8
VMEM_LIMIT_BYTES = 56 * 1024 * 1024

MM_LN_WEIGHT_TILE_BYTES = 12 * 1024 * 1024
WKV_CHUNK = 64
WKV_CHAINS = 16
WKV_HEADS = 4
S5_SLAB = LANES
S5_STATES = (S5_SLAB // S5_GROUP) * S5_P


def _params(n_axes):
    return pltpu.CompilerParams(dimension_semantics=("arbitrary",) * n_axes,
                                vmem_limit_bytes=VMEM_LIMIT_BYTES)


def _sigmoid(x):
    return 1.0 / (1.0 + jnp.exp(-x))


def _softplus(x):
    return jnp.maximum(x, 0.0) + jnp.log1p(jnp.exp(-jnp.abs(x)))


def _dg(a, b, dims):
    return lax.dot_general(a, b, (dims, ((), ())), preferred_element_type=F32)


_NN = ((1,), (0,))
_NT = ((1,), (1,))
_TN = ((0,), (0,))


def _split(a):
    hi = a.astype(BF16)
    lo = (a - hi.astype(F32)).astype(BF16)
    return hi, lo


def _dot3(a, b, dims=_NN):
    ah, al = _split(a)
    bh, bl = _split(b)
    return _dg(ah, bh, dims) + (_dg(al, bh, dims) + _dg(ah, bl, dims))


def _dot1(a, b, dims=_NN):
    return _dg(a.astype(BF16), b.astype(BF16), dims)


def _dot_exact_lhs(lhs_bf16, x):
    x1 = x.astype(BF16)
    r1 = x - x1.astype(F32)
    x2 = r1.astype(BF16)
    x3 = (r1 - x2.astype(F32)).astype(BF16)
    return _dg(lhs_bf16, x1, _NN) + (_dg(lhs_bf16, x2, _NN) + _dg(lhs_bf16, x3, _NN))


def _layer_norm(y, g, b):
    mu = jnp.mean(y, axis=-1, keepdims=True)
    yc = y - mu
    var = jnp.mean(yc * yc, axis=-1, keepdims=True)
    return yc * lax.rsqrt(var + LN_EPS) * g + b


def _mm_kernel(*refs, n_row, n_vec, n_w, n_erow, n_evec, n_out, prologue, epilogue):
    it = iter(refs)
    rows = [next(it) for _ in range(n_row)]
    vecs = [next(it) for _ in range(n_vec)]
    ws = [next(it) for _ in range(n_w)]
    erows = [next(it) for _ in range(n_erow)]
    evecs = [next(it) for _ in range(n_evec)]
    outs = [next(it) for _ in range(n_out)]
    a_scr = next(it)

    @pl.when(pl.program_id(1) == 0)
    def _():
        a_scr[...] = prologue(*[r[...] for r in rows], *[v[...] for v in vecs]).astype(BF16)

    a = a_scr[...]
    accs = [jnp.dot(a, w[...], preferred_element_type=F32) for w in ws]
    res = epilogue(*accs, *[e[...] for e in erows], *[e[...] for e in evecs])
    for o_ref, o in zip(outs, res):
        o_ref[...] = o.astype(o_ref.dtype)


def _mm(name, rows, vecs, ws, erows, evecs, prologue, epilogue, out_dtypes):
    M, K = rows[0].shape
    N = ws[0].shape[1]
    tm = min(512, M)
    tn = 512 if N % 512 == 0 else N
    assert M % tm == 0 and N % tn == 0
    in_specs = ([pl.BlockSpec((tm, K), lambda i, j: (i, 0)) for _ in rows]
                + [pl.BlockSpec((1, K), lambda i, j: (0, 0)) for _ in vecs]
                + [pl.BlockSpec((K, tn), lambda i, j: (0, j)) for _ in ws]
                + [pl.BlockSpec((tm, tn), lambda i, j: (i, j)) for _ in erows]
                + [pl.BlockSpec((1, tn), lambda i, j: (0, j)) for _ in evecs])
    out_specs = [pl.BlockSpec((tm, tn), lambda i, j: (i, j)) for _ in out_dtypes]
    out_shape = [jax.ShapeDtypeStruct((M, N), dt) for dt in out_dtypes]
    kern = functools.partial(_mm_kernel, n_row=len(rows), n_vec=len(vecs), n_w=len(ws), n_erow=len(erows),
                             n_evec=len(evecs), n_out=len(out_dtypes), prologue=prologue, epilogue=epilogue)
    return pl.pallas_call(
        kern, grid=(M // tm, N // tn), in_specs=in_specs, out_specs=out_specs, out_shape=out_shape,
        scratch_shapes=[pltpu.VMEM((tm, K), BF16)], compiler_params=_params(2), name=name,
    )(*rows, *vecs, *ws, *erows, *evecs)


def _mm_ln_kernel(*refs, n_row, prologue, n_k):
    it = iter(refs)
    rows = [next(it) for _ in range(n_row)]
    w_ref, x_ref, g_ref, b_ref, o_ref, acc = (next(it) for _ in range(6))
    k = pl.program_id(1)
    a = prologue(*[r[...] for r in rows]).astype(BF16)
    p = jnp.dot(a, w_ref[...], preferred_element_type=F32)
    if n_k == 1:
        o_ref[...] = _layer_norm(DN_ALPHA * x_ref[...] + p, g_ref[...], b_ref[...])
        return

    @pl.when(k == 0)
    def _():
        acc[...] = p

    @pl.when(jnp.logical_and(k > 0, k < n_k - 1))
    def _():
        acc[...] += p

    @pl.when(k == n_k - 1)
    def _():
        o_ref[...] = _layer_norm(DN_ALPHA * x_ref[...] + (acc[...] + p), g_ref[...], b_ref[...])


def _mm_ln(name, rows, w, x, g, b, prologue):
    M, K = rows[0].shape
    N = w.shape[1]
    n_k = -(-K * N * 2 // MM_LN_WEIGHT_TILE_BYTES)
    tk = K // n_k
    tm = min(512 if n_k > 1 else 256, M)
    assert M % tm == 0 and K % n_k == 0 and tk % LANES == 0
    in_specs = ([pl.BlockSpec((tm, tk), lambda i, k: (i, k)) for _ in rows]
                + [pl.BlockSpec((tk, N), lambda i, k: (k, 0)),
                   pl.BlockSpec((tm, N), lambda i, k: (i, 0)),
                   pl.BlockSpec((1, N), lambda i, k: (0, 0)),
                   pl.BlockSpec((1, N), lambda i, k: (0, 0))])
    kern = functools.partial(_mm_ln_kernel, n_row=len(rows), prologue=prologue, n_k=n_k)
    return pl.pallas_call(
        kern, grid=(M // tm, n_k), in_specs=in_specs,
        out_specs=pl.BlockSpec((tm, N), lambda i, k: (i, 0)),
        out_shape=jax.ShapeDtypeStruct((M, N), F32),
        scratch_shapes=[pltpu.VMEM((tm, N), F32)], compiler_params=_params(2), name=name,
    )(*rows, w, x, g, b)


def _ln_kernel(x_ref, h_ref, g_ref, b_ref, o_ref):
    o_ref[...] = _layer_norm(DN_ALPHA * x_ref[...] + h_ref[...], g_ref[...], b_ref[...])


def _ln(name, x, h, g, b):
    M, N = x.shape
    tm = min(512, M)
    row = pl.BlockSpec((tm, N), lambda i: (i, 0))
    vec = pl.BlockSpec((1, N), lambda i: (0, 0))
    return pl.pallas_call(_ln_kernel, grid=(M // tm,), in_specs=[row, row, vec, vec], out_specs=row,
                          out_shape=jax.ShapeDtypeStruct((M, N), F32), compiler_params=_params(1), name=name)(x, h, g, b)


def _ffn_in_kernel(x_ref, wg_ref, wv_ref, c0_ref, cw_ref, cb_ref, h_ref, cs_ref, a_scr, carry, *, bt, tt):
    t = pl.program_id(1)
    j = pl.program_id(2)
    rows = bt * tt
    tn = wg_ref.shape[1]

    @pl.when(j == 0)
    def _():
        a_scr[...] = x_ref[...].reshape(rows, x_ref.shape[2]).astype(BF16)

    @pl.when(t == 0)
    def _():
        carry[j] = c0_ref[...]

    a = a_scr[...]
    hg = jnp.dot(a, wg_ref[...], preferred_element_type=F32)
    hv = jnp.dot(a, wv_ref[...], preferred_element_type=F32)
    prev = carry[j]
    p2 = jnp.broadcast_to(prev[:, 0:1, :], (bt, tt, tn)).reshape(rows, tn)
    p1 = jnp.broadcast_to(prev[:, 1:2, :], (bt, tt, tn)).reshape(rows, tn)
    r = lax.broadcasted_iota(jnp.int32, (rows, tn), 0) % tt
    s1 = jnp.where(r == 0, p1, pltpu.roll(hg, 1, axis=0))
    s2 = jnp.where(r == 0, p2, jnp.where(r == 1, p1, pltpu.roll(hg, 2, axis=0)))
    cw = cw_ref[...]
    c = cb_ref[...] + cw[0:1, :] * s2 + cw[1:2, :] * s1 + cw[2:3, :] * hg
    h_ref[...] = (c * _sigmoid(c) * hv).astype(h_ref.dtype)
    last = hg.reshape(bt, tt, tn)[:, tt - 2:, :]
    carry[j] = last
    cs_ref[0] = last


def _ffn_in(name, x3, wg, wv, conv0, cw, cb):
    B, T, K = x3.shape
    F = wg.shape[1]
    bt, tt = (1, 512) if T >= 512 else (B, T)
    tn = 512
    assert B % bt == 0 and T % tt == 0 and F % tn == 0 and tt >= CONV_W - 1
    n_j = F // tn
    kern = functools.partial(_ffn_in_kernel, bt=bt, tt=tt)
    n_t = T // tt
    h, cst = pl.pallas_call(
        kern, grid=(B // bt, n_t, n_j),
        in_specs=[pl.BlockSpec((bt, tt, K), lambda b, t, j: (b, t, 0)),
                  pl.BlockSpec((K, tn), lambda b, t, j: (0, j)),
                  pl.BlockSpec((K, tn), lambda b, t, j: (0, j)),
                  pl.BlockSpec((bt, CONV_W - 1, tn), lambda b, t, j: (b, 0, j)),
                  pl.BlockSpec((CONV_W, tn), lambda b, t, j: (0, j)),
                  pl.BlockSpec((1, tn), lambda b, t, j: (0, j))],
        out_specs=[pl.BlockSpec((bt * tt, tn), lambda b, t, j: (b * n_t + t, j)),
                   pl.BlockSpec((1, bt, CONV_W - 1, tn), lambda b, t, j: (t, b, 0, j))],
        out_shape=[jax.ShapeDtypeStruct((B * T, F), BF16),
                   jax.ShapeDtypeStruct((n_t, B, CONV_W - 1, F), F32)],
        scratch_shapes=[pltpu.VMEM((bt * tt, K), BF16), pltpu.VMEM((n_j, bt, CONV_W - 1, tn), F32)],
        compiler_params=_params(3), name=name,
    )(x3, wg, wv, conv0, cw, cb)
    return h, cst[n_t - 1]


def _wkv_chunks(r, lw, k, v, a, prm, S, consts):
    C = WKV_CHUNK
    bdmask, l_incl, strict, incl, eye = consts

    def bd(xp):
        return jnp.concatenate([xp.astype(BF16)] * WKV_HEADS, axis=0) * bdmask

    def headsum(xp):
        hi, lo = _split(xp)
        return _dg(hi, bdmask, _NN) + _dg(lo, bdmask, _NN)

    def each(f, *lists):
        return [f(*args) for args in zip(*lists)]

    kkp, kap, rkp, lng, lnb = (list(t) for t in zip(*prm))
    kkn = each(lambda k_, p_: k_ * p_, k, kkp)
    kp = each(lambda k_, a_, p_: k_ * (1.0 + (a_ - 1.0) * p_), k, a, kap)
    sums = each(lambda kkn_, r_, kp_, p_: headsum(jnp.concatenate([kkn_ * kkn_, r_ * kp_ * p_], axis=0)),
                kkn, r, kp, rkp)
    kk = each(lambda kkn_, s_: kkn_ / jnp.maximum(jnp.sqrt(s_[:C]), 1e-12), kkn, sums)
    bonus = each(lambda s_, v_: s_[C:] * v_, sums, v)
    b = each(lambda kk_, a_: kk_ * a_, kk, a)
    cum = each(lambda lw_: _dot_exact_lhs(l_incl, lw_), lw)
    e_neg = each(lambda c_: jnp.exp(-c_), cum)
    x = each(lambda kk_, c_, lw_, r_: jnp.concatenate([kk_ * jnp.exp(c_ - lw_), r_ * jnp.exp(c_)],
                                                       axis=0).astype(BF16), kk, cum, lw, r)
    g_b = each(lambda x_, b_, e_: _dg(x_, bd(b_ * e_), _NT), x, b, e_neg)
    g_k = each(lambda x_, k_, e_: _dg(x_, bd(k_ * e_), _NT), x, kp, e_neg)
    xs = each(lambda x_, s_: _dg(x_, s_.astype(BF16), _NT), x, S)
    mb = each(lambda g_: jnp.where(incl, g_[C:], 0.0), g_b)
    kv = each(lambda g_, v_: _dg(jnp.concatenate([jnp.where(strict, g_[:C], 0.0), jnp.where(incl, g_[C:], 0.0)],
                                                 axis=0).astype(BF16), bd(v_), _NN), g_k, v)
    rhs = each(lambda xs_, kv_: -(xs_[:C] + kv_[:C]), xs, kv)
    npow = each(lambda g_: -jnp.where(strict, g_[:C], 0.0), g_b)
    tinv = each(lambda n_: eye + n_, npow)
    npow = each(lambda n_: _dg(n_.astype(BF16), bd(n_), _NN), npow)
    for _ in range(int(math.log2(C)) - 2):
        both = each(lambda t_, n_: _dg(jnp.concatenate([t_, n_], axis=0).astype(BF16), bd(n_), _NN), tinv, npow)
        tinv = each(lambda t_, b_: t_ + b_[:C], tinv, both)
        npow = each(lambda b_: b_[C:], both)
    tinv = each(lambda t_, n_: t_ + _dg(t_.astype(BF16), bd(n_), _NN), tinv, npow)
    u = each(lambda t_, r_: _dg(t_.astype(BF16), bd(r_), _NN), tinv, rhs)
    y = each(lambda xs_, kv_, mb_, u_: xs_[C:] + kv_[C:] + _dg(mb_.astype(BF16), bd(u_), _NN), xs, kv, mb, u)
    upd = each(lambda u_, v_, b_, k_, c_: _dg(
        jnp.concatenate([u_, v_], axis=0).astype(BF16),
        (jnp.concatenate([b_, k_], axis=0) * jnp.exp(c_[C - 1:C, :] - jnp.concatenate([c_, c_], axis=0))).astype(BF16),
        _TN), u, v, b, kp, cum)
    bdmask_f = bdmask.astype(F32)
    s_new = each(lambda s_, c_, u_: s_ * jnp.exp(c_[C - 1:C, :]) + u_ * bdmask_f, S, cum, upd)
    mu = each(lambda y_: headsum(y_) * (1.0 / RWKV_N), y)
    yc = each(lambda y_, m_: y_ - m_, y, mu)
    var = each(lambda yc_: headsum(yc_ * yc_) * (1.0 / RWKV_N), yc)
    z = each(lambda yc_, var_, g_, b_, bo_: yc_ * lax.rsqrt(var_ + RWKV_GN_EPS) * g_ + b_ + bo_,
             yc, var, lng, lnb, bonus)
    return z, s_new


def _wkv_kernel(r_ref, lw_ref, k_ref, v_ref, a_ref, kk_ref, ka_ref, rk_ref, g_ref, b_ref, s0_ref,
                z_ref, so_ref, s_scr, *, n_t, nb, n_grp):
    t = pl.program_id(1)
    C, N, HW = WKV_CHUNK, RWKV_N, WKV_HEADS * RWKV_N
    ri = lax.broadcasted_iota(jnp.int32, (HW, HW), 0) // N
    ci = lax.broadcasted_iota(jnp.int32, (HW, HW), 1) // N
    bdmask_f = (ri == ci).astype(F32)

    chains = [(bi, g) for bi in range(nb) for g in range(n_grp)]

    @pl.when(t == 0)
    def _():
        for n, (bi, g) in enumerate(chains):
            rows = s0_ref[bi, g * WKV_HEADS:(g + 1) * WKV_HEADS].reshape(HW, N)
            s_scr[n] = jnp.concatenate([rows] * WKV_HEADS, axis=1) * bdmask_f

    tt = lax.broadcasted_iota(jnp.int32, (C, C), 0)
    ss = lax.broadcasted_iota(jnp.int32, (C, C), 1)
    tp = lax.broadcasted_iota(jnp.int32, (C, HW), 0)
    sp = lax.broadcasted_iota(jnp.int32, (C, HW), 1) % N
    consts = (bdmask_f.astype(BF16), (tt >= ss).astype(BF16), tp > sp, tp >= sp, (tp == sp).astype(F32))
    sls = [slice(g * HW, (g + 1) * HW) for _, g in chains]
    tok = lambda ref: [ref[bi, :, sl] for (bi, _), sl in zip(chains, sls)]
    prm = [tuple(ref[:, sl] for ref in (kk_ref, ka_ref, rk_ref, g_ref, b_ref)) for sl in sls]
    zs, s_news = _wkv_chunks(tok(r_ref), tok(lw_ref), tok(k_ref), tok(v_ref), tok(a_ref), prm,
                             [s_scr[n] for n in range(len(chains))], consts)
    for n, ((bi, _), sl) in enumerate(zip(chains, sls)):
        s_scr[n] = s_news[n]
        z_ref[bi, :, sl] = zs[n]

    @pl.when(t == n_t - 1)
    def _():
        for n, (bi, g) in enumerate(chains):
            for h in range(WKV_HEADS):
                so_ref[bi, g * WKV_HEADS + h] = s_scr[n, h * N:(h + 1) * N, h * N:(h + 1) * N]


def _wkv(name, r, lw, k, v, a, kkp, kap, rkp, lng, lnb, s0):
    B, T, D = r.shape
    C, HW = WKV_CHUNK, WKV_HEADS * RWKV_N
    n_grp = min(max(1, WKV_CHAINS // B), D // HW)
    W = n_grp * HW
    assert T % C == 0 and D % W == 0
    n_t = T // C
    tok = pl.BlockSpec((B, C, W), lambda h, t: (0, t, h))
    vec = pl.BlockSpec((1, W), lambda h, t: (0, h))
    st = pl.BlockSpec((B, n_grp * WKV_HEADS, RWKV_N, RWKV_N), lambda h, t: (0, h, 0, 0))
    return pl.pallas_call(
        functools.partial(_wkv_kernel, n_t=n_t, nb=B, n_grp=n_grp), grid=(D // W, n_t),
        in_specs=[tok] * 5 + [vec] * 5 + [st], out_specs=[tok, st],
        out_shape=[jax.ShapeDtypeStruct((B, T, D), F32), jax.ShapeDtypeStruct(s0.shape, F32)],
        scratch_shapes=[pltpu.VMEM((B * n_grp, HW, HW), F32)], compiler_params=_params(2), name=name,
    )(r, lw, k, v, a, kkp, kap, rkp, lng, lnb, s0)


def _cmul(ar, ai, br, bi):
    return ar * br - ai * bi, ar * bi + ai * br


def _s5_kernel(x_ref, wb_ref, wc_ref, lr_ref, li_ref, dt_ref, d_ref, h0r_ref, h0i_ref,
               z_ref, hr_ref, hi_ref, pw_re, pw_im, cf, hc, *, n_t, tt):
    t = pl.program_id(2)
    R = SUBLANES

    @pl.when(t == 0)
    def _():
        lr, li, dt = lr_ref[0], li_ref[0], dt_ref[0]
        n = (lax.broadcasted_iota(jnp.int32, (R, S5_STATES), 0) + 1).astype(F32)
        mag = jnp.exp(lr * dt * n)
        ang = li * dt * n
        pw_re[...] = mag * jnp.cos(ang)
        pw_im[...] = mag * jnp.sin(ang)
        ar, ai = pw_re[0:1, :], pw_im[0:1, :]
        den = lr * lr + li * li
        cf[0:1, :] = ((ar - 1.0) * lr + ai * li) / den
        cf[1:2, :] = (ai * lr - (ar - 1.0) * li) / den
        hc[0:1, :] = h0r_ref[0]
        hc[1:2, :] = h0i_ref[0]

    x = x_ref[0]
    bu = _dot3(x, wb_ref[0])
    cr, ci = cf[0:1, :], cf[1:2, :]
    br, bi = _cmul(cr, ci, bu[:, :S5_STATES], bu[:, S5_STATES:])
    rid = lax.broadcasted_iota(jnp.int32, (R, S5_STATES), 0)
    pr, pi = pw_re[...], pw_im[...]
    hr, hi = hc[0:1, :], hc[1:2, :]
    xrs, xis = [], []
    for i in range(tt // R):
        vr, vi = br[i * R:(i + 1) * R, :], bi[i * R:(i + 1) * R, :]
        for sh in (1, 2, 4):
            mr, mi = pr[sh - 1:sh, :], pi[sh - 1:sh, :]
            sr, si = pltpu.roll(vr, sh, axis=0), pltpu.roll(vi, sh, axis=0)
            ur, ui = _cmul(mr, mi, sr, si)
            keep = rid >= sh
            vr, vi = vr + jnp.where(keep, ur, 0.0), vi + jnp.where(keep, ui, 0.0)
        ur, ui = _cmul(pr, pi, hr, hi)
        vr, vi = vr + ur, vi + ui
        xrs.append(vr)
        xis.append(vi)
        hr, hi = vr[R - 1:R, :], vi[R - 1:R, :]
    hc[0:1, :] = hr
    hc[1:2, :] = hi
    xcat = jnp.concatenate([jnp.concatenate(xrs, axis=0), jnp.concatenate(xis, axis=0)], axis=-1)
    y = _dot3(xcat, wc_ref[0]) + d_ref[...] * x
    z_ref[0] = 0.5 * y * (1.0 + jnp.tanh(math.sqrt(2.0 / math.pi) * (y + 0.044715 * (y * y * y))))

    @pl.when(t == n_t - 1)
    def _():
        hr_ref[0] = hr
        hi_ref[0] = hi


def _s5(name, x3, wb, wc, lr, li, dt, d, h0r, h0i):
    B, T, D = x3.shape
    tt = min(256, T)
    assert T % tt == 0 and tt % SUBLANES == 0
    n_t = T // tt
    n_s = D // S5_SLAB
    S = S5_STATES
    st = pl.BlockSpec((1, 1, S), lambda b, s, t: (b, 0, s))
    pv = pl.BlockSpec((1, 1, S), lambda b, s, t: (0, 0, s))
    return pl.pallas_call(
        functools.partial(_s5_kernel, n_t=n_t, tt=tt), grid=(B, n_s, n_t),
        in_specs=[pl.BlockSpec((1, tt, S5_SLAB), lambda b, s, t: (b, t, s)),
                  pl.BlockSpec((1, S5_SLAB, 2 * S), lambda b, s, t: (s, 0, 0)),
                  pl.BlockSpec((1, 2 * S, S5_SLAB), lambda b, s, t: (s, 0, 0)),
                  pv, pv, pv,
                  pl.BlockSpec((1, S5_SLAB), lambda b, s, t: (0, s)),
                  st, st],
        out_specs=[pl.BlockSpec((1, tt, S5_SLAB), lambda b, s, t: (b, t, s)), st, st],
        out_shape=[jax.ShapeDtypeStruct((B, T, D), F32),
                   jax.ShapeDtypeStruct((B, 1, n_s * S), F32), jax.ShapeDtypeStruct((B, 1, n_s * S), F32)],
        scratch_shapes=[pltpu.VMEM((SUBLANES, S), F32), pltpu.VMEM((SUBLANES, S), F32),
                        pltpu.VMEM((SUBLANES, S), F32), pltpu.VMEM((SUBLANES, S), F32)],
        compiler_params=_params(3), name=name,
    )(x3, wb, wc, lr, li, dt, d, h0r, h0i)


def _select_topk(gate, idx, valid, axis):
    g = jnp.where(valid, gate, NEG)
    big = jnp.int32(1 << 30)
    sel = jnp.zeros(gate.shape, jnp.bool_)
    for _ in range(MOBA_TOPK):
        m = jnp.max(g, axis=axis, keepdims=True)
        first = jnp.min(jnp.where(g == m, idx, big), axis=axis, keepdims=True)
        pick = idx == first
        sel = jnp.logical_or(sel, pick)
        g = jnp.where(pick, -jnp.inf, g)
    return jnp.logical_and(sel, valid)


def _moba_prompt_kernel(q_ref, k_ref, v_ref, sl_ref, o_ref, kmean_scr, *, n_blk):
    i = pl.program_id(2)
    BK = MOBA_BLOCK

    @pl.when(i == 0)
    def _():
        kmean_scr[...] = jnp.concatenate([jnp.sum(k_ref[0, n * BK:(n + 1) * BK, :], axis=0, keepdims=True)
                                          for n in range(n_blk)], axis=0) * (1.0 / BK)

    q = q_ref[0]
    slope = sl_ref[0][:, 0:1]
    qi = lax.broadcasted_iota(jnp.int32, (BK, BK), 0)
    ki = lax.broadcasted_iota(jnp.int32, (BK, BK), 1)
    causal = qi >= ki
    bias0 = slope * (qi - ki).astype(F32)
    qb = q.astype(BF16)
    scale = MOBA_DH ** -0.5

    def tile(c):
        ss = []
        if c > 0:
            gate = _dot3(q, kmean_scr[...], _NT)
            nidx = lax.broadcasted_iota(jnp.int32, (BK, n_blk), 1)
            sel = _select_topk(gate, nidx, nidx < c, axis=1)
        for n in range(c + 1):
            s = _dg(qb, k_ref[0, n * BK:(n + 1) * BK, :].astype(BF16), _NT) * scale
            s = s - (bias0 + slope * float((c - n) * BK))
            ss.append(jnp.where(causal if n == c else sel[:, n:n + 1], s, NEG))
        m = functools.reduce(jnp.maximum, [jnp.max(s, axis=-1, keepdims=True) for s in ss])
        ps = [jnp.exp(s - m) for s in ss]
        l = functools.reduce(lambda x, y: x + y, [jnp.sum(p, axis=-1, keepdims=True) for p in ps])
        acc = functools.reduce(lambda x, y: x + y,
                               [_dg(p.astype(BF16), v_ref[0, n * BK:(n + 1) * BK, :].astype(BF16), _NN)
                                for n, p in enumerate(ps)])
        o_ref[0] = acc / l

    for c in range(n_blk):
        pl.when(i == c)(functools.partial(tile, c))


def _moba_prompt(name, q, k, v, slopes):
    B, T, D = q.shape
    n_blk = T // MOBA_BLOCK
    assert T % MOBA_BLOCK == 0
    qs = pl.BlockSpec((1, MOBA_BLOCK, MOBA_DH), lambda b, h, i: (b, i, h))
    kv = pl.BlockSpec((1, T, MOBA_DH), lambda b, h, i: (b, 0, h))
    return pl.pallas_call(
        functools.partial(_moba_prompt_kernel, n_blk=n_blk), grid=(B, MOBA_H, n_blk),
        in_specs=[qs, kv, kv, pl.BlockSpec((1, 1, LANES), lambda b, h, i: (h, 0, 0))], out_specs=qs,
        out_shape=jax.ShapeDtypeStruct((B, T, D), F32),
        scratch_shapes=[pltpu.VMEM((n_blk, MOBA_DH), F32)],
        compiler_params=_params(3), name=name,
    )(q, k, v, slopes)


def _q_rows(q, nq):
    return jnp.concatenate([q[:, h * MOBA_DH:(h + 1) * MOBA_DH] for h in range(MOBA_H)], axis=0)


def _head_match(n_rows, nq):
    shape = (n_rows * MOBA_H, MOBA_H * nq)
    return lax.broadcasted_iota(jnp.int32, shape, 0) % MOBA_H == lax.broadcasted_iota(jnp.int32, shape, 1) // nq


def _token_scores(k3, qr, nq, dot):
    n_rows = k3.shape[0]
    s2 = dot(k3.reshape(n_rows * MOBA_H, MOBA_DH), qr, _NT)
    s2 = jnp.where(_head_match(n_rows, nq), s2, 0.0)
    return jnp.sum(s2.reshape(n_rows, MOBA_H, MOBA_H * nq), axis=1)


def _moba_scores_kernel(pt_ref, q_ref, k_ref, s_ref, ks_ref, *, nq):
    kp = k_ref[0, 0]
    s_ref[0, 0] = _token_scores(kp, _q_rows(q_ref[0], nq), nq, _dot1)
    ks_ref[0, 0] = jnp.sum(kp, axis=0)


def _moba_scores(name, page_table, q, cache_k, layer):
    B, nq, D = q.shape
    n_pg = page_table.shape[1]
    gs = pltpu.PrefetchScalarGridSpec(
        num_scalar_prefetch=1, grid=(B, n_pg),
        in_specs=[pl.BlockSpec((1, nq, D), lambda b, p, pt: (b, 0, 0)),
                  pl.BlockSpec((1, 1, PAGE_SIZE, MOBA_H, MOBA_DH), lambda b, p, pt: (layer, pt[b, p], 0, 0, 0))],
        out_specs=[pl.BlockSpec((1, 1, PAGE_SIZE, MOBA_H * nq), lambda b, p, pt: (b, p, 0, 0)),
                   pl.BlockSpec((1, 1, MOBA_H, MOBA_DH), lambda b, p, pt: (b, p, 0, 0))])
    return pl.pallas_call(
        functools.partial(_moba_scores_kernel, nq=nq), grid_spec=gs,
        out_shape=[jax.ShapeDtypeStruct((B, n_pg, PAGE_SIZE, MOBA_H * nq), F32),
                   jax.ShapeDtypeStruct((B, n_pg, MOBA_H, MOBA_DH), F32)],
        compiler_params=_params(2), name=name,
    )(page_table, q, cache_k)


def _moba_probs_kernel(s_ref, ks_ref, q_ref, kn_ref, sl_ref, p_ref, pn_ref, sel_scr, *, nq, n_blk, past):
    BK = MOBA_BLOCK
    HQ = MOBA_H * nq
    qr = _q_rows(q_ref[0], nq)
    ppb = BK // PAGE_SIZE
    kmean = jnp.sum(ks_ref[0].reshape(n_blk, ppb, MOBA_H, MOBA_DH), axis=1) * (1.0 / BK)
    gate = _token_scores(kmean, qr, nq, _dot3)
    nidx = lax.broadcasted_iota(jnp.int32, (n_blk, HQ), 0)
    own = past // BK
    sel = _select_topk(gate, nidx, nidx < own, axis=0)
    sel_scr[...] = sel.astype(F32)
    slope = sl_ref[...]
    scale = MOBA_DH ** -0.5
    qpos = past + lax.broadcasted_iota(jnp.int32, (1, HQ), 1) % nq
    tok = lax.broadcasted_iota(jnp.int32, (BK, HQ), 0)

    def scores(n):
        s = s_ref[0, n] * scale - slope * (qpos - (n * BK + tok)).astype(F32)
        return jnp.where(sel_scr[pl.ds(n, 1), :] > 0.5, s, NEG)

    sn = _token_scores(kn_ref[0], qr, nq, _dot1) * scale
    tn = lax.broadcasted_iota(jnp.int32, (PAGE_SIZE, HQ), 0)
    dn = qpos - (past + tn)
    sn = jnp.where(jnp.logical_and(dn >= 0, tn < nq), sn - slope * dn.astype(F32), NEG)
    m = lax.fori_loop(0, n_blk, lambda n, m_: jnp.maximum(m_, jnp.max(scores(n), axis=0, keepdims=True)),
                      jnp.max(sn, axis=0, keepdims=True))
    pn = jnp.exp(sn - m)
    l = lax.fori_loop(0, n_blk, lambda n, l_: l_ + jnp.sum(jnp.exp(scores(n) - m), axis=0, keepdims=True),
                      jnp.sum(pn, axis=0, keepdims=True))
    pn_ref[0] = (pn / l).astype(pn_ref.dtype)

    def write(n, c):
        p_ref[0, n] = (jnp.exp(scores(n) - m) / l).astype(p_ref.dtype)
        return c

    lax.fori_loop(0, n_blk, write, 0)


def _moba_probs(name, s, ksum, q, k_new_pad, slopes_hq, past):
    B, n_blk, BK, HQ = s.shape
    nq = q.shape[1]
    blk = pl.BlockSpec((1, n_blk, BK, HQ), lambda b: (b, 0, 0, 0))
    return pl.pallas_call(
        functools.partial(_moba_probs_kernel, nq=nq, n_blk=n_blk, past=past), grid=(B,),
        in_specs=[blk,
                  pl.BlockSpec((1, ksum.shape[1], MOBA_H, MOBA_DH), lambda b: (b, 0, 0, 0)),
                  pl.BlockSpec((1, nq, D_MODEL), lambda b: (b, 0, 0)),
                  pl.BlockSpec((1, PAGE_SIZE, MOBA_H, MOBA_DH), lambda b: (b, 0, 0, 0)),
                  pl.BlockSpec((1, HQ), lambda b: (0, 0))],
        out_specs=[blk, pl.BlockSpec((1, PAGE_SIZE, HQ), lambda b: (b, 0, 0))],
        out_shape=[jax.ShapeDtypeStruct((B, n_blk, BK, HQ), BF16), jax.ShapeDtypeStruct((B, PAGE_SIZE, HQ), BF16)],
        scratch_shapes=[pltpu.VMEM((n_blk, HQ), F32)],
        compiler_params=_params(1), name=name,
    )(s, ksum, q, k_new_pad, slopes_hq)


def _moba_pv_kernel(pt_ref, p_ref, v_ref, pn_ref, vn_ref, o_ref, acc, *, nq, n_pg):
    pg = pl.program_id(1)

    match = _head_match(PAGE_SIZE, nq)

    def page_pv(p, v3):
        p2 = jnp.broadcast_to(p.astype(F32)[:, None, :], (PAGE_SIZE, MOBA_H, MOBA_H * nq))
        p2 = jnp.where(match, p2.reshape(PAGE_SIZE * MOBA_H, MOBA_H * nq), 0.0)
        return _dot1(p2, v3.reshape(PAGE_SIZE * MOBA_H, MOBA_DH), _TN)

    @pl.when(pg == 0)
    def _():
        acc[...] = page_pv(pn_ref[0], vn_ref[0])

    acc[...] += page_pv(p_ref[0, 0], v_ref[0, 0])

    @pl.when(pg == n_pg - 1)
    def _():
        a = acc[...]
        o_ref[0] = jnp.concatenate([a[h * nq:(h + 1) * nq, :] for h in range(MOBA_H)], axis=-1)


def _moba_pv(name, page_table, p, cache_v, pn, v_new_pad, nq, layer):
    B, n_pg, _, HQ = p.shape
    D = D_MODEL
    gs = pltpu.PrefetchScalarGridSpec(
        num_scalar_prefetch=1, grid=(B, n_pg),
        in_specs=[pl.BlockSpec((1, 1, PAGE_SIZE, HQ), lambda b, g, pt: (b, g, 0, 0)),
                  pl.BlockSpec((1, 1, PAGE_SIZE, MOBA_H, MOBA_DH), lambda b, g, pt: (layer, pt[b, g], 0, 0, 0)),
                  pl.BlockSpec((1, PAGE_SIZE, HQ), lambda b, g, pt: (b, 0, 0)),
                  pl.BlockSpec((1, PAGE_SIZE, MOBA_H, MOBA_DH), lambda b, g, pt: (b, 0, 0, 0))],
        out_specs=pl.BlockSpec((1, nq, D), lambda b, g, pt: (b, 0, 0)),
        scratch_shapes=[pltpu.VMEM((HQ, MOBA_DH), F32)])
    return pl.pallas_call(
        functools.partial(_moba_pv_kernel, nq=nq, n_pg=n_pg), grid_spec=gs,
        out_shape=jax.ShapeDtypeStruct((B, nq, D), F32), compiler_params=_params(2), name=name,
    )(page_table, p, cache_v, pn, v_new_pad)


def _ident(x):
    return x


def _mix(x, xp, mu):
    return x + (xp - x) * mu


def _pad_cols(w, n):
    return jnp.pad(w, ((0, 0), (0, n - w.shape[1])))


def _pad_rows(w, n):
    return jnp.pad(w, ((0, n - w.shape[0]), (0, 0)))


def _rwkv_layer(tag, x3, shift0, s0, v_first, P, j, ln_g, ln_b):
    B, T, D = x3.shape
    M = B * T
    x = x3.reshape(M, D)
    xp = jnp.concatenate([shift0[:, None, :], x3[:, :-1]], axis=1).reshape(M, D)
    mu = P['rwkv_mu'][j]
    mus = [mu[n][None, :] for n in range(6)]
    wb = lambda w: w.astype(BF16)
    one = lambda acc: (acc,)
    rows = [x, xp]
    r, = _mm(tag + 'r', rows, [mus[0]], [wb(P['rwkv_w_rkv'][j, 0])], [], [], _mix, one, [F32])
    k, = _mm(tag + 'k', rows, [mus[2]], [wb(P['rwkv_w_rkv'][j, 1])], [], [], _mix, one, [F32])
    v, = _mm(tag + 'v', rows, [mus[3]], [wb(P['rwkv_w_rkv'][j, 2])], [], [], _mix, one, [F32])
    lp = LANES
    hw, = _mm(tag + 'w1', rows, [mus[1]], [wb(_pad_cols(P['rwkv_w1'][j], lp))], [], [], _mix,
              lambda acc: (jnp.tanh(acc),), [F32])
    lw, = _mm(tag + 'w2', [hw], [], [wb(_pad_rows(P['rwkv_w2'][j], lp))], [], [P['rwkv_w0'][j][None, :]], _ident,
              lambda acc, w0: (-jnp.exp(-_softplus(-(w0 + acc)) - 0.5),), [F32])
    ha, = _mm(tag + 'a1', rows, [mus[4]], [wb(_pad_cols(P['rwkv_a1'][j], lp))], [], [], _mix, one, [F32])
    a, = _mm(tag + 'a2', [ha], [], [wb(_pad_rows(P['rwkv_a2'][j], lp))], [], [P['rwkv_a0'][j][None, :]], _ident,
             lambda acc, a0: (_sigmoid(a0 + acc),), [F32])
    hg, = _mm(tag + 'g1', rows, [mus[5]], [wb(P['rwkv_g1'][j])], [], [], _mix,
              lambda acc: (_sigmoid(acc),), [F32])
    g, = _mm(tag + 'g2', [hg], [], [wb(P['rwkv_g2'][j])], [], [], _ident, one, [F32])
    if j == 0:
        v_first = v
    else:
        hv, = _mm(tag + 'v1', rows, [mus[3]], [wb(_pad_cols(P['rwkv_v1'][j - 1], lp))], [], [], _mix, one, [F32])
        v, = _mm(tag + 'v2', [hv], [], [wb(_pad_rows(P['rwkv_v2'][j - 1], lp))], [v, v_first],
                 [P['rwkv_v0'][j - 1][None, :]], _ident,
                 lambda acc, v_, vf, v0: (v_ + (vf - v_) * _sigmoid(v0 + acc),), [F32])
    Tp = -(-T // WKV_CHUNK) * WKV_CHUNK
    seq = lambda t: jnp.pad(t.reshape(B, T, D), ((0, 0), (0, Tp - T), (0, 0)))
    vecp = lambda name: P[name][j].reshape(1, D)
    z, s_new = _wkv(tag + 'wkv', seq(r), seq(lw), seq(k), seq(v), seq(a), vecp('rwkv_k_k'), vecp('rwkv_k_a'),
                    vecp('rwkv_r_k'), vecp('rwkv_lnx_g'), vecp('rwkv_lnx_b'), s0)
    z = z[:, :T].reshape(M, D)
    xn = _mm_ln(tag + 'wo', [z, g], wb(P['rwkv_wo'][j]), x, ln_g, ln_b, lambda z_, g_: z_ * g_)
    return xn, s_new, x3[:, -1], v_first


def _s5_weights(P, j):
    ns, gs = D_MODEL // S5_SLAB, S5_SLAB // S5_GROUP
    eye = jnp.eye(gs, dtype=F32)

    def bd_in(b):
        bt = jnp.swapaxes(b.reshape(ns, gs, S5_P, S5_GROUP), 2, 3)
        return jnp.einsum('sgcp,gh->sgchp', bt, eye).reshape(ns, S5_SLAB, S5_STATES)

    def bd_out(c):
        ct = jnp.swapaxes(c.reshape(ns, gs, S5_GROUP, S5_P), 2, 3)
        return jnp.einsum('sgpc,gh->sgphc', ct, eye).reshape(ns, S5_STATES, S5_SLAB)

    wb = jnp.concatenate([bd_in(P['s5_b_re'][j]), bd_in(P['s5_b_im'][j])], axis=2)
    wc = jnp.concatenate([bd_out(P['s5_c_re'][j]), -bd_out(P['s5_c_im'][j])], axis=1)
    flat = lambda t: t.reshape(1, 1, S5_G * S5_P)
    dt = jnp.broadcast_to(jnp.exp(P['s5_log_dt'][j])[:, None], (S5_G, S5_P))
    return wb, wc, flat(P['s5_lam_re'][j]), flat(P['s5_lam_im'][j]), flat(dt)


def _s5_layer(tag, x3, h0r, h0i, P, j, ln_g, ln_b):
    B, T, D = x3.shape
    M = B * T
    x = x3.reshape(M, D)
    wb, wc, lr, li, dt = _s5_weights(P, j)
    st = lambda h: h.reshape(B, 1, S5_G * S5_P)
    z, hr, hi = _s5(tag + 'scan', x3, wb, wc, lr, li, dt, P['s5_d'][j][None, :], st(h0r), st(h0i))
    out, = _mm(tag + 'gate', [z.reshape(M, D)], [], [P['s5_w_val'][j].astype(BF16), P['s5_w_gate'][j].astype(BF16)],
               [], [], _ident, lambda val, gate: (val * _sigmoid(gate),), [F32])
    xn = _ln(tag + 'ln', x, out, ln_g, ln_b)
    return xn, hr.reshape(B, S5_G, S5_P), hi.reshape(B, S5_G, S5_P)


def _alibi_slopes():
    return 2.0 ** (-8.0 * jnp.arange(1, MOBA_H + 1, dtype=F32) / MOBA_H)


def _moba_layer(tag, x3, cache_k, cache_v, page_table, P, j, ln_g, ln_b):
    B, T, D = x3.shape
    M = B * T
    x = x3.reshape(M, D)
    wqkv = P['moba_w_qkv'][j]
    ws = [wqkv[:, n * D:(n + 1) * D].astype(BF16) for n in range(3)]
    q, k, v = _mm(tag + 'qkv', [x], [], ws, [], [], _ident, lambda a, b, c: (a, b, c), [F32, F32, F32])
    q3, k3, v3 = (t.reshape(B, T, D) for t in (q, k, v))
    slopes = _alibi_slopes()
    if cache_k is None:
        sl = jnp.broadcast_to(slopes[:, None, None], (MOBA_H, 1, LANES))
        o = _moba_prompt(tag + 'attn', q3, k3, v3, sl)
    else:
        n_pg = page_table.shape[1]
        past = n_pg * PAGE_SIZE
        assert past % MOBA_BLOCK == 0 and T <= PAGE_SIZE and T <= MOBA_BLOCK
        HQ = MOBA_H * T
        s, ksum = _moba_scores(tag + 'scores', page_table, q3, cache_k, j)
        sl = jnp.repeat(slopes, T)[None, :]
        pad = lambda t: jnp.pad(t.reshape(B, T, MOBA_H, MOBA_DH), ((0, 0), (0, PAGE_SIZE - T), (0, 0), (0, 0)))
        p, pn = _moba_probs(tag + 'probs', s.reshape(B, past // MOBA_BLOCK, MOBA_BLOCK, HQ), ksum, q3, pad(k3),
                            sl, past)
        o = _moba_pv(tag + 'pv', page_table, p.reshape(B, n_pg, PAGE_SIZE, HQ), cache_v, pn, pad(v3), T, j)
    xn = _mm_ln(tag + 'wo', [o.reshape(M, D)], P['moba_wo'][j].astype(BF16), x, ln_g, ln_b, _ident)
    return xn, k3.reshape(B, T, MOBA_H, MOBA_DH), v3.reshape(B, T, MOBA_H, MOBA_DH)


def _ffn_layer(tag, x3, conv0, P, i, ln_g, ln_b):
    B, T, D = x3.shape
    w_in = P['ffn_w_in'][i]
    h, cst = _ffn_in(tag + 'in', x3, w_in[:, :D_FF].astype(BF16), w_in[:, D_FF:].astype(BF16), conv0,
                     P['ffn_conv_w'][i], P['ffn_conv_b'][i][None, :])
    xn = _mm_ln(tag + 'down', [h], P['ffn_w_down'][i].astype(BF16), x3.reshape(B * T, D), ln_g, ln_b, _ident)
    return xn.reshape(B, T, D), cst


def _trunk(grp, x3, wkv0, shift0, s5re0, s5im0, conv0, cache_k, cache_v, page_table, P):
    B, T, D = x3.shape
    wkv_new, shift_new, s5re_new, s5im_new, k_new, v_new, conv_new = [], [], [], [], [], [], []
    v_first = None
    for i in range(DEPTH):
        kind, j = i % 3, i // 3
        tag = f'{grp}{i}_'
        g0, b0 = P['ln_g'][i, 0][None, :], P['ln_b'][i, 0][None, :]
        g1, b1 = P['ln_g'][i, 1][None, :], P['ln_b'][i, 1][None, :]
        if kind == 0:
            xn, S, last, v_first = _rwkv_layer(tag, x3, shift0[j], wkv0[j], v_first, P, j, g0, b0)
            wkv_new.append(S)
            shift_new.append(last)
        elif kind == 1:
            xn, hr, hi = _s5_layer(tag, x3, s5re0[j], s5im0[j], P, j, g0, b0)
            s5re_new.append(hr)
            s5im_new.append(hi)
        else:
            xn, kr, vr = _moba_layer(tag, x3, cache_k, cache_v, page_table, P, j, g0, b0)
            k_new.append(kr)
            v_new.append(vr)
        x3, cst = _ffn_layer(tag + 'ffn_', xn.reshape(B, T, D), conv0[i], P, i, g1, b1)
        conv_new.append(cst)
    return (x3, jnp.stack(wkv_new), jnp.stack(shift_new), jnp.stack(s5re_new), jnp.stack(s5im_new),
            jnp.stack(conv_new), jnp.stack(k_new), jnp.stack(v_new))


def kernel(x_prompt, x_sample, state_rwkv_wkv, state_rwkv_shift, state_s5_re, state_s5_im, state_ffn_conv, cache_k, cache_v, page_table, ln_g, ln_b, rwkv_mu, rwkv_w_rkv, rwkv_w0, rwkv_w1, rwkv_w2, rwkv_a0, rwkv_a1, rwkv_a2, rwkv_v0, rwkv_v1, rwkv_v2, rwkv_g1, rwkv_g2, rwkv_k_k, rwkv_k_a, rwkv_r_k, rwkv_lnx_g, rwkv_lnx_b, rwkv_wo, s5_log_dt, s5_lam_re, s5_lam_im, s5_b_re, s5_b_im, s5_c_re, s5_c_im, s5_d, s5_w_val, s5_w_gate, moba_w_qkv, moba_wo, ffn_w_in, ffn_conv_w, ffn_conv_b, ffn_w_down):
    P = dict(ln_g=ln_g, ln_b=ln_b, rwkv_mu=rwkv_mu, rwkv_w_rkv=rwkv_w_rkv, rwkv_w0=rwkv_w0, rwkv_w1=rwkv_w1,
             rwkv_w2=rwkv_w2, rwkv_a0=rwkv_a0, rwkv_a1=rwkv_a1, rwkv_a2=rwkv_a2, rwkv_v0=rwkv_v0,
             rwkv_v1=rwkv_v1, rwkv_v2=rwkv_v2, rwkv_g1=rwkv_g1, rwkv_g2=rwkv_g2, rwkv_k_k=rwkv_k_k,
             rwkv_k_a=rwkv_k_a, rwkv_r_k=rwkv_r_k.reshape(rwkv_r_k.shape[0], D_MODEL), rwkv_lnx_g=rwkv_lnx_g,
             rwkv_lnx_b=rwkv_lnx_b, rwkv_wo=rwkv_wo, s5_log_dt=s5_log_dt, s5_lam_re=s5_lam_re,
             s5_lam_im=s5_lam_im, s5_b_re=s5_b_re, s5_b_im=s5_b_im, s5_c_re=s5_c_re, s5_c_im=s5_c_im, s5_d=s5_d,
             s5_w_val=s5_w_val, s5_w_gate=s5_w_gate, moba_w_qkv=moba_w_qkv, moba_wo=moba_wo,
             ffn_w_in=ffn_w_in, ffn_conv_w=ffn_conv_w, ffn_conv_b=ffn_conv_b, ffn_w_down=ffn_w_down)
    B = x_prompt.shape[0]
    n_rwkv, n_s5 = state_rwkv_wkv.shape[0], state_s5_re.shape[0]
    zeros = lambda *s: jnp.zeros(s, F32)
    outs_p = _trunk('p', x_prompt, zeros(n_rwkv, B, RWKV_H, RWKV_N, RWKV_N), zeros(n_rwkv, B, D_MODEL),
                    zeros(n_s5, B, S5_G, S5_P), zeros(n_s5, B, S5_G, S5_P), zeros(DEPTH, B, CONV_W - 1, D_FF),
                    None, None, None, P)
    outs_s = _trunk('s', x_sample, state_rwkv_wkv, state_rwkv_shift, state_s5_re, state_s5_im, state_ffn_conv,
                    cache_k, cache_v, page_table, P)
    return (outs_p[0], outs_s[0]) + tuple(outs_p[1:]) + tuple(outs_s[1:])
```

```python
import functools
import math

import jax
import jax.numpy as jnp
from jax import lax
from jax.experimental import pallas as pl
from jax.experimental.pallas import tpu as pltpu

F32 = jnp.float32
BF16 = jnp.bfloat16

D_MODEL = 2048
DEPTH = 4
RWKV_N = 64
RWKV_H = D_MODEL // RWKV_N
RWKV_GN_EPS = 64e-5
S5_GROUP = 16
S5_G = D_MODEL // S5_GROUP
S5_P = 64
MOBA_H = 16
MOBA_DH = D_MODEL // MOBA_H
MOBA_BLOCK = 256
MOBA_TOPK = 3
PAGE_SIZE = 128
D_FF = 11 * D_MODEL // 4
CONV_W = 3
LN_EPS = 1e-5
DN_ALPHA = (2.0 * DEPTH) ** 0.25
NEG = -1e30

LANES = 128
SUBLANES = 8
VMEM_LIMIT_BYTES = 56 * 1024 * 1024

MM_LN_WEIGHT_TILE_BYTES = 12 * 1024 * 1024
WKV_CHUNK = 64
WKV_CHAINS = 16
WKV_HEADS = 4
MOBA_PAGES_PER_STEP = 4
S5_SLAB = LANES
S5_STATES = (S5_SLAB // S5_GROUP) * S5_P


def _params(n_axes):
    return pltpu.CompilerParams(dimension_semantics=("arbitrary",) * n_axes,
                                vmem_limit_bytes=VMEM_LIMIT_BYTES)


def _sigmoid(x):
    return 1.0 / (1.0 + jnp.exp(-x))


def _softplus(x):
    return jnp.maximum(x, 0.0) + jnp.log1p(jnp.exp(-jnp.abs(x)))


def _dg(a, b, dims):
    return lax.dot_general(a, b, (dims, ((), ())), preferred_element_type=F32)


_NN = ((1,), (0,))
_NT = ((1,), (1,))
_TN = ((0,), (0,))


def _split(a):
    hi = a.astype(BF16)
    lo = (a - hi.astype(F32)).astype(BF16)
    return hi, lo


def _dot3(a, b, dims=_NN):
    ah, al = _split(a)
    bh, bl = _split(b)
    return _dg(ah, bh, dims) + (_dg(al, bh, dims) + _dg(ah, bl, dims))


def _dot1(a, b, dims=_NN):
    return _dg(a.astype(BF16), b.astype(BF16), dims)


def _dot_exact_lhs(lhs_bf16, x):
    x1 = x.astype(BF16)
    r1 = x - x1.astype(F32)
    x2 = r1.astype(BF16)
    x3 = (r1 - x2.astype(F32)).astype(BF16)
    return _dg(lhs_bf16, x1, _NN) + (_dg(lhs_bf16, x2, _NN) + _dg(lhs_bf16, x3, _NN))


def _layer_norm(y, g, b):
    mu = jnp.mean(y, axis=-1, keepdims=True)
    yc = y - mu
    var = jnp.mean(yc * yc, axis=-1, keepdims=True)
    return yc * lax.rsqrt(var + LN_EPS) * g + b


def _mm_kernel(*refs, n_row, n_vec, n_w, n_erow, n_evec, n_out, prologue, epilogue):
    it = iter(refs)
    rows = [next(it) for _ in range(n_row)]
    vecs = [next(it) for _ in range(n_vec)]
    ws = [next(it) for _ in range(n_w)]
    erows = [next(it) for _ in range(n_erow)]
    evecs = [next(it) for _ in range(n_evec)]
    outs = [next(it) for _ in range(n_out)]
    a_scr = next(it)

    @pl.when(pl.program_id(1) == 0)
    def _():
        a_scr[...] = prologue(*[r[...] for r in rows], *[v[...] for v in vecs]).astype(BF16)

    a = a_scr[...]
    accs = [jnp.dot(a, w[...], preferred_element_type=F32) for w in ws]
    res = epilogue(*accs, *[e[...] for e in erows], *[e[...] for e in evecs])
    for o_ref, o in zip(outs, res):
        o_ref[...] = o.astype(o_ref.dtype)


def _mm(name, rows, vecs, ws, erows, evecs, prologue, epilogue, out_dtypes):
    M, K = rows[0].shape
    N = ws[0].shape[1]
    tm = min(512, M)
    tn = 512 if N % 512 == 0 else N
    assert M % tm == 0 and N % tn == 0
    in_specs = ([pl.BlockSpec((tm, K), lambda i, j: (i, 0)) for _ in rows]
                + [pl.BlockSpec((1, K), lambda i, j: (0, 0)) for _ in vecs]
                + [pl.BlockSpec((K, tn), lambda i, j: (0, j)) for _ in ws]
                + [pl.BlockSpec((tm, tn), lambda i, j: (i, j)) for _ in erows]
                + [pl.BlockSpec((1, tn), lambda i, j: (0, j)) for _ in evecs])
    out_specs = [pl.BlockSpec((tm, tn), lambda i, j: (i, j)) for _ in out_dtypes]
    out_shape = [jax.ShapeDtypeStruct((M, N), dt) for dt in out_dtypes]
    kern = functools.partial(_mm_kernel, n_row=len(rows), n_vec=len(vecs), n_w=len(ws), n_erow=len(erows),
                             n_evec=len(evecs), n_out=len(out_dtypes), prologue=prologue, epilogue=epilogue)
    return pl.pallas_call(
        kern, grid=(M // tm, N // tn), in_specs=in_specs, out_specs=out_specs, out_shape=out_shape,
        scratch_shapes=[pltpu.VMEM((tm, K), BF16)], compiler_params=_params(2), name=name,
    )(*rows, *vecs, *ws, *erows, *evecs)


def _mm_ln_kernel(*refs, n_row, prologue, n_k):
    it = iter(refs)
    rows = [next(it) for _ in range(n_row)]
    w_ref, x_ref, g_ref, b_ref, o_ref, acc = (next(it) for _ in range(6))
    k = pl.program_id(1)
    a = prologue(*[r[...] for r in rows]).astype(BF16)
    p = jnp.dot(a, w_ref[...], preferred_element_type=F32)
    if n_k == 1:
        o_ref[...] = _layer_norm(DN_ALPHA * x_ref[...] + p, g_ref[...], b_ref[...])
        return

    @pl.when(k == 0)
    def _():
        acc[...] = p

    @pl.when(jnp.logical_and(k > 0, k < n_k - 1))
    def _():
        acc[...] += p

    @pl.when(k == n_k - 1)
    def _():
        o_ref[...] = _layer_norm(DN_ALPHA * x_ref[...] + (acc[...] + p), g_ref[...], b_ref[...])


def _mm_ln(name, rows, w, x, g, b, prologue):
    M, K = rows[0].shape
    N = w.shape[1]
    n_k = -(-K * N * 2 // MM_LN_WEIGHT_TILE_BYTES)
    tk = K // n_k
    tm = min(512 if n_k > 1 else 256, M)
    assert M % tm == 0 and K % n_k == 0 and tk % LANES == 0
    in_specs = ([pl.BlockSpec((tm, tk), lambda i, k: (i, k)) for _ in rows]
                + [pl.BlockSpec((tk, N), lambda i, k: (k, 0)),
                   pl.BlockSpec((tm, N), lambda i, k: (i, 0)),
                   pl.BlockSpec((1, N), lambda i, k: (0, 0)),
                   pl.BlockSpec((1, N), lambda i, k: (0, 0))])
    kern = functools.partial(_mm_ln_kernel, n_row=len(rows), prologue=prologue, n_k=n_k)
    return pl.pallas_call(
        kern, grid=(M // tm, n_k), in_specs=in_specs,
        out_specs=pl.BlockSpec((tm, N), lambda i, k: (i, 0)),
        out_shape=jax.ShapeDtypeStruct((M, N), F32),
        scratch_shapes=[pltpu.VMEM((tm, N), F32)], compiler_params=_params(2), name=name,
    )(*rows, w, x, g, b)


def _ln_kernel(x_ref, h_ref, g_ref, b_ref, o_ref):
    o_ref[...] = _layer_norm(DN_ALPHA * x_ref[...] + h_ref[...], g_ref[...], b_ref[...])


def _ln(name, x, h, g, b):
    M, N = x.shape
    tm = min(512, M)
    row = pl.BlockSpec((tm, N), lambda i: (i, 0))
    vec = pl.BlockSpec((1, N), lambda i: (0, 0))
    return pl.pallas_call(_ln_kernel, grid=(M // tm,), in_specs=[row, row, vec, vec], out_specs=row,
                          out_shape=jax.ShapeDtypeStruct((M, N), F32), compiler_params=_params(1), name=name)(x, h, g, b)


def _ffn_in_kernel(x_ref, wg_ref, wv_ref, c0_ref, cw_ref, cb_ref, h_ref, cs_ref, a_scr, carry, *, bt, tt):
    t = pl.program_id(1)
    j = pl.program_id(2)
    rows = bt * tt
    tn = wg_ref.shape[1]

    @pl.when(j == 0)
    def _():
        a_scr[...] = x_ref[...].reshape(rows, x_ref.shape[2]).astype(BF16)

    @pl.when(t == 0)
    def _():
        carry[j] = c0_ref[...]

    a = a_scr[...]
    hg = jnp.dot(a, wg_ref[...], preferred_element_type=F32)
    hv = jnp.dot(a, wv_ref[...], preferred_element_type=F32)
    prev = carry[j]
    p2 = jnp.broadcast_to(prev[:, 0:1, :], (bt, tt, tn)).reshape(rows, tn)
    p1 = jnp.broadcast_to(prev[:, 1:2, :], (bt, tt, tn)).reshape(rows, tn)
    r = lax.broadcasted_iota(jnp.int32, (rows, tn), 0) % tt
    s1 = jnp.where(r == 0, p1, pltpu.roll(hg, 1, axis=0))
    s2 = jnp.where(r == 0, p2, jnp.where(r == 1, p1, pltpu.roll(hg, 2, axis=0)))
    cw = cw_ref[...]
    c = cb_ref[...] + cw[0:1, :] * s2 + cw[1:2, :] * s1 + cw[2:3, :] * hg
    h_ref[...] = (c * _sigmoid(c) * hv).astype(h_ref.dtype)
    last = hg.reshape(bt, tt, tn)[:, tt - 2:, :]
    carry[j] = last
    cs_ref[0] = last


def _ffn_in(name, x3, wg, wv, conv0, cw, cb):
    B, T, K = x3.shape
    F = wg.shape[1]
    bt, tt = (1, 512) if T >= 512 else (B, T)
    tn = 512
    assert B % bt == 0 and T % tt == 0 and F % tn == 0 and tt >= CONV_W - 1
    n_j = F // tn
    kern = functools.partial(_ffn_in_kernel, bt=bt, tt=tt)
    n_t = T // tt
    h, cst = pl.pallas_call(
        kern, grid=(B // bt, n_t, n_j),
        in_specs=[pl.BlockSpec((bt, tt, K), lambda b, t, j: (b, t, 0)),
                  pl.BlockSpec((K, tn), lambda b, t, j: (0, j)),
                  pl.BlockSpec((K, tn), lambda b, t, j: (0, j)),
                  pl.BlockSpec((bt, CONV_W - 1, tn), lambda b, t, j: (b, 0, j)),
                  pl.BlockSpec((CONV_W, tn), lambda b, t, j: (0, j)),
                  pl.BlockSpec((1, tn), lambda b, t, j: (0, j))],
        out_specs=[pl.BlockSpec((bt * tt, tn), lambda b, t, j: (b * n_t + t, j)),
                   pl.BlockSpec((1, bt, CONV_W - 1, tn), lambda b, t, j: (t, b, 0, j))],
        out_shape=[jax.ShapeDtypeStruct((B * T, F), BF16),
                   jax.ShapeDtypeStruct((n_t, B, CONV_W - 1, F), F32)],
        scratch_shapes=[pltpu.VMEM((bt * tt, K), BF16), pltpu.VMEM((n_j, bt, CONV_W - 1, tn), F32)],
        compiler_params=_params(3), name=name,
    )(x3, wg, wv, conv0, cw, cb)
    return h, cst[n_t - 1]


RWKV_PROJ_TN = 512
RWKV_PROJ_TILES = D_MODEL // RWKV_PROJ_TN


def _rwkv_proj_kernel(*refs, has_vgate):
    it = iter(refs)
    x_ref, xp_ref, mu_ref, wm_ref, w1_ref, a1_ref, g1_ref, w2_ref, a2_ref, g2_ref, w0_ref, a0_ref = (
        next(it) for _ in range(12))
    if has_vgate:
        v1_ref, v2_ref, v0_ref, vf_ref = (next(it) for _ in range(4))
    r_ref, k_ref, v_ref, lw_ref, a_ref, g_ref = (next(it) for _ in range(6))
    mix_scr, hw_scr, ha_scr, hg_scr = (next(it) for _ in range(4))
    if has_vgate:
        hv_scr = next(it)
    j = pl.program_id(1)
    dot = lambda a, b: jnp.dot(a, b, preferred_element_type=F32)

    @pl.when(j == 0)
    def _():
        x = x_ref[...]
        d = xp_ref[...] - x
        mix = lambda n: (x + d * mu_ref[n:n + 1, :]).astype(BF16)
        mix_scr[0] = mix(0)
        mix_scr[1] = mix(2)
        xv = mix(3)
        mix_scr[2] = xv
        hw_scr[...] = jnp.tanh(dot(mix(1), w1_ref[...])).astype(BF16)
        ha_scr[...] = dot(mix(4), a1_ref[...]).astype(BF16)
        hg_scr[...] = _sigmoid(dot(mix(5), g1_ref[...])).astype(BF16)
        if has_vgate:
            hv_scr[...] = dot(xv, v1_ref[...]).astype(BF16)

    grp = j // RWKV_PROJ_TILES

    @pl.when(grp == 0)
    def _():
        r_ref[...] = dot(mix_scr[0], wm_ref[...])
        lw_ref[...] = -jnp.exp(-_softplus(-(w0_ref[...] + dot(hw_scr[...], w2_ref[...]))) - 0.5)

    @pl.when(grp == 1)
    def _():
        k_ref[...] = dot(mix_scr[1], wm_ref[...])
        a_ref[...] = _sigmoid(a0_ref[...] + dot(ha_scr[...], a2_ref[...]))

    @pl.when(grp == 2)
    def _():
        v = dot(mix_scr[2], wm_ref[...])
        if has_vgate:
            v = v + (vf_ref[...] - v) * _sigmoid(v0_ref[...] + dot(hv_scr[...], v2_ref[...]))
        v_ref[...] = v
        g_ref[...] = dot(hg_scr[...], g2_ref[...])


def _rwkv_proj(name, x, xp, mu, w_main, lora1, lora2, bias, vgate):
    M, K = x.shape
    tm, tn, nt = min(512, M), RWKV_PROJ_TN, RWKV_PROJ_TILES
    assert M % tm == 0 and w_main.shape == (K, 3 * D_MODEL)
    row = pl.BlockSpec((tm, K), lambda i, j: (i, 0))
    full = lambda a: pl.BlockSpec(a.shape, lambda i, j: (0, 0))
    col = lambda a: pl.BlockSpec((a.shape[0], tn), lambda i, j: (0, j % nt))
    ins = [x, xp, mu, w_main, *lora1, *lora2, *bias]
    specs = ([row, row, full(mu), pl.BlockSpec((K, tn), lambda i, j: (0, j))] + [full(w) for w in lora1]
             + [col(w) for w in lora2] + [col(b) for b in bias])
    scratch = [pltpu.VMEM((3, tm, K), BF16)] + [pltpu.VMEM((tm, w.shape[1]), BF16) for w in lora1]
    if vgate is not None:
        v1, v2, v0, v_first = vgate
        ins += [v1, v2, v0, v_first]
        specs += [full(v1), col(v2), col(v0), pl.BlockSpec((tm, tn), lambda i, j: (i, j % nt))]
        scratch.append(pltpu.VMEM((tm, v1.shape[1]), BF16))
    out = lambda grp: pl.BlockSpec((tm, tn), lambda i, j: (i, jnp.clip(j - grp * nt, 0, nt - 1)))
    return pl.pallas_call(
        functools.partial(_rwkv_proj_kernel, has_vgate=vgate is not None), grid=(M // tm, 3 * nt),
        in_specs=specs, out_specs=[out(0), out(1), out(2), out(0), out(1), out(2)],
        out_shape=[jax.ShapeDtypeStruct((M, D_MODEL), F32)] * 6,
        scratch_shapes=scratch, compiler_params=_params(2), name=name,
    )(*ins)


def _wkv_chunks(r, lw, k, v, a, prm, S, consts):
    C = WKV_CHUNK
    bdmask, l_incl, strict, incl, eye = consts

    def bd(xp):
        return jnp.concatenate([xp.astype(BF16)] * WKV_HEADS, axis=0) * bdmask

    def headsum(xp):
        hi, lo = _split(xp)
        return _dg(hi, bdmask, _NN) + _dg(lo, bdmask, _NN)

    def each(f, *lists):
        return [f(*args) for args in zip(*lists)]

    kkp, kap, rkp, lng, lnb = (list(t) for t in zip(*prm))
    kkn = each(lambda k_, p_: k_ * p_, k, kkp)
    kp = each(lambda k_, a_, p_: k_ * (1.0 + (a_ - 1.0) * p_), k, a, kap)
    sums = each(lambda kkn_, r_, kp_, p_: headsum(jnp.concatenate([kkn_ * kkn_, r_ * kp_ * p_], axis=0)),
                kkn, r, kp, rkp)
    kk = each(lambda kkn_, s_: kkn_ / jnp.maximum(jnp.sqrt(s_[:C]), 1e-12), kkn, sums)
    bonus = each(lambda s_, v_: s_[C:] * v_, sums, v)
    b = each(lambda kk_, a_: kk_ * a_, kk, a)
    cum = each(lambda lw_: _dot_exact_lhs(l_incl, lw_), lw)
    e_neg = each(lambda c_: jnp.exp(-c_), cum)
    x = each(lambda kk_, c_, lw_, r_: jnp.concatenate([kk_ * jnp.exp(c_ - lw_), r_ * jnp.exp(c_)],
                                                       axis=0).astype(BF16), kk, cum, lw, r)
    g_b = each(lambda x_, b_, e_: _dg(x_, bd(b_ * e_), _NT), x, b, e_neg)
    g_k = each(lambda x_, k_, e_: _dg(x_, bd(k_ * e_), _NT), x, kp, e_neg)
    xs = each(lambda x_, s_: _dg(x_, s_.astype(BF16), _NT), x, S)
    mb = each(lambda g_: jnp.where(incl, g_[C:], 0.0), g_b)
    kv = each(lambda g_, v_: _dg(jnp.concatenate([jnp.where(strict, g_[:C], 0.0), jnp.where(incl, g_[C:], 0.0)],
                                                 axis=0).astype(BF16), bd(v_), _NN), g_k, v)
    rhs = each(lambda xs_, kv_: -(xs_[:C] + kv_[:C]), xs, kv)
    npow = each(lambda g_: -jnp.where(strict, g_[:C], 0.0), g_b)
    tinv = each(lambda n_: eye + n_, npow)
    npow = each(lambda n_: _dg(n_.astype(BF16), bd(n_), _NN), npow)
    for _ in range(int(math.log2(C)) - 2):
        both = each(lambda t_, n_: _dg(jnp.concatenate([t_, n_], axis=0).astype(BF16), bd(n_), _NN), tinv, npow)
        tinv = each(lambda t_, b_: t_ + b_[:C], tinv, both)
        npow = each(lambda b_: b_[C:], both)
    tinv = each(lambda t_, n_: t_ + _dg(t_.astype(BF16), bd(n_), _NN), tinv, npow)
    u = each(lambda t_, r_: _dg(t_.astype(BF16), bd(r_), _NN), tinv, rhs)
    y = each(lambda xs_, kv_, mb_, u_: xs_[C:] + kv_[C:] + _dg(mb_.astype(BF16), bd(u_), _NN), xs, kv, mb, u)
    upd = each(lambda u_, v_, b_, k_, c_: _dg(
        jnp.concatenate([u_, v_], axis=0).astype(BF16),
        (jnp.concatenate([b_, k_], axis=0) * jnp.exp(c_[C - 1:C, :] - jnp.concatenate([c_, c_], axis=0))).astype(BF16),
        _TN), u, v, b, kp, cum)
    bdmask_f = bdmask.astype(F32)
    s_new = each(lambda s_, c_, u_: s_ * jnp.exp(c_[C - 1:C, :]) + u_ * bdmask_f, S, cum, upd)
    mu = each(lambda y_: headsum(y_) * (1.0 / RWKV_N), y)
    yc = each(lambda y_, m_: y_ - m_, y, mu)
    var = each(lambda yc_: headsum(yc_ * yc_) * (1.0 / RWKV_N), yc)
    z = each(lambda yc_, var_, g_, b_, bo_: yc_ * lax.rsqrt(var_ + RWKV_GN_EPS) * g_ + b_ + bo_,
             yc, var, lng, lnb, bonus)
    return z, s_new


def _wkv_kernel(r_ref, lw_ref, k_ref, v_ref, a_ref, kk_ref, ka_ref, rk_ref, g_ref, b_ref, s0_ref,
                z_ref, so_ref, s_scr, *, n_t, nb, n_grp):
    t = pl.program_id(1)
    C, N, HW = WKV_CHUNK, RWKV_N, WKV_HEADS * RWKV_N
    ri = lax.broadcasted_iota(jnp.int32, (HW, HW), 0) // N
    ci = lax.broadcasted_iota(jnp.int32, (HW, HW), 1) // N
    bdmask_f = (ri == ci).astype(F32)

    chains = [(bi, g) for bi in range(nb) for g in range(n_grp)]

    @pl.when(t == 0)
    def _():
        for n, (bi, g) in enumerate(chains):
            rows = s0_ref[bi, g * WKV_HEADS:(g + 1) * WKV_HEADS].reshape(HW, N)
            s_scr[n] = jnp.concatenate([rows] * WKV_HEADS, axis=1) * bdmask_f

    tt = lax.broadcasted_iota(jnp.int32, (C, C), 0)
    ss = lax.broadcasted_iota(jnp.int32, (C, C), 1)
    tp = lax.broadcasted_iota(jnp.int32, (C, HW), 0)
    sp = lax.broadcasted_iota(jnp.int32, (C, HW), 1) % N
    consts = (bdmask_f.astype(BF16), (tt >= ss).astype(BF16), tp > sp, tp >= sp, (tp == sp).astype(F32))
    sls = [slice(g * HW, (g + 1) * HW) for _, g in chains]
    tok = lambda ref: [ref[bi, :, sl] for (bi, _), sl in zip(chains, sls)]
    prm = [tuple(ref[:, sl] for ref in (kk_ref, ka_ref, rk_ref, g_ref, b_ref)) for sl in sls]
    zs, s_news = _wkv_chunks(tok(r_ref), tok(lw_ref), tok(k_ref), tok(v_ref), tok(a_ref), prm,
                             [s_scr[n] for n in range(len(chains))], consts)
    for n, ((bi, _), sl) in enumerate(zip(chains, sls)):
        s_scr[n] = s_news[n]
        z_ref[bi, :, sl] = zs[n]

    @pl.when(t == n_t - 1)
    def _():
        for n, (bi, g) in enumerate(chains):
            for h in range(WKV_HEADS):
                so_ref[bi, g * WKV_HEADS + h] = s_scr[n, h * N:(h + 1) * N, h * N:(h + 1) * N]


def _wkv(name, r, lw, k, v, a, kkp, kap, rkp, lng, lnb, s0):
    B, T, D = r.shape
    C, HW = WKV_CHUNK, WKV_HEADS * RWKV_N
    n_grp = min(max(1, WKV_CHAINS // B), D // HW)
    W = n_grp * HW
    assert T % C == 0 and D % W == 0
    n_t = T // C
    tok = pl.BlockSpec((B, C, W), lambda h, t: (0, t, h))
    vec = pl.BlockSpec((1, W), lambda h, t: (0, h))
    st = pl.BlockSpec((B, n_grp * WKV_HEADS, RWKV_N, RWKV_N), lambda h, t: (0, h, 0, 0))
    return pl.pallas_call(
        functools.partial(_wkv_kernel, n_t=n_t, nb=B, n_grp=n_grp), grid=(D // W, n_t),
        in_specs=[tok] * 5 + [vec] * 5 + [st], out_specs=[tok, st],
        out_shape=[jax.ShapeDtypeStruct((B, T, D), F32), jax.ShapeDtypeStruct(s0.shape, F32)],
        scratch_shapes=[pltpu.VMEM((B * n_grp, HW, HW), F32)], compiler_params=_params(2), name=name,
    )(r, lw, k, v, a, kkp, kap, rkp, lng, lnb, s0)


def _cmul(ar, ai, br, bi):
    return ar * br - ai * bi, ar * bi + ai * br


def _s5_kernel(x_ref, wb_ref, wc_ref, lr_ref, li_ref, dt_ref, d_ref, h0r_ref, h0i_ref,
               z_ref, hr_ref, hi_ref, pw_re, pw_im, cf, hc, *, n_t, tt):
    t = pl.program_id(2)
    R = SUBLANES

    @pl.when(t == 0)
    def _():
        lr, li, dt = lr_ref[0], li_ref[0], dt_ref[0]
        n = (lax.broadcasted_iota(jnp.int32, (R, S5_STATES), 0) + 1).astype(F32)
        mag = jnp.exp(lr * dt * n)
        ang = li * dt * n
        pw_re[...] = mag * jnp.cos(ang)
        pw_im[...] = mag * jnp.sin(ang)
        ar, ai = pw_re[0:1, :], pw_im[0:1, :]
        den = lr * lr + li * li
        cf[0:1, :] = ((ar - 1.0) * lr + ai * li) / den
        cf[1:2, :] = (ai * lr - (ar - 1.0) * li) / den
        hc[0:1, :] = h0r_ref[0]
        hc[1:2, :] = h0i_ref[0]

    x = x_ref[0]
    bu = _dot3(x, wb_ref[0])
    cr, ci = cf[0:1, :], cf[1:2, :]
    br, bi = _cmul(cr, ci, bu[:, :S5_STATES], bu[:, S5_STATES:])
    rid = lax.broadcasted_iota(jnp.int32, (R, S5_STATES), 0)
    pr, pi = pw_re[...], pw_im[...]
    hr, hi = hc[0:1, :], hc[1:2, :]
    xrs, xis = [], []
    for i in range(tt // R):
        vr, vi = br[i * R:(i + 1) * R, :], bi[i * R:(i + 1) * R, :]
        for sh in (1, 2, 4):
            mr, mi = pr[sh - 1:sh, :], pi[sh - 1:sh, :]
            sr, si = pltpu.roll(vr, sh, axis=0), pltpu.roll(vi, sh, axis=0)
            ur, ui = _cmul(mr, mi, sr, si)
            keep = rid >= sh
            vr, vi = vr + jnp.where(keep, ur, 0.0), vi + jnp.where(keep, ui, 0.0)
        ur, ui = _cmul(pr, pi, hr, hi)
        vr, vi = vr + ur, vi + ui
        xrs.append(vr)
        xis.append(vi)
        hr, hi = vr[R - 1:R, :], vi[R - 1:R, :]
    hc[0:1, :] = hr
    hc[1:2, :] = hi
    xcat = jnp.concatenate([jnp.concatenate(xrs, axis=0), jnp.concatenate(xis, axis=0)], axis=-1)
    y = _dot3(xcat, wc_ref[0]) + d_ref[...] * x
    z_ref[0] = 0.5 * y * (1.0 + jnp.tanh(math.sqrt(2.0 / math.pi) * (y + 0.044715 * (y * y * y))))

    @pl.when(t == n_t - 1)
    def _():
        hr_ref[0] = hr
        hi_ref[0] = hi


def _s5(name, x3, wb, wc, lr, li, dt, d, h0r, h0i):
    B, T, D = x3.shape
    tt = min(256, T)
    assert T % tt == 0 and tt % SUBLANES == 0
    n_t = T // tt
    n_s = D // S5_SLAB
    S = S5_STATES
    st = pl.BlockSpec((1, 1, S), lambda b, s, t: (b, 0, s))
    pv = pl.BlockSpec((1, 1, S), lambda b, s, t: (0, 0, s))
    return pl.pallas_call(
        functools.partial(_s5_kernel, n_t=n_t, tt=tt), grid=(B, n_s, n_t),
        in_specs=[pl.BlockSpec((1, tt, S5_SLAB), lambda b, s, t: (b, t, s)),
                  pl.BlockSpec((1, S5_SLAB, 2 * S), lambda b, s, t: (s, 0, 0)),
                  pl.BlockSpec((1, 2 * S, S5_SLAB), lambda b, s, t: (s, 0, 0)),
                  pv, pv, pv,
                  pl.BlockSpec((1, S5_SLAB), lambda b, s, t: (0, s)),
                  st, st],
        out_specs=[pl.BlockSpec((1, tt, S5_SLAB), lambda b, s, t: (b, t, s)), st, st],
        out_shape=[jax.ShapeDtypeStruct((B, T, D), F32),
                   jax.ShapeDtypeStruct((B, 1, n_s * S), F32), jax.ShapeDtypeStruct((B, 1, n_s * S), F32)],
        scratch_shapes=[pltpu.VMEM((SUBLANES, S), F32), pltpu.VMEM((SUBLANES, S), F32),
                        pltpu.VMEM((SUBLANES, S), F32), pltpu.VMEM((SUBLANES, S), F32)],
        compiler_params=_params(3), name=name,
    )(x3, wb, wc, lr, li, dt, d, h0r, h0i)


def _select_topk(gate, idx, valid, axis):
    g = jnp.where(valid, gate, NEG)
    big = jnp.int32(1 << 30)
    sel = jnp.zeros(gate.shape, jnp.bool_)
    for _ in range(MOBA_TOPK):
        m = jnp.max(g, axis=axis, keepdims=True)
        first = jnp.min(jnp.where(g == m, idx, big), axis=axis, keepdims=True)
        pick = idx == first
        sel = jnp.logical_or(sel, pick)
        g = jnp.where(pick, -jnp.inf, g)
    return jnp.logical_and(sel, valid)


def _moba_prompt_kernel(q_ref, k_ref, v_ref, sl_ref, o_ref, kmean_scr, *, n_blk):
    i = pl.program_id(2)
    BK = MOBA_BLOCK

    @pl.when(i == 0)
    def _():
        kmean_scr[...] = jnp.concatenate([jnp.sum(k_ref[0, n * BK:(n + 1) * BK, :], axis=0, keepdims=True)
                                          for n in range(n_blk)], axis=0) * (1.0 / BK)

    q = q_ref[0]
    slope = sl_ref[0][:, 0:1]
    qi = lax.broadcasted_iota(jnp.int32, (BK, BK), 0)
    ki = lax.broadcasted_iota(jnp.int32, (BK, BK), 1)
    causal = qi >= ki
    bias0 = slope * (qi - ki).astype(F32)
    qb = q.astype(BF16)
    scale = MOBA_DH ** -0.5

    def tile(c):
        ss = []
        if c > 0:
            gate = _dot3(q, kmean_scr[...], _NT)
            nidx = lax.broadcasted_iota(jnp.int32, (BK, n_blk), 1)
            sel = _select_topk(gate, nidx, nidx < c, axis=1)
        for n in range(c + 1):
            s = _dg(qb, k_ref[0, n * BK:(n + 1) * BK, :].astype(BF16), _NT) * scale
            s = s - (bias0 + slope * float((c - n) * BK))
            ss.append(jnp.where(causal if n == c else sel[:, n:n + 1], s, NEG))
        m = functools.reduce(jnp.maximum, [jnp.max(s, axis=-1, keepdims=True) for s in ss])
        ps = [jnp.exp(s - m) for s in ss]
        l = functools.reduce(lambda x, y: x + y, [jnp.sum(p, axis=-1, keepdims=True) for p in ps])
        acc = functools.reduce(lambda x, y: x + y,
                               [_dg(p.astype(BF16), v_ref[0, n * BK:(n + 1) * BK, :].astype(BF16), _NN)
                                for n, p in enumerate(ps)])
        o_ref[0] = acc / l

    for c in range(n_blk):
        pl.when(i == c)(functools.partial(tile, c))


def _moba_prompt(name, q, k, v, slopes):
    B, T, D = q.shape
    n_blk = T // MOBA_BLOCK
    assert T % MOBA_BLOCK == 0
    qs = pl.BlockSpec((1, MOBA_BLOCK, MOBA_DH), lambda b, h, i: (b, i, h))
    kv = pl.BlockSpec((1, T, MOBA_DH), lambda b, h, i: (b, 0, h))
    return pl.pallas_call(
        functools.partial(_moba_prompt_kernel, n_blk=n_blk), grid=(B, MOBA_H, n_blk),
        in_specs=[qs, kv, kv, pl.BlockSpec((1, 1, LANES), lambda b, h, i: (h, 0, 0))], out_specs=qs,
        out_shape=jax.ShapeDtypeStruct((B, T, D), F32),
        scratch_shapes=[pltpu.VMEM((n_blk, MOBA_DH), F32)],
        compiler_params=_params(3), name=name,
    )(q, k, v, slopes)


def _q_rows(q, nq):
    return jnp.concatenate([q[:, h * MOBA_DH:(h + 1) * MOBA_DH] for h in range(MOBA_H)], axis=0)


def _head_match(n_rows, nq):
    shape = (n_rows * MOBA_H, MOBA_H * nq)
    return lax.broadcasted_iota(jnp.int32, shape, 0) % MOBA_H == lax.broadcasted_iota(jnp.int32, shape, 1) // nq


def _token_scores(k3, qr, nq, dot):
    n_rows = k3.shape[0]
    s2 = dot(k3.reshape(n_rows * MOBA_H, MOBA_DH), qr, _NT)
    s2 = jnp.where(_head_match(n_rows, nq), s2, 0.0)
    return jnp.sum(s2.reshape(n_rows, MOBA_H, MOBA_H * nq), axis=1)


def _moba_scores_kernel(pt_ref, q_ref, m_ref, *refs, nq, pp):
    k_refs, (s_ref, ks_ref) = refs[:pp], refs[pp:]
    qr = _q_rows(q_ref[0], nq).astype(BF16)
    kps = [k_ref[0, 0] for k_ref in k_refs]
    s2s = [_dg(kp.reshape(PAGE_SIZE * MOBA_H, MOBA_DH).astype(BF16), qr, _NT) for kp in kps]
    for u in range(pp):
        s_ref[0, u] = jnp.sum((s2s[u] * m_ref[...]).reshape(PAGE_SIZE, MOBA_H, MOBA_H * nq), axis=1)
        ks_ref[0, u] = jnp.sum(kps[u], axis=0)


def _page_specs(layer, pp):
    return [pl.BlockSpec((1, 1, PAGE_SIZE, MOBA_H, MOBA_DH),
                         lambda b, g, pt, u=u: (layer, pt[b, g * pp + u], 0, 0, 0)) for u in range(pp)]


def _moba_scores(name, page_table, q, cache_k, layer, match_f):
    B, nq, D = q.shape
    n_pg = page_table.shape[1]
    pp = MOBA_PAGES_PER_STEP
    assert n_pg % pp == 0
    gs = pltpu.PrefetchScalarGridSpec(
        num_scalar_prefetch=1, grid=(B, n_pg // pp),
        in_specs=[pl.BlockSpec((1, nq, D), lambda b, g, pt: (b, 0, 0)),
                  pl.BlockSpec(match_f.shape, lambda b, g, pt: (0, 0))] + _page_specs(layer, pp),
        out_specs=[pl.BlockSpec((1, pp, PAGE_SIZE, MOBA_H * nq), lambda b, g, pt: (b, g, 0, 0)),
                   pl.BlockSpec((1, pp, MOBA_H, MOBA_DH), lambda b, g, pt: (b, g, 0, 0))])
    return pl.pallas_call(
        functools.partial(_moba_scores_kernel, nq=nq, pp=pp), grid_spec=gs,
        out_shape=[jax.ShapeDtypeStruct((B, n_pg, PAGE_SIZE, MOBA_H * nq), F32),
                   jax.ShapeDtypeStruct((B, n_pg, MOBA_H, MOBA_DH), F32)],
        compiler_params=_params(2), name=name,
    )(page_table, q, match_f, *([cache_k] * pp))


def _moba_probs_kernel(s_ref, ks_ref, q_ref, kn_ref, sl_ref, p_ref, pn_ref, sel_scr, *, nq, n_blk, past):
    BK = MOBA_BLOCK
    HQ = MOBA_H * nq
    qr = _q_rows(q_ref[0], nq)
    ppb = BK // PAGE_SIZE
    kmean = jnp.sum(ks_ref[0].reshape(n_blk, ppb, MOBA_H, MOBA_DH), axis=1) * (1.0 / BK)
    gate = _token_scores(kmean, qr, nq, _dot3)
    nidx = lax.broadcasted_iota(jnp.int32, (n_blk, HQ), 0)
    own = past // BK
    sel = _select_topk(gate, nidx, nidx < own, axis=0)
    sel_scr[...] = sel.astype(F32)
    slope = sl_ref[...]
    scale = MOBA_DH ** -0.5
    qpos = past + lax.broadcasted_iota(jnp.int32, (1, HQ), 1) % nq
    tok = lax.broadcasted_iota(jnp.int32, (BK, HQ), 0)

    def scores(n):
        s = s_ref[0, n] * scale - slope * (qpos - (n * BK + tok)).astype(F32)
        return jnp.where(sel_scr[pl.ds(n, 1), :] > 0.5, s, NEG)

    sn = _token_scores(kn_ref[0], qr, nq, _dot1) * scale
    tn = lax.broadcasted_iota(jnp.int32, (PAGE_SIZE, HQ), 0)
    dn = qpos - (past + tn)
    sn = jnp.where(jnp.logical_and(dn >= 0, tn < nq), sn - slope * dn.astype(F32), NEG)
    m = lax.fori_loop(0, n_blk, lambda n, m_: jnp.maximum(m_, jnp.max(scores(n), axis=0, keepdims=True)),
                      jnp.max(sn, axis=0, keepdims=True))
    pn = jnp.exp(sn - m)
    l = lax.fori_loop(0, n_blk, lambda n, l_: l_ + jnp.sum(jnp.exp(scores(n) - m), axis=0, keepdims=True),
                      jnp.sum(pn, axis=0, keepdims=True))
    pn_ref[0] = (pn / l).astype(pn_ref.dtype)

    def write(n, c):
        p_ref[0, n] = (jnp.exp(scores(n) - m) / l).astype(p_ref.dtype)
        return c

    lax.fori_loop(0, n_blk, write, 0)


def _moba_probs(name, s, ksum, q, k_new_pad, slopes_hq, past):
    B, n_blk, BK, HQ = s.shape
    nq = q.shape[1]
    blk = pl.BlockSpec((1, n_blk, BK, HQ), lambda b: (b, 0, 0, 0))
    return pl.pallas_call(
        functools.partial(_moba_probs_kernel, nq=nq, n_blk=n_blk, past=past), grid=(B,),
        in_specs=[blk,
                  pl.BlockSpec((1, ksum.shape[1], MOBA_H, MOBA_DH), lambda b: (b, 0, 0, 0)),
                  pl.BlockSpec((1, nq, D_MODEL), lambda b: (b, 0, 0)),
                  pl.BlockSpec((1, PAGE_SIZE, MOBA_H, MOBA_DH), lambda b: (b, 0, 0, 0)),
                  pl.BlockSpec((1, HQ), lambda b: (0, 0))],
        out_specs=[blk, pl.BlockSpec((1, PAGE_SIZE, HQ), lambda b: (b, 0, 0))],
        out_shape=[jax.ShapeDtypeStruct((B, n_blk, BK, HQ), BF16), jax.ShapeDtypeStruct((B, PAGE_SIZE, HQ), BF16)],
        scratch_shapes=[pltpu.VMEM((n_blk, HQ), F32)],
        compiler_params=_params(1), name=name,
    )(s, ksum, q, k_new_pad, slopes_hq)


def _moba_pv_kernel(pt_ref, p_ref, m_ref, pn_ref, vn_ref, *refs, nq, n_steps, pp):
    v_refs, (o_ref, acc) = refs[:pp], refs[pp:]
    g = pl.program_id(1)
    HQ = MOBA_H * nq

    def spread(p):
        p2 = jnp.broadcast_to(p.astype(F32)[:, None, :], (PAGE_SIZE, MOBA_H, HQ))
        return (p2.reshape(PAGE_SIZE * MOBA_H, HQ) * m_ref[...]).astype(BF16)

    def values(v3):
        return v3.reshape(PAGE_SIZE * MOBA_H, MOBA_DH).astype(BF16)

    @pl.when(g == 0)
    def _():
        acc[...] = _dg(spread(pn_ref[0]), values(vn_ref[0]), _TN)

    p2s = [spread(p_ref[0, u]) for u in range(pp)]
    v2s = [values(v_ref[0, 0]) for v_ref in v_refs]
    acc[...] += functools.reduce(lambda x, y: x + y, [_dg(p2, v2, _TN) for p2, v2 in zip(p2s, v2s)])

    @pl.when(g == n_steps - 1)
    def _():
        a = acc[...]
        o_ref[0] = jnp.concatenate([a[h * nq:(h + 1) * nq, :] for h in range(MOBA_H)], axis=-1)


def _moba_pv(name, page_table, p, cache_v, pn, v_new_pad, nq, layer, match_f):
    B, n_pg, _, HQ = p.shape
    D = D_MODEL
    pp = MOBA_PAGES_PER_STEP
    assert n_pg % pp == 0
    gs = pltpu.PrefetchScalarGridSpec(
        num_scalar_prefetch=1, grid=(B, n_pg // pp),
        in_specs=[pl.BlockSpec((1, pp, PAGE_SIZE, HQ), lambda b, g, pt: (b, g, 0, 0)),
                  pl.BlockSpec(match_f.shape, lambda b, g, pt: (0, 0)),
                  pl.BlockSpec((1, PAGE_SIZE, HQ), lambda b, g, pt: (b, 0, 0)),
                  pl.BlockSpec((1, PAGE_SIZE, MOBA_H, MOBA_DH), lambda b, g, pt: (b, 0, 0, 0))]
        + _page_specs(layer, pp),
        out_specs=pl.BlockSpec((1, nq, D), lambda b, g, pt: (b, 0, 0)),
        scratch_shapes=[pltpu.VMEM((HQ, MOBA_DH), F32)])
    return pl.pallas_call(
        functools.partial(_moba_pv_kernel, nq=nq, n_steps=n_pg // pp, pp=pp), grid_spec=gs,
        out_shape=jax.ShapeDtypeStruct((B, nq, D), F32), compiler_params=_params(2), name=name,
    )(page_table, p, match_f, pn, v_new_pad, *([cache_v] * pp))


def _ident(x):
    return x


def _mix(x, xp, mu):
    return x + (xp - x) * mu


def _pad_cols(w, n):
    return jnp.pad(w, ((0, 0), (0, n - w.shape[1])))


def _pad_rows(w, n):
    return jnp.pad(w, ((0, n - w.shape[0]), (0, 0)))


def _rwkv_layer(tag, x3, shift0, s0, v_first, P, j, ln_g, ln_b):
    B, T, D = x3.shape
    M = B * T
    x = x3.reshape(M, D)
    xp = jnp.concatenate([shift0[:, None, :], x3[:, :-1]], axis=1).reshape(M, D)
    wb = lambda w: w.astype(BF16)
    lp = LANES
    w_rkv = P['rwkv_w_rkv'][j]
    w_main = jnp.concatenate([wb(w_rkv[0]), wb(w_rkv[1]), wb(w_rkv[2])], axis=1)
    lora1 = [wb(_pad_cols(P['rwkv_w1'][j], lp)), wb(_pad_cols(P['rwkv_a1'][j], lp)), wb(P['rwkv_g1'][j])]
    lora2 = [wb(_pad_rows(P['rwkv_w2'][j], lp)), wb(_pad_rows(P['rwkv_a2'][j], lp)), wb(P['rwkv_g2'][j])]
    bias = [P['rwkv_w0'][j][None, :], P['rwkv_a0'][j][None, :]]
    vgate = None
    if j > 0:
        vgate = (wb(_pad_cols(P['rwkv_v1'][j - 1], lp)), wb(_pad_rows(P['rwkv_v2'][j - 1], lp)),
                 P['rwkv_v0'][j - 1][None, :], v_first)
    r, k, v, lw, a, g = _rwkv_proj(tag + 'proj', x, xp, P['rwkv_mu'][j], w_main, lora1, lora2, bias, vgate)
    if j == 0:
        v_first = v
    Tp = -(-T // WKV_CHUNK) * WKV_CHUNK
    seq = lambda t: jnp.pad(t.reshape(B, T, D), ((0, 0), (0, Tp - T), (0, 0)))
    vecp = lambda name: P[name][j].reshape(1, D)
    z, s_new = _wkv(tag + 'wkv', seq(r), seq(lw), seq(k), seq(v), seq(a), vecp('rwkv_k_k'), vecp('rwkv_k_a'),
                    vecp('rwkv_r_k'), vecp('rwkv_lnx_g'), vecp('rwkv_lnx_b'), s0)
    z = z[:, :T].reshape(M, D)
    xn = _mm_ln(tag + 'wo', [z, g], wb(P['rwkv_wo'][j]), x, ln_g, ln_b, lambda z_, g_: z_ * g_)
    return xn, s_new, x3[:, -1], v_first


def _s5_weights(P, j):
    ns, gs = D_MODEL // S5_SLAB, S5_SLAB // S5_GROUP
    eye = jnp.eye(gs, dtype=F32)

    def bd_in(b):
        bt = jnp.swapaxes(b.reshape(ns, gs, S5_P, S5_GROUP), 2, 3)
        return jnp.einsum('sgcp,gh->sgchp', bt, eye).reshape(ns, S5_SLAB, S5_STATES)

    def bd_out(c):
        ct = jnp.swapaxes(c.reshape(ns, gs, S5_GROUP, S5_P), 2, 3)
        return jnp.einsum('sgpc,gh->sgphc', ct, eye).reshape(ns, S5_STATES, S5_SLAB)

    wb = jnp.concatenate([bd_in(P['s5_b_re'][j]), bd_in(P['s5_b_im'][j])], axis=2)
    wc = jnp.concatenate([bd_out(P['s5_c_re'][j]), -bd_out(P['s5_c_im'][j])], axis=1)
    flat = lambda t: t.reshape(1, 1, S5_G * S5_P)
    dt = jnp.broadcast_to(jnp.exp(P['s5_log_dt'][j])[:, None], (S5_G, S5_P))
    return wb, wc, flat(P['s5_lam_re'][j]), flat(P['s5_lam_im'][j]), flat(dt)


def _s5_layer(tag, x3, h0r, h0i, P, j, ln_g, ln_b):
    B, T, D = x3.shape
    M = B * T
    x = x3.reshape(M, D)
    wb, wc, lr, li, dt = _s5_weights(P, j)
    st = lambda h: h.reshape(B, 1, S5_G * S5_P)
    z, hr, hi = _s5(tag + 'scan', x3, wb, wc, lr, li, dt, P['s5_d'][j][None, :], st(h0r), st(h0i))
    out, = _mm(tag + 'gate', [z.reshape(M, D)], [], [P['s5_w_val'][j].astype(BF16), P['s5_w_gate'][j].astype(BF16)],
               [], [], _ident, lambda val, gate: (val * _sigmoid(gate),), [F32])
    xn = _ln(tag + 'ln', x, out, ln_g, ln_b)
    return xn, hr.reshape(B, S5_G, S5_P), hi.reshape(B, S5_G, S5_P)


def _alibi_slopes():
    return 2.0 ** (-8.0 * jnp.arange(1, MOBA_H + 1, dtype=F32) / MOBA_H)


def _moba_layer(tag, x3, cache_k, cache_v, page_table, P, j, ln_g, ln_b):
    B, T, D = x3.shape
    M = B * T
    x = x3.reshape(M, D)
    wqkv = P['moba_w_qkv'][j]
    ws = [wqkv[:, n * D:(n + 1) * D].astype(BF16) for n in range(3)]
    q, k, v = _mm(tag + 'qkv', [x], [], ws, [], [], _ident, lambda a, b, c: (a, b, c), [F32, F32, F32])
    q3, k3, v3 = (t.reshape(B, T, D) for t in (q, k, v))
    slopes = _alibi_slopes()
    if cache_k is None:
        sl = jnp.broadcast_to(slopes[:, None, None], (MOBA_H, 1, LANES))
        o = _moba_prompt(tag + 'attn', q3, k3, v3, sl)
    else:
        n_pg = page_table.shape[1]
        past = n_pg * PAGE_SIZE
        assert past % MOBA_BLOCK == 0 and T <= PAGE_SIZE and T <= MOBA_BLOCK
        HQ = MOBA_H * T
        match_f = _head_match(PAGE_SIZE, T).astype(F32)
        s, ksum = _moba_scores(tag + 'scores', page_table, q3, cache_k, j, match_f)
        sl = jnp.repeat(slopes, T)[None, :]
        pad = lambda t: jnp.pad(t.reshape(B, T, MOBA_H, MOBA_DH), ((0, 0), (0, PAGE_SIZE - T), (0, 0), (0, 0)))
        p, pn = _moba_probs(tag + 'probs', s.reshape(B, past // MOBA_BLOCK, MOBA_BLOCK, HQ), ksum, q3, pad(k3),
                            sl, past)
        o = _moba_pv(tag + 'pv', page_table, p.reshape(B, n_pg, PAGE_SIZE, HQ), cache_v, pn, pad(v3), T, j, match_f)
    xn = _mm_ln(tag + 'wo', [o.reshape(M, D)], P['moba_wo'][j].astype(BF16), x, ln_g, ln_b, _ident)
    return xn, k3.reshape(B, T, MOBA_H, MOBA_DH), v3.reshape(B, T, MOBA_H, MOBA_DH)


def _ffn_layer(tag, x3, conv0, P, i, ln_g, ln_b):
    B, T, D = x3.shape
    w_in = P['ffn_w_in'][i]
    h, cst = _ffn_in(tag + 'in', x3, w_in[:, :D_FF].astype(BF16), w_in[:, D_FF:].astype(BF16), conv0,
                     P['ffn_conv_w'][i], P['ffn_conv_b'][i][None, :])
    xn = _mm_ln(tag + 'down', [h], P['ffn_w_down'][i].astype(BF16), x3.reshape(B * T, D), ln_g, ln_b, _ident)
    return xn.reshape(B, T, D), cst


def _trunk(grp, x3, wkv0, shift0, s5re0, s5im0, conv0, cache_k, cache_v, page_table, P):
    B, T, D = x3.shape
    wkv_new, shift_new, s5re_new, s5im_new, k_new, v_new, conv_new = [], [], [], [], [], [], []
    v_first = None
    for i in range(DEPTH):
        kind, j = i % 3, i // 3
        tag = f'{grp}{i}_'
        g0, b0 = P['ln_g'][i, 0][None, :], P['ln_b'][i, 0][None, :]
        g1, b1 = P['ln_g'][i, 1][None, :], P['ln_b'][i, 1][None, :]
        if kind == 0:
            xn, S, last, v_first = _rwkv_layer(tag, x3, shift0[j], wkv0[j], v_first, P, j, g0, b0)
            wkv_new.append(S)
            shift_new.append(last)
        elif kind == 1:
            xn, hr, hi = _s5_layer(tag, x3, s5re0[j], s5im0[j], P, j, g0, b0)
            s5re_new.append(hr)
            s5im_new.append(hi)
        else:
            xn, kr, vr = _moba_layer(tag, x3, cache_k, cache_v, page_table, P, j, g0, b0)
            k_new.append(kr)
            v_new.append(vr)
        x3, cst = _ffn_layer(tag + 'ffn_', xn.reshape(B, T, D), conv0[i], P, i, g1, b1)
        conv_new.append(cst)
    return (x3, jnp.stack(wkv_new), jnp.stack(shift_new), jnp.stack(s5re_new), jnp.stack(s5im_new),
            jnp.stack(conv_new), jnp.stack(k_new), jnp.stack(v_new))


def kernel(x_prompt, x_sample, state_rwkv_wkv, state_rwkv_shift, state_s5_re, state_s5_im, state_ffn_conv, cache_k, cache_v, page_table, ln_g, ln_b, rwkv_mu, rwkv_w_rkv, rwkv_w0, rwkv_w1, rwkv_w2, rwkv_a0, rwkv_a1, rwkv_a2, rwkv_v0, rwkv_v1, rwkv_v2, rwkv_g1, rwkv_g2, rwkv_k_k, rwkv_k_a, rwkv_r_k, rwkv_lnx_g, rwkv_lnx_b, rwkv_wo, s5_log_dt, s5_lam_re, s5_lam_im, s5_b_re, s5_b_im, s5_c_re, s5_c_im, s5_d, s5_w_val, s5_w_gate, moba_w_qkv, moba_wo, ffn_w_in, ffn_conv_w, ffn_conv_b, ffn_w_down):
    P = dict(ln_g=ln_g, ln_b=ln_b, rwkv_mu=rwkv_mu, rwkv_w_rkv=rwkv_w_rkv, rwkv_w0=rwkv_w0, rwkv_w1=rwkv_w1,
             rwkv_w2=rwkv_w2, rwkv_a0=rwkv_a0, rwkv_a1=rwkv_a1, rwkv_a2=rwkv_a2, rwkv_v0=rwkv_v0,
             rwkv_v1=rwkv_v1, rwkv_v2=rwkv_v2, rwkv_g1=rwkv_g1, rwkv_g2=rwkv_g2, rwkv_k_k=rwkv_k_k,
             rwkv_k_a=rwkv_k_a, rwkv_r_k=rwkv_r_k.reshape(rwkv_r_k.shape[0], D_MODEL), rwkv_lnx_g=rwkv_lnx_g,
             rwkv_lnx_b=rwkv_lnx_b, rwkv_wo=rwkv_wo, s5_log_dt=s5_log_dt, s5_lam_re=s5_lam_re,
             s5_lam_im=s5_lam_im, s5_b_re=s5_b_re, s5_b_im=s5_b_im, s5_c_re=s5_c_re, s5_c_im=s5_c_im, s5_d=s5_d,
             s5_w_val=s5_w_val, s5_w_gate=s5_w_gate, moba_w_qkv=moba_w_qkv, moba_wo=moba_wo,
             ffn_w_in=ffn_w_in, ffn_conv_w=ffn_conv_w, ffn_conv_b=ffn_conv_b, ffn_w_down=ffn_w_down)
    B = x_prompt.shape[0]
    n_rwkv, n_s5 = state_rwkv_wkv.shape[0], state_s5_re.shape[0]
    zeros = lambda *s: jnp.zeros(s, F32)
    outs_p = _trunk('p', x_prompt, zeros(n_rwkv, B, RWKV_H, RWKV_N, RWKV_N), zeros(n_rwkv, B, D_MODEL),
                    zeros(n_s5, B, S5_G, S5_P), zeros(n_s5, B, S5_G, S5_P), zeros(DEPTH, B, CONV_W - 1, D_FF),
                    None, None, None, P)
    outs_s = _trunk('s', x_sample, state_rwkv_wkv, state_rwkv_shift, state_s5_re, state_s5_im, state_ffn_conv,
                    cache_k, cache_v, page_table, P)
    return (outs_p[0], outs_s[0]) + tuple(outs_p[1:]) + tuple(outs_s[1:])
```

```python
import functools
import math

import jax
import jax.numpy as jnp
from jax import lax
from jax.experimental import pallas as pl
from jax.experimental.pallas import tpu as pltpu

F32 = jnp.float32
BF16 = jnp.bfloat16

D_MODEL = 2048
DEPTH = 4
RWKV_N = 64
RWKV_H = D_MODEL // RWKV_N
RWKV_GN_EPS = 64e-5
S5_GROUP = 16
S5_G = D_MODEL // S5_GROUP
S5_P = 64
MOBA_H = 16
MOBA_DH = D_MODEL // MOBA_H
MOBA_BLOCK = 256
MOBA_TOPK = 3
PAGE_SIZE = 128
D_FF = 11 * D_MODEL // 4
CONV_W = 3
LN_EPS = 1e-5
DN_ALPHA = (2.0 * DEPTH) ** 0.25
NEG = -1e30

LANES = 128
SUBLANES = 8
VMEM_LIMIT_BYTES = 56 * 1024 * 1024

MM_ROW_TILE = 1024
MM_LN_WEIGHT_TILE_BYTES = 12 * 1024 * 1024
WKV_CHUNK = 64
WKV_CHAINS = 16
WKV_HEADS = 4
MOBA_PAGES_PER_STEP = 4
S5_SLAB = LANES
S5_STATES = (S5_SLAB // S5_GROUP) * S5_P


def _params(n_axes):
    return pltpu.CompilerParams(dimension_semantics=("arbitrary",) * n_axes,
                                vmem_limit_bytes=VMEM_LIMIT_BYTES)


def _sigmoid(x):
    return 1.0 / (1.0 + jnp.exp(-x))


def _softplus(x):
    return jnp.maximum(x, 0.0) + jnp.log1p(jnp.exp(-jnp.abs(x)))


def _dg(a, b, dims):
    return lax.dot_general(a, b, (dims, ((), ())), preferred_element_type=F32)


_NN = ((1,), (0,))
_NT = ((1,), (1,))
_TN = ((0,), (0,))


def _split(a):
    hi = a.astype(BF16)
    lo = (a - hi.astype(F32)).astype(BF16)
    return hi, lo


def _dot3(a, b, dims=_NN):
    ah, al = _split(a)
    bh, bl = _split(b)
    return _dg(ah, bh, dims) + (_dg(al, bh, dims) + _dg(ah, bl, dims))


def _dot1(a, b, dims=_NN):
    return _dg(a.astype(BF16), b.astype(BF16), dims)


def _dot_exact_lhs(lhs_bf16, x):
    x1 = x.astype(BF16)
    r1 = x - x1.astype(F32)
    x2 = r1.astype(BF16)
    x3 = (r1 - x2.astype(F32)).astype(BF16)
    return _dg(lhs_bf16, x1, _NN) + (_dg(lhs_bf16, x2, _NN) + _dg(lhs_bf16, x3, _NN))


def _layer_norm(y, g, b):
    mu = jnp.mean(y, axis=-1, keepdims=True)
    yc = y - mu
    var = jnp.mean(yc * yc, axis=-1, keepdims=True)
    return yc * lax.rsqrt(var + LN_EPS) * g + b


def _mm_kernel(*refs, n_row, n_vec, n_w, n_erow, n_evec, n_out, prologue, epilogue):
    it = iter(refs)
    rows = [next(it) for _ in range(n_row)]
    vecs = [next(it) for _ in range(n_vec)]
    ws = [next(it) for _ in range(n_w)]
    erows = [next(it) for _ in range(n_erow)]
    evecs = [next(it) for _ in range(n_evec)]
    outs = [next(it) for _ in range(n_out)]
    a_scr = next(it)

    @pl.when(pl.program_id(1) == 0)
    def _():
        a_scr[...] = prologue(*[r[...] for r in rows], *[v[...] for v in vecs]).astype(BF16)

    a = a_scr[...]
    accs = [jnp.dot(a, w[...], preferred_element_type=F32) for w in ws]
    res = epilogue(*accs, *[e[...] for e in erows], *[e[...] for e in evecs])
    for o_ref, o in zip(outs, res):
        o_ref[...] = o.astype(o_ref.dtype)


def _mm(name, rows, vecs, ws, erows, evecs, prologue, epilogue, out_dtypes):
    M, K = rows[0].shape
    N = ws[0].shape[1]
    tm = MM_ROW_TILE if (M % MM_ROW_TILE == 0 and len(rows) == 1) else min(512, M)
    tn = 512 if N % 512 == 0 else N
    assert M % tm == 0 and N % tn == 0
    in_specs = ([pl.BlockSpec((tm, K), lambda i, j: (i, 0)) for _ in rows]
                + [pl.BlockSpec((1, K), lambda i, j: (0, 0)) for _ in vecs]
                + [pl.BlockSpec((K, tn), lambda i, j: (0, j)) for _ in ws]
                + [pl.BlockSpec((tm, tn), lambda i, j: (i, j)) for _ in erows]
                + [pl.BlockSpec((1, tn), lambda i, j: (0, j)) for _ in evecs])
    out_specs = [pl.BlockSpec((tm, tn), lambda i, j: (i, j)) for _ in out_dtypes]
    out_shape = [jax.ShapeDtypeStruct((M, N), dt) for dt in out_dtypes]
    kern = functools.partial(_mm_kernel, n_row=len(rows), n_vec=len(vecs), n_w=len(ws), n_erow=len(erows),
                             n_evec=len(evecs), n_out=len(out_dtypes), prologue=prologue, epilogue=epilogue)
    return pl.pallas_call(
        kern, grid=(M // tm, N // tn), in_specs=in_specs, out_specs=out_specs, out_shape=out_shape,
        scratch_shapes=[pltpu.VMEM((tm, K), BF16)], compiler_params=_params(2), name=name,
    )(*rows, *vecs, *ws, *erows, *evecs)


def _mm_ln_kernel(*refs, n_row, prologue, n_k):
    it = iter(refs)
    rows = [next(it) for _ in range(n_row)]
    w_ref, x_ref, g_ref, b_ref, o_ref, acc = (next(it) for _ in range(6))
    k = pl.program_id(1)
    a = prologue(*[r[...] for r in rows]).astype(BF16)
    p = jnp.dot(a, w_ref[...], preferred_element_type=F32)
    if n_k == 1:
        o_ref[...] = _layer_norm(DN_ALPHA * x_ref[...] + p, g_ref[...], b_ref[...])
        return

    @pl.when(k == 0)
    def _():
        acc[...] = p

    @pl.when(jnp.logical_and(k > 0, k < n_k - 1))
    def _():
        acc[...] += p

    @pl.when(k == n_k - 1)
    def _():
        o_ref[...] = _layer_norm(DN_ALPHA * x_ref[...] + (acc[...] + p), g_ref[...], b_ref[...])


def _mm_ln(name, rows, w, x, g, b, prologue):
    M, K = rows[0].shape
    N = w.shape[1]
    n_k = -(-K * N * 2 // MM_LN_WEIGHT_TILE_BYTES)
    tk = K // n_k
    tm = min(512 if n_k > 1 else 256, M)
    assert M % tm == 0 and K % n_k == 0 and tk % LANES == 0
    in_specs = ([pl.BlockSpec((tm, tk), lambda i, k: (i, k)) for _ in rows]
                + [pl.BlockSpec((tk, N), lambda i, k: (k, 0)),
                   pl.BlockSpec((tm, N), lambda i, k: (i, 0)),
                   pl.BlockSpec((1, N), lambda i, k: (0, 0)),
                   pl.BlockSpec((1, N), lambda i, k: (0, 0))])
    kern = functools.partial(_mm_ln_kernel, n_row=len(rows), prologue=prologue, n_k=n_k)
    return pl.pallas_call(
        kern, grid=(M // tm, n_k), in_specs=in_specs,
        out_specs=pl.BlockSpec((tm, N), lambda i, k: (i, 0)),
        out_shape=jax.ShapeDtypeStruct((M, N), F32),
        scratch_shapes=[pltpu.VMEM((tm, N), F32)], compiler_params=_params(2), name=name,
    )(*rows, w, x, g, b)


def _ln_kernel(x_ref, h_ref, g_ref, b_ref, o_ref):
    o_ref[...] = _layer_norm(DN_ALPHA * x_ref[...] + h_ref[...], g_ref[...], b_ref[...])


def _ln(name, x, h, g, b):
    M, N = x.shape
    tm = min(512, M)
    row = pl.BlockSpec((tm, N), lambda i: (i, 0))
    vec = pl.BlockSpec((1, N), lambda i: (0, 0))
    return pl.pallas_call(_ln_kernel, grid=(M // tm,), in_specs=[row, row, vec, vec], out_specs=row,
                          out_shape=jax.ShapeDtypeStruct((M, N), F32), compiler_params=_params(1), name=name)(x, h, g, b)


def _ffn_in_kernel(x_ref, wg_ref, wv_ref, c0_ref, cw_ref, cb_ref, h_ref, cs_ref, a_scr, carry, *, bt, tt):
    t = pl.program_id(1)
    j = pl.program_id(2)
    rows = bt * tt
    tn = wg_ref.shape[1]

    @pl.when(j == 0)
    def _():
        a_scr[...] = x_ref[...].reshape(rows, x_ref.shape[2]).astype(BF16)

    @pl.when(t == 0)
    def _():
        carry[j] = c0_ref[...]

    a = a_scr[...]
    hg = jnp.dot(a, wg_ref[...], preferred_element_type=F32)
    hv = jnp.dot(a, wv_ref[...], preferred_element_type=F32)
    prev = carry[j]
    p2 = jnp.broadcast_to(prev[:, 0:1, :], (bt, tt, tn)).reshape(rows, tn)
    p1 = jnp.broadcast_to(prev[:, 1:2, :], (bt, tt, tn)).reshape(rows, tn)
    r = lax.broadcasted_iota(jnp.int32, (rows, tn), 0) % tt
    s1 = jnp.where(r == 0, p1, pltpu.roll(hg, 1, axis=0))
    s2 = jnp.where(r == 0, p2, jnp.where(r == 1, p1, pltpu.roll(hg, 2, axis=0)))
    cw = cw_ref[...]
    c = cb_ref[...] + cw[0:1, :] * s2 + cw[1:2, :] * s1 + cw[2:3, :] * hg
    h_ref[...] = (c * _sigmoid(c) * hv).astype(h_ref.dtype)
    last = hg.reshape(bt, tt, tn)[:, tt - 2:, :]
    carry[j] = last
    cs_ref[0] = last


def _ffn_in(name, x3, wg, wv, conv0, cw, cb):
    B, T, K = x3.shape
    F = wg.shape[1]
    bt, tt = (1, MM_ROW_TILE) if T % MM_ROW_TILE == 0 else (B, T)
    tn = 512
    assert B % bt == 0 and T % tt == 0 and F % tn == 0 and tt >= CONV_W - 1
    n_j = F // tn
    kern = functools.partial(_ffn_in_kernel, bt=bt, tt=tt)
    n_t = T // tt
    h, cst = pl.pallas_call(
        kern, grid=(B // bt, n_t, n_j),
        in_specs=[pl.BlockSpec((bt, tt, K), lambda b, t, j: (b, t, 0)),
                  pl.BlockSpec((K, tn), lambda b, t, j: (0, j)),
                  pl.BlockSpec((K, tn), lambda b, t, j: (0, j)),
                  pl.BlockSpec((bt, CONV_W - 1, tn), lambda b, t, j: (b, 0, j)),
                  pl.BlockSpec((CONV_W, tn), lambda b, t, j: (0, j)),
                  pl.BlockSpec((1, tn), lambda b, t, j: (0, j))],
        out_specs=[pl.BlockSpec((bt * tt, tn), lambda b, t, j: (b * n_t + t, j)),
                   pl.BlockSpec((1, bt, CONV_W - 1, tn), lambda b, t, j: (t, b, 0, j))],
        out_shape=[jax.ShapeDtypeStruct((B * T, F), BF16),
                   jax.ShapeDtypeStruct((n_t, B, CONV_W - 1, F), F32)],
        scratch_shapes=[pltpu.VMEM((bt * tt, K), BF16), pltpu.VMEM((n_j, bt, CONV_W - 1, tn), F32)],
        compiler_params=_params(3), name=name,
    )(x3, wg, wv, conv0, cw, cb)
    return h, cst[n_t - 1]


RWKV_PROJ_TN = 512
RWKV_PROJ_TILES = D_MODEL // RWKV_PROJ_TN


def _rwkv_proj_kernel(*refs, has_vgate):
    it = iter(refs)
    x_ref, xp_ref, mu_ref, wm_ref, w1_ref, a1_ref, g1_ref, w2_ref, a2_ref, g2_ref, w0_ref, a0_ref = (
        next(it) for _ in range(12))
    if has_vgate:
        v1_ref, v2_ref, v0_ref, vf_ref = (next(it) for _ in range(4))
    r_ref, k_ref, v_ref, lw_ref, a_ref, g_ref = (next(it) for _ in range(6))
    mix_scr, hw_scr, ha_scr, hg_scr = (next(it) for _ in range(4))
    if has_vgate:
        hv_scr = next(it)
    j = pl.program_id(1)
    dot = lambda a, b: jnp.dot(a, b, preferred_element_type=F32)

    @pl.when(j == 0)
    def _():
        x = x_ref[...]
        d = xp_ref[...] - x
        mix = lambda n: (x + d * mu_ref[n:n + 1, :]).astype(BF16)
        mix_scr[0] = mix(0)
        mix_scr[1] = mix(2)
        xv = mix(3)
        mix_scr[2] = xv
        hw_scr[...] = jnp.tanh(dot(mix(1), w1_ref[...])).astype(BF16)
        ha_scr[...] = dot(mix(4), a1_ref[...]).astype(BF16)
        hg_scr[...] = _sigmoid(dot(mix(5), g1_ref[...])).astype(BF16)
        if has_vgate:
            hv_scr[...] = dot(xv, v1_ref[...]).astype(BF16)

    grp = j // RWKV_PROJ_TILES

    @pl.when(grp == 0)
    def _():
        r_ref[...] = dot(mix_scr[0], wm_ref[...])
        lw_ref[...] = -jnp.exp(-_softplus(-(w0_ref[...] + dot(hw_scr[...], w2_ref[...]))) - 0.5)

    @pl.when(grp == 1)
    def _():
        k_ref[...] = dot(mix_scr[1], wm_ref[...])
        a_ref[...] = _sigmoid(a0_ref[...] + dot(ha_scr[...], a2_ref[...]))

    @pl.when(grp == 2)
    def _():
        v = dot(mix_scr[2], wm_ref[...])
        if has_vgate:
            v = v + (vf_ref[...] - v) * _sigmoid(v0_ref[...] + dot(hv_scr[...], v2_ref[...]))
        v_ref[...] = v
        g_ref[...] = dot(hg_scr[...], g2_ref[...])


def _rwkv_proj(name, x, xp, mu, w_main, lora1, lora2, bias, vgate):
    M, K = x.shape
    tm, tn, nt = min(512, M), RWKV_PROJ_TN, RWKV_PROJ_TILES
    assert M % tm == 0 and w_main.shape == (K, 3 * D_MODEL)
    row = pl.BlockSpec((tm, K), lambda i, j: (i, 0))
    full = lambda a: pl.BlockSpec(a.shape, lambda i, j: (0, 0))
    col = lambda a: pl.BlockSpec((a.shape[0], tn), lambda i, j: (0, j % nt))
    ins = [x, xp, mu, w_main, *lora1, *lora2, *bias]
    specs = ([row, row, full(mu), pl.BlockSpec((K, tn), lambda i, j: (0, j))] + [full(w) for w in lora1]
             + [col(w) for w in lora2] + [col(b) for b in bias])
    scratch = [pltpu.VMEM((3, tm, K), BF16)] + [pltpu.VMEM((tm, w.shape[1]), BF16) for w in lora1]
    if vgate is not None:
        v1, v2, v0, v_first = vgate
        ins += [v1, v2, v0, v_first]
        specs += [full(v1), col(v2), col(v0), pl.BlockSpec((tm, tn), lambda i, j: (i, j % nt))]
        scratch.append(pltpu.VMEM((tm, v1.shape[1]), BF16))
    out = lambda grp: pl.BlockSpec((tm, tn), lambda i, j: (i, jnp.clip(j - grp * nt, 0, nt - 1)))
    return pl.pallas_call(
        functools.partial(_rwkv_proj_kernel, has_vgate=vgate is not None), grid=(M // tm, 3 * nt),
        in_specs=specs, out_specs=[out(0), out(1), out(2), out(0), out(1), out(2)],
        out_shape=[jax.ShapeDtypeStruct((M, D_MODEL), F32)] * 6,
        scratch_shapes=scratch, compiler_params=_params(2), name=name,
    )(*ins)


def _wkv_chunks(r, lw, k, v, a, prm, S, consts):
    C = WKV_CHUNK
    bdmask, l_incl, strict, incl, eye = consts

    def bd(xp):
        return jnp.concatenate([xp.astype(BF16)] * WKV_HEADS, axis=0) * bdmask

    def headsum(xp):
        hi, lo = _split(xp)
        return _dg(hi, bdmask, _NN) + _dg(lo, bdmask, _NN)

    def each(f, *lists):
        return [f(*args) for args in zip(*lists)]

    kkp, kap, rkp, lng, lnb = (list(t) for t in zip(*prm))
    kkn = each(lambda k_, p_: k_ * p_, k, kkp)
    kp = each(lambda k_, a_, p_: k_ * (1.0 + (a_ - 1.0) * p_), k, a, kap)
    sums = each(lambda kkn_, r_, kp_, p_: headsum(jnp.concatenate([kkn_ * kkn_, r_ * kp_ * p_], axis=0)),
                kkn, r, kp, rkp)
    kk = each(lambda kkn_, s_: kkn_ / jnp.maximum(jnp.sqrt(s_[:C]), 1e-12), kkn, sums)
    bonus = each(lambda s_, v_: s_[C:] * v_, sums, v)
    b = each(lambda kk_, a_: kk_ * a_, kk, a)
    cum = each(lambda lw_: _dot_exact_lhs(l_incl, lw_), lw)
    e_neg = each(lambda c_: jnp.exp(-c_), cum)
    x = each(lambda kk_, c_, lw_, r_: jnp.concatenate([kk_ * jnp.exp(c_ - lw_), r_ * jnp.exp(c_)],
                                                       axis=0).astype(BF16), kk, cum, lw, r)
    g_b = each(lambda x_, b_, e_: _dg(x_, bd(b_ * e_), _NT), x, b, e_neg)
    g_k = each(lambda x_, k_, e_: _dg(x_, bd(k_ * e_), _NT), x, kp, e_neg)
    xs = each(lambda x_, s_: _dg(x_, s_.astype(BF16), _NT), x, S)
    mb = each(lambda g_: jnp.where(incl, g_[C:], 0.0), g_b)
    kv = each(lambda g_, v_: _dg(jnp.concatenate([jnp.where(strict, g_[:C], 0.0), jnp.where(incl, g_[C:], 0.0)],
                                                 axis=0).astype(BF16), bd(v_), _NN), g_k, v)
    rhs = each(lambda xs_, kv_: -(xs_[:C] + kv_[:C]), xs, kv)
    npow = each(lambda g_: -jnp.where(strict, g_[:C], 0.0), g_b)
    tinv = each(lambda n_: eye + n_, npow)
    npow = each(lambda n_: _dg(n_.astype(BF16), bd(n_), _NN), npow)
    for _ in range(int(math.log2(C)) - 2):
        both = each(lambda t_, n_: _dg(jnp.concatenate([t_, n_], axis=0).astype(BF16), bd(n_), _NN), tinv, npow)
        tinv = each(lambda t_, b_: t_ + b_[:C], tinv, both)
        npow = each(lambda b_: b_[C:], both)
    tinv = each(lambda t_, n_: t_ + _dg(t_.astype(BF16), bd(n_), _NN), tinv, npow)
    u = each(lambda t_, r_: _dg(t_.astype(BF16), bd(r_), _NN), tinv, rhs)
    y = each(lambda xs_, kv_, mb_, u_: xs_[C:] + kv_[C:] + _dg(mb_.astype(BF16), bd(u_), _NN), xs, kv, mb, u)
    upd = each(lambda u_, v_, b_, k_, c_: _dg(
        jnp.concatenate([u_, v_], axis=0).astype(BF16),
        (jnp.concatenate([b_, k_], axis=0) * jnp.exp(c_[C - 1:C, :] - jnp.concatenate([c_, c_], axis=0))).astype(BF16),
        _TN), u, v, b, kp, cum)
    bdmask_f = bdmask.astype(F32)
    s_new = each(lambda s_, c_, u_: s_ * jnp.exp(c_[C - 1:C, :]) + u_ * bdmask_f, S, cum, upd)
    mu = each(lambda y_: headsum(y_) * (1.0 / RWKV_N), y)
    yc = each(lambda y_, m_: y_ - m_, y, mu)
    var = each(lambda yc_: headsum(yc_ * yc_) * (1.0 / RWKV_N), yc)
    z = each(lambda yc_, var_, g_, b_, bo_: yc_ * lax.rsqrt(var_ + RWKV_GN_EPS) * g_ + b_ + bo_,
             yc, var, lng, lnb, bonus)
    return z, s_new


def _wkv_kernel(r_ref, lw_ref, k_ref, v_ref, a_ref, kk_ref, ka_ref, rk_ref, g_ref, b_ref, s0_ref,
                z_ref, so_ref, s_scr, *, n_t, nb, n_grp):
    t = pl.program_id(1)
    C, N, HW = WKV_CHUNK, RWKV_N, WKV_HEADS * RWKV_N
    ri = lax.broadcasted_iota(jnp.int32, (HW, HW), 0) // N
    ci = lax.broadcasted_iota(jnp.int32, (HW, HW), 1) // N
    bdmask_f = (ri == ci).astype(F32)

    chains = [(bi, g) for bi in range(nb) for g in range(n_grp)]

    @pl.when(t == 0)
    def _():
        for n, (bi, g) in enumerate(chains):
            rows = s0_ref[bi, g * WKV_HEADS:(g + 1) * WKV_HEADS].reshape(HW, N)
            s_scr[n] = jnp.concatenate([rows] * WKV_HEADS, axis=1) * bdmask_f

    tt = lax.broadcasted_iota(jnp.int32, (C, C), 0)
    ss = lax.broadcasted_iota(jnp.int32, (C, C), 1)
    tp = lax.broadcasted_iota(jnp.int32, (C, HW), 0)
    sp = lax.broadcasted_iota(jnp.int32, (C, HW), 1) % N
    consts = (bdmask_f.astype(BF16), (tt >= ss).astype(BF16), tp > sp, tp >= sp, (tp == sp).astype(F32))
    sls = [slice(g * HW, (g + 1) * HW) for _, g in chains]
    tok = lambda ref: [ref[bi, :, sl] for (bi, _), sl in zip(chains, sls)]
    prm = [tuple(ref[:, sl] for ref in (kk_ref, ka_ref, rk_ref, g_ref, b_ref)) for sl in sls]
    zs, s_news = _wkv_chunks(tok(r_ref), tok(lw_ref), tok(k_ref), tok(v_ref), tok(a_ref), prm,
                             [s_scr[n] for n in range(len(chains))], consts)
    for n, ((bi, _), sl) in enumerate(zip(chains, sls)):
        s_scr[n] = s_news[n]
        z_ref[bi, :, sl] = zs[n]

    @pl.when(t == n_t - 1)
    def _():
        for n, (bi, g) in enumerate(chains):
            for h in range(WKV_HEADS):
                so_ref[bi, g * WKV_HEADS + h] = s_scr[n, h * N:(h + 1) * N, h * N:(h + 1) * N]


def _wkv(name, r, lw, k, v, a, kkp, kap, rkp, lng, lnb, s0):
    B, T, D = r.shape
    C, HW = WKV_CHUNK, WKV_HEADS * RWKV_N
    n_grp = min(max(1, WKV_CHAINS // B), D // HW)
    W = n_grp * HW
    assert T % C == 0 and D % W == 0
    n_t = T // C
    tok = pl.BlockSpec((B, C, W), lambda h, t: (0, t, h))
    vec = pl.BlockSpec((1, W), lambda h, t: (0, h))
    st = pl.BlockSpec((B, n_grp * WKV_HEADS, RWKV_N, RWKV_N), lambda h, t: (0, h, 0, 0))
    return pl.pallas_call(
        functools.partial(_wkv_kernel, n_t=n_t, nb=B, n_grp=n_grp), grid=(D // W, n_t),
        in_specs=[tok] * 5 + [vec] * 5 + [st], out_specs=[tok, st],
        out_shape=[jax.ShapeDtypeStruct((B, T, D), F32), jax.ShapeDtypeStruct(s0.shape, F32)],
        scratch_shapes=[pltpu.VMEM((B * n_grp, HW, HW), F32)], compiler_params=_params(2), name=name,
    )(r, lw, k, v, a, kkp, kap, rkp, lng, lnb, s0)


def _cmul(ar, ai, br, bi):
    return ar * br - ai * bi, ar * bi + ai * br


def _dot3w(a, w_hi, w_lo, dims=_NN):
    ah, al = _split(a)
    return _dg(ah, w_hi, dims) + (_dg(al, w_hi, dims) + _dg(ah, w_lo, dims))


S5_SHIFTS = (1, 2, 4)


def _s5_kernel(x_ref, wbh_ref, wbl_ref, wch_ref, wcl_ref, lr_ref, li_ref, dt_ref, d_ref, h0r_ref, h0i_ref,
               z_ref, hr_ref, hi_ref, pw_re, pw_im, sh_re, sh_im, cf, hc, *, n_t, tt):
    t = pl.program_id(2)
    R = SUBLANES

    @pl.when(t == 0)
    def _():
        lr, li, dt = lr_ref[0], li_ref[0], dt_ref[0]
        rid = lax.broadcasted_iota(jnp.int32, (R, S5_STATES), 0)
        n = (rid + 1).astype(F32)
        mag = jnp.exp(lr * dt * n)
        ang = li * dt * n
        pw_re[...] = mag * jnp.cos(ang)
        pw_im[...] = mag * jnp.sin(ang)
        for m, sh in enumerate(S5_SHIFTS):
            sh_re[m] = jnp.where(rid >= sh, pw_re[sh - 1:sh, :], 0.0)
            sh_im[m] = jnp.where(rid >= sh, pw_im[sh - 1:sh, :], 0.0)
        ar, ai = pw_re[0:1, :], pw_im[0:1, :]
        den = lr * lr + li * li
        cf[0:1, :] = ((ar - 1.0) * lr + ai * li) / den
        cf[1:2, :] = (ai * lr - (ar - 1.0) * li) / den
        hc[0:1, :] = h0r_ref[0]
        hc[1:2, :] = h0i_ref[0]

    x = x_ref[0]
    bu = _dot3w(x, wbh_ref[0], wbl_ref[0])
    cr, ci = cf[0:1, :], cf[1:2, :]
    br, bi = _cmul(cr, ci, bu[:, :S5_STATES], bu[:, S5_STATES:])
    pr, pi = pw_re[...], pw_im[...]
    hr, hi = hc[0:1, :], hc[1:2, :]
    xrs, xis = [], []
    for i in range(tt // R):
        vr, vi = br[i * R:(i + 1) * R, :], bi[i * R:(i + 1) * R, :]
        for m, sh in enumerate(S5_SHIFTS):
            ur, ui = _cmul(sh_re[m], sh_im[m], pltpu.roll(vr, sh, axis=0), pltpu.roll(vi, sh, axis=0))
            vr, vi = vr + ur, vi + ui
        ur, ui = _cmul(pr, pi, hr, hi)
        vr, vi = vr + ur, vi + ui
        xrs.append(vr)
        xis.append(vi)
        hr, hi = vr[R - 1:R, :], vi[R - 1:R, :]
    hc[0:1, :] = hr
    hc[1:2, :] = hi
    xcat = jnp.concatenate([jnp.concatenate(xrs, axis=0), jnp.concatenate(xis, axis=0)], axis=-1)
    y = _dot3w(xcat, wch_ref[0], wcl_ref[0]) + d_ref[...] * x
    z_ref[0] = 0.5 * y * (1.0 + jnp.tanh(math.sqrt(2.0 / math.pi) * (y + 0.044715 * (y * y * y))))

    @pl.when(t == n_t - 1)
    def _():
        hr_ref[0] = hr
        hi_ref[0] = hi


def _s5(name, x3, wb, wc, lr, li, dt, d, h0r, h0i):
    B, T, D = x3.shape
    tt = min(256, T)
    assert T % tt == 0 and tt % SUBLANES == 0
    n_t = T // tt
    n_s = D // S5_SLAB
    S = S5_STATES
    st = pl.BlockSpec((1, 1, S), lambda b, s, t: (b, 0, s))
    pv = pl.BlockSpec((1, 1, S), lambda b, s, t: (0, 0, s))
    w_in = pl.BlockSpec((1, S5_SLAB, 2 * S), lambda b, s, t: (s, 0, 0))
    w_out = pl.BlockSpec((1, 2 * S, S5_SLAB), lambda b, s, t: (s, 0, 0))
    split = lambda w: (w.astype(BF16), (w - w.astype(BF16).astype(F32)).astype(BF16))
    return pl.pallas_call(
        functools.partial(_s5_kernel, n_t=n_t, tt=tt), grid=(B, n_s, n_t),
        in_specs=[pl.BlockSpec((1, tt, S5_SLAB), lambda b, s, t: (b, t, s)),
                  w_in, w_in, w_out, w_out,
                  pv, pv, pv,
                  pl.BlockSpec((1, S5_SLAB), lambda b, s, t: (0, s)),
                  st, st],
        out_specs=[pl.BlockSpec((1, tt, S5_SLAB), lambda b, s, t: (b, t, s)), st, st],
        out_shape=[jax.ShapeDtypeStruct((B, T, D), F32),
                   jax.ShapeDtypeStruct((B, 1, n_s * S), F32), jax.ShapeDtypeStruct((B, 1, n_s * S), F32)],
        scratch_shapes=[pltpu.VMEM((SUBLANES, S), F32), pltpu.VMEM((SUBLANES, S), F32),
                        pltpu.VMEM((len(S5_SHIFTS), SUBLANES, S), F32),
                        pltpu.VMEM((len(S5_SHIFTS), SUBLANES, S), F32),
                        pltpu.VMEM((SUBLANES, S), F32), pltpu.VMEM((SUBLANES, S), F32)],
        compiler_params=_params(3), name=name,
    )(x3, *split(wb), *split(wc), lr, li, dt, d, h0r, h0i)


def _select_topk(gate, idx, valid, axis):
    g = jnp.where(valid, gate, NEG)
    big = jnp.int32(1 << 30)
    sel = jnp.zeros(gate.shape, jnp.bool_)
    for _ in range(MOBA_TOPK):
        m = jnp.max(g, axis=axis, keepdims=True)
        first = jnp.min(jnp.where(g == m, idx, big), axis=axis, keepdims=True)
        pick = idx == first
        sel = jnp.logical_or(sel, pick)
        g = jnp.where(pick, -jnp.inf, g)
    return jnp.logical_and(sel, valid)


def _moba_prompt_kernel(q_ref, k_ref, v_ref, sl_ref, o_ref, kmean_scr, *, n_blk):
    i = pl.program_id(2)
    BK = MOBA_BLOCK

    @pl.when(i == 0)
    def _():
        kmean_scr[...] = jnp.concatenate([jnp.sum(k_ref[0, n * BK:(n + 1) * BK, :], axis=0, keepdims=True)
                                          for n in range(n_blk)], axis=0) * (1.0 / BK)

    q = q_ref[0]
    slope = sl_ref[0][:, 0:1]
    qi = lax.broadcasted_iota(jnp.int32, (BK, BK), 0)
    ki = lax.broadcasted_iota(jnp.int32, (BK, BK), 1)
    causal = qi >= ki
    bias0 = slope * (qi - ki).astype(F32)
    qb = q.astype(BF16)
    scale = MOBA_DH ** -0.5

    def tile(c):
        ss = []
        if c > 0:
            gate = _dot3(q, kmean_scr[...], _NT)
            nidx = lax.broadcasted_iota(jnp.int32, (BK, n_blk), 1)
            sel = _select_topk(gate, nidx, nidx < c, axis=1)
        for n in range(c + 1):
            s = _dg(qb, k_ref[0, n * BK:(n + 1) * BK, :].astype(BF16), _NT) * scale
            s = s - (bias0 + slope * float((c - n) * BK))
            ss.append(jnp.where(causal if n == c else sel[:, n:n + 1], s, NEG))
        m = functools.reduce(jnp.maximum, [jnp.max(s, axis=-1, keepdims=True) for s in ss])
        ps = [jnp.exp(s - m) for s in ss]
        l = functools.reduce(lambda x, y: x + y, [jnp.sum(p, axis=-1, keepdims=True) for p in ps])
        acc = functools.reduce(lambda x, y: x + y,
                               [_dg(p.astype(BF16), v_ref[0, n * BK:(n + 1) * BK, :].astype(BF16), _NN)
                                for n, p in enumerate(ps)])
        o_ref[0] = acc / l

    for c in range(n_blk):
        pl.when(i == c)(functools.partial(tile, c))


def _moba_prompt(name, q, k, v, slopes):
    B, T, D = q.shape
    n_blk = T // MOBA_BLOCK
    assert T % MOBA_BLOCK == 0
    qs = pl.BlockSpec((1, MOBA_BLOCK, MOBA_DH), lambda b, h, i: (b, i, h))
    kv = pl.BlockSpec((1, T, MOBA_DH), lambda b, h, i: (b, 0, h))
    return pl.pallas_call(
        functools.partial(_moba_prompt_kernel, n_blk=n_blk), grid=(B, MOBA_H, n_blk),
        in_specs=[qs, kv, kv, pl.BlockSpec((1, 1, LANES), lambda b, h, i: (h, 0, 0))], out_specs=qs,
        out_shape=jax.ShapeDtypeStruct((B, T, D), F32),
        scratch_shapes=[pltpu.VMEM((n_blk, MOBA_DH), F32)],
        compiler_params=_params(3), name=name,
    )(q, k, v, slopes)


def _q_rows(q, nq):
    return jnp.concatenate([q[:, h * MOBA_DH:(h + 1) * MOBA_DH] for h in range(MOBA_H)], axis=0)


def _head_match(n_rows, nq):
    shape = (n_rows * MOBA_H, MOBA_H * nq)
    return lax.broadcasted_iota(jnp.int32, shape, 0) % MOBA_H == lax.broadcasted_iota(jnp.int32, shape, 1) // nq


def _token_scores(k3, qr, nq, dot):
    n_rows = k3.shape[0]
    s2 = dot(k3.reshape(n_rows * MOBA_H, MOBA_DH), qr, _NT)
    s2 = jnp.where(_head_match(n_rows, nq), s2, 0.0)
    return jnp.sum(s2.reshape(n_rows, MOBA_H, MOBA_H * nq), axis=1)


def _moba_scores_kernel(pt_ref, q_ref, m_ref, *refs, nq, pp):
    k_refs, (s_ref, ks_ref) = refs[:pp], refs[pp:]
    qr = _q_rows(q_ref[0], nq).astype(BF16)
    kps = [k_ref[0, 0] for k_ref in k_refs]
    s2s = [_dg(kp.reshape(PAGE_SIZE * MOBA_H, MOBA_DH).astype(BF16), qr, _NT) for kp in kps]
    for u in range(pp):
        s_ref[0, u] = jnp.sum((s2s[u] * m_ref[...]).reshape(PAGE_SIZE, MOBA_H, MOBA_H * nq), axis=1)
        ks_ref[0, u] = jnp.sum(kps[u], axis=0)


def _page_specs(layer, pp):
    return [pl.BlockSpec((1, 1, PAGE_SIZE, MOBA_H, MOBA_DH),
                         lambda b, g, pt, u=u: (layer, pt[b, g * pp + u], 0, 0, 0)) for u in range(pp)]


def _moba_scores(name, page_table, q, cache_k, layer, match_f):
    B, nq, D = q.shape
    n_pg = page_table.shape[1]
    pp = MOBA_PAGES_PER_STEP
    assert n_pg % pp == 0
    gs = pltpu.PrefetchScalarGridSpec(
        num_scalar_prefetch=1, grid=(B, n_pg // pp),
        in_specs=[pl.BlockSpec((1, nq, D), lambda b, g, pt: (b, 0, 0)),
                  pl.BlockSpec(match_f.shape, lambda b, g, pt: (0, 0))] + _page_specs(layer, pp),
        out_specs=[pl.BlockSpec((1, pp, PAGE_SIZE, MOBA_H * nq), lambda b, g, pt: (b, g, 0, 0)),
                   pl.BlockSpec((1, pp, MOBA_H, MOBA_DH), lambda b, g, pt: (b, g, 0, 0))])
    return pl.pallas_call(
        functools.partial(_moba_scores_kernel, nq=nq, pp=pp), grid_spec=gs,
        out_shape=[jax.ShapeDtypeStruct((B, n_pg, PAGE_SIZE, MOBA_H * nq), F32),
                   jax.ShapeDtypeStruct((B, n_pg, MOBA_H, MOBA_DH), F32)],
        compiler_params=_params(2), name=name,
    )(page_table, q, match_f, *([cache_k] * pp))


def _moba_probs_kernel(s_ref, ks_ref, q_ref, kn_ref, sl_ref, p_ref, pn_ref, sel_scr, *, nq, n_blk, past):
    BK = MOBA_BLOCK
    HQ = MOBA_H * nq
    qr = _q_rows(q_ref[0], nq)
    ppb = BK // PAGE_SIZE
    kmean = jnp.sum(ks_ref[0].reshape(n_blk, ppb, MOBA_H, MOBA_DH), axis=1) * (1.0 / BK)
    gate = _token_scores(kmean, qr, nq, _dot3)
    nidx = lax.broadcasted_iota(jnp.int32, (n_blk, HQ), 0)
    own = past // BK
    sel = _select_topk(gate, nidx, nidx < own, axis=0)
    sel_scr[...] = sel.astype(F32)
    slope = sl_ref[...]
    scale = MOBA_DH ** -0.5
    qpos = past + lax.broadcasted_iota(jnp.int32, (1, HQ), 1) % nq
    tok = lax.broadcasted_iota(jnp.int32, (BK, HQ), 0)

    def scores(n):
        s = s_ref[0, n] * scale - slope * (qpos - (n * BK + tok)).astype(F32)
        return jnp.where(sel_scr[pl.ds(n, 1), :] > 0.5, s, NEG)

    sn = _token_scores(kn_ref[0], qr, nq, _dot1) * scale
    tn = lax.broadcasted_iota(jnp.int32, (PAGE_SIZE, HQ), 0)
    dn = qpos - (past + tn)
    sn = jnp.where(jnp.logical_and(dn >= 0, tn < nq), sn - slope * dn.astype(F32), NEG)
    m = lax.fori_loop(0, n_blk, lambda n, m_: jnp.maximum(m_, jnp.max(scores(n), axis=0, keepdims=True)),
                      jnp.max(sn, axis=0, keepdims=True))
    pn = jnp.exp(sn - m)
    l = lax.fori_loop(0, n_blk, lambda n, l_: l_ + jnp.sum(jnp.exp(scores(n) - m), axis=0, keepdims=True),
                      jnp.sum(pn, axis=0, keepdims=True))
    pn_ref[0] = (pn / l).astype(pn_ref.dtype)

    def write(n, c):
        p_ref[0, n] = (jnp.exp(scores(n) - m) / l).astype(p_ref.dtype)
        return c

    lax.fori_loop(0, n_blk, write, 0)


def _moba_probs(name, s, ksum, q, k_new_pad, slopes_hq, past):
    B, n_blk, BK, HQ = s.shape
    nq = q.shape[1]
    blk = pl.BlockSpec((1, n_blk, BK, HQ), lambda b: (b, 0, 0, 0))
    return pl.pallas_call(
        functools.partial(_moba_probs_kernel, nq=nq, n_blk=n_blk, past=past), grid=(B,),
        in_specs=[blk,
                  pl.BlockSpec((1, ksum.shape[1], MOBA_H, MOBA_DH), lambda b: (b, 0, 0, 0)),
                  pl.BlockSpec((1, nq, D_MODEL), lambda b: (b, 0, 0)),
                  pl.BlockSpec((1, PAGE_SIZE, MOBA_H, MOBA_DH), lambda b: (b, 0, 0, 0)),
                  pl.BlockSpec((1, HQ), lambda b: (0, 0))],
        out_specs=[blk, pl.BlockSpec((1, PAGE_SIZE, HQ), lambda b: (b, 0, 0))],
        out_shape=[jax.ShapeDtypeStruct((B, n_blk, BK, HQ), BF16), jax.ShapeDtypeStruct((B, PAGE_SIZE, HQ), BF16)],
        scratch_shapes=[pltpu.VMEM((n_blk, HQ), F32)],
        compiler_params=_params(1), name=name,
    )(s, ksum, q, k_new_pad, slopes_hq)


def _moba_pv_kernel(pt_ref, p_ref, m_ref, pn_ref, vn_ref, *refs, nq, n_steps, pp):
    v_refs, (o_ref, acc) = refs[:pp], refs[pp:]
    g = pl.program_id(1)
    HQ = MOBA_H * nq

    def spread(p):
        p2 = jnp.broadcast_to(p.astype(F32)[:, None, :], (PAGE_SIZE, MOBA_H, HQ))
        return (p2.reshape(PAGE_SIZE * MOBA_H, HQ) * m_ref[...]).astype(BF16)

    def values(v3):
        return v3.reshape(PAGE_SIZE * MOBA_H, MOBA_DH).astype(BF16)

    @pl.when(g == 0)
    def _():
        acc[...] = _dg(spread(pn_ref[0]), values(vn_ref[0]), _TN)

    p2s = [spread(p_ref[0, u]) for u in range(pp)]
    v2s = [values(v_ref[0, 0]) for v_ref in v_refs]
    acc[...] += functools.reduce(lambda x, y: x + y, [_dg(p2, v2, _TN) for p2, v2 in zip(p2s, v2s)])

    @pl.when(g == n_steps - 1)
    def _():
        a = acc[...]
        o_ref[0] = jnp.concatenate([a[h * nq:(h + 1) * nq, :] for h in range(MOBA_H)], axis=-1)


def _moba_pv(name, page_table, p, cache_v, pn, v_new_pad, nq, layer, match_f):
    B, n_pg, _, HQ = p.shape
    D = D_MODEL
    pp = MOBA_PAGES_PER_STEP
    assert n_pg % pp == 0
    gs = pltpu.PrefetchScalarGridSpec(
        num_scalar_prefetch=1, grid=(B, n_pg // pp),
        in_specs=[pl.BlockSpec((1, pp, PAGE_SIZE, HQ), lambda b, g, pt: (b, g, 0, 0)),
                  pl.BlockSpec(match_f.shape, lambda b, g, pt: (0, 0)),
                  pl.BlockSpec((1, PAGE_SIZE, HQ), lambda b, g, pt: (b, 0, 0)),
                  pl.BlockSpec((1, PAGE_SIZE, MOBA_H, MOBA_DH), lambda b, g, pt: (b, 0, 0, 0))]
        + _page_specs(layer, pp),
        out_specs=pl.BlockSpec((1, nq, D), lambda b, g, pt: (b, 0, 0)),
        scratch_shapes=[pltpu.VMEM((HQ, MOBA_DH), F32)])
    return pl.pallas_call(
        functools.partial(_moba_pv_kernel, nq=nq, n_steps=n_pg // pp, pp=pp), grid_spec=gs,
        out_shape=jax.ShapeDtypeStruct((B, nq, D), F32), compiler_params=_params(2), name=name,
    )(page_table, p, match_f, pn, v_new_pad, *([cache_v] * pp))


def _ident(x):
    return x


def _mix(x, xp, mu):
    return x + (xp - x) * mu


def _pad_cols(w, n):
    return jnp.pad(w, ((0, 0), (0, n - w.shape[1])))


def _pad_rows(w, n):
    return jnp.pad(w, ((0, n - w.shape[0]), (0, 0)))


def _rwkv_layer(tag, x3, shift0, s0, v_first, P, j, ln_g, ln_b):
    B, T, D = x3.shape
    M = B * T
    x = x3.reshape(M, D)
    xp = jnp.concatenate([shift0[:, None, :], x3[:, :-1]], axis=1).reshape(M, D)
    wb = lambda w: w.astype(BF16)
    lp = LANES
    w_rkv = P['rwkv_w_rkv'][j]
    w_main = jnp.concatenate([wb(w_rkv[0]), wb(w_rkv[1]), wb(w_rkv[2])], axis=1)
    lora1 = [wb(_pad_cols(P['rwkv_w1'][j], lp)), wb(_pad_cols(P['rwkv_a1'][j], lp)), wb(P['rwkv_g1'][j])]
    lora2 = [wb(_pad_rows(P['rwkv_w2'][j], lp)), wb(_pad_rows(P['rwkv_a2'][j], lp)), wb(P['rwkv_g2'][j])]
    bias = [P['rwkv_w0'][j][None, :], P['rwkv_a0'][j][None, :]]
    vgate = None
    if j > 0:
        vgate = (wb(_pad_cols(P['rwkv_v1'][j - 1], lp)), wb(_pad_rows(P['rwkv_v2'][j - 1], lp)),
                 P['rwkv_v0'][j - 1][None, :], v_first)
    r, k, v, lw, a, g = _rwkv_proj(tag + 'proj', x, xp, P['rwkv_mu'][j], w_main, lora1, lora2, bias, vgate)
    if j == 0:
        v_first = v
    Tp = -(-T // WKV_CHUNK) * WKV_CHUNK
    seq = lambda t: jnp.pad(t.reshape(B, T, D), ((0, 0), (0, Tp - T), (0, 0)))
    vecp = lambda name: P[name][j].reshape(1, D)
    z, s_new = _wkv(tag + 'wkv', seq(r), seq(lw), seq(k), seq(v), seq(a), vecp('rwkv_k_k'), vecp('rwkv_k_a'),
                    vecp('rwkv_r_k'), vecp('rwkv_lnx_g'), vecp('rwkv_lnx_b'), s0)
    z = z[:, :T].reshape(M, D)
    xn = _mm_ln(tag + 'wo', [z, g], wb(P['rwkv_wo'][j]), x, ln_g, ln_b, lambda z_, g_: z_ * g_)
    return xn, s_new, x3[:, -1], v_first


def _s5_weights(P, j):
    ns, gs = D_MODEL // S5_SLAB, S5_SLAB // S5_GROUP
    eye = jnp.eye(gs, dtype=F32)

    def bd_in(b):
        bt = jnp.swapaxes(b.reshape(ns, gs, S5_P, S5_GROUP), 2, 3)
        return jnp.einsum('sgcp,gh->sgchp', bt, eye).reshape(ns, S5_SLAB, S5_STATES)

    def bd_out(c):
        ct = jnp.swapaxes(c.reshape(ns, gs, S5_GROUP, S5_P), 2, 3)
        return jnp.einsum('sgpc,gh->sgphc', ct, eye).reshape(ns, S5_STATES, S5_SLAB)

    wb = jnp.concatenate([bd_in(P['s5_b_re'][j]), bd_in(P['s5_b_im'][j])], axis=2)
    wc = jnp.concatenate([bd_out(P['s5_c_re'][j]), -bd_out(P['s5_c_im'][j])], axis=1)
    flat = lambda t: t.reshape(1, 1, S5_G * S5_P)
    dt = jnp.broadcast_to(jnp.exp(P['s5_log_dt'][j])[:, None], (S5_G, S5_P))
    return wb, wc, flat(P['s5_lam_re'][j]), flat(P['s5_lam_im'][j]), flat(dt)


def _s5_layer(tag, x3, h0r, h0i, P, j, ln_g, ln_b):
    B, T, D = x3.shape
    M = B * T
    x = x3.reshape(M, D)
    wb, wc, lr, li, dt = _s5_weights(P, j)
    st = lambda h: h.reshape(B, 1, S5_G * S5_P)
    z, hr, hi = _s5(tag + 'scan', x3, wb, wc, lr, li, dt, P['s5_d'][j][None, :], st(h0r), st(h0i))
    out, = _mm(tag + 'gate', [z.reshape(M, D)], [], [P['s5_w_val'][j].astype(BF16), P['s5_w_gate'][j].astype(BF16)],
               [], [], _ident, lambda val, gate: (val * _sigmoid(gate),), [F32])
    xn = _ln(tag + 'ln', x, out, ln_g, ln_b)
    return xn, hr.reshape(B, S5_G, S5_P), hi.reshape(B, S5_G, S5_P)


def _alibi_slopes():
    return 2.0 ** (-8.0 * jnp.arange(1, MOBA_H + 1, dtype=F32) / MOBA_H)


def _moba_layer(tag, x3, cache_k, cache_v, page_table, P, j, ln_g, ln_b):
    B, T, D = x3.shape
    M = B * T
    x = x3.reshape(M, D)
    wqkv = P['moba_w_qkv'][j]
    ws = [wqkv[:, n * D:(n + 1) * D].astype(BF16) for n in range(3)]
    q, k, v = _mm(tag + 'qkv', [x], [], ws, [], [], _ident, lambda a, b, c: (a, b, c), [F32, F32, F32])
    q3, k3, v3 = (t.reshape(B, T, D) for t in (q, k, v))
    slopes = _alibi_slopes()
    if cache_k is None:
        sl = jnp.broadcast_to(slopes[:, None, None], (MOBA_H, 1, LANES))
        o = _moba_prompt(tag + 'attn', q3, k3, v3, sl)
    else:
        n_pg = page_table.shape[1]
        past = n_pg * PAGE_SIZE
        assert past % MOBA_BLOCK == 0 and T <= PAGE_SIZE and T <= MOBA_BLOCK
        HQ = MOBA_H * T
        match_f = _head_match(PAGE_SIZE, T).astype(F32)
        s, ksum = _moba_scores(tag + 'scores', page_table, q3, cache_k, j, match_f)
        sl = jnp.repeat(slopes, T)[None, :]
        pad = lambda t: jnp.pad(t.reshape(B, T, MOBA_H, MOBA_DH), ((0, 0), (0, PAGE_SIZE - T), (0, 0), (0, 0)))
        p, pn = _moba_probs(tag + 'probs', s.reshape(B, past // MOBA_BLOCK, MOBA_BLOCK, HQ), ksum, q3, pad(k3),
                            sl, past)
        o = _moba_pv(tag + 'pv', page_table, p.reshape(B, n_pg, PAGE_SIZE, HQ), cache_v, pn, pad(v3), T, j, match_f)
    xn = _mm_ln(tag + 'wo', [o.reshape(M, D)], P['moba_wo'][j].astype(BF16), x, ln_g, ln_b, _ident)
    return xn, k3.reshape(B, T, MOBA_H, MOBA_DH), v3.reshape(B, T, MOBA_H, MOBA_DH)


def _ffn_layer(tag, x3, conv0, P, i, ln_g, ln_b):
    B, T, D = x3.shape
    w_in = P['ffn_w_in'][i]
    h, cst = _ffn_in(tag + 'in', x3, w_in[:, :D_FF].astype(BF16), w_in[:, D_FF:].astype(BF16), conv0,
                     P['ffn_conv_w'][i], P['ffn_conv_b'][i][None, :])
    xn = _mm_ln(tag + 'down', [h], P['ffn_w_down'][i].astype(BF16), x3.reshape(B * T, D), ln_g, ln_b, _ident)
    return xn.reshape(B, T, D), cst


def _trunk(grp, x3, wkv0, shift0, s5re0, s5im0, conv0, cache_k, cache_v, page_table, P):
    B, T, D = x3.shape
    wkv_new, shift_new, s5re_new, s5im_new, k_new, v_new, conv_new = [], [], [], [], [], [], []
    v_first = None
    for i in range(DEPTH):
        kind, j = i % 3, i // 3
        tag = f'{grp}{i}_'
        g0, b0 = P['ln_g'][i, 0][None, :], P['ln_b'][i, 0][None, :]
        g1, b1 = P['ln_g'][i, 1][None, :], P['ln_b'][i, 1][None, :]
        if kind == 0:
            xn, S, last, v_first = _rwkv_layer(tag, x3, shift0[j], wkv0[j], v_first, P, j, g0, b0)
            wkv_new.append(S)
            shift_new.append(last)
        elif kind == 1:
            xn, hr, hi = _s5_layer(tag, x3, s5re0[j], s5im0[j], P, j, g0, b0)
            s5re_new.append(hr)
            s5im_new.append(hi)
        else:
            xn, kr, vr = _moba_layer(tag, x3, cache_k, cache_v, page_table, P, j, g0, b0)
            k_new.append(kr)
            v_new.append(vr)
        x3, cst = _ffn_layer(tag + 'ffn_', xn.reshape(B, T, D), conv0[i], P, i, g1, b1)
        conv_new.append(cst)
    return (x3, jnp.stack(wkv_new), jnp.stack(shift_new), jnp.stack(s5re_new), jnp.stack(s5im_new),
            jnp.stack(conv_new), jnp.stack(k_new), jnp.stack(v_new))


def kernel(x_prompt, x_sample, state_rwkv_wkv, state_rwkv_shift, state_s5_re, state_s5_im, state_ffn_conv, cache_k, cache_v, page_table, ln_g, ln_b, rwkv_mu, rwkv_w_rkv, rwkv_w0, rwkv_w1, rwkv_w2, rwkv_a0, rwkv_a1, rwkv_a2, rwkv_v0, rwkv_v1, rwkv_v2, rwkv_g1, rwkv_g2, rwkv_k_k, rwkv_k_a, rwkv_r_k, rwkv_lnx_g, rwkv_lnx_b, rwkv_wo, s5_log_dt, s5_lam_re, s5_lam_im, s5_b_re, s5_b_im, s5_c_re, s5_c_im, s5_d, s5_w_val, s5_w_gate, moba_w_qkv, moba_wo, ffn_w_in, ffn_conv_w, ffn_conv_b, ffn_w_down):
    P = dict(ln_g=ln_g, ln_b=ln_b, rwkv_mu=rwkv_mu, rwkv_w_rkv=rwkv_w_rkv, rwkv_w0=rwkv_w0, rwkv_w1=rwkv_w1,
             rwkv_w2=rwkv_w2, rwkv_a0=rwkv_a0, rwkv_a1=rwkv_a1, rwkv_a2=rwkv_a2, rwkv_v0=rwkv_v0,
             rwkv_v1=rwkv_v1, rwkv_v2=rwkv_v2, rwkv_g1=rwkv_g1, rwkv_g2=rwkv_g2, rwkv_k_k=rwkv_k_k,
             rwkv_k_a=rwkv_k_a, rwkv_r_k=rwkv_r_k.reshape(rwkv_r_k.shape[0], D_MODEL), rwkv_lnx_g=rwkv_lnx_g,
             rwkv_lnx_b=rwkv_lnx_b, rwkv_wo=rwkv_wo, s5_log_dt=s5_log_dt, s5_lam_re=s5_lam_re,
             s5_lam_im=s5_lam_im, s5_b_re=s5_b_re, s5_b_im=s5_b_im, s5_c_re=s5_c_re, s5_c_im=s5_c_im, s5_d=s5_d,
             s5_w_val=s5_w_val, s5_w_gate=s5_w_gate, moba_w_qkv=moba_w_qkv, moba_wo=moba_wo,
             ffn_w_in=ffn_w_in, ffn_conv_w=ffn_conv_w, ffn_conv_b=ffn_conv_b, ffn_w_down=ffn_w_down)
    B = x_prompt.shape[0]
    n_rwkv, n_s5 = state_rwkv_wkv.shape[0], state_s5_re.shape[0]
    zeros = lambda *s: jnp.zeros(s, F32)
    outs_p = _trunk('p', x_prompt, zeros(n_rwkv, B, RWKV_H, RWKV_N, RWKV_N), zeros(n_rwkv, B, D_MODEL),
                    zeros(n_s5, B, S5_G, S5_P), zeros(n_s5, B, S5_G, S5_P), zeros(DEPTH, B, CONV_W - 1, D_FF),
                    None, None, None, P)
    outs_s = _trunk('s', x_sample, state_rwkv_wkv, state_rwkv_shift, state_s5_re, state_s5_im, state_ffn_conv,
                    cache_k, cache_v, page_table, P)
    return (outs_p[0], outs_s[0]) + tuple(outs_p[1:]) + tuple(outs_s[1:])
```

```python
import functools
import math

import jax
import jax.numpy as jnp
from jax import lax
from jax.experimental import pallas as pl
from jax.experimental.pallas import tpu as pltpu

F32 = jnp.float32
BF16 = jnp.bfloat16

D_MODEL = 2048
DEPTH = 4
RWKV_N = 64
RWKV_H = D_MODEL // RWKV_N
RWKV_GN_EPS = 64e-5
S5_GROUP = 16
S5_G = D_MODEL // S5_GROUP
S5_P = 64
MOBA_H = 16
MOBA_DH = D_MODEL // MOBA_H
MOBA_BLOCK = 256
MOBA_TOPK = 3
PAGE_SIZE = 128
D_FF = 11 * D_MODEL // 4
CONV_W = 3
LN_EPS = 1e-5
DN_ALPHA = (2.0 * DEPTH) ** 0.25
NEG = -1e30

LANES = 128
SUBLANES = 8
VMEM_LIMIT_BYTES = 56 * 1024 * 1024

MM_ROW_TILE = 1024
MM_LN_WEIGHT_TILE_BYTES = 12 * 1024 * 1024
WKV_CHUNK = 64
WKV_CHAINS = 16
WKV_HEADS = 4
MOBA_PAGES_PER_STEP = 4
S5_SLAB = LANES
S5_STATES = (S5_SLAB // S5_GROUP) * S5_P


def _params(n_axes):
    return pltpu.CompilerParams(dimension_semantics=("arbitrary",) * n_axes,
                                vmem_limit_bytes=VMEM_LIMIT_BYTES)


def _sigmoid(x):
    return 1.0 / (1.0 + jnp.exp(-x))


def _softplus(x):
    return jnp.maximum(x, 0.0) + jnp.log1p(jnp.exp(-jnp.abs(x)))


def _dg(a, b, dims):
    return lax.dot_general(a, b, (dims, ((), ())), preferred_element_type=F32)


_NN = ((1,), (0,))
_NT = ((1,), (1,))
_TN = ((0,), (0,))


def _split(a):
    hi = a.astype(BF16)
    lo = (a - hi.astype(F32)).astype(BF16)
    return hi, lo


def _dot3(a, b, dims=_NN):
    ah, al = _split(a)
    bh, bl = _split(b)
    return _dg(ah, bh, dims) + (_dg(al, bh, dims) + _dg(ah, bl, dims))


def _dot1(a, b, dims=_NN):
    return _dg(a.astype(BF16), b.astype(BF16), dims)


def _dot_exact_lhs(lhs_bf16, x):
    x1 = x.astype(BF16)
    r1 = x - x1.astype(F32)
    x2 = r1.astype(BF16)
    x3 = (r1 - x2.astype(F32)).astype(BF16)
    return _dg(lhs_bf16, x1, _NN) + (_dg(lhs_bf16, x2, _NN) + _dg(lhs_bf16, x3, _NN))


def _layer_norm(y, g, b):
    mu = jnp.mean(y, axis=-1, keepdims=True)
    yc = y - mu
    var = jnp.mean(yc * yc, axis=-1, keepdims=True)
    return yc * lax.rsqrt(var + LN_EPS) * g + b


def _mm_kernel(*refs, n_row, n_vec, n_w, n_erow, n_evec, n_out, prologue, epilogue):
    it = iter(refs)
    rows = [next(it) for _ in range(n_row)]
    vecs = [next(it) for _ in range(n_vec)]
    ws = [next(it) for _ in range(n_w)]
    erows = [next(it) for _ in range(n_erow)]
    evecs = [next(it) for _ in range(n_evec)]
    outs = [next(it) for _ in range(n_out)]
    a_scr = next(it)

    @pl.when(pl.program_id(1) == 0)
    def _():
        a_scr[...] = prologue(*[r[...] for r in rows], *[v[...] for v in vecs]).astype(BF16)

    a = a_scr[...]
    accs = [jnp.dot(a, w[...], preferred_element_type=F32) for w in ws]
    res = epilogue(*accs, *[e[...] for e in erows], *[e[...] for e in evecs])
    for o_ref, o in zip(outs, res):
        o_ref[...] = o.astype(o_ref.dtype)


def _mm(name, rows, vecs, ws, erows, evecs, prologue, epilogue, out_dtypes):
    M, K = rows[0].shape
    N = ws[0].shape[1]
    tm = MM_ROW_TILE if (M % MM_ROW_TILE == 0 and len(rows) == 1) else min(512, M)
    tn = 512 if N % 512 == 0 else N
    assert M % tm == 0 and N % tn == 0
    in_specs = ([pl.BlockSpec((tm, K), lambda i, j: (i, 0)) for _ in rows]
                + [pl.BlockSpec((1, K), lambda i, j: (0, 0)) for _ in vecs]
                + [pl.BlockSpec((K, tn), lambda i, j: (0, j)) for _ in ws]
                + [pl.BlockSpec((tm, tn), lambda i, j: (i, j)) for _ in erows]
                + [pl.BlockSpec((1, tn), lambda i, j: (0, j)) for _ in evecs])
    out_specs = [pl.BlockSpec((tm, tn), lambda i, j: (i, j)) for _ in out_dtypes]
    out_shape = [jax.ShapeDtypeStruct((M, N), dt) for dt in out_dtypes]
    kern = functools.partial(_mm_kernel, n_row=len(rows), n_vec=len(vecs), n_w=len(ws), n_erow=len(erows),
                             n_evec=len(evecs), n_out=len(out_dtypes), prologue=prologue, epilogue=epilogue)
    return pl.pallas_call(
        kern, grid=(M // tm, N // tn), in_specs=in_specs, out_specs=out_specs, out_shape=out_shape,
        scratch_shapes=[pltpu.VMEM((tm, K), BF16)], compiler_params=_params(2), name=name,
    )(*rows, *vecs, *ws, *erows, *evecs)


def _mm_ln_kernel(*refs, n_row, prologue, n_k):
    it = iter(refs)
    rows = [next(it) for _ in range(n_row)]
    w_ref, x_ref, g_ref, b_ref, o_ref, acc = (next(it) for _ in range(6))
    k = pl.program_id(1)
    a = prologue(*[r[...] for r in rows]).astype(BF16)
    p = jnp.dot(a, w_ref[...], preferred_element_type=F32)
    if n_k == 1:
        o_ref[...] = _layer_norm(DN_ALPHA * x_ref[...] + p, g_ref[...], b_ref[...])
        return

    @pl.when(k == 0)
    def _():
        acc[...] = p

    @pl.when(jnp.logical_and(k > 0, k < n_k - 1))
    def _():
        acc[...] += p

    @pl.when(k == n_k - 1)
    def _():
        o_ref[...] = _layer_norm(DN_ALPHA * x_ref[...] + (acc[...] + p), g_ref[...], b_ref[...])


def _mm_ln(name, rows, w, x, g, b, prologue):
    M, K = rows[0].shape
    N = w.shape[1]
    n_k = -(-K * N * 2 // MM_LN_WEIGHT_TILE_BYTES)
    tk = K // n_k
    tm = min(512 if n_k > 1 else 256, M)
    assert M % tm == 0 and K % n_k == 0 and tk % LANES == 0
    in_specs = ([pl.BlockSpec((tm, tk), lambda i, k: (i, k)) for _ in rows]
                + [pl.BlockSpec((tk, N), lambda i, k: (k, 0)),
                   pl.BlockSpec((tm, N), lambda i, k: (i, 0)),
                   pl.BlockSpec((1, N), lambda i, k: (0, 0)),
                   pl.BlockSpec((1, N), lambda i, k: (0, 0))])
    kern = functools.partial(_mm_ln_kernel, n_row=len(rows), prologue=prologue, n_k=n_k)
    return pl.pallas_call(
        kern, grid=(M // tm, n_k), in_specs=in_specs,
        out_specs=pl.BlockSpec((tm, N), lambda i, k: (i, 0)),
        out_shape=jax.ShapeDtypeStruct((M, N), F32),
        scratch_shapes=[pltpu.VMEM((tm, N), F32)], compiler_params=_params(2), name=name,
    )(*rows, w, x, g, b)


def _ln_kernel(x_ref, h_ref, g_ref, b_ref, o_ref):
    o_ref[...] = _layer_norm(DN_ALPHA * x_ref[...] + h_ref[...], g_ref[...], b_ref[...])


def _ln(name, x, h, g, b):
    M, N = x.shape
    tm = min(512, M)
    row = pl.BlockSpec((tm, N), lambda i: (i, 0))
    vec = pl.BlockSpec((1, N), lambda i: (0, 0))
    return pl.pallas_call(_ln_kernel, grid=(M // tm,), in_specs=[row, row, vec, vec], out_specs=row,
                          out_shape=jax.ShapeDtypeStruct((M, N), F32), compiler_params=_params(1), name=name)(x, h, g, b)


def _ffn_in_kernel(x_ref, wg_ref, wv_ref, c0_ref, cw_ref, cb_ref, h_ref, cs_ref, a_scr, carry, *, bt, tt):
    t = pl.program_id(1)
    j = pl.program_id(2)
    rows = bt * tt
    tn = wg_ref.shape[1]

    @pl.when(j == 0)
    def _():
        a_scr[...] = x_ref[...].reshape(rows, x_ref.shape[2]).astype(BF16)

    @pl.when(t == 0)
    def _():
        carry[j] = c0_ref[...]

    a = a_scr[...]
    hg = jnp.dot(a, wg_ref[...], preferred_element_type=F32)
    hv = jnp.dot(a, wv_ref[...], preferred_element_type=F32)
    prev = carry[j]
    p2 = jnp.broadcast_to(prev[:, 0:1, :], (bt, tt, tn)).reshape(rows, tn)
    p1 = jnp.broadcast_to(prev[:, 1:2, :], (bt, tt, tn)).reshape(rows, tn)
    r = lax.broadcasted_iota(jnp.int32, (rows, tn), 0) % tt
    s1 = jnp.where(r == 0, p1, pltpu.roll(hg, 1, axis=0))
    s2 = jnp.where(r == 0, p2, jnp.where(r == 1, p1, pltpu.roll(hg, 2, axis=0)))
    cw = cw_ref[...]
    c = cb_ref[...] + cw[0:1, :] * s2 + cw[1:2, :] * s1 + cw[2:3, :] * hg
    h_ref[...] = (c * _sigmoid(c) * hv).astype(h_ref.dtype)
    last = hg.reshape(bt, tt, tn)[:, tt - 2:, :]
    carry[j] = last
    cs_ref[0] = last


def _ffn_in(name, x3, wg, wv, conv0, cw, cb):
    B, T, K = x3.shape
    F = wg.shape[1]
    bt, tt = (1, MM_ROW_TILE) if T % MM_ROW_TILE == 0 else (B, T)
    tn = 512
    assert B % bt == 0 and T % tt == 0 and F % tn == 0 and tt >= CONV_W - 1
    n_j = F // tn
    kern = functools.partial(_ffn_in_kernel, bt=bt, tt=tt)
    n_t = T // tt
    h, cst = pl.pallas_call(
        kern, grid=(B // bt, n_t, n_j),
        in_specs=[pl.BlockSpec((bt, tt, K), lambda b, t, j: (b, t, 0)),
                  pl.BlockSpec((K, tn), lambda b, t, j: (0, j)),
                  pl.BlockSpec((K, tn), lambda b, t, j: (0, j)),
                  pl.BlockSpec((bt, CONV_W - 1, tn), lambda b, t, j: (b, 0, j)),
                  pl.BlockSpec((CONV_W, tn), lambda b, t, j: (0, j)),
                  pl.BlockSpec((1, tn), lambda b, t, j: (0, j))],
        out_specs=[pl.BlockSpec((bt * tt, tn), lambda b, t, j: (b * n_t + t, j)),
                   pl.BlockSpec((1, bt, CONV_W - 1, tn), lambda b, t, j: (t, b, 0, j))],
        out_shape=[jax.ShapeDtypeStruct((B * T, F), BF16),
                   jax.ShapeDtypeStruct((n_t, B, CONV_W - 1, F), F32)],
        scratch_shapes=[pltpu.VMEM((bt * tt, K), BF16), pltpu.VMEM((n_j, bt, CONV_W - 1, tn), F32)],
        compiler_params=_params(3), name=name,
    )(x3, wg, wv, conv0, cw, cb)
    return h, cst[n_t - 1]


RWKV_PROJ_TN = 512
RWKV_PROJ_TILES = D_MODEL // RWKV_PROJ_TN


def _rwkv_proj_kernel(*refs, has_vgate):
    it = iter(refs)
    x_ref, xp_ref, mu_ref, wm_ref, w1_ref, a1_ref, g1_ref, w2_ref, a2_ref, g2_ref, w0_ref, a0_ref = (
        next(it) for _ in range(12))
    if has_vgate:
        v1_ref, v2_ref, v0_ref, vf_ref = (next(it) for _ in range(4))
    r_ref, k_ref, v_ref, lw_ref, a_ref, g_ref = (next(it) for _ in range(6))
    mix_scr, hw_scr, ha_scr, hg_scr = (next(it) for _ in range(4))
    if has_vgate:
        hv_scr = next(it)
    j = pl.program_id(1)
    dot = lambda a, b: jnp.dot(a, b, preferred_element_type=F32)

    @pl.when(j == 0)
    def _():
        x = x_ref[...]
        d = xp_ref[...] - x
        mix = lambda n: (x + d * mu_ref[n:n + 1, :]).astype(BF16)
        mix_scr[0] = mix(0)
        mix_scr[1] = mix(2)
        xv = mix(3)
        mix_scr[2] = xv
        hw_scr[...] = jnp.tanh(dot(mix(1), w1_ref[...])).astype(BF16)
        ha_scr[...] = dot(mix(4), a1_ref[...]).astype(BF16)
        hg_scr[...] = _sigmoid(dot(mix(5), g1_ref[...])).astype(BF16)
        if has_vgate:
            hv_scr[...] = dot(xv, v1_ref[...]).astype(BF16)

    grp = j // RWKV_PROJ_TILES

    @pl.when(grp == 0)
    def _():
        r_ref[...] = dot(mix_scr[0], wm_ref[...]).astype(r_ref.dtype)
        lw_ref[...] = -jnp.exp(-_softplus(-(w0_ref[...] + dot(hw_scr[...], w2_ref[...]))) - 0.5)

    @pl.when(grp == 1)
    def _():
        k_ref[...] = dot(mix_scr[1], wm_ref[...]).astype(k_ref.dtype)
        a_ref[...] = _sigmoid(a0_ref[...] + dot(ha_scr[...], a2_ref[...])).astype(a_ref.dtype)

    @pl.when(grp == 2)
    def _():
        v = dot(mix_scr[2], wm_ref[...])
        if has_vgate:
            v = v + (vf_ref[...] - v) * _sigmoid(v0_ref[...] + dot(hv_scr[...], v2_ref[...]))
        v_ref[...] = v.astype(v_ref.dtype)
        g_ref[...] = dot(hg_scr[...], g2_ref[...]).astype(g_ref.dtype)


def _rwkv_proj(name, x, xp, mu, w_main, lora1, lora2, bias, vgate):
    M, K = x.shape
    tm, tn, nt = min(512, M), RWKV_PROJ_TN, RWKV_PROJ_TILES
    assert M % tm == 0 and w_main.shape == (K, 3 * D_MODEL)
    row = pl.BlockSpec((tm, K), lambda i, j: (i, 0))
    full = lambda a: pl.BlockSpec(a.shape, lambda i, j: (0, 0))
    col = lambda a: pl.BlockSpec((a.shape[0], tn), lambda i, j: (0, j % nt))
    ins = [x, xp, mu, w_main, *lora1, *lora2, *bias]
    specs = ([row, row, full(mu), pl.BlockSpec((K, tn), lambda i, j: (0, j))] + [full(w) for w in lora1]
             + [col(w) for w in lora2] + [col(b) for b in bias])
    scratch = [pltpu.VMEM((3, tm, K), BF16)] + [pltpu.VMEM((tm, w.shape[1]), BF16) for w in lora1]
    if vgate is not None:
        v1, v2, v0, v_first = vgate
        ins += [v1, v2, v0, v_first]
        specs += [full(v1), col(v2), col(v0), pl.BlockSpec((tm, tn), lambda i, j: (i, j % nt))]
        scratch.append(pltpu.VMEM((tm, v1.shape[1]), BF16))
    out = lambda grp: pl.BlockSpec((tm, tn), lambda i, j: (i, jnp.clip(j - grp * nt, 0, nt - 1)))
    return pl.pallas_call(
        functools.partial(_rwkv_proj_kernel, has_vgate=vgate is not None), grid=(M // tm, 3 * nt),
        in_specs=specs, out_specs=[out(0), out(1), out(2), out(0), out(1), out(2)],
        out_shape=[jax.ShapeDtypeStruct((M, D_MODEL), dt) for dt in (BF16, BF16, BF16, F32, BF16, BF16)],
        scratch_shapes=scratch, compiler_params=_params(2), name=name,
    )(*ins)


def _wkv_chunks(r, lw, k, v, a, prm, S, consts):
    C = WKV_CHUNK
    bdmask, l_incl, strict, incl, eye = consts

    def bd(xp):
        return jnp.concatenate([xp.astype(BF16)] * WKV_HEADS, axis=0) * bdmask

    def headsum(xp):
        hi, lo = _split(xp)
        return _dg(hi, bdmask, _NN) + _dg(lo, bdmask, _NN)

    def each(f, *lists):
        return [f(*args) for args in zip(*lists)]

    kkp, kap, rkp, lng, lnb = (list(t) for t in zip(*prm))
    kkn = each(lambda k_, p_: k_ * p_, k, kkp)
    kp = each(lambda k_, a_, p_: k_ * (1.0 + (a_ - 1.0) * p_), k, a, kap)
    sums = each(lambda kkn_, r_, kp_, p_: headsum(jnp.concatenate([kkn_ * kkn_, r_ * kp_ * p_], axis=0)),
                kkn, r, kp, rkp)
    kk = each(lambda kkn_, s_: kkn_ / jnp.maximum(jnp.sqrt(s_[:C]), 1e-12), kkn, sums)
    bonus = each(lambda s_, v_: s_[C:] * v_, sums, v)
    b = each(lambda kk_, a_: kk_ * a_, kk, a)
    cum = each(lambda lw_: _dot_exact_lhs(l_incl, lw_), lw)
    e_neg = each(lambda c_: jnp.exp(-c_), cum)
    x = each(lambda kk_, c_, lw_, r_: jnp.concatenate([kk_ * jnp.exp(c_ - lw_), r_ * jnp.exp(c_)],
                                                       axis=0).astype(BF16), kk, cum, lw, r)
    g_b = each(lambda x_, b_, e_: _dg(x_, bd(b_ * e_), _NT), x, b, e_neg)
    g_k = each(lambda x_, k_, e_: _dg(x_, bd(k_ * e_), _NT), x, kp, e_neg)
    xs = each(lambda x_, s_: _dg(x_, s_.astype(BF16), _NT), x, S)
    mb = each(lambda g_: jnp.where(incl, g_[C:], 0.0), g_b)
    kv = each(lambda g_, v_: _dg(jnp.concatenate([jnp.where(strict, g_[:C], 0.0), jnp.where(incl, g_[C:], 0.0)],
                                                 axis=0).astype(BF16), bd(v_), _NN), g_k, v)
    rhs = each(lambda xs_, kv_: -(xs_[:C] + kv_[:C]), xs, kv)
    npow = each(lambda g_: -jnp.where(strict, g_[:C], 0.0), g_b)
    tinv = each(lambda n_: eye + n_, npow)
    npow = each(lambda n_: _dg(n_.astype(BF16), bd(n_), _NN), npow)
    for _ in range(int(math.log2(C)) - 2):
        both = each(lambda t_, n_: _dg(jnp.concatenate([t_, n_], axis=0).astype(BF16), bd(n_), _NN), tinv, npow)
        tinv = each(lambda t_, b_: t_ + b_[:C], tinv, both)
        npow = each(lambda b_: b_[C:], both)
    tinv = each(lambda t_, n_: t_ + _dg(t_.astype(BF16), bd(n_), _NN), tinv, npow)
    u = each(lambda t_, r_: _dg(t_.astype(BF16), bd(r_), _NN), tinv, rhs)
    y = each(lambda xs_, kv_, mb_, u_: xs_[C:] + kv_[C:] + _dg(mb_.astype(BF16), bd(u_), _NN), xs, kv, mb, u)
    upd = each(lambda u_, v_, b_, k_, c_: _dg(
        jnp.concatenate([u_, v_], axis=0).astype(BF16),
        (jnp.concatenate([b_, k_], axis=0) * jnp.exp(c_[C - 1:C, :] - jnp.concatenate([c_, c_], axis=0))).astype(BF16),
        _TN), u, v, b, kp, cum)
    bdmask_f = bdmask.astype(F32)
    s_new = each(lambda s_, c_, u_: s_ * jnp.exp(c_[C - 1:C, :]) + u_ * bdmask_f, S, cum, upd)
    mu = each(lambda y_: headsum(y_) * (1.0 / RWKV_N), y)
    yc = each(lambda y_, m_: y_ - m_, y, mu)
    var = each(lambda yc_: headsum(yc_ * yc_) * (1.0 / RWKV_N), yc)
    z = each(lambda yc_, var_, g_, b_, bo_: yc_ * lax.rsqrt(var_ + RWKV_GN_EPS) * g_ + b_ + bo_,
             yc, var, lng, lnb, bonus)
    return z, s_new


def _wkv_kernel(r_ref, lw_ref, k_ref, v_ref, a_ref, kk_ref, ka_ref, rk_ref, g_ref, b_ref, s0_ref,
                z_ref, so_ref, s_scr, *, n_t, nb, n_grp):
    t = pl.program_id(1)
    C, N, HW = WKV_CHUNK, RWKV_N, WKV_HEADS * RWKV_N
    ri = lax.broadcasted_iota(jnp.int32, (HW, HW), 0) // N
    ci = lax.broadcasted_iota(jnp.int32, (HW, HW), 1) // N
    bdmask_f = (ri == ci).astype(F32)

    chains = [(bi, g) for bi in range(nb) for g in range(n_grp)]

    @pl.when(t == 0)
    def _():
        for n, (bi, g) in enumerate(chains):
            rows = s0_ref[bi, g * WKV_HEADS:(g + 1) * WKV_HEADS].reshape(HW, N)
            s_scr[n] = jnp.concatenate([rows] * WKV_HEADS, axis=1) * bdmask_f

    tt = lax.broadcasted_iota(jnp.int32, (C, C), 0)
    ss = lax.broadcasted_iota(jnp.int32, (C, C), 1)
    tp = lax.broadcasted_iota(jnp.int32, (C, HW), 0)
    sp = lax.broadcasted_iota(jnp.int32, (C, HW), 1) % N
    consts = (bdmask_f.astype(BF16), (tt >= ss).astype(BF16), tp > sp, tp >= sp, (tp == sp).astype(F32))
    sls = [slice(g * HW, (g + 1) * HW) for _, g in chains]
    tok = lambda ref: [ref[bi, :, sl].astype(F32) for (bi, _), sl in zip(chains, sls)]
    prm = [tuple(ref[:, sl] for ref in (kk_ref, ka_ref, rk_ref, g_ref, b_ref)) for sl in sls]
    zs, s_news = _wkv_chunks(tok(r_ref), tok(lw_ref), tok(k_ref), tok(v_ref), tok(a_ref), prm,
                             [s_scr[n] for n in range(len(chains))], consts)
    for n, ((bi, _), sl) in enumerate(zip(chains, sls)):
        s_scr[n] = s_news[n]
        z_ref[bi, :, sl] = zs[n]

    @pl.when(t == n_t - 1)
    def _():
        for n, (bi, g) in enumerate(chains):
            for h in range(WKV_HEADS):
                so_ref[bi, g * WKV_HEADS + h] = s_scr[n, h * N:(h + 1) * N, h * N:(h + 1) * N]


def _wkv(name, r, lw, k, v, a, kkp, kap, rkp, lng, lnb, s0):
    B, T, D = r.shape
    C, HW = WKV_CHUNK, WKV_HEADS * RWKV_N
    n_grp = min(max(1, WKV_CHAINS // B), D // HW)
    W = n_grp * HW
    assert T % C == 0 and D % W == 0
    n_t = T // C
    tok = pl.BlockSpec((B, C, W), lambda h, t: (0, t, h))
    vec = pl.BlockSpec((1, W), lambda h, t: (0, h))
    st = pl.BlockSpec((B, n_grp * WKV_HEADS, RWKV_N, RWKV_N), lambda h, t: (0, h, 0, 0))
    return pl.pallas_call(
        functools.partial(_wkv_kernel, n_t=n_t, nb=B, n_grp=n_grp), grid=(D // W, n_t),
        in_specs=[tok] * 5 + [vec] * 5 + [st], out_specs=[tok, st],
        out_shape=[jax.ShapeDtypeStruct((B, T, D), F32), jax.ShapeDtypeStruct(s0.shape, F32)],
        scratch_shapes=[pltpu.VMEM((B * n_grp, HW, HW), F32)], compiler_params=_params(2), name=name,
    )(r, lw, k, v, a, kkp, kap, rkp, lng, lnb, s0)


def _cmul(ar, ai, br, bi):
    return ar * br - ai * bi, ar * bi + ai * br


def _dot3w(a, w_hi, w_lo, dims=_NN):
    ah, al = _split(a)
    return _dg(ah, w_hi, dims) + (_dg(al, w_hi, dims) + _dg(ah, w_lo, dims))


S5_SHIFTS = (1, 2, 4)


def _s5_kernel(x_ref, wbh_ref, wbl_ref, wch_ref, wcl_ref, lr_ref, li_ref, dt_ref, d_ref, h0r_ref, h0i_ref,
               z_ref, hr_ref, hi_ref, pw_re, pw_im, sh_re, sh_im, cf, hc, *, n_t, tt):
    t = pl.program_id(2)
    R = SUBLANES

    @pl.when(t == 0)
    def _():
        lr, li, dt = lr_ref[0], li_ref[0], dt_ref[0]
        rid = lax.broadcasted_iota(jnp.int32, (R, S5_STATES), 0)
        n = (rid + 1).astype(F32)
        mag = jnp.exp(lr * dt * n)
        ang = li * dt * n
        pw_re[...] = mag * jnp.cos(ang)
        pw_im[...] = mag * jnp.sin(ang)
        for m, sh in enumerate(S5_SHIFTS):
            sh_re[m] = jnp.where(rid >= sh, pw_re[sh - 1:sh, :], 0.0)
            sh_im[m] = jnp.where(rid >= sh, pw_im[sh - 1:sh, :], 0.0)
        ar, ai = pw_re[0:1, :], pw_im[0:1, :]
        den = lr * lr + li * li
        cf[0:1, :] = ((ar - 1.0) * lr + ai * li) / den
        cf[1:2, :] = (ai * lr - (ar - 1.0) * li) / den
        hc[0:1, :] = h0r_ref[0]
        hc[1:2, :] = h0i_ref[0]

    x = x_ref[0]
    bu = _dot3w(x, wbh_ref[0], wbl_ref[0])
    cr, ci = cf[0:1, :], cf[1:2, :]
    br, bi = _cmul(cr, ci, bu[:, :S5_STATES], bu[:, S5_STATES:])
    pr, pi = pw_re[...], pw_im[...]
    hr, hi = hc[0:1, :], hc[1:2, :]
    xrs, xis = [], []
    for i in range(tt // R):
        vr, vi = br[i * R:(i + 1) * R, :], bi[i * R:(i + 1) * R, :]
        for m, sh in enumerate(S5_SHIFTS):
            ur, ui = _cmul(sh_re[m], sh_im[m], pltpu.roll(vr, sh, axis=0), pltpu.roll(vi, sh, axis=0))
            vr, vi = vr + ur, vi + ui
        ur, ui = _cmul(pr, pi, hr, hi)
        vr, vi = vr + ur, vi + ui
        xrs.append(vr)
        xis.append(vi)
        hr, hi = vr[R - 1:R, :], vi[R - 1:R, :]
    hc[0:1, :] = hr
    hc[1:2, :] = hi
    xcat = jnp.concatenate([jnp.concatenate(xrs, axis=0), jnp.concatenate(xis, axis=0)], axis=-1)
    y = _dot3w(xcat, wch_ref[0], wcl_ref[0]) + d_ref[...] * x
    z_ref[0] = 0.5 * y * (1.0 + jnp.tanh(math.sqrt(2.0 / math.pi) * (y + 0.044715 * (y * y * y))))

    @pl.when(t == n_t - 1)
    def _():
        hr_ref[0] = hr
        hi_ref[0] = hi


def _s5(name, x3, wb, wc, lr, li, dt, d, h0r, h0i):
    B, T, D = x3.shape
    tt = min(256, T)
    assert T % tt == 0 and tt % SUBLANES == 0
    n_t = T // tt
    n_s = D // S5_SLAB
    S = S5_STATES
    st = pl.BlockSpec((1, 1, S), lambda b, s, t: (b, 0, s))
    pv = pl.BlockSpec((1, 1, S), lambda b, s, t: (0, 0, s))
    w_in = pl.BlockSpec((1, S5_SLAB, 2 * S), lambda b, s, t: (s, 0, 0))
    w_out = pl.BlockSpec((1, 2 * S, S5_SLAB), lambda b, s, t: (s, 0, 0))
    split = lambda w: (w.astype(BF16), (w - w.astype(BF16).astype(F32)).astype(BF16))
    return pl.pallas_call(
        functools.partial(_s5_kernel, n_t=n_t, tt=tt), grid=(B, n_s, n_t),
        in_specs=[pl.BlockSpec((1, tt, S5_SLAB), lambda b, s, t: (b, t, s)),
                  w_in, w_in, w_out, w_out,
                  pv, pv, pv,
                  pl.BlockSpec((1, S5_SLAB), lambda b, s, t: (0, s)),
                  st, st],
        out_specs=[pl.BlockSpec((1, tt, S5_SLAB), lambda b, s, t: (b, t, s)), st, st],
        out_shape=[jax.ShapeDtypeStruct((B, T, D), F32),
                   jax.ShapeDtypeStruct((B, 1, n_s * S), F32), jax.ShapeDtypeStruct((B, 1, n_s * S), F32)],
        scratch_shapes=[pltpu.VMEM((SUBLANES, S), F32), pltpu.VMEM((SUBLANES, S), F32),
                        pltpu.VMEM((len(S5_SHIFTS), SUBLANES, S), F32),
                        pltpu.VMEM((len(S5_SHIFTS), SUBLANES, S), F32),
                        pltpu.VMEM((SUBLANES, S), F32), pltpu.VMEM((SUBLANES, S), F32)],
        compiler_params=_params(3), name=name,
    )(x3, *split(wb), *split(wc), lr, li, dt, d, h0r, h0i)


def _select_topk(gate, idx, valid, axis):
    g = jnp.where(valid, gate, NEG)
    big = jnp.int32(1 << 30)
    sel = jnp.zeros(gate.shape, jnp.bool_)
    for _ in range(MOBA_TOPK):
        m = jnp.max(g, axis=axis, keepdims=True)
        first = jnp.min(jnp.where(g == m, idx, big), axis=axis, keepdims=True)
        pick = idx == first
        sel = jnp.logical_or(sel, pick)
        g = jnp.where(pick, -jnp.inf, g)
    return jnp.logical_and(sel, valid)


def _moba_prompt_kernel(q_ref, k_ref, v_ref, sl_ref, o_ref, kmean_scr, *, n_blk):
    i = pl.program_id(2)
    BK = MOBA_BLOCK

    @pl.when(i == 0)
    def _():
        kmean_scr[...] = jnp.concatenate([jnp.sum(k_ref[0, n * BK:(n + 1) * BK, :], axis=0, keepdims=True)
                                          for n in range(n_blk)], axis=0) * (1.0 / BK)

    q = q_ref[0]
    slope = sl_ref[0][:, 0:1]
    qi = lax.broadcasted_iota(jnp.int32, (BK, BK), 0)
    ki = lax.broadcasted_iota(jnp.int32, (BK, BK), 1)
    causal = qi >= ki
    bias0 = slope * (qi - ki).astype(F32)
    qb = q.astype(BF16)
    scale = MOBA_DH ** -0.5

    def tile(c):
        ss = []
        if c > 0:
            gate_t = _dot3(kmean_scr[...], q, _NT)
            nidx = lax.broadcasted_iota(jnp.int32, (n_blk, BK), 0)
            sel_t = jnp.where(_select_topk(gate_t, nidx, nidx < c, axis=0), 1.0, 0.0).astype(BF16)
            sel = _dg(jnp.where(qi == ki, 1.0, 0.0).astype(BF16), sel_t, _NT) > 0.5
        for n in range(c + 1):
            s = _dg(qb, k_ref[0, n * BK:(n + 1) * BK, :].astype(BF16), _NT) * scale
            s = s - (bias0 + slope * float((c - n) * BK))
            ss.append(jnp.where(causal if n == c else sel[:, n:n + 1], s, NEG))
        m = jnp.max(functools.reduce(jnp.maximum, ss), axis=-1, keepdims=True)
        ps = [jnp.exp(s - m) for s in ss]
        l = jnp.sum(functools.reduce(lambda x, y: x + y, ps), axis=-1, keepdims=True)
        acc = functools.reduce(lambda x, y: x + y,
                               [_dg(p.astype(BF16), v_ref[0, n * BK:(n + 1) * BK, :].astype(BF16), _NN)
                                for n, p in enumerate(ps)])
        o_ref[0] = acc / l

    for c in range(n_blk):
        pl.when(i == c)(functools.partial(tile, c))


def _moba_prompt(name, q, k, v, slopes):
    B, T, D = q.shape
    n_blk = T // MOBA_BLOCK
    assert T % MOBA_BLOCK == 0
    qs = pl.BlockSpec((1, MOBA_BLOCK, MOBA_DH), lambda b, h, i: (b, i, h))
    kv = pl.BlockSpec((1, T, MOBA_DH), lambda b, h, i: (b, 0, h))
    return pl.pallas_call(
        functools.partial(_moba_prompt_kernel, n_blk=n_blk), grid=(B, MOBA_H, n_blk),
        in_specs=[qs, kv, kv, pl.BlockSpec((1, 1, LANES), lambda b, h, i: (h, 0, 0))], out_specs=qs,
        out_shape=jax.ShapeDtypeStruct((B, T, D), F32),
        scratch_shapes=[pltpu.VMEM((n_blk, MOBA_DH), F32)],
        compiler_params=_params(3), name=name,
    )(q, k, v, slopes)


def _q_rows(q, nq):
    return jnp.concatenate([q[:, h * MOBA_DH:(h + 1) * MOBA_DH] for h in range(MOBA_H)], axis=0)


def _head_match(n_rows, nq):
    shape = (n_rows * MOBA_H, MOBA_H * nq)
    return lax.broadcasted_iota(jnp.int32, shape, 0) % MOBA_H == lax.broadcasted_iota(jnp.int32, shape, 1) // nq


def _token_scores(k3, qr, nq, dot):
    n_rows = k3.shape[0]
    s2 = dot(k3.reshape(n_rows * MOBA_H, MOBA_DH), qr, _NT)
    s2 = jnp.where(_head_match(n_rows, nq), s2, 0.0)
    return jnp.sum(s2.reshape(n_rows, MOBA_H, MOBA_H * nq), axis=1)


def _moba_scores_kernel(pt_ref, q_ref, m_ref, *refs, nq, pp):
    k_refs, (s_ref, ks_ref) = refs[:pp], refs[pp:]
    qr = _q_rows(q_ref[0], nq).astype(BF16)
    kps = [k_ref[0, 0] for k_ref in k_refs]
    s2s = [_dg(kp.reshape(PAGE_SIZE * MOBA_H, MOBA_DH).astype(BF16), qr, _NT) for kp in kps]
    for u in range(pp):
        s_ref[0, u] = jnp.sum((s2s[u] * m_ref[...]).reshape(PAGE_SIZE, MOBA_H, MOBA_H * nq), axis=1)
        ks_ref[0, u] = jnp.sum(kps[u], axis=0)


def _page_specs(layer, pp):
    return [pl.BlockSpec((1, 1, PAGE_SIZE, MOBA_H, MOBA_DH),
                         lambda b, g, pt, u=u: (layer, pt[b, g * pp + u], 0, 0, 0)) for u in range(pp)]


def _moba_scores(name, page_table, q, cache_k, layer, match_f):
    B, nq, D = q.shape
    n_pg = page_table.shape[1]
    pp = MOBA_PAGES_PER_STEP
    assert n_pg % pp == 0
    gs = pltpu.PrefetchScalarGridSpec(
        num_scalar_prefetch=1, grid=(B, n_pg // pp),
        in_specs=[pl.BlockSpec((1, nq, D), lambda b, g, pt: (b, 0, 0)),
                  pl.BlockSpec(match_f.shape, lambda b, g, pt: (0, 0))] + _page_specs(layer, pp),
        out_specs=[pl.BlockSpec((1, pp, PAGE_SIZE, MOBA_H * nq), lambda b, g, pt: (b, g, 0, 0)),
                   pl.BlockSpec((1, pp, MOBA_H, MOBA_DH), lambda b, g, pt: (b, g, 0, 0))])
    return pl.pallas_call(
        functools.partial(_moba_scores_kernel, nq=nq, pp=pp), grid_spec=gs,
        out_shape=[jax.ShapeDtypeStruct((B, n_pg, PAGE_SIZE, MOBA_H * nq), F32),
                   jax.ShapeDtypeStruct((B, n_pg, MOBA_H, MOBA_DH), F32)],
        compiler_params=_params(2), name=name,
    )(page_table, q, match_f, *([cache_k] * pp))


def _moba_probs_kernel(s_ref, ks_ref, q_ref, kn_ref, sl_ref, p_ref, pn_ref, sel_scr, *, nq, n_blk, past):
    BK = MOBA_BLOCK
    HQ = MOBA_H * nq
    qr = _q_rows(q_ref[0], nq)
    ppb = BK // PAGE_SIZE
    kmean = jnp.sum(ks_ref[0].reshape(n_blk, ppb, MOBA_H, MOBA_DH), axis=1) * (1.0 / BK)
    gate = _token_scores(kmean, qr, nq, _dot3)
    nidx = lax.broadcasted_iota(jnp.int32, (n_blk, HQ), 0)
    own = past // BK
    sel = _select_topk(gate, nidx, nidx < own, axis=0)
    sel_scr[...] = sel.astype(F32)
    slope = sl_ref[...]
    scale = MOBA_DH ** -0.5
    qpos = past + lax.broadcasted_iota(jnp.int32, (1, HQ), 1) % nq
    tok = lax.broadcasted_iota(jnp.int32, (BK, HQ), 0)

    def scores(n):
        s = s_ref[0, n] * scale - slope * (qpos - (n * BK + tok)).astype(F32)
        return jnp.where(sel_scr[pl.ds(n, 1), :] > 0.5, s, NEG)

    sn = _token_scores(kn_ref[0], qr, nq, _dot1) * scale
    tn = lax.broadcasted_iota(jnp.int32, (PAGE_SIZE, HQ), 0)
    dn = qpos - (past + tn)
    sn = jnp.where(jnp.logical_and(dn >= 0, tn < nq), sn - slope * dn.astype(F32), NEG)
    m = lax.fori_loop(0, n_blk, lambda n, m_: jnp.maximum(m_, jnp.max(scores(n), axis=0, keepdims=True)),
                      jnp.max(sn, axis=0, keepdims=True))
    pn = jnp.exp(sn - m)
    l = lax.fori_loop(0, n_blk, lambda n, l_: l_ + jnp.sum(jnp.exp(scores(n) - m), axis=0, keepdims=True),
                      jnp.sum(pn, axis=0, keepdims=True))
    pn_ref[0] = (pn / l).astype(pn_ref.dtype)

    def write(n, c):
        p_ref[0, n] = (jnp.exp(scores(n) - m) / l).astype(p_ref.dtype)
        return c

    lax.fori_loop(0, n_blk, write, 0)


def _moba_probs(name, s, ksum, q, k_new_pad, slopes_hq, past):
    B, n_blk, BK, HQ = s.shape
    nq = q.shape[1]
    blk = pl.BlockSpec((1, n_blk, BK, HQ), lambda b: (b, 0, 0, 0))
    return pl.pallas_call(
        functools.partial(_moba_probs_kernel, nq=nq, n_blk=n_blk, past=past), grid=(B,),
        in_specs=[blk,
                  pl.BlockSpec((1, ksum.shape[1], MOBA_H, MOBA_DH), lambda b: (b, 0, 0, 0)),
                  pl.BlockSpec((1, nq, D_MODEL), lambda b: (b, 0, 0)),
                  pl.BlockSpec((1, PAGE_SIZE, MOBA_H, MOBA_DH), lambda b: (b, 0, 0, 0)),
                  pl.BlockSpec((1, HQ), lambda b: (0, 0))],
        out_specs=[blk, pl.BlockSpec((1, PAGE_SIZE, HQ), lambda b: (b, 0, 0))],
        out_shape=[jax.ShapeDtypeStruct((B, n_blk, BK, HQ), BF16), jax.ShapeDtypeStruct((B, PAGE_SIZE, HQ), BF16)],
        scratch_shapes=[pltpu.VMEM((n_blk, HQ), F32)],
        compiler_params=_params(1), name=name,
    )(s, ksum, q, k_new_pad, slopes_hq)


def _moba_pv_kernel(pt_ref, p_ref, m_ref, pn_ref, vn_ref, *refs, nq, n_steps, pp):
    v_refs, (o_ref, acc) = refs[:pp], refs[pp:]
    g = pl.program_id(1)
    HQ = MOBA_H * nq

    def spread(p):
        p2 = jnp.broadcast_to(p.astype(F32)[:, None, :], (PAGE_SIZE, MOBA_H, HQ))
        return (p2.reshape(PAGE_SIZE * MOBA_H, HQ) * m_ref[...]).astype(BF16)

    def values(v3):
        return v3.reshape(PAGE_SIZE * MOBA_H, MOBA_DH).astype(BF16)

    @pl.when(g == 0)
    def _():
        acc[...] = _dg(spread(pn_ref[0]), values(vn_ref[0]), _TN)

    p2s = [spread(p_ref[0, u]) for u in range(pp)]
    v2s = [values(v_ref[0, 0]) for v_ref in v_refs]
    acc[...] += functools.reduce(lambda x, y: x + y, [_dg(p2, v2, _TN) for p2, v2 in zip(p2s, v2s)])

    @pl.when(g == n_steps - 1)
    def _():
        a = acc[...]
        o_ref[0] = jnp.concatenate([a[h * nq:(h + 1) * nq, :] for h in range(MOBA_H)], axis=-1)


def _moba_pv(name, page_table, p, cache_v, pn, v_new_pad, nq, layer, match_f):
    B, n_pg, _, HQ = p.shape
    D = D_MODEL
    pp = MOBA_PAGES_PER_STEP
    assert n_pg % pp == 0
    gs = pltpu.PrefetchScalarGridSpec(
        num_scalar_prefetch=1, grid=(B, n_pg // pp),
        in_specs=[pl.BlockSpec((1, pp, PAGE_SIZE, HQ), lambda b, g, pt: (b, g, 0, 0)),
                  pl.BlockSpec(match_f.shape, lambda b, g, pt: (0, 0)),
                  pl.BlockSpec((1, PAGE_SIZE, HQ), lambda b, g, pt: (b, 0, 0)),
                  pl.BlockSpec((1, PAGE_SIZE, MOBA_H, MOBA_DH), lambda b, g, pt: (b, 0, 0, 0))]
        + _page_specs(layer, pp),
        out_specs=pl.BlockSpec((1, nq, D), lambda b, g, pt: (b, 0, 0)),
        scratch_shapes=[pltpu.VMEM((HQ, MOBA_DH), F32)])
    return pl.pallas_call(
        functools.partial(_moba_pv_kernel, nq=nq, n_steps=n_pg // pp, pp=pp), grid_spec=gs,
        out_shape=jax.ShapeDtypeStruct((B, nq, D), F32), compiler_params=_params(2), name=name,
    )(page_table, p, match_f, pn, v_new_pad, *([cache_v] * pp))


def _ident(x):
    return x


def _mix(x, xp, mu):
    return x + (xp - x) * mu


def _pad_cols(w, n):
    return jnp.pad(w, ((0, 0), (0, n - w.shape[1])))


def _pad_rows(w, n):
    return jnp.pad(w, ((0, n - w.shape[0]), (0, 0)))


def _rwkv_layer(tag, x3, shift0, s0, v_first, P, j, ln_g, ln_b):
    B, T, D = x3.shape
    M = B * T
    x = x3.reshape(M, D)
    xp = jnp.concatenate([shift0[:, None, :], x3[:, :-1]], axis=1).reshape(M, D)
    wb = lambda w: w.astype(BF16)
    lp = LANES
    w_rkv = P['rwkv_w_rkv'][j]
    w_main = jnp.concatenate([wb(w_rkv[0]), wb(w_rkv[1]), wb(w_rkv[2])], axis=1)
    lora1 = [wb(_pad_cols(P['rwkv_w1'][j], lp)), wb(_pad_cols(P['rwkv_a1'][j], lp)), wb(P['rwkv_g1'][j])]
    lora2 = [wb(_pad_rows(P['rwkv_w2'][j], lp)), wb(_pad_rows(P['rwkv_a2'][j], lp)), wb(P['rwkv_g2'][j])]
    bias = [P['rwkv_w0'][j][None, :], P['rwkv_a0'][j][None, :]]
    vgate = None
    if j > 0:
        vgate = (wb(_pad_cols(P['rwkv_v1'][j - 1], lp)), wb(_pad_rows(P['rwkv_v2'][j - 1], lp)),
                 P['rwkv_v0'][j - 1][None, :], v_first)
    r, k, v, lw, a, g = _rwkv_proj(tag + 'proj', x, xp, P['rwkv_mu'][j], w_main, lora1, lora2, bias, vgate)
    if j == 0:
        v_first = v
    Tp = -(-T // WKV_CHUNK) * WKV_CHUNK
    seq = lambda t: jnp.pad(t.reshape(B, T, D), ((0, 0), (0, Tp - T), (0, 0)))
    vecp = lambda name: P[name][j].reshape(1, D)
    z, s_new = _wkv(tag + 'wkv', seq(r), seq(lw), seq(k), seq(v), seq(a), vecp('rwkv_k_k'), vecp('rwkv_k_a'),
                    vecp('rwkv_r_k'), vecp('rwkv_lnx_g'), vecp('rwkv_lnx_b'), s0)
    z = z[:, :T].reshape(M, D)
    xn = _mm_ln(tag + 'wo', [z, g], wb(P['rwkv_wo'][j]), x, ln_g, ln_b, lambda z_, g_: z_ * g_)
    return xn, s_new, x3[:, -1], v_first


def _s5_weights(P, j):
    ns, gs = D_MODEL // S5_SLAB, S5_SLAB // S5_GROUP
    eye = jnp.eye(gs, dtype=F32)

    def bd_in(b):
        bt = jnp.swapaxes(b.reshape(ns, gs, S5_P, S5_GROUP), 2, 3)
        return jnp.einsum('sgcp,gh->sgchp', bt, eye).reshape(ns, S5_SLAB, S5_STATES)

    def bd_out(c):
        ct = jnp.swapaxes(c.reshape(ns, gs, S5_GROUP, S5_P), 2, 3)
        return jnp.einsum('sgpc,gh->sgphc', ct, eye).reshape(ns, S5_STATES, S5_SLAB)

    wb = jnp.concatenate([bd_in(P['s5_b_re'][j]), bd_in(P['s5_b_im'][j])], axis=2)
    wc = jnp.concatenate([bd_out(P['s5_c_re'][j]), -bd_out(P['s5_c_im'][j])], axis=1)
    flat = lambda t: t.reshape(1, 1, S5_G * S5_P)
    dt = jnp.broadcast_to(jnp.exp(P['s5_log_dt'][j])[:, None], (S5_G, S5_P))
    return wb, wc, flat(P['s5_lam_re'][j]), flat(P['s5_lam_im'][j]), flat(dt)


def _s5_layer(tag, x3, h0r, h0i, P, j, ln_g, ln_b):
    B, T, D = x3.shape
    M = B * T
    x = x3.reshape(M, D)
    wb, wc, lr, li, dt = _s5_weights(P, j)
    st = lambda h: h.reshape(B, 1, S5_G * S5_P)
    z, hr, hi = _s5(tag + 'scan', x3, wb, wc, lr, li, dt, P['s5_d'][j][None, :], st(h0r), st(h0i))
    out, = _mm(tag + 'gate', [z.reshape(M, D)], [], [P['s5_w_val'][j].astype(BF16), P['s5_w_gate'][j].astype(BF16)],
               [], [], _ident, lambda val, gate: (val * _sigmoid(gate),), [F32])
    xn = _ln(tag + 'ln', x, out, ln_g, ln_b)
    return xn, hr.reshape(B, S5_G, S5_P), hi.reshape(B, S5_G, S5_P)


def _alibi_slopes():
    return 2.0 ** (-8.0 * jnp.arange(1, MOBA_H + 1, dtype=F32) / MOBA_H)


def _moba_layer(tag, x3, cache_k, cache_v, page_table, P, j, ln_g, ln_b):
    B, T, D = x3.shape
    M = B * T
    x = x3.reshape(M, D)
    wqkv = P['moba_w_qkv'][j]
    ws = [wqkv[:, n * D:(n + 1) * D].astype(BF16) for n in range(3)]
    q, k, v = _mm(tag + 'qkv', [x], [], ws, [], [], _ident, lambda a, b, c: (a, b, c), [F32, F32, F32])
    q3, k3, v3 = (t.reshape(B, T, D) for t in (q, k, v))
    slopes = _alibi_slopes()
    if cache_k is None:
        sl = jnp.broadcast_to(slopes[:, None, None], (MOBA_H, 1, LANES))
        o = _moba_prompt(tag + 'attn', q3, k3, v3, sl)
    else:
        n_pg = page_table.shape[1]
        past = n_pg * PAGE_SIZE
        assert past % MOBA_BLOCK == 0 and T <= PAGE_SIZE and T <= MOBA_BLOCK
        HQ = MOBA_H * T
        match_f = _head_match(PAGE_SIZE, T).astype(F32)
        s, ksum = _moba_scores(tag + 'scores', page_table, q3, cache_k, j, match_f)
        sl = jnp.repeat(slopes, T)[None, :]
        pad = lambda t: jnp.pad(t.reshape(B, T, MOBA_H, MOBA_DH), ((0, 0), (0, PAGE_SIZE - T), (0, 0), (0, 0)))
        p, pn = _moba_probs(tag + 'probs', s.reshape(B, past // MOBA_BLOCK, MOBA_BLOCK, HQ), ksum, q3, pad(k3),
                            sl, past)
        o = _moba_pv(tag + 'pv', page_table, p.reshape(B, n_pg, PAGE_SIZE, HQ), cache_v, pn, pad(v3), T, j, match_f)
    xn = _mm_ln(tag + 'wo', [o.reshape(M, D)], P['moba_wo'][j].astype(BF16), x, ln_g, ln_b, _ident)
    return xn, k3.reshape(B, T, MOBA_H, MOBA_DH), v3.reshape(B, T, MOBA_H, MOBA_DH)


def _ffn_layer(tag, x3, conv0, P, i, ln_g, ln_b):
    B, T, D = x3.shape
    w_in = P['ffn_w_in'][i]
    h, cst = _ffn_in(tag + 'in', x3, w_in[:, :D_FF].astype(BF16), w_in[:, D_FF:].astype(BF16), conv0,
                     P['ffn_conv_w'][i], P['ffn_conv_b'][i][None, :])
    xn = _mm_ln(tag + 'down', [h], P['ffn_w_down'][i].astype(BF16), x3.reshape(B * T, D), ln_g, ln_b, _ident)
    return xn.reshape(B, T, D), cst


def _trunk(grp, x3, wkv0, shift0, s5re0, s5im0, conv0, cache_k, cache_v, page_table, P):
    B, T, D = x3.shape
    wkv_new, shift_new, s5re_new, s5im_new, k_new, v_new, conv_new = [], [], [], [], [], [], []
    v_first = None
    for i in range(DEPTH):
        kind, j = i % 3, i // 3
        tag = f'{grp}{i}_'
        g0, b0 = P['ln_g'][i, 0][None, :], P['ln_b'][i, 0][None, :]
        g1, b1 = P['ln_g'][i, 1][None, :], P['ln_b'][i, 1][None, :]
        if kind == 0:
            xn, S, last, v_first = _rwkv_layer(tag, x3, shift0[j], wkv0[j], v_first, P, j, g0, b0)
            wkv_new.append(S)
            shift_new.append(last)
        elif kind == 1:
            xn, hr, hi = _s5_layer(tag, x3, s5re0[j], s5im0[j], P, j, g0, b0)
            s5re_new.append(hr)
            s5im_new.append(hi)
        else:
            xn, kr, vr = _moba_layer(tag, x3, cache_k, cache_v, page_table, P, j, g0, b0)
            k_new.append(kr)
            v_new.append(vr)
        x3, cst = _ffn_layer(tag + 'ffn_', xn.reshape(B, T, D), conv0[i], P, i, g1, b1)
        conv_new.append(cst)
    return (x3, jnp.stack(wkv_new), jnp.stack(shift_new), jnp.stack(s5re_new), jnp.stack(s5im_new),
            jnp.stack(conv_new), jnp.stack(k_new), jnp.stack(v_new))


def kernel(x_prompt, x_sample, state_rwkv_wkv, state_rwkv_shift, state_s5_re, state_s5_im, state_ffn_conv, cache_k, cache_v, page_table, ln_g, ln_b, rwkv_mu, rwkv_w_rkv, rwkv_w0, rwkv_w1, rwkv_w2, rwkv_a0, rwkv_a1, rwkv_a2, rwkv_v0, rwkv_v1, rwkv_v2, rwkv_g1, rwkv_g2, rwkv_k_k, rwkv_k_a, rwkv_r_k, rwkv_lnx_g, rwkv_lnx_b, rwkv_wo, s5_log_dt, s5_lam_re, s5_lam_im, s5_b_re, s5_b_im, s5_c_re, s5_c_im, s5_d, s5_w_val, s5_w_gate, moba_w_qkv, moba_wo, ffn_w_in, ffn_conv_w, ffn_conv_b, ffn_w_down):
    P = dict(ln_g=ln_g, ln_b=ln_b, rwkv_mu=rwkv_mu, rwkv_w_rkv=rwkv_w_rkv, rwkv_w0=rwkv_w0, rwkv_w1=rwkv_w1,
             rwkv_w2=rwkv_w2, rwkv_a0=rwkv_a0, rwkv_a1=rwkv_a1, rwkv_a2=rwkv_a2, rwkv_v0=rwkv_v0,
             rwkv_v1=rwkv_v1, rwkv_v2=rwkv_v2, rwkv_g1=rwkv_g1, rwkv_g2=rwkv_g2, rwkv_k_k=rwkv_k_k,
             rwkv_k_a=rwkv_k_a, rwkv_r_k=rwkv_r_k.reshape(rwkv_r_k.shape[0], D_MODEL), rwkv_lnx_g=rwkv_lnx_g,
             rwkv_lnx_b=rwkv_lnx_b, rwkv_wo=rwkv_wo, s5_log_dt=s5_log_dt, s5_lam_re=s5_lam_re,
             s5_lam_im=s5_lam_im, s5_b_re=s5_b_re, s5_b_im=s5_b_im, s5_c_re=s5_c_re, s5_c_im=s5_c_im, s5_d=s5_d,
             s5_w_val=s5_w_val, s5_w_gate=s5_w_gate, moba_w_qkv=moba_w_qkv, moba_wo=moba_wo,
             ffn_w_in=ffn_w_in, ffn_conv_w=ffn_conv_w, ffn_conv_b=ffn_conv_b, ffn_w_down=ffn_w_down)
    B = x_prompt.shape[0]
    n_rwkv, n_s5 = state_rwkv_wkv.shape[0], state_s5_re.shape[0]
    zeros = lambda *s: jnp.zeros(s, F32)
    outs_p = _trunk('p', x_prompt, zeros(n_rwkv, B, RWKV_H, RWKV_N, RWKV_N), zeros(n_rwkv, B, D_MODEL),
                    zeros(n_s5, B, S5_G, S5_P), zeros(n_s5, B, S5_G, S5_P), zeros(DEPTH, B, CONV_W - 1, D_FF),
                    None, None, None, P)
    outs_s = _trunk('s', x_sample, state_rwkv_wkv, state_rwkv_shift, state_s5_re, state_s5_im, state_ffn_conv,
                    cache_k, cache_v, page_table, P)
    return (outs_p[0], outs_s[0]) + tuple(outs_p[1:]) + tuple(outs_s[1:])
```

```python
import functools
import math

import jax
import jax.numpy as jnp
from jax import lax
from jax.experimental import pallas as pl
from jax.experimental.pallas import tpu as pltpu

F32 = jnp.float32
BF16 = jnp.bfloat16

D_MODEL = 2048
DEPTH = 4
RWKV_N = 64
RWKV_H = D_MODEL // RWKV_N
RWKV_GN_EPS = 64e-5
S5_GROUP = 16
S5_G = D_MODEL // S5_GROUP
S5_P = 64
MOBA_H = 16
MOBA_DH = D_MODEL // MOBA_H
MOBA_BLOCK = 256
MOBA_TOPK = 3
PAGE_SIZE = 128
D_FF = 11 * D_MODEL // 4
CONV_W = 3
LN_EPS = 1e-5
DN_ALPHA = (2.0 * DEPTH) ** 0.25
NEG = -1e30

LANES = 128
SUBLANES = 8
VMEM_LIMIT_BYTES = 56 * 1024 * 1024

MM_ROW_TILE = 1024
MM_LN_WEIGHT_TILE_BYTES = 12 * 1024 * 1024
WKV_CHUNK = 64
WKV_CHAINS = 16
WKV_HEADS = 4
MOBA_PAGES_PER_STEP = 8
S5_SLAB = LANES
S5_STATES = (S5_SLAB // S5_GROUP) * S5_P


def _params(n_axes):
    return pltpu.CompilerParams(dimension_semantics=("arbitrary",) * n_axes,
                                vmem_limit_bytes=VMEM_LIMIT_BYTES)


def _sigmoid(x):
    return 1.0 / (1.0 + jnp.exp(-x))


def _softplus(x):
    return jnp.maximum(x, 0.0) + jnp.log1p(jnp.exp(-jnp.abs(x)))


def _dg(a, b, dims):
    return lax.dot_general(a, b, (dims, ((), ())), preferred_element_type=F32)


_NN = ((1,), (0,))
_NT = ((1,), (1,))
_TN = ((0,), (0,))


def _split(a):
    hi = a.astype(BF16)
    lo = (a - hi.astype(F32)).astype(BF16)
    return hi, lo


def _dot3(a, b, dims=_NN):
    ah, al = _split(a)
    bh, bl = _split(b)
    return _dg(ah, bh, dims) + (_dg(al, bh, dims) + _dg(ah, bl, dims))


def _dot1(a, b, dims=_NN):
    return _dg(a.astype(BF16), b.astype(BF16), dims)


def _dot_exact_lhs(lhs_bf16, x):
    x1 = x.astype(BF16)
    r1 = x - x1.astype(F32)
    x2 = r1.astype(BF16)
    x3 = (r1 - x2.astype(F32)).astype(BF16)
    return _dg(lhs_bf16, x1, _NN) + (_dg(lhs_bf16, x2, _NN) + _dg(lhs_bf16, x3, _NN))


def _layer_norm(y, g, b):
    mu = jnp.mean(y, axis=-1, keepdims=True)
    yc = y - mu
    var = jnp.mean(yc * yc, axis=-1, keepdims=True)
    return yc * lax.rsqrt(var + LN_EPS) * g + b


def _mm_kernel(*refs, n_row, n_vec, n_w, n_erow, n_evec, n_out, prologue, epilogue):
    it = iter(refs)
    rows = [next(it) for _ in range(n_row)]
    vecs = [next(it) for _ in range(n_vec)]
    ws = [next(it) for _ in range(n_w)]
    erows = [next(it) for _ in range(n_erow)]
    evecs = [next(it) for _ in range(n_evec)]
    outs = [next(it) for _ in range(n_out)]
    a_scr = next(it)

    @pl.when(pl.program_id(1) == 0)
    def _():
        a_scr[...] = prologue(*[r[...] for r in rows], *[v[...] for v in vecs]).astype(BF16)

    a = a_scr[...]
    accs = [jnp.dot(a, w[...], preferred_element_type=F32) for w in ws]
    res = epilogue(*accs, *[e[...] for e in erows], *[e[...] for e in evecs])
    for o_ref, o in zip(outs, res):
        o_ref[...] = o.astype(o_ref.dtype)


def _mm(name, rows, vecs, ws, erows, evecs, prologue, epilogue, out_dtypes):
    M, K = rows[0].shape
    N = ws[0].shape[1]
    tm = MM_ROW_TILE if (M % MM_ROW_TILE == 0 and len(rows) == 1) else min(512, M)
    tn = 512 if N % 512 == 0 else N
    assert M % tm == 0 and N % tn == 0
    in_specs = ([pl.BlockSpec((tm, K), lambda i, j: (i, 0)) for _ in rows]
                + [pl.BlockSpec((1, K), lambda i, j: (0, 0)) for _ in vecs]
                + [pl.BlockSpec((K, tn), lambda i, j: (0, j)) for _ in ws]
                + [pl.BlockSpec((tm, tn), lambda i, j: (i, j)) for _ in erows]
                + [pl.BlockSpec((1, tn), lambda i, j: (0, j)) for _ in evecs])
    out_specs = [pl.BlockSpec((tm, tn), lambda i, j: (i, j)) for _ in out_dtypes]
    out_shape = [jax.ShapeDtypeStruct((M, N), dt) for dt in out_dtypes]
    kern = functools.partial(_mm_kernel, n_row=len(rows), n_vec=len(vecs), n_w=len(ws), n_erow=len(erows),
                             n_evec=len(evecs), n_out=len(out_dtypes), prologue=prologue, epilogue=epilogue)
    return pl.pallas_call(
        kern, grid=(M // tm, N // tn), in_specs=in_specs, out_specs=out_specs, out_shape=out_shape,
        scratch_shapes=[pltpu.VMEM((tm, K), BF16)], compiler_params=_params(2), name=name,
    )(*rows, *vecs, *ws, *erows, *evecs)


def _mm_ln_kernel(*refs, n_row, prologue, n_k):
    it = iter(refs)
    rows = [next(it) for _ in range(n_row)]
    w_ref, x_ref, g_ref, b_ref, o_ref, acc = (next(it) for _ in range(6))
    k = pl.program_id(1)
    a = prologue(*[r[...] for r in rows]).astype(BF16)
    p = jnp.dot(a, w_ref[...], preferred_element_type=F32)
    if n_k == 1:
        o_ref[...] = _layer_norm(DN_ALPHA * x_ref[...] + p, g_ref[...], b_ref[...])
        return

    @pl.when(k == 0)
    def _():
        acc[...] = p

    @pl.when(jnp.logical_and(k > 0, k < n_k - 1))
    def _():
        acc[...] += p

    @pl.when(k == n_k - 1)
    def _():
        o_ref[...] = _layer_norm(DN_ALPHA * x_ref[...] + (acc[...] + p), g_ref[...], b_ref[...])


def _mm_ln(name, rows, w, x, g, b, prologue):
    M, K = rows[0].shape
    N = w.shape[1]
    n_k = -(-K * N * 2 // MM_LN_WEIGHT_TILE_BYTES)
    tk = K // n_k
    tm = min(512, M)
    assert M % tm == 0 and K % n_k == 0 and tk % LANES == 0
    in_specs = ([pl.BlockSpec((tm, tk), lambda i, k: (i, k)) for _ in rows]
                + [pl.BlockSpec((tk, N), lambda i, k: (k, 0)),
                   pl.BlockSpec((tm, N), lambda i, k: (i, 0)),
                   pl.BlockSpec((1, N), lambda i, k: (0, 0)),
                   pl.BlockSpec((1, N), lambda i, k: (0, 0))])
    kern = functools.partial(_mm_ln_kernel, n_row=len(rows), prologue=prologue, n_k=n_k)
    return pl.pallas_call(
        kern, grid=(M // tm, n_k), in_specs=in_specs,
        out_specs=pl.BlockSpec((tm, N), lambda i, k: (i, 0)),
        out_shape=jax.ShapeDtypeStruct((M, N), F32),
        scratch_shapes=[pltpu.VMEM((tm, N), F32)], compiler_params=_params(2), name=name,
    )(*rows, w, x, g, b)


def _ln_kernel(x_ref, h_ref, g_ref, b_ref, o_ref):
    o_ref[...] = _layer_norm(DN_ALPHA * x_ref[...] + h_ref[...], g_ref[...], b_ref[...])


def _ln(name, x, h, g, b):
    M, N = x.shape
    tm = min(512, M)
    row = pl.BlockSpec((tm, N), lambda i: (i, 0))
    vec = pl.BlockSpec((1, N), lambda i: (0, 0))
    return pl.pallas_call(_ln_kernel, grid=(M // tm,), in_specs=[row, row, vec, vec], out_specs=row,
                          out_shape=jax.ShapeDtypeStruct((M, N), F32), compiler_params=_params(1), name=name)(x, h, g, b)


def _ffn_in_kernel(x_ref, wg_ref, wv_ref, c0_ref, cw_ref, cb_ref, h_ref, cs_ref, a_scr, carry, *, bt, tt):
    t = pl.program_id(1)
    j = pl.program_id(2)
    rows = bt * tt
    tn = wg_ref.shape[1]

    @pl.when(j == 0)
    def _():
        a_scr[...] = x_ref[...].reshape(rows, x_ref.shape[2]).astype(BF16)

    @pl.when(t == 0)
    def _():
        carry[j] = c0_ref[...]

    a = a_scr[...]
    hg = jnp.dot(a, wg_ref[...], preferred_element_type=F32)
    hv = jnp.dot(a, wv_ref[...], preferred_element_type=F32)
    prev = carry[j]
    p2 = jnp.broadcast_to(prev[:, 0:1, :], (bt, tt, tn)).reshape(rows, tn)
    p1 = jnp.broadcast_to(prev[:, 1:2, :], (bt, tt, tn)).reshape(rows, tn)
    r = lax.broadcasted_iota(jnp.int32, (rows, tn), 0) % tt
    s1 = jnp.where(r == 0, p1, pltpu.roll(hg, 1, axis=0))
    s2 = jnp.where(r == 0, p2, jnp.where(r == 1, p1, pltpu.roll(hg, 2, axis=0)))
    cw = cw_ref[...]
    c = cb_ref[...] + cw[0:1, :] * s2 + cw[1:2, :] * s1 + cw[2:3, :] * hg
    h_ref[...] = (c * _sigmoid(c) * hv).astype(h_ref.dtype)
    last = hg.reshape(bt, tt, tn)[:, tt - 2:, :]
    carry[j] = last
    cs_ref[0] = last


def _ffn_in(name, x3, wg, wv, conv0, cw, cb):
    B, T, K = x3.shape
    F = wg.shape[1]
    bt, tt = (1, MM_ROW_TILE) if T % MM_ROW_TILE == 0 else (B, T)
    tn = 512
    assert B % bt == 0 and T % tt == 0 and F % tn == 0 and tt >= CONV_W - 1
    n_j = F // tn
    kern = functools.partial(_ffn_in_kernel, bt=bt, tt=tt)
    n_t = T // tt
    h, cst = pl.pallas_call(
        kern, grid=(B // bt, n_t, n_j),
        in_specs=[pl.BlockSpec((bt, tt, K), lambda b, t, j: (b, t, 0)),
                  pl.BlockSpec((K, tn), lambda b, t, j: (0, j)),
                  pl.BlockSpec((K, tn), lambda b, t, j: (0, j)),
                  pl.BlockSpec((bt, CONV_W - 1, tn), lambda b, t, j: (b, 0, j)),
                  pl.BlockSpec((CONV_W, tn), lambda b, t, j: (0, j)),
                  pl.BlockSpec((1, tn), lambda b, t, j: (0, j))],
        out_specs=[pl.BlockSpec((bt * tt, tn), lambda b, t, j: (b * n_t + t, j)),
                   pl.BlockSpec((1, bt, CONV_W - 1, tn), lambda b, t, j: (t, b, 0, j))],
        out_shape=[jax.ShapeDtypeStruct((B * T, F), BF16),
                   jax.ShapeDtypeStruct((n_t, B, CONV_W - 1, F), F32)],
        scratch_shapes=[pltpu.VMEM((bt * tt, K), BF16), pltpu.VMEM((n_j, bt, CONV_W - 1, tn), F32)],
        compiler_params=_params(3), name=name,
    )(x3, wg, wv, conv0, cw, cb)
    return h, cst[n_t - 1]


RWKV_PROJ_TN = 1024
RWKV_PROJ_TILES = D_MODEL // RWKV_PROJ_TN


def _rwkv_proj_kernel(*refs, has_vgate):
    it = iter(refs)
    x_ref, xp_ref, mu_ref, wm_ref, w1_ref, a1_ref, g1_ref, w2_ref, a2_ref, g2_ref, w0_ref, a0_ref = (
        next(it) for _ in range(12))
    if has_vgate:
        v1_ref, v2_ref, v0_ref, vf_ref = (next(it) for _ in range(4))
    r_ref, k_ref, v_ref, lw_ref, a_ref, g_ref = (next(it) for _ in range(6))
    mix_scr, hw_scr, ha_scr, hg_scr = (next(it) for _ in range(4))
    if has_vgate:
        hv_scr = next(it)
    j = pl.program_id(1)
    dot = lambda a, b: jnp.dot(a, b, preferred_element_type=F32)

    @pl.when(j == 0)
    def _():
        x = x_ref[...]
        d = xp_ref[...] - x
        mix = lambda n: (x + d * mu_ref[n:n + 1, :]).astype(BF16)
        mix_scr[0] = mix(0)
        mix_scr[1] = mix(2)
        xv = mix(3)
        mix_scr[2] = xv
        hw_scr[...] = jnp.tanh(dot(mix(1), w1_ref[...])).astype(BF16)
        ha_scr[...] = dot(mix(4), a1_ref[...]).astype(BF16)
        hg_scr[...] = _sigmoid(dot(mix(5), g1_ref[...])).astype(BF16)
        if has_vgate:
            hv_scr[...] = dot(xv, v1_ref[...]).astype(BF16)

    grp = j // RWKV_PROJ_TILES

    @pl.when(grp == 0)
    def _():
        r_ref[...] = dot(mix_scr[0], wm_ref[...]).astype(r_ref.dtype)
        lw_ref[...] = -jnp.exp(-_softplus(-(w0_ref[...] + dot(hw_scr[...], w2_ref[...]))) - 0.5)

    @pl.when(grp == 1)
    def _():
        k_ref[...] = dot(mix_scr[1], wm_ref[...]).astype(k_ref.dtype)
        a_ref[...] = _sigmoid(a0_ref[...] + dot(ha_scr[...], a2_ref[...])).astype(a_ref.dtype)

    @pl.when(grp == 2)
    def _():
        v = dot(mix_scr[2], wm_ref[...])
        if has_vgate:
            v = v + (vf_ref[...] - v) * _sigmoid(v0_ref[...] + dot(hv_scr[...], v2_ref[...]))
        v_ref[...] = v.astype(v_ref.dtype)
        g_ref[...] = dot(hg_scr[...], g2_ref[...]).astype(g_ref.dtype)


def _rwkv_proj(name, x, xp, mu, w_main, lora1, lora2, bias, vgate):
    M, K = x.shape
    tm, tn, nt = min(512, M), RWKV_PROJ_TN, RWKV_PROJ_TILES
    assert M % tm == 0 and w_main.shape == (K, 3 * D_MODEL)
    row = pl.BlockSpec((tm, K), lambda i, j: (i, 0))
    full = lambda a: pl.BlockSpec(a.shape, lambda i, j: (0, 0))
    col = lambda a: pl.BlockSpec((a.shape[0], tn), lambda i, j: (0, j % nt))
    ins = [x, xp, mu, w_main, *lora1, *lora2, *bias]
    specs = ([row, row, full(mu), pl.BlockSpec((K, tn), lambda i, j: (0, j))] + [full(w) for w in lora1]
             + [col(w) for w in lora2] + [col(b) for b in bias])
    scratch = [pltpu.VMEM((3, tm, K), BF16)] + [pltpu.VMEM((tm, w.shape[1]), BF16) for w in lora1]
    if vgate is not None:
        v1, v2, v0, v_first = vgate
        ins += [v1, v2, v0, v_first]
        specs += [full(v1), col(v2), col(v0), pl.BlockSpec((tm, tn), lambda i, j: (i, j % nt))]
        scratch.append(pltpu.VMEM((tm, v1.shape[1]), BF16))
    out = lambda grp: pl.BlockSpec((tm, tn), lambda i, j: (i, jnp.clip(j - grp * nt, 0, nt - 1)))
    return pl.pallas_call(
        functools.partial(_rwkv_proj_kernel, has_vgate=vgate is not None), grid=(M // tm, 3 * nt),
        in_specs=specs, out_specs=[out(0), out(1), out(2), out(0), out(1), out(2)],
        out_shape=[jax.ShapeDtypeStruct((M, D_MODEL), dt) for dt in (BF16, BF16, BF16, F32, BF16, BF16)],
        scratch_shapes=scratch, compiler_params=_params(2), name=name,
    )(*ins)


def _wkv_chunks(r, lw, k, v, a, prm, S, consts):
    C = WKV_CHUNK
    bdmask, l_incl, strict, incl, eye = consts

    def bd(xp):
        return jnp.concatenate([xp.astype(BF16)] * WKV_HEADS, axis=0) * bdmask

    def headsum(xp):
        hi, lo = _split(xp)
        return _dg(hi, bdmask, _NN) + _dg(lo, bdmask, _NN)

    def each(f, *lists):
        return [f(*args) for args in zip(*lists)]

    kkp, kap, rkp, lng, lnb = (list(t) for t in zip(*prm))
    kkn = each(lambda k_, p_: k_ * p_, k, kkp)
    kp = each(lambda k_, a_, p_: k_ * (1.0 + (a_ - 1.0) * p_), k, a, kap)
    sums = each(lambda kkn_, r_, kp_, p_: headsum(jnp.concatenate([kkn_ * kkn_, r_ * kp_ * p_], axis=0)),
                kkn, r, kp, rkp)
    kk = each(lambda kkn_, s_: kkn_ / jnp.maximum(jnp.sqrt(s_[:C]), 1e-12), kkn, sums)
    bonus = each(lambda s_, v_: s_[C:] * v_, sums, v)
    b = each(lambda kk_, a_: kk_ * a_, kk, a)
    cum = each(lambda lw_: _dot_exact_lhs(l_incl, lw_), lw)
    e_neg = each(lambda c_: jnp.exp(-c_), cum)
    x = each(lambda kk_, c_, lw_, r_: jnp.concatenate([kk_ * jnp.exp(c_ - lw_), r_ * jnp.exp(c_)],
                                                       axis=0).astype(BF16), kk, cum, lw, r)
    g_b = each(lambda x_, b_, e_: _dg(x_, bd(b_ * e_), _NT), x, b, e_neg)
    g_k = each(lambda x_, k_, e_: _dg(x_, bd(k_ * e_), _NT), x, kp, e_neg)
    xs = each(lambda x_, s_: _dg(x_, s_.astype(BF16), _NT), x, S)
    mb = each(lambda g_: jnp.where(incl, g_[C:], 0.0), g_b)
    kv = each(lambda g_, v_: _dg(jnp.concatenate([jnp.where(strict, g_[:C], 0.0), jnp.where(incl, g_[C:], 0.0)],
                                                 axis=0).astype(BF16), bd(v_), _NN), g_k, v)
    rhs = each(lambda xs_, kv_: -(xs_[:C] + kv_[:C]), xs, kv)
    npow = each(lambda g_: -jnp.where(strict, g_[:C], 0.0), g_b)
    tinv = each(lambda n_: eye + n_, npow)
    npow = each(lambda n_: _dg(n_.astype(BF16), bd(n_), _NN), npow)
    for _ in range(int(math.log2(C)) - 2):
        both = each(lambda t_, n_: _dg(jnp.concatenate([t_, n_], axis=0).astype(BF16), bd(n_), _NN), tinv, npow)
        tinv = each(lambda t_, b_: t_ + b_[:C], tinv, both)
        npow = each(lambda b_: b_[C:], both)
    tinv = each(lambda t_, n_: t_ + _dg(t_.astype(BF16), bd(n_), _NN), tinv, npow)
    u = each(lambda t_, r_: _dg(t_.astype(BF16), bd(r_), _NN), tinv, rhs)
    y = each(lambda xs_, kv_, mb_, u_: xs_[C:] + kv_[C:] + _dg(mb_.astype(BF16), bd(u_), _NN), xs, kv, mb, u)
    upd = each(lambda u_, v_, b_, k_, c_: _dg(
        jnp.concatenate([u_, v_], axis=0).astype(BF16),
        (jnp.concatenate([b_, k_], axis=0) * jnp.exp(c_[C - 1:C, :] - jnp.concatenate([c_, c_], axis=0))).astype(BF16),
        _TN), u, v, b, kp, cum)
    bdmask_f = bdmask.astype(F32)
    s_new = each(lambda s_, c_, u_: s_ * jnp.exp(c_[C - 1:C, :]) + u_ * bdmask_f, S, cum, upd)
    mu = each(lambda y_: headsum(y_) * (1.0 / RWKV_N), y)
    yc = each(lambda y_, m_: y_ - m_, y, mu)
    var = each(lambda yc_: headsum(yc_ * yc_) * (1.0 / RWKV_N), yc)
    z = each(lambda yc_, var_, g_, b_, bo_: yc_ * lax.rsqrt(var_ + RWKV_GN_EPS) * g_ + b_ + bo_,
             yc, var, lng, lnb, bonus)
    return z, s_new


def _wkv_kernel(r_ref, lw_ref, k_ref, v_ref, a_ref, kk_ref, ka_ref, rk_ref, g_ref, b_ref, s0_ref,
                z_ref, so_ref, s_scr, *, n_t, nb, n_grp):
    t = pl.program_id(1)
    C, N, HW = WKV_CHUNK, RWKV_N, WKV_HEADS * RWKV_N
    ri = lax.broadcasted_iota(jnp.int32, (HW, HW), 0) // N
    ci = lax.broadcasted_iota(jnp.int32, (HW, HW), 1) // N
    bdmask_f = (ri == ci).astype(F32)

    chains = [(bi, g) for bi in range(nb) for g in range(n_grp)]

    @pl.when(t == 0)
    def _():
        for n, (bi, g) in enumerate(chains):
            rows = s0_ref[bi, g * WKV_HEADS:(g + 1) * WKV_HEADS].reshape(HW, N)
            s_scr[n] = jnp.concatenate([rows] * WKV_HEADS, axis=1) * bdmask_f

    tt = lax.broadcasted_iota(jnp.int32, (C, C), 0)
    ss = lax.broadcasted_iota(jnp.int32, (C, C), 1)
    tp = lax.broadcasted_iota(jnp.int32, (C, HW), 0)
    sp = lax.broadcasted_iota(jnp.int32, (C, HW), 1) % N
    consts = (bdmask_f.astype(BF16), (tt >= ss).astype(BF16), tp > sp, tp >= sp, (tp == sp).astype(F32))
    sls = [slice(g * HW, (g + 1) * HW) for _, g in chains]
    tok = lambda ref: [ref[bi, :, sl].astype(F32) for (bi, _), sl in zip(chains, sls)]
    prm = [tuple(ref[:, sl] for ref in (kk_ref, ka_ref, rk_ref, g_ref, b_ref)) for sl in sls]
    zs, s_news = _wkv_chunks(tok(r_ref), tok(lw_ref), tok(k_ref), tok(v_ref), tok(a_ref), prm,
                             [s_scr[n] for n in range(len(chains))], consts)
    for n, ((bi, _), sl) in enumerate(zip(chains, sls)):
        s_scr[n] = s_news[n]
        z_ref[bi, :, sl] = zs[n]

    @pl.when(t == n_t - 1)
    def _():
        for n, (bi, g) in enumerate(chains):
            for h in range(WKV_HEADS):
                so_ref[bi, g * WKV_HEADS + h] = s_scr[n, h * N:(h + 1) * N, h * N:(h + 1) * N]


def _wkv(name, r, lw, k, v, a, kkp, kap, rkp, lng, lnb, s0):
    B, T, D = r.shape
    C, HW = WKV_CHUNK, WKV_HEADS * RWKV_N
    n_grp = min(max(1, WKV_CHAINS // B), D // HW)
    W = n_grp * HW
    assert T % C == 0 and D % W == 0
    n_t = T // C
    tok = pl.BlockSpec((B, C, W), lambda h, t: (0, t, h))
    vec = pl.BlockSpec((1, W), lambda h, t: (0, h))
    st = pl.BlockSpec((B, n_grp * WKV_HEADS, RWKV_N, RWKV_N), lambda h, t: (0, h, 0, 0))
    return pl.pallas_call(
        functools.partial(_wkv_kernel, n_t=n_t, nb=B, n_grp=n_grp), grid=(D // W, n_t),
        in_specs=[tok] * 5 + [vec] * 5 + [st], out_specs=[tok, st],
        out_shape=[jax.ShapeDtypeStruct((B, T, D), F32), jax.ShapeDtypeStruct(s0.shape, F32)],
        scratch_shapes=[pltpu.VMEM((B * n_grp, HW, HW), F32)], compiler_params=_params(2), name=name,
    )(r, lw, k, v, a, kkp, kap, rkp, lng, lnb, s0)


def _cmul(ar, ai, br, bi):
    return ar * br - ai * bi, ar * bi + ai * br


def _dot3w(a, w_hi, w_lo, dims=_NN):
    ah, al = _split(a)
    return _dg(ah, w_hi, dims) + (_dg(al, w_hi, dims) + _dg(ah, w_lo, dims))


S5_SHIFTS = (1, 2, 4)


def _s5_kernel(x_ref, wbh_ref, wbl_ref, wc_ref, lr_ref, li_ref, dt_ref, d_ref, h0r_ref, h0i_ref,
               z_ref, hr_ref, hi_ref, pw_re, pw_im, sh_re, sh_im, cf, hc, *, n_t, tt):
    t = pl.program_id(2)
    R = SUBLANES

    @pl.when(t == 0)
    def _():
        lr, li, dt = lr_ref[0], li_ref[0], dt_ref[0]
        rid = lax.broadcasted_iota(jnp.int32, (R, S5_STATES), 0)
        n = (rid + 1).astype(F32)
        mag = jnp.exp(lr * dt * n)
        ang = li * dt * n
        pw_re[...] = mag * jnp.cos(ang)
        pw_im[...] = mag * jnp.sin(ang)
        for m, sh in enumerate(S5_SHIFTS):
            sh_re[m] = jnp.where(rid >= sh, pw_re[sh - 1:sh, :], 0.0)
            sh_im[m] = jnp.where(rid >= sh, pw_im[sh - 1:sh, :], 0.0)
        ar, ai = pw_re[0:1, :], pw_im[0:1, :]
        den = lr * lr + li * li
        cf[0:1, :] = ((ar - 1.0) * lr + ai * li) / den
        cf[1:2, :] = (ai * lr - (ar - 1.0) * li) / den
        hc[0:1, :] = h0r_ref[0]
        hc[1:2, :] = h0i_ref[0]

    x = x_ref[0]
    bu = _dot3w(x, wbh_ref[0], wbl_ref[0])
    cr, ci = cf[0:1, :], cf[1:2, :]
    br, bi = _cmul(cr, ci, bu[:, :S5_STATES], bu[:, S5_STATES:])
    pr, pi = pw_re[...], pw_im[...]
    hr, hi = hc[0:1, :], hc[1:2, :]
    xrs, xis = [], []
    for i in range(tt // R):
        vr, vi = br[i * R:(i + 1) * R, :], bi[i * R:(i + 1) * R, :]
        for m, sh in enumerate(S5_SHIFTS):
            ur, ui = _cmul(sh_re[m], sh_im[m], pltpu.roll(vr, sh, axis=0), pltpu.roll(vi, sh, axis=0))
            vr, vi = vr + ur, vi + ui
        ur, ui = _cmul(pr, pi, hr, hi)
        vr, vi = vr + ur, vi + ui
        xrs.append(vr)
        xis.append(vi)
        hr, hi = vr[R - 1:R, :], vi[R - 1:R, :]
    hc[0:1, :] = hr
    hc[1:2, :] = hi
    xcat = jnp.concatenate([jnp.concatenate(xrs, axis=0), jnp.concatenate(xis, axis=0)], axis=-1)
    y = _dg(xcat.astype(BF16), wc_ref[0], _NN) + d_ref[...] * x
    z_ref[0] = 0.5 * y * (1.0 + jnp.tanh(math.sqrt(2.0 / math.pi) * (y + 0.044715 * (y * y * y))))

    @pl.when(t == n_t - 1)
    def _():
        hr_ref[0] = hr
        hi_ref[0] = hi


def _s5(name, x3, wb, wc, lr, li, dt, d, h0r, h0i):
    B, T, D = x3.shape
    tt = min(256, T)
    assert T % tt == 0 and tt % SUBLANES == 0
    n_t = T // tt
    n_s = D // S5_SLAB
    S = S5_STATES
    st = pl.BlockSpec((1, 1, S), lambda b, s, t: (b, 0, s))
    pv = pl.BlockSpec((1, 1, S), lambda b, s, t: (0, 0, s))
    w_in = pl.BlockSpec((1, S5_SLAB, 2 * S), lambda b, s, t: (s, 0, 0))
    w_out = pl.BlockSpec((1, 2 * S, S5_SLAB), lambda b, s, t: (s, 0, 0))
    split = lambda w: (w.astype(BF16), (w - w.astype(BF16).astype(F32)).astype(BF16))
    return pl.pallas_call(
        functools.partial(_s5_kernel, n_t=n_t, tt=tt), grid=(B, n_s, n_t),
        in_specs=[pl.BlockSpec((1, tt, S5_SLAB), lambda b, s, t: (b, t, s)),
                  w_in, w_in, w_out,
                  pv, pv, pv,
                  pl.BlockSpec((1, S5_SLAB), lambda b, s, t: (0, s)),
                  st, st],
        out_specs=[pl.BlockSpec((1, tt, S5_SLAB), lambda b, s, t: (b, t, s)), st, st],
        out_shape=[jax.ShapeDtypeStruct((B, T, D), F32),
                   jax.ShapeDtypeStruct((B, 1, n_s * S), F32), jax.ShapeDtypeStruct((B, 1, n_s * S), F32)],
        scratch_shapes=[pltpu.VMEM((SUBLANES, S), F32), pltpu.VMEM((SUBLANES, S), F32),
                        pltpu.VMEM((len(S5_SHIFTS), SUBLANES, S), F32),
                        pltpu.VMEM((len(S5_SHIFTS), SUBLANES, S), F32),
                        pltpu.VMEM((SUBLANES, S), F32), pltpu.VMEM((SUBLANES, S), F32)],
        compiler_params=_params(3), name=name,
    )(x3, *split(wb), wc.astype(BF16), lr, li, dt, d, h0r, h0i)


def _select_topk(gate, idx, valid, axis):
    g = jnp.where(valid, gate, NEG)
    big = jnp.int32(1 << 30)
    sel = jnp.zeros(gate.shape, jnp.bool_)
    for _ in range(MOBA_TOPK):
        m = jnp.max(g, axis=axis, keepdims=True)
        first = jnp.min(jnp.where(g == m, idx, big), axis=axis, keepdims=True)
        pick = idx == first
        sel = jnp.logical_or(sel, pick)
        g = jnp.where(pick, -jnp.inf, g)
    return jnp.logical_and(sel, valid)


def _moba_prompt_kernel(q_ref, k_ref, v_ref, sl_ref, o_ref, kmean_scr, *, n_blk):
    i = pl.program_id(2)
    BK = MOBA_BLOCK

    @pl.when(i == 0)
    def _():
        kmean_scr[...] = jnp.concatenate([jnp.sum(k_ref[0, n * BK:(n + 1) * BK, :], axis=0, keepdims=True)
                                          for n in range(n_blk)], axis=0) * (1.0 / BK)

    q = q_ref[0]
    slope = sl_ref[0][:, 0:1]
    qi = lax.broadcasted_iota(jnp.int32, (BK, BK), 0)
    ki = lax.broadcasted_iota(jnp.int32, (BK, BK), 1)
    causal = qi >= ki
    bias0 = slope * (qi - ki).astype(F32)
    qb = q.astype(BF16)
    scale = MOBA_DH ** -0.5

    def tile(c):
        ss = []
        if c > 0:
            gate_t = _dot3(kmean_scr[...], q, _NT)
            nidx = lax.broadcasted_iota(jnp.int32, (n_blk, BK), 0)
            sel_t = jnp.where(_select_topk(gate_t, nidx, nidx < c, axis=0), 1.0, 0.0).astype(BF16)
            sel = _dg(jnp.where(qi == ki, 1.0, 0.0).astype(BF16), sel_t, _NT) > 0.5
        for n in range(c + 1):
            s = _dg(qb, k_ref[0, n * BK:(n + 1) * BK, :].astype(BF16), _NT) * scale
            s = s - (bias0 + slope * float((c - n) * BK))
            ss.append(jnp.where(causal if n == c else sel[:, n:n + 1], s, NEG))
        m = jnp.max(functools.reduce(jnp.maximum, ss), axis=-1, keepdims=True)
        ps = [jnp.exp(s - m) for s in ss]
        l = jnp.sum(functools.reduce(lambda x, y: x + y, ps), axis=-1, keepdims=True)
        acc = functools.reduce(lambda x, y: x + y,
                               [_dg(p.astype(BF16), v_ref[0, n * BK:(n + 1) * BK, :].astype(BF16), _NN)
                                for n, p in enumerate(ps)])
        o_ref[0] = acc / l

    for c in range(n_blk):
        pl.when(i == c)(functools.partial(tile, c))


def _moba_prompt(name, q, k, v, slopes):
    B, T, D = q.shape
    n_blk = T // MOBA_BLOCK
    assert T % MOBA_BLOCK == 0
    qs = pl.BlockSpec((1, MOBA_BLOCK, MOBA_DH), lambda b, h, i: (b, i, h))
    kv = pl.BlockSpec((1, T, MOBA_DH), lambda b, h, i: (b, 0, h))
    return pl.pallas_call(
        functools.partial(_moba_prompt_kernel, n_blk=n_blk), grid=(B, MOBA_H, n_blk),
        in_specs=[qs, kv, kv, pl.BlockSpec((1, 1, LANES), lambda b, h, i: (h, 0, 0))], out_specs=qs,
        out_shape=jax.ShapeDtypeStruct((B, T, D), F32),
        scratch_shapes=[pltpu.VMEM((n_blk, MOBA_DH), F32)],
        compiler_params=_params(3), name=name,
    )(q, k, v, slopes)


def _q_rows(q, nq):
    return jnp.concatenate([q[:, h * MOBA_DH:(h + 1) * MOBA_DH] for h in range(MOBA_H)], axis=0)


def _head_match(n_rows, nq):
    shape = (n_rows * MOBA_H, MOBA_H * nq)
    return lax.broadcasted_iota(jnp.int32, shape, 0) % MOBA_H == lax.broadcasted_iota(jnp.int32, shape, 1) // nq


def _token_scores(k3, qr, nq, dot):
    n_rows = k3.shape[0]
    s2 = dot(k3.reshape(n_rows * MOBA_H, MOBA_DH), qr, _NT)
    s2 = jnp.where(_head_match(n_rows, nq), s2, 0.0)
    return jnp.sum(s2.reshape(n_rows, MOBA_H, MOBA_H * nq), axis=1)


def _moba_scores_kernel(pt_ref, q_ref, m_ref, *refs, nq, pp):
    k_refs, (s_ref, ks_ref) = refs[:pp], refs[pp:]
    qr = _q_rows(q_ref[0], nq).astype(BF16)
    kps = [k_ref[0, 0] for k_ref in k_refs]
    s2s = [_dg(kp.reshape(PAGE_SIZE * MOBA_H, MOBA_DH).astype(BF16), qr, _NT) for kp in kps]
    for u in range(pp):
        s_ref[0, u] = jnp.sum((s2s[u] * m_ref[...]).reshape(PAGE_SIZE, MOBA_H, MOBA_H * nq), axis=1)
        ks_ref[0, u] = jnp.sum(kps[u], axis=0)


def _page_specs(layer, pp):
    return [pl.BlockSpec((1, 1, PAGE_SIZE, MOBA_H, MOBA_DH),
                         lambda b, g, pt, u=u: (layer, pt[b, g * pp + u], 0, 0, 0)) for u in range(pp)]


def _moba_scores(name, page_table, q, cache_k, layer, match_f):
    B, nq, D = q.shape
    n_pg = page_table.shape[1]
    pp = MOBA_PAGES_PER_STEP
    assert n_pg % pp == 0
    gs = pltpu.PrefetchScalarGridSpec(
        num_scalar_prefetch=1, grid=(B, n_pg // pp),
        in_specs=[pl.BlockSpec((1, nq, D), lambda b, g, pt: (b, 0, 0)),
                  pl.BlockSpec(match_f.shape, lambda b, g, pt: (0, 0))] + _page_specs(layer, pp),
        out_specs=[pl.BlockSpec((1, pp, PAGE_SIZE, MOBA_H * nq), lambda b, g, pt: (b, g, 0, 0)),
                   pl.BlockSpec((1, pp, MOBA_H, MOBA_DH), lambda b, g, pt: (b, g, 0, 0))])
    return pl.pallas_call(
        functools.partial(_moba_scores_kernel, nq=nq, pp=pp), grid_spec=gs,
        out_shape=[jax.ShapeDtypeStruct((B, n_pg, PAGE_SIZE, MOBA_H * nq), F32),
                   jax.ShapeDtypeStruct((B, n_pg, MOBA_H, MOBA_DH), F32)],
        compiler_params=_params(2), name=name,
    )(page_table, q, match_f, *([cache_k] * pp))


def _moba_probs_kernel(s_ref, ks_ref, q_ref, kn_ref, sl_ref, p_ref, pn_ref, sel_scr, *, nq, n_blk, past):
    BK = MOBA_BLOCK
    HQ = MOBA_H * nq
    qr = _q_rows(q_ref[0], nq)
    ppb = BK // PAGE_SIZE
    kmean = jnp.sum(ks_ref[0].reshape(n_blk, ppb, MOBA_H, MOBA_DH), axis=1) * (1.0 / BK)
    gate = _token_scores(kmean, qr, nq, _dot3)
    nidx = lax.broadcasted_iota(jnp.int32, (n_blk, HQ), 0)
    own = past // BK
    sel = _select_topk(gate, nidx, nidx < own, axis=0)
    sel_scr[...] = sel.astype(F32)
    slope = sl_ref[...]
    scale = MOBA_DH ** -0.5
    qpos = past + lax.broadcasted_iota(jnp.int32, (1, HQ), 1) % nq
    tok = lax.broadcasted_iota(jnp.int32, (BK, HQ), 0)

    def scores(n):
        s = s_ref[0, n] * scale - slope * (qpos - (n * BK + tok)).astype(F32)
        return jnp.where(sel_scr[pl.ds(n, 1), :] > 0.5, s, NEG)

    sn = _token_scores(kn_ref[0], qr, nq, _dot1) * scale
    tn = lax.broadcasted_iota(jnp.int32, (PAGE_SIZE, HQ), 0)
    dn = qpos - (past + tn)
    sn = jnp.where(jnp.logical_and(dn >= 0, tn < nq), sn - slope * dn.astype(F32), NEG)
    m = lax.fori_loop(0, n_blk, lambda n, m_: jnp.maximum(m_, jnp.max(scores(n), axis=0, keepdims=True)),
                      jnp.max(sn, axis=0, keepdims=True))
    pn = jnp.exp(sn - m)
    l = lax.fori_loop(0, n_blk, lambda n, l_: l_ + jnp.sum(jnp.exp(scores(n) - m), axis=0, keepdims=True),
                      jnp.sum(pn, axis=0, keepdims=True))
    pn_ref[0] = (pn / l).astype(pn_ref.dtype)

    def write(n, c):
        p_ref[0, n] = (jnp.exp(scores(n) - m) / l).astype(p_ref.dtype)
        return c

    lax.fori_loop(0, n_blk, write, 0)


def _moba_probs(name, s, ksum, q, k_new_pad, slopes_hq, past):
    B, n_blk, BK, HQ = s.shape
    nq = q.shape[1]
    blk = pl.BlockSpec((1, n_blk, BK, HQ), lambda b: (b, 0, 0, 0))
    return pl.pallas_call(
        functools.partial(_moba_probs_kernel, nq=nq, n_blk=n_blk, past=past), grid=(B,),
        in_specs=[blk,
                  pl.BlockSpec((1, ksum.shape[1], MOBA_H, MOBA_DH), lambda b: (b, 0, 0, 0)),
                  pl.BlockSpec((1, nq, D_MODEL), lambda b: (b, 0, 0)),
                  pl.BlockSpec((1, PAGE_SIZE, MOBA_H, MOBA_DH), lambda b: (b, 0, 0, 0)),
                  pl.BlockSpec((1, HQ), lambda b: (0, 0))],
        out_specs=[blk, pl.BlockSpec((1, PAGE_SIZE, HQ), lambda b: (b, 0, 0))],
        out_shape=[jax.ShapeDtypeStruct((B, n_blk, BK, HQ), BF16), jax.ShapeDtypeStruct((B, PAGE_SIZE, HQ), BF16)],
        scratch_shapes=[pltpu.VMEM((n_blk, HQ), F32)],
        compiler_params=_params(1), name=name,
    )(s, ksum, q, k_new_pad, slopes_hq)


def _moba_pv_kernel(pt_ref, p_ref, m_ref, pn_ref, vn_ref, *refs, nq, n_steps, pp):
    v_refs, (o_ref, acc) = refs[:pp], refs[pp:]
    g = pl.program_id(1)
    HQ = MOBA_H * nq

    def spread(p):
        p2 = jnp.broadcast_to(p.astype(F32)[:, None, :], (PAGE_SIZE, MOBA_H, HQ))
        return (p2.reshape(PAGE_SIZE * MOBA_H, HQ) * m_ref[...]).astype(BF16)

    def values(v3):
        return v3.reshape(PAGE_SIZE * MOBA_H, MOBA_DH).astype(BF16)

    @pl.when(g == 0)
    def _():
        acc[...] = _dg(spread(pn_ref[0]), values(vn_ref[0]), _TN)

    p2s = [spread(p_ref[0, u]) for u in range(pp)]
    v2s = [values(v_ref[0, 0]) for v_ref in v_refs]
    acc[...] += functools.reduce(lambda x, y: x + y, [_dg(p2, v2, _TN) for p2, v2 in zip(p2s, v2s)])

    @pl.when(g == n_steps - 1)
    def _():
        a = acc[...]
        o_ref[0] = jnp.concatenate([a[h * nq:(h + 1) * nq, :] for h in range(MOBA_H)], axis=-1)


def _moba_pv(name, page_table, p, cache_v, pn, v_new_pad, nq, layer, match_f):
    B, n_pg, _, HQ = p.shape
    D = D_MODEL
    pp = MOBA_PAGES_PER_STEP
    assert n_pg % pp == 0
    gs = pltpu.PrefetchScalarGridSpec(
        num_scalar_prefetch=1, grid=(B, n_pg // pp),
        in_specs=[pl.BlockSpec((1, pp, PAGE_SIZE, HQ), lambda b, g, pt: (b, g, 0, 0)),
                  pl.BlockSpec(match_f.shape, lambda b, g, pt: (0, 0)),
                  pl.BlockSpec((1, PAGE_SIZE, HQ), lambda b, g, pt: (b, 0, 0)),
                  pl.BlockSpec((1, PAGE_SIZE, MOBA_H, MOBA_DH), lambda b, g, pt: (b, 0, 0, 0))]
        + _page_specs(layer, pp),
        out_specs=pl.BlockSpec((1, nq, D), lambda b, g, pt: (b, 0, 0)),
        scratch_shapes=[pltpu.VMEM((HQ, MOBA_DH), F32)])
    return pl.pallas_call(
        functools.partial(_moba_pv_kernel, nq=nq, n_steps=n_pg // pp, pp=pp), grid_spec=gs,
        out_shape=jax.ShapeDtypeStruct((B, nq, D), F32), compiler_params=_params(2), name=name,
    )(page_table, p, match_f, pn, v_new_pad, *([cache_v] * pp))


def _ident(x):
    return x


def _mix(x, xp, mu):
    return x + (xp - x) * mu


def _pad_cols(w, n):
    return jnp.pad(w, ((0, 0), (0, n - w.shape[1])))


def _pad_rows(w, n):
    return jnp.pad(w, ((0, n - w.shape[0]), (0, 0)))


def _rwkv_layer(tag, x3, shift0, s0, v_first, P, j, ln_g, ln_b):
    B, T, D = x3.shape
    M = B * T
    x = x3.reshape(M, D)
    xp = jnp.concatenate([shift0[:, None, :], x3[:, :-1]], axis=1).reshape(M, D)
    wb = lambda w: w.astype(BF16)
    lp = LANES
    w_rkv = P['rwkv_w_rkv'][j]
    w_main = jnp.concatenate([wb(w_rkv[0]), wb(w_rkv[1]), wb(w_rkv[2])], axis=1)
    lora1 = [wb(_pad_cols(P['rwkv_w1'][j], lp)), wb(_pad_cols(P['rwkv_a1'][j], lp)), wb(P['rwkv_g1'][j])]
    lora2 = [wb(_pad_rows(P['rwkv_w2'][j], lp)), wb(_pad_rows(P['rwkv_a2'][j], lp)), wb(P['rwkv_g2'][j])]
    bias = [P['rwkv_w0'][j][None, :], P['rwkv_a0'][j][None, :]]
    vgate = None
    if j > 0:
        vgate = (wb(_pad_cols(P['rwkv_v1'][j - 1], lp)), wb(_pad_rows(P['rwkv_v2'][j - 1], lp)),
                 P['rwkv_v0'][j - 1][None, :], v_first)
    r, k, v, lw, a, g = _rwkv_proj(tag + 'proj', x, xp, P['rwkv_mu'][j], w_main, lora1, lora2, bias, vgate)
    if j == 0:
        v_first = v
    Tp = -(-T // WKV_CHUNK) * WKV_CHUNK
    seq = lambda t: jnp.pad(t.reshape(B, T, D), ((0, 0), (0, Tp - T), (0, 0)))
    vecp = lambda name: P[name][j].reshape(1, D)
    z, s_new = _wkv(tag + 'wkv', seq(r), seq(lw), seq(k), seq(v), seq(a), vecp('rwkv_k_k'), vecp('rwkv_k_a'),
                    vecp('rwkv_r_k'), vecp('rwkv_lnx_g'), vecp('rwkv_lnx_b'), s0)
    z = z[:, :T].reshape(M, D)
    xn = _mm_ln(tag + 'wo', [z, g], wb(P['rwkv_wo'][j]), x, ln_g, ln_b, lambda z_, g_: z_ * g_)
    return xn, s_new, x3[:, -1], v_first


def _s5_weights(P, j):
    ns, gs = D_MODEL // S5_SLAB, S5_SLAB // S5_GROUP
    eye = jnp.eye(gs, dtype=F32)

    def bd_in(b):
        bt = jnp.swapaxes(b.reshape(ns, gs, S5_P, S5_GROUP), 2, 3)
        return jnp.einsum('sgcp,gh->sgchp', bt, eye).reshape(ns, S5_SLAB, S5_STATES)

    def bd_out(c):
        ct = jnp.swapaxes(c.reshape(ns, gs, S5_GROUP, S5_P), 2, 3)
        return jnp.einsum('sgpc,gh->sgphc', ct, eye).reshape(ns, S5_STATES, S5_SLAB)

    wb = jnp.concatenate([bd_in(P['s5_b_re'][j]), bd_in(P['s5_b_im'][j])], axis=2)
    wc = jnp.concatenate([bd_out(P['s5_c_re'][j]), -bd_out(P['s5_c_im'][j])], axis=1)
    flat = lambda t: t.reshape(1, 1, S5_G * S5_P)
    dt = jnp.broadcast_to(jnp.exp(P['s5_log_dt'][j])[:, None], (S5_G, S5_P))
    return wb, wc, flat(P['s5_lam_re'][j]), flat(P['s5_lam_im'][j]), flat(dt)


def _s5_layer(tag, x3, h0r, h0i, P, j, ln_g, ln_b):
    B, T, D = x3.shape
    M = B * T
    x = x3.reshape(M, D)
    wb, wc, lr, li, dt = _s5_weights(P, j)
    st = lambda h: h.reshape(B, 1, S5_G * S5_P)
    z, hr, hi = _s5(tag + 'scan', x3, wb, wc, lr, li, dt, P['s5_d'][j][None, :], st(h0r), st(h0i))
    out, = _mm(tag + 'gate', [z.reshape(M, D)], [], [P['s5_w_val'][j].astype(BF16), P['s5_w_gate'][j].astype(BF16)],
               [], [], _ident, lambda val, gate: (val * _sigmoid(gate),), [F32])
    xn = _ln(tag + 'ln', x, out, ln_g, ln_b)
    return xn, hr.reshape(B, S5_G, S5_P), hi.reshape(B, S5_G, S5_P)


def _alibi_slopes():
    return 2.0 ** (-8.0 * jnp.arange(1, MOBA_H + 1, dtype=F32) / MOBA_H)


def _moba_layer(tag, x3, cache_k, cache_v, page_table, P, j, ln_g, ln_b):
    B, T, D = x3.shape
    M = B * T
    x = x3.reshape(M, D)
    wqkv = P['moba_w_qkv'][j]
    ws = [wqkv[:, n * D:(n + 1) * D].astype(BF16) for n in range(3)]
    q, k, v = _mm(tag + 'qkv', [x], [], ws, [], [], _ident, lambda a, b, c: (a, b, c), [F32, F32, F32])
    q3, k3, v3 = (t.reshape(B, T, D) for t in (q, k, v))
    slopes = _alibi_slopes()
    if cache_k is None:
        sl = jnp.broadcast_to(slopes[:, None, None], (MOBA_H, 1, LANES))
        o = _moba_prompt(tag + 'attn', q3, k3, v3, sl)
    else:
        n_pg = page_table.shape[1]
        past = n_pg * PAGE_SIZE
        assert past % MOBA_BLOCK == 0 and T <= PAGE_SIZE and T <= MOBA_BLOCK
        HQ = MOBA_H * T
        match_f = _head_match(PAGE_SIZE, T).astype(F32)
        s, ksum = _moba_scores(tag + 'scores', page_table, q3, cache_k, j, match_f)
        sl = jnp.repeat(slopes, T)[None, :]
        pad = lambda t: jnp.pad(t.reshape(B, T, MOBA_H, MOBA_DH), ((0, 0), (0, PAGE_SIZE - T), (0, 0), (0, 0)))
        p, pn = _moba_probs(tag + 'probs', s.reshape(B, past // MOBA_BLOCK, MOBA_BLOCK, HQ), ksum, q3, pad(k3),
                            sl, past)
        o = _moba_pv(tag + 'pv', page_table, p.reshape(B, n_pg, PAGE_SIZE, HQ), cache_v, pn, pad(v3), T, j, match_f)
    xn = _mm_ln(tag + 'wo', [o.reshape(M, D)], P['moba_wo'][j].astype(BF16), x, ln_g, ln_b, _ident)
    return xn, k3.reshape(B, T, MOBA_H, MOBA_DH), v3.reshape(B, T, MOBA_H, MOBA_DH)


def _ffn_layer(tag, x3, conv0, P, i, ln_g, ln_b):
    B, T, D = x3.shape
    w_in = P['ffn_w_in'][i]
    h, cst = _ffn_in(tag + 'in', x3, w_in[:, :D_FF].astype(BF16), w_in[:, D_FF:].astype(BF16), conv0,
                     P['ffn_conv_w'][i], P['ffn_conv_b'][i][None, :])
    xn = _mm_ln(tag + 'down', [h], P['ffn_w_down'][i].astype(BF16), x3.reshape(B * T, D), ln_g, ln_b, _ident)
    return xn.reshape(B, T, D), cst


def _trunk(grp, x3, wkv0, shift0, s5re0, s5im0, conv0, cache_k, cache_v, page_table, P):
    B, T, D = x3.shape
    wkv_new, shift_new, s5re_new, s5im_new, k_new, v_new, conv_new = [], [], [], [], [], [], []
    v_first = None
    for i in range(DEPTH):
        kind, j = i % 3, i // 3
        tag = f'{grp}{i}_'
        g0, b0 = P['ln_g'][i, 0][None, :], P['ln_b'][i, 0][None, :]
        g1, b1 = P['ln_g'][i, 1][None, :], P['ln_b'][i, 1][None, :]
        if kind == 0:
            xn, S, last, v_first = _rwkv_layer(tag, x3, shift0[j], wkv0[j], v_first, P, j, g0, b0)
            wkv_new.append(S)
            shift_new.append(last)
        elif kind == 1:
            xn, hr, hi = _s5_layer(tag, x3, s5re0[j], s5im0[j], P, j, g0, b0)
            s5re_new.append(hr)
            s5im_new.append(hi)
        else:
            xn, kr, vr = _moba_layer(tag, x3, cache_k, cache_v, page_table, P, j, g0, b0)
            k_new.append(kr)
            v_new.append(vr)
        x3, cst = _ffn_layer(tag + 'ffn_', xn.reshape(B, T, D), conv0[i], P, i, g1, b1)
        conv_new.append(cst)
    return (x3, jnp.stack(wkv_new), jnp.stack(shift_new), jnp.stack(s5re_new), jnp.stack(s5im_new),
            jnp.stack(conv_new), jnp.stack(k_new), jnp.stack(v_new))


def kernel(x_prompt, x_sample, state_rwkv_wkv, state_rwkv_shift, state_s5_re, state_s5_im, state_ffn_conv, cache_k, cache_v, page_table, ln_g, ln_b, rwkv_mu, rwkv_w_rkv, rwkv_w0, rwkv_w1, rwkv_w2, rwkv_a0, rwkv_a1, rwkv_a2, rwkv_v0, rwkv_v1, rwkv_v2, rwkv_g1, rwkv_g2, rwkv_k_k, rwkv_k_a, rwkv_r_k, rwkv_lnx_g, rwkv_lnx_b, rwkv_wo, s5_log_dt, s5_lam_re, s5_lam_im, s5_b_re, s5_b_im, s5_c_re, s5_c_im, s5_d, s5_w_val, s5_w_gate, moba_w_qkv, moba_wo, ffn_w_in, ffn_conv_w, ffn_conv_b, ffn_w_down):
    P = dict(ln_g=ln_g, ln_b=ln_b, rwkv_mu=rwkv_mu, rwkv_w_rkv=rwkv_w_rkv, rwkv_w0=rwkv_w0, rwkv_w1=rwkv_w1,
             rwkv_w2=rwkv_w2, rwkv_a0=rwkv_a0, rwkv_a1=rwkv_a1, rwkv_a2=rwkv_a2, rwkv_v0=rwkv_v0,
             rwkv_v1=rwkv_v1, rwkv_v2=rwkv_v2, rwkv_g1=rwkv_g1, rwkv_g2=rwkv_g2, rwkv_k_k=rwkv_k_k,
             rwkv_k_a=rwkv_k_a, rwkv_r_k=rwkv_r_k.reshape(rwkv_r_k.shape[0], D_MODEL), rwkv_lnx_g=rwkv_lnx_g,
             rwkv_lnx_b=rwkv_lnx_b, rwkv_wo=rwkv_wo, s5_log_dt=s5_log_dt, s5_lam_re=s5_lam_re,
             s5_lam_im=s5_lam_im, s5_b_re=s5_b_re, s5_b_im=s5_b_im, s5_c_re=s5_c_re, s5_c_im=s5_c_im, s5_d=s5_d,
             s5_w_val=s5_w_val, s5_w_gate=s5_w_gate, moba_w_qkv=moba_w_qkv, moba_wo=moba_wo,
             ffn_w_in=ffn_w_in, ffn_conv_w=ffn_conv_w, ffn_conv_b=ffn_conv_b, ffn_w_down=ffn_w_down)
    B = x_prompt.shape[0]
    n_rwkv, n_s5 = state_rwkv_wkv.shape[0], state_s5_re.shape[0]
    zeros = lambda *s: jnp.zeros(s, F32)
    outs_p = _trunk('p', x_prompt, zeros(n_rwkv, B, RWKV_H, RWKV_N, RWKV_N), zeros(n_rwkv, B, D_MODEL),
                    zeros(n_s5, B, S5_G, S5_P), zeros(n_s5, B, S5_G, S5_P), zeros(DEPTH, B, CONV_W - 1, D_FF),
                    None, None, None, P)
    outs_s = _trunk('s', x_sample, state_rwkv_wkv, state_rwkv_shift, state_s5_re, state_s5_im, state_ffn_conv,
                    cache_k, cache_v, page_table, P)
    return (outs_p[0], outs_s[0]) + tuple(outs_p[1:]) + tuple(outs_s[1:])
```

```python
import functools
import math

import jax
import jax.numpy as jnp
from jax import lax
from jax.experimental import pallas as pl
from jax.experimental.pallas import tpu as pltpu

F32 = jnp.float32
BF16 = jnp.bfloat16

D_MODEL = 2048
DEPTH = 4
RWKV_N = 64
RWKV_H = D_MODEL // RWKV_N
RWKV_GN_EPS = 64e-5
S5_GROUP = 16
S5_G = D_MODEL // S5_GROUP
S5_P = 64
MOBA_H = 16
MOBA_DH = D_MODEL // MOBA_H
MOBA_BLOCK = 256
MOBA_TOPK = 3
PAGE_SIZE = 128
D_FF = 11 * D_MODEL // 4
CONV_W = 3
LN_EPS = 1e-5
DN_ALPHA = (2.0 * DEPTH) ** 0.25
NEG = -1e30

LANES = 128
SUBLANES = 8
VMEM_LIMIT_BYTES = 56 * 1024 * 1024

MM_ROW_TILE = 1024
MM_LN_WEIGHT_TILE_BYTES = 12 * 1024 * 1024
WKV_CHUNK = 64
WKV_CHAINS = 16
WKV_HEADS = 4
MOBA_PAGES_PER_STEP = 8
S5_SLAB = LANES
S5_STATES = (S5_SLAB // S5_GROUP) * S5_P


def _params(n_axes):
    return pltpu.CompilerParams(dimension_semantics=("arbitrary",) * n_axes,
                                vmem_limit_bytes=VMEM_LIMIT_BYTES)


def _sigmoid(x):
    return 1.0 / (1.0 + jnp.exp(-x))


def _softplus(x):
    return jnp.maximum(x, 0.0) + jnp.log1p(jnp.exp(-jnp.abs(x)))


def _dg(a, b, dims):
    return lax.dot_general(a, b, (dims, ((), ())), preferred_element_type=F32)


_NN = ((1,), (0,))
_NT = ((1,), (1,))
_TN = ((0,), (0,))


def _split(a):
    hi = a.astype(BF16)
    lo = (a - hi.astype(F32)).astype(BF16)
    return hi, lo


def _dot3(a, b, dims=_NN):
    ah, al = _split(a)
    bh, bl = _split(b)
    return _dg(ah, bh, dims) + (_dg(al, bh, dims) + _dg(ah, bl, dims))


def _dot1(a, b, dims=_NN):
    return _dg(a.astype(BF16), b.astype(BF16), dims)


def _dot_exact_lhs(lhs_bf16, x):
    x1 = x.astype(BF16)
    r1 = x - x1.astype(F32)
    x2 = r1.astype(BF16)
    x3 = (r1 - x2.astype(F32)).astype(BF16)
    return _dg(lhs_bf16, x1, _NN) + (_dg(lhs_bf16, x2, _NN) + _dg(lhs_bf16, x3, _NN))


def _layer_norm(y, g, b):
    mu = jnp.mean(y, axis=-1, keepdims=True)
    yc = y - mu
    var = jnp.mean(yc * yc, axis=-1, keepdims=True)
    return yc * lax.rsqrt(var + LN_EPS) * g + b


def _mm_kernel(*refs, n_row, n_vec, n_w, n_erow, n_evec, n_out, prologue, epilogue):
    it = iter(refs)
    rows = [next(it) for _ in range(n_row)]
    vecs = [next(it) for _ in range(n_vec)]
    ws = [next(it) for _ in range(n_w)]
    erows = [next(it) for _ in range(n_erow)]
    evecs = [next(it) for _ in range(n_evec)]
    outs = [next(it) for _ in range(n_out)]
    a_scr = next(it)

    @pl.when(pl.program_id(1) == 0)
    def _():
        a_scr[...] = prologue(*[r[...] for r in rows], *[v[...] for v in vecs]).astype(BF16)

    a = a_scr[...]
    accs = [jnp.dot(a, w[...], preferred_element_type=F32) for w in ws]
    res = epilogue(*accs, *[e[...] for e in erows], *[e[...] for e in evecs])
    for o_ref, o in zip(outs, res):
        o_ref[...] = o.astype(o_ref.dtype)


def _mm(name, rows, vecs, ws, erows, evecs, prologue, epilogue, out_dtypes):
    M, K = rows[0].shape
    N = ws[0].shape[1]
    tm = MM_ROW_TILE if (M % MM_ROW_TILE == 0 and len(rows) == 1) else min(512, M)
    tn = 512 if N % 512 == 0 else N
    assert M % tm == 0 and N % tn == 0
    in_specs = ([pl.BlockSpec((tm, K), lambda i, j: (i, 0)) for _ in rows]
                + [pl.BlockSpec((1, K), lambda i, j: (0, 0)) for _ in vecs]
                + [pl.BlockSpec((K, tn), lambda i, j: (0, j)) for _ in ws]
                + [pl.BlockSpec((tm, tn), lambda i, j: (i, j)) for _ in erows]
                + [pl.BlockSpec((1, tn), lambda i, j: (0, j)) for _ in evecs])
    out_specs = [pl.BlockSpec((tm, tn), lambda i, j: (i, j)) for _ in out_dtypes]
    out_shape = [jax.ShapeDtypeStruct((M, N), dt) for dt in out_dtypes]
    kern = functools.partial(_mm_kernel, n_row=len(rows), n_vec=len(vecs), n_w=len(ws), n_erow=len(erows),
                             n_evec=len(evecs), n_out=len(out_dtypes), prologue=prologue, epilogue=epilogue)
    return pl.pallas_call(
        kern, grid=(M // tm, N // tn), in_specs=in_specs, out_specs=out_specs, out_shape=out_shape,
        scratch_shapes=[pltpu.VMEM((tm, K), BF16)], compiler_params=_params(2), name=name,
    )(*rows, *vecs, *ws, *erows, *evecs)


def _mm_ln_kernel(*refs, n_row, prologue, n_k):
    it = iter(refs)
    rows = [next(it) for _ in range(n_row)]
    w_ref, x_ref, g_ref, b_ref, o_ref, acc = (next(it) for _ in range(6))
    k = pl.program_id(1)
    a = prologue(*[r[...] for r in rows]).astype(BF16)
    p = jnp.dot(a, w_ref[...], preferred_element_type=F32)
    if n_k == 1:
        o_ref[...] = _layer_norm(DN_ALPHA * x_ref[...] + p, g_ref[...], b_ref[...])
        return

    @pl.when(k == 0)
    def _():
        acc[...] = p

    @pl.when(jnp.logical_and(k > 0, k < n_k - 1))
    def _():
        acc[...] += p

    @pl.when(k == n_k - 1)
    def _():
        o_ref[...] = _layer_norm(DN_ALPHA * x_ref[...] + (acc[...] + p), g_ref[...], b_ref[...])


def _mm_ln(name, rows, w, x, g, b, prologue):
    M, K = rows[0].shape
    N = w.shape[1]
    n_k = -(-K * N * 2 // MM_LN_WEIGHT_TILE_BYTES)
    tk = K // n_k
    tm = min(512, M)
    assert M % tm == 0 and K % n_k == 0 and tk % LANES == 0
    in_specs = ([pl.BlockSpec((tm, tk), lambda i, k: (i, k)) for _ in rows]
                + [pl.BlockSpec((tk, N), lambda i, k: (k, 0)),
                   pl.BlockSpec((tm, N), lambda i, k: (i, 0)),
                   pl.BlockSpec((1, N), lambda i, k: (0, 0)),
                   pl.BlockSpec((1, N), lambda i, k: (0, 0))])
    kern = functools.partial(_mm_ln_kernel, n_row=len(rows), prologue=prologue, n_k=n_k)
    return pl.pallas_call(
        kern, grid=(M // tm, n_k), in_specs=in_specs,
        out_specs=pl.BlockSpec((tm, N), lambda i, k: (i, 0)),
        out_shape=jax.ShapeDtypeStruct((M, N), F32),
        scratch_shapes=[pltpu.VMEM((tm, N), F32)], compiler_params=_params(2), name=name,
    )(*rows, w, x, g, b)


def _ln_kernel(x_ref, h_ref, g_ref, b_ref, o_ref):
    o_ref[...] = _layer_norm(DN_ALPHA * x_ref[...] + h_ref[...], g_ref[...], b_ref[...])


def _ln(name, x, h, g, b):
    M, N = x.shape
    tm = min(512, M)
    row = pl.BlockSpec((tm, N), lambda i: (i, 0))
    vec = pl.BlockSpec((1, N), lambda i: (0, 0))
    return pl.pallas_call(_ln_kernel, grid=(M // tm,), in_specs=[row, row, vec, vec], out_specs=row,
                          out_shape=jax.ShapeDtypeStruct((M, N), F32), compiler_params=_params(1), name=name)(x, h, g, b)


def _ffn_in_kernel(x_ref, wg_ref, wv_ref, c0_ref, cw_ref, cb_ref, h_ref, cs_ref, a_scr, carry, *, bt, tt):
    t = pl.program_id(1)
    j = pl.program_id(2)
    rows = bt * tt
    tn = wg_ref.shape[1]

    @pl.when(j == 0)
    def _():
        a_scr[...] = x_ref[...].reshape(rows, x_ref.shape[2]).astype(BF16)

    @pl.when(t == 0)
    def _():
        carry[j] = c0_ref[...]

    a = a_scr[...]
    hg = jnp.dot(a, wg_ref[...], preferred_element_type=F32)
    hv = jnp.dot(a, wv_ref[...], preferred_element_type=F32)
    prev = carry[j]
    p2 = jnp.broadcast_to(prev[:, 0:1, :], (bt, tt, tn)).reshape(rows, tn)
    p1 = jnp.broadcast_to(prev[:, 1:2, :], (bt, tt, tn)).reshape(rows, tn)
    r = lax.broadcasted_iota(jnp.int32, (rows, tn), 0) % tt
    s1 = jnp.where(r == 0, p1, pltpu.roll(hg, 1, axis=0))
    s2 = jnp.where(r == 0, p2, jnp.where(r == 1, p1, pltpu.roll(hg, 2, axis=0)))
    cw = cw_ref[...]
    c = cb_ref[...] + cw[0:1, :] * s2 + cw[1:2, :] * s1 + cw[2:3, :] * hg
    h_ref[...] = (c * _sigmoid(c) * hv).astype(h_ref.dtype)
    last = hg.reshape(bt, tt, tn)[:, tt - 2:, :]
    carry[j] = last
    cs_ref[0] = last


def _ffn_in(name, x3, wg, wv, conv0, cw, cb):
    B, T, K = x3.shape
    F = wg.shape[1]
    bt, tt = (1, MM_ROW_TILE) if T % MM_ROW_TILE == 0 else (B, T)
    tn = 512
    assert B % bt == 0 and T % tt == 0 and F % tn == 0 and tt >= CONV_W - 1
    n_j = F // tn
    kern = functools.partial(_ffn_in_kernel, bt=bt, tt=tt)
    n_t = T // tt
    h, cst = pl.pallas_call(
        kern, grid=(B // bt, n_t, n_j),
        in_specs=[pl.BlockSpec((bt, tt, K), lambda b, t, j: (b, t, 0)),
                  pl.BlockSpec((K, tn), lambda b, t, j: (0, j)),
                  pl.BlockSpec((K, tn), lambda b, t, j: (0, j)),
                  pl.BlockSpec((bt, CONV_W - 1, tn), lambda b, t, j: (b, 0, j)),
                  pl.BlockSpec((CONV_W, tn), lambda b, t, j: (0, j)),
                  pl.BlockSpec((1, tn), lambda b, t, j: (0, j))],
        out_specs=[pl.BlockSpec((bt * tt, tn), lambda b, t, j: (b * n_t + t, j)),
                   pl.BlockSpec((1, bt, CONV_W - 1, tn), lambda b, t, j: (t, b, 0, j))],
        out_shape=[jax.ShapeDtypeStruct((B * T, F), BF16),
                   jax.ShapeDtypeStruct((n_t, B, CONV_W - 1, F), F32)],
        scratch_shapes=[pltpu.VMEM((bt * tt, K), BF16), pltpu.VMEM((n_j, bt, CONV_W - 1, tn), F32)],
        compiler_params=_params(3), name=name,
    )(x3, wg, wv, conv0, cw, cb)
    return h, cst[n_t - 1]


RWKV_PROJ_TN = 1024
RWKV_PROJ_TILES = D_MODEL // RWKV_PROJ_TN


def _rwkv_proj_kernel(*refs, has_vgate, seq_len, tiles_per_seq):
    it = iter(refs)
    x_ref, sh_ref, mu_ref, wm_ref, w1_ref, a1_ref, g1_ref, w2_ref, a2_ref, g2_ref, w0_ref, a0_ref = (
        next(it) for _ in range(12))
    if has_vgate:
        v1_ref, v2_ref, v0_ref, vf_ref = (next(it) for _ in range(4))
    r_ref, k_ref, v_ref, lw_ref, a_ref, g_ref = (next(it) for _ in range(6))
    mix_scr, hw_scr, ha_scr, hg_scr, last_scr = (next(it) for _ in range(5))
    if has_vgate:
        hv_scr = next(it)
    i = pl.program_id(0)
    j = pl.program_id(1)
    dot = lambda a, b: jnp.dot(a, b, preferred_element_type=F32)

    @pl.when(j == 0)
    def _():
        x = x_ref[...]
        tm, K = x.shape
        row = lax.broadcasted_iota(jnp.int32, (tm, K), 0)
        prev = pltpu.roll(x, 1, axis=0)
        if tiles_per_seq > 1:
            first = jnp.where(i % tiles_per_seq == 0, sh_ref[0], last_scr[0:1, :])
            prev = jnp.where(row == 0, first, prev)
            last_scr[0:1, :] = x[tm - 1:tm, :]
        else:
            n_seq = tm // seq_len
            sh = jnp.broadcast_to(sh_ref[...], (n_seq, seq_len, K)).reshape(tm, K)
            prev = jnp.where(row % seq_len == 0, sh, prev)
        d = prev - x
        mix = lambda n: (x + d * mu_ref[n:n + 1, :]).astype(BF16)
        mix_scr[0] = mix(0)
        mix_scr[1] = mix(2)
        xv = mix(3)
        mix_scr[2] = xv
        hw_scr[...] = jnp.tanh(dot(mix(1), w1_ref[...])).astype(BF16)
        ha_scr[...] = dot(mix(4), a1_ref[...]).astype(BF16)
        hg_scr[...] = _sigmoid(dot(mix(5), g1_ref[...])).astype(BF16)
        if has_vgate:
            hv_scr[...] = dot(xv, v1_ref[...]).astype(BF16)

    grp = j // RWKV_PROJ_TILES

    @pl.when(grp == 0)
    def _():
        r_ref[...] = dot(mix_scr[0], wm_ref[...]).astype(r_ref.dtype)
        lw_ref[...] = -jnp.exp(-_softplus(-(w0_ref[...] + dot(hw_scr[...], w2_ref[...]))) - 0.5)

    @pl.when(grp == 1)
    def _():
        k_ref[...] = dot(mix_scr[1], wm_ref[...]).astype(k_ref.dtype)
        a_ref[...] = _sigmoid(a0_ref[...] + dot(ha_scr[...], a2_ref[...])).astype(a_ref.dtype)

    @pl.when(grp == 2)
    def _():
        v = dot(mix_scr[2], wm_ref[...])
        if has_vgate:
            v = v + (vf_ref[...] - v) * _sigmoid(v0_ref[...] + dot(hv_scr[...], v2_ref[...]))
        v_ref[...] = v.astype(v_ref.dtype)
        g_ref[...] = dot(hg_scr[...], g2_ref[...]).astype(g_ref.dtype)


def _rwkv_proj(name, x3, shift0, mu, w_main, lora1, lora2, bias, vgate):
    B, T, K = x3.shape
    M = B * T
    tm, tn, nt = min(512, M), RWKV_PROJ_TN, RWKV_PROJ_TILES
    assert M % tm == 0 and w_main.shape == (K, 3 * D_MODEL) and (T % tm == 0 or tm % T == 0)
    tiles_per_seq = max(1, T // tm)
    seqs_per_tile = max(1, tm // T)
    row = pl.BlockSpec((tm, K), lambda i, j: (i, 0))
    full = lambda a: pl.BlockSpec(a.shape, lambda i, j: (0, 0))
    col = lambda a: pl.BlockSpec((a.shape[0], tn), lambda i, j: (0, j % nt))
    shift = pl.BlockSpec((seqs_per_tile, 1, K), lambda i, j: (i // tiles_per_seq, 0, 0))
    ins = [x3.reshape(M, K), shift0.reshape(B, 1, K), mu, w_main, *lora1, *lora2, *bias]
    specs = ([row, shift, full(mu), pl.BlockSpec((K, tn), lambda i, j: (0, j))] + [full(w) for w in lora1]
             + [col(w) for w in lora2] + [col(b) for b in bias])
    scratch = ([pltpu.VMEM((3, tm, K), BF16)] + [pltpu.VMEM((tm, w.shape[1]), BF16) for w in lora1]
               + [pltpu.VMEM((SUBLANES, K), F32)])
    if vgate is not None:
        v1, v2, v0, v_first = vgate
        ins += [v1, v2, v0, v_first]
        specs += [full(v1), col(v2), col(v0), pl.BlockSpec((tm, tn), lambda i, j: (i, j % nt))]
        scratch.append(pltpu.VMEM((tm, v1.shape[1]), BF16))
    out = lambda grp: pl.BlockSpec((tm, tn), lambda i, j: (i, jnp.clip(j - grp * nt, 0, nt - 1)))
    return pl.pallas_call(
        functools.partial(_rwkv_proj_kernel, has_vgate=vgate is not None, seq_len=T, tiles_per_seq=tiles_per_seq),
        grid=(M // tm, 3 * nt),
        in_specs=specs, out_specs=[out(0), out(1), out(2), out(0), out(1), out(2)],
        out_shape=[jax.ShapeDtypeStruct((M, D_MODEL), dt) for dt in (BF16, BF16, BF16, F32, BF16, BF16)],
        scratch_shapes=scratch, compiler_params=_params(2), name=name,
    )(*ins)


def _wkv_chunks(r, lw, k, v, a, prm, S, consts):
    C = WKV_CHUNK
    bdmask, l_incl, strict, incl, eye = consts

    def bd(xp):
        return jnp.concatenate([xp.astype(BF16)] * WKV_HEADS, axis=0) * bdmask

    def headsum(xp):
        hi, lo = _split(xp)
        return _dg(hi, bdmask, _NN) + _dg(lo, bdmask, _NN)

    def each(f, *lists):
        return [f(*args) for args in zip(*lists)]

    kkp, kap, rkp, lng, lnb = (list(t) for t in zip(*prm))
    kkn = each(lambda k_, p_: k_ * p_, k, kkp)
    kp = each(lambda k_, a_, p_: k_ * (1.0 + (a_ - 1.0) * p_), k, a, kap)
    sums = each(lambda kkn_, r_, kp_, p_: headsum(jnp.concatenate([kkn_ * kkn_, r_ * kp_ * p_], axis=0)),
                kkn, r, kp, rkp)
    kk = each(lambda kkn_, s_: kkn_ / jnp.maximum(jnp.sqrt(s_[:C]), 1e-12), kkn, sums)
    bonus = each(lambda s_, v_: s_[C:] * v_, sums, v)
    b = each(lambda kk_, a_: kk_ * a_, kk, a)
    cum = each(lambda lw_: _dot_exact_lhs(l_incl, lw_), lw)
    e_neg = each(lambda c_: jnp.exp(-c_), cum)
    x = each(lambda kk_, c_, lw_, r_: jnp.concatenate([kk_ * jnp.exp(c_ - lw_), r_ * jnp.exp(c_)],
                                                       axis=0).astype(BF16), kk, cum, lw, r)
    g_b = each(lambda x_, b_, e_: _dg(x_, bd(b_ * e_), _NT), x, b, e_neg)
    g_k = each(lambda x_, k_, e_: _dg(x_, bd(k_ * e_), _NT), x, kp, e_neg)
    xs = each(lambda x_, s_: _dg(x_, s_.astype(BF16), _NT), x, S)
    mb = each(lambda g_: jnp.where(incl, g_[C:], 0.0), g_b)
    kv = each(lambda g_, v_: _dg(jnp.concatenate([jnp.where(strict, g_[:C], 0.0), jnp.where(incl, g_[C:], 0.0)],
                                                 axis=0).astype(BF16), bd(v_), _NN), g_k, v)
    rhs = each(lambda xs_, kv_: -(xs_[:C] + kv_[:C]), xs, kv)
    npow = each(lambda g_: -jnp.where(strict, g_[:C], 0.0), g_b)
    tinv = each(lambda n_: eye + n_, npow)
    npow = each(lambda n_: _dg(n_.astype(BF16), bd(n_), _NN), npow)
    for _ in range(int(math.log2(C)) - 2):
        both = each(lambda t_, n_: _dg(jnp.concatenate([t_, n_], axis=0).astype(BF16), bd(n_), _NN), tinv, npow)
        tinv = each(lambda t_, b_: t_ + b_[:C], tinv, both)
        npow = each(lambda b_: b_[C:], both)
    tinv = each(lambda t_, n_: t_ + _dg(t_.astype(BF16), bd(n_), _NN), tinv, npow)
    u = each(lambda t_, r_: _dg(t_.astype(BF16), bd(r_), _NN), tinv, rhs)
    y = each(lambda xs_, kv_, mb_, u_: xs_[C:] + kv_[C:] + _dg(mb_.astype(BF16), bd(u_), _NN), xs, kv, mb, u)
    upd = each(lambda u_, v_, b_, k_, c_: _dg(
        jnp.concatenate([u_, v_], axis=0).astype(BF16),
        (jnp.concatenate([b_, k_], axis=0) * jnp.exp(c_[C - 1:C, :] - jnp.concatenate([c_, c_], axis=0))).astype(BF16),
        _TN), u, v, b, kp, cum)
    bdmask_f = bdmask.astype(F32)
    s_new = each(lambda s_, c_, u_: s_ * jnp.exp(c_[C - 1:C, :]) + u_ * bdmask_f, S, cum, upd)
    mu = each(lambda y_: _dg(y_.astype(BF16), bdmask, _NN) * (1.0 / RWKV_N), y)
    yc = each(lambda y_, m_: y_ - m_, y, mu)
    var = each(lambda yc_: _dg((yc_ * yc_).astype(BF16), bdmask, _NN) * (1.0 / RWKV_N), yc)
    z = each(lambda yc_, var_, g_, b_, bo_: yc_ * lax.rsqrt(var_ + RWKV_GN_EPS) * g_ + b_ + bo_,
             yc, var, lng, lnb, bonus)
    return z, s_new


def _wkv_kernel(r_ref, lw_ref, k_ref, v_ref, a_ref, kk_ref, ka_ref, rk_ref, g_ref, b_ref, s0_ref,
                z_ref, so_ref, s_scr, *, n_t, nb, n_grp):
    t = pl.program_id(1)
    C, N, HW = WKV_CHUNK, RWKV_N, WKV_HEADS * RWKV_N
    ri = lax.broadcasted_iota(jnp.int32, (HW, HW), 0) // N
    ci = lax.broadcasted_iota(jnp.int32, (HW, HW), 1) // N
    bdmask_f = (ri == ci).astype(F32)

    chains = [(bi, g) for bi in range(nb) for g in range(n_grp)]

    @pl.when(t == 0)
    def _():
        for n, (bi, g) in enumerate(chains):
            rows = s0_ref[bi, g * WKV_HEADS:(g + 1) * WKV_HEADS].reshape(HW, N)
            s_scr[n] = jnp.concatenate([rows] * WKV_HEADS, axis=1) * bdmask_f

    tt = lax.broadcasted_iota(jnp.int32, (C, C), 0)
    ss = lax.broadcasted_iota(jnp.int32, (C, C), 1)
    tp = lax.broadcasted_iota(jnp.int32, (C, HW), 0)
    sp = lax.broadcasted_iota(jnp.int32, (C, HW), 1) % N
    consts = (bdmask_f.astype(BF16), (tt >= ss).astype(BF16), tp > sp, tp >= sp, (tp == sp).astype(F32))
    sls = [slice(g * HW, (g + 1) * HW) for _, g in chains]
    tok = lambda ref: [ref[bi, :, sl].astype(F32) for (bi, _), sl in zip(chains, sls)]
    prm = [tuple(ref[:, sl] for ref in (kk_ref, ka_ref, rk_ref, g_ref, b_ref)) for sl in sls]
    zs, s_news = _wkv_chunks(tok(r_ref), tok(lw_ref), tok(k_ref), tok(v_ref), tok(a_ref), prm,
                             [s_scr[n] for n in range(len(chains))], consts)
    for n, ((bi, _), sl) in enumerate(zip(chains, sls)):
        s_scr[n] = s_news[n]
        z_ref[bi, :, sl] = zs[n]

    @pl.when(t == n_t - 1)
    def _():
        for n, (bi, g) in enumerate(chains):
            for h in range(WKV_HEADS):
                so_ref[bi, g * WKV_HEADS + h] = s_scr[n, h * N:(h + 1) * N, h * N:(h + 1) * N]


def _wkv(name, r, lw, k, v, a, kkp, kap, rkp, lng, lnb, s0):
    B, T, D = r.shape
    C, HW = WKV_CHUNK, WKV_HEADS * RWKV_N
    n_grp = min(max(1, WKV_CHAINS // B), D // HW)
    W = n_grp * HW
    assert T % C == 0 and D % W == 0
    n_t = T // C
    tok = pl.BlockSpec((B, C, W), lambda h, t: (0, t, h))
    vec = pl.BlockSpec((1, W), lambda h, t: (0, h))
    st = pl.BlockSpec((B, n_grp * WKV_HEADS, RWKV_N, RWKV_N), lambda h, t: (0, h, 0, 0))
    return pl.pallas_call(
        functools.partial(_wkv_kernel, n_t=n_t, nb=B, n_grp=n_grp), grid=(D // W, n_t),
        in_specs=[tok] * 5 + [vec] * 5 + [st], out_specs=[tok, st],
        out_shape=[jax.ShapeDtypeStruct((B, T, D), F32), jax.ShapeDtypeStruct(s0.shape, F32)],
        scratch_shapes=[pltpu.VMEM((B * n_grp, HW, HW), F32)], compiler_params=_params(2), name=name,
    )(r, lw, k, v, a, kkp, kap, rkp, lng, lnb, s0)


def _cmul(ar, ai, br, bi):
    return ar * br - ai * bi, ar * bi + ai * br


def _dot3w(a, w_hi, w_lo, dims=_NN):
    ah, al = _split(a)
    return _dg(ah, w_hi, dims) + (_dg(al, w_hi, dims) + _dg(ah, w_lo, dims))


S5_SHIFTS = (1, 2, 4)


def _s5_kernel(x_ref, wbh_ref, wbl_ref, wc_ref, lr_ref, li_ref, dt_ref, d_ref, h0r_ref, h0i_ref,
               z_ref, hr_ref, hi_ref, pw_re, pw_im, sh_re, sh_im, cf, hc, *, n_t, tt):
    t = pl.program_id(2)
    R = SUBLANES

    @pl.when(t == 0)
    def _():
        hc[0:1, :] = h0r_ref[0]
        hc[1:2, :] = h0i_ref[0]

    @pl.when(jnp.logical_and(t == 0, pl.program_id(1) == 0))
    def _():
        lr, li, dt = lr_ref[0], li_ref[0], dt_ref[0]
        rid = lax.broadcasted_iota(jnp.int32, (R, S5_STATES), 0)
        n = (rid + 1).astype(F32)
        mag = jnp.exp(lr * dt * n)
        ang = li * dt * n
        pw_re[...] = mag * jnp.cos(ang)
        pw_im[...] = mag * jnp.sin(ang)
        for m, sh in enumerate(S5_SHIFTS):
            sh_re[m] = jnp.where(rid >= sh, pw_re[sh - 1:sh, :], 0.0)
            sh_im[m] = jnp.where(rid >= sh, pw_im[sh - 1:sh, :], 0.0)
        ar, ai = pw_re[0:1, :], pw_im[0:1, :]
        den = lr * lr + li * li
        cf[0:1, :] = ((ar - 1.0) * lr + ai * li) / den
        cf[1:2, :] = (ai * lr - (ar - 1.0) * li) / den

    x = x_ref[0]
    bu = _dot3w(x, wbh_ref[0], wbl_ref[0])
    cr, ci = cf[0:1, :], cf[1:2, :]
    br, bi = _cmul(cr, ci, bu[:, :S5_STATES], bu[:, S5_STATES:])
    pr, pi = pw_re[...], pw_im[...]
    hr, hi = hc[0:1, :], hc[1:2, :]
    xrs, xis = [], []
    for i in range(tt // R):
        vr, vi = br[i * R:(i + 1) * R, :], bi[i * R:(i + 1) * R, :]
        for m, sh in enumerate(S5_SHIFTS):
            ur, ui = _cmul(sh_re[m], sh_im[m], pltpu.roll(vr, sh, axis=0), pltpu.roll(vi, sh, axis=0))
            vr, vi = vr + ur, vi + ui
        ur, ui = _cmul(pr, pi, hr, hi)
        vr, vi = vr + ur, vi + ui
        xrs.append(vr)
        xis.append(vi)
        hr, hi = vr[R - 1:R, :], vi[R - 1:R, :]
    hc[0:1, :] = hr
    hc[1:2, :] = hi
    xcat = jnp.concatenate([jnp.concatenate(xrs, axis=0), jnp.concatenate(xis, axis=0)], axis=-1)
    y = _dg(xcat.astype(BF16), wc_ref[0], _NN) + d_ref[...] * x
    z_ref[0] = 0.5 * y * (1.0 + jnp.tanh(math.sqrt(2.0 / math.pi) * (y + 0.044715 * (y * y * y))))

    @pl.when(t == n_t - 1)
    def _():
        hr_ref[0] = hr
        hi_ref[0] = hi


def _s5(name, x3, wb, wc, lr, li, dt, d, h0r, h0i):
    B, T, D = x3.shape
    tt = min(256, T)
    assert T % tt == 0 and tt % SUBLANES == 0
    n_t = T // tt
    n_s = D // S5_SLAB
    S = S5_STATES
    st = pl.BlockSpec((1, 1, S), lambda s, b, t: (b, 0, s))
    pv = pl.BlockSpec((1, 1, S), lambda s, b, t: (0, 0, s))
    w_in = pl.BlockSpec((1, S5_SLAB, 2 * S), lambda s, b, t: (s, 0, 0))
    w_out = pl.BlockSpec((1, 2 * S, S5_SLAB), lambda s, b, t: (s, 0, 0))
    split = lambda w: (w.astype(BF16), (w - w.astype(BF16).astype(F32)).astype(BF16))
    return pl.pallas_call(
        functools.partial(_s5_kernel, n_t=n_t, tt=tt), grid=(n_s, B, n_t),
        in_specs=[pl.BlockSpec((1, tt, S5_SLAB), lambda s, b, t: (b, t, s)),
                  w_in, w_in, w_out,
                  pv, pv, pv,
                  pl.BlockSpec((1, S5_SLAB), lambda s, b, t: (0, s)),
                  st, st],
        out_specs=[pl.BlockSpec((1, tt, S5_SLAB), lambda s, b, t: (b, t, s)), st, st],
        out_shape=[jax.ShapeDtypeStruct((B, T, D), F32),
                   jax.ShapeDtypeStruct((B, 1, n_s * S), F32), jax.ShapeDtypeStruct((B, 1, n_s * S), F32)],
        scratch_shapes=[pltpu.VMEM((SUBLANES, S), F32), pltpu.VMEM((SUBLANES, S), F32),
                        pltpu.VMEM((len(S5_SHIFTS), SUBLANES, S), F32),
                        pltpu.VMEM((len(S5_SHIFTS), SUBLANES, S), F32),
                        pltpu.VMEM((SUBLANES, S), F32), pltpu.VMEM((SUBLANES, S), F32)],
        compiler_params=_params(3), name=name,
    )(x3, *split(wb), wc.astype(BF16), lr, li, dt, d, h0r, h0i)


def _select_topk(gate, idx, valid, axis):
    g = jnp.where(valid, gate, NEG)
    big = jnp.int32(1 << 30)
    sel = jnp.zeros(gate.shape, jnp.bool_)
    for _ in range(MOBA_TOPK):
        m = jnp.max(g, axis=axis, keepdims=True)
        first = jnp.min(jnp.where(g == m, idx, big), axis=axis, keepdims=True)
        pick = idx == first
        sel = jnp.logical_or(sel, pick)
        g = jnp.where(pick, -jnp.inf, g)
    return jnp.logical_and(sel, valid)


def _moba_prompt_kernel(q_ref, k_ref, v_ref, sl_ref, o_ref, kmean_scr, *, n_blk):
    i = pl.program_id(2)
    BK = MOBA_BLOCK

    @pl.when(i == 0)
    def _():
        kmean_scr[...] = jnp.concatenate([jnp.sum(k_ref[0, n * BK:(n + 1) * BK, :], axis=0, keepdims=True)
                                          for n in range(n_blk)], axis=0) * (1.0 / BK)

    q = q_ref[0]
    slope = sl_ref[0][:, 0:1]
    qi = lax.broadcasted_iota(jnp.int32, (BK, BK), 0)
    ki = lax.broadcasted_iota(jnp.int32, (BK, BK), 1)
    causal = qi >= ki
    bias0 = slope * (qi - ki).astype(F32)
    qb = q.astype(BF16)
    scale = MOBA_DH ** -0.5

    def tile(c):
        ss = []
        if c > 0:
            gate_t = _dot3(kmean_scr[...], q, _NT)
            nidx = lax.broadcasted_iota(jnp.int32, (n_blk, BK), 0)
            sel_t = jnp.where(_select_topk(gate_t, nidx, nidx < c, axis=0), 1.0, 0.0).astype(BF16)
            sel = _dg(jnp.where(qi == ki, 1.0, 0.0).astype(BF16), sel_t, _NT) > 0.5
        for n in range(c + 1):
            s = _dg(qb, k_ref[0, n * BK:(n + 1) * BK, :].astype(BF16), _NT) * scale
            s = s - (bias0 + slope * float((c - n) * BK))
            ss.append(jnp.where(causal if n == c else sel[:, n:n + 1], s, NEG))
        m = jnp.max(functools.reduce(jnp.maximum, ss), axis=-1, keepdims=True)
        ps = [jnp.exp(s - m) for s in ss]
        l = jnp.sum(functools.reduce(lambda x, y: x + y, ps), axis=-1, keepdims=True)
        acc = functools.reduce(lambda x, y: x + y,
                               [_dg(p.astype(BF16), v_ref[0, n * BK:(n + 1) * BK, :].astype(BF16), _NN)
                                for n, p in enumerate(ps)])
        o_ref[0] = acc / l

    for c in range(n_blk):
        pl.when(i == c)(functools.partial(tile, c))


def _moba_prompt(name, q, k, v, slopes):
    B, T, D = q.shape
    n_blk = T // MOBA_BLOCK
    assert T % MOBA_BLOCK == 0
    qs = pl.BlockSpec((1, MOBA_BLOCK, MOBA_DH), lambda b, h, i: (b, i, h))
    kv = pl.BlockSpec((1, T, MOBA_DH), lambda b, h, i: (b, 0, h))
    return pl.pallas_call(
        functools.partial(_moba_prompt_kernel, n_blk=n_blk), grid=(B, MOBA_H, n_blk),
        in_specs=[qs, kv, kv, pl.BlockSpec((1, 1, LANES), lambda b, h, i: (h, 0, 0))], out_specs=qs,
        out_shape=jax.ShapeDtypeStruct((B, T, D), F32),
        scratch_shapes=[pltpu.VMEM((n_blk, MOBA_DH), F32)],
        compiler_params=_params(3), name=name,
    )(q, k, v, slopes)


def _q_rows(q, nq):
    return jnp.concatenate([q[:, h * MOBA_DH:(h + 1) * MOBA_DH] for h in range(MOBA_H)], axis=0)


def _head_match(n_rows, nq):
    shape = (n_rows * MOBA_H, MOBA_H * nq)
    return lax.broadcasted_iota(jnp.int32, shape, 0) % MOBA_H == lax.broadcasted_iota(jnp.int32, shape, 1) // nq


def _token_scores(k3, qr, nq, dot):
    n_rows = k3.shape[0]
    s2 = dot(k3.reshape(n_rows * MOBA_H, MOBA_DH), qr, _NT)
    s2 = jnp.where(_head_match(n_rows, nq), s2, 0.0)
    return jnp.sum(s2.reshape(n_rows, MOBA_H, MOBA_H * nq), axis=1)


def _moba_scores_kernel(pt_ref, q_ref, m_ref, *refs, nq, pp):
    k_refs, (s_ref, ks_ref) = refs[:pp], refs[pp:]
    qr = _q_rows(q_ref[0], nq).astype(BF16)
    kps = [k_ref[0, 0] for k_ref in k_refs]
    s2s = [_dg(kp.reshape(PAGE_SIZE * MOBA_H, MOBA_DH).astype(BF16), qr, _NT) for kp in kps]
    for u in range(pp):
        s_ref[0, u] = jnp.sum((s2s[u] * m_ref[...]).reshape(PAGE_SIZE, MOBA_H, MOBA_H * nq), axis=1)
        ks_ref[0, u] = jnp.sum(kps[u], axis=0)


def _page_specs(layer, pp):
    return [pl.BlockSpec((1, 1, PAGE_SIZE, MOBA_H, MOBA_DH),
                         lambda b, g, pt, u=u: (layer, pt[b, g * pp + u], 0, 0, 0)) for u in range(pp)]


def _moba_scores(name, page_table, q, cache_k, layer, match_f):
    B, nq, D = q.shape
    n_pg = page_table.shape[1]
    pp = MOBA_PAGES_PER_STEP
    assert n_pg % pp == 0
    gs = pltpu.PrefetchScalarGridSpec(
        num_scalar_prefetch=1, grid=(B, n_pg // pp),
        in_specs=[pl.BlockSpec((1, nq, D), lambda b, g, pt: (b, 0, 0)),
                  pl.BlockSpec(match_f.shape, lambda b, g, pt: (0, 0))] + _page_specs(layer, pp),
        out_specs=[pl.BlockSpec((1, pp, PAGE_SIZE, MOBA_H * nq), lambda b, g, pt: (b, g, 0, 0)),
                   pl.BlockSpec((1, pp, MOBA_H, MOBA_DH), lambda b, g, pt: (b, g, 0, 0))])
    return pl.pallas_call(
        functools.partial(_moba_scores_kernel, nq=nq, pp=pp), grid_spec=gs,
        out_shape=[jax.ShapeDtypeStruct((B, n_pg, PAGE_SIZE, MOBA_H * nq), F32),
                   jax.ShapeDtypeStruct((B, n_pg, MOBA_H, MOBA_DH), F32)],
        compiler_params=_params(2), name=name,
    )(page_table, q, match_f, *([cache_k] * pp))


def _moba_probs_kernel(s_ref, ks_ref, q_ref, kn_ref, sl_ref, p_ref, pn_ref, sel_scr, *, nq, n_blk, past):
    BK = MOBA_BLOCK
    HQ = MOBA_H * nq
    qr = _q_rows(q_ref[0], nq)
    ppb = BK // PAGE_SIZE
    kmean = jnp.sum(ks_ref[0].reshape(n_blk, ppb, MOBA_H, MOBA_DH), axis=1) * (1.0 / BK)
    gate = _token_scores(kmean, qr, nq, _dot3)
    nidx = lax.broadcasted_iota(jnp.int32, (n_blk, HQ), 0)
    own = past // BK
    sel = _select_topk(gate, nidx, nidx < own, axis=0)
    sel_scr[...] = sel.astype(F32)
    slope = sl_ref[...]
    scale = MOBA_DH ** -0.5
    qpos = past + lax.broadcasted_iota(jnp.int32, (1, HQ), 1) % nq
    tok = lax.broadcasted_iota(jnp.int32, (BK, HQ), 0)

    def scores(n):
        s = s_ref[0, n] * scale - slope * (qpos - (n * BK + tok)).astype(F32)
        return jnp.where(sel_scr[pl.ds(n, 1), :] > 0.5, s, NEG)

    sn = _token_scores(kn_ref[0], qr, nq, _dot1) * scale
    tn = lax.broadcasted_iota(jnp.int32, (PAGE_SIZE, HQ), 0)
    dn = qpos - (past + tn)
    sn = jnp.where(jnp.logical_and(dn >= 0, tn < nq), sn - slope * dn.astype(F32), NEG)
    m = lax.fori_loop(0, n_blk, lambda n, m_: jnp.maximum(m_, jnp.max(scores(n), axis=0, keepdims=True)),
                      jnp.max(sn, axis=0, keepdims=True))
    pn = jnp.exp(sn - m)
    l = lax.fori_loop(0, n_blk, lambda n, l_: l_ + jnp.sum(jnp.exp(scores(n) - m), axis=0, keepdims=True),
                      jnp.sum(pn, axis=0, keepdims=True))
    pn_ref[0] = (pn / l).astype(pn_ref.dtype)

    def write(n, c):
        p_ref[0, n] = (jnp.exp(scores(n) - m) / l).astype(p_ref.dtype)
        return c

    lax.fori_loop(0, n_blk, write, 0)


def _moba_probs(name, s, ksum, q, k_new_pad, slopes_hq, past):
    B, n_blk, BK, HQ = s.shape
    nq = q.shape[1]
    blk = pl.BlockSpec((1, n_blk, BK, HQ), lambda b: (b, 0, 0, 0))
    return pl.pallas_call(
        functools.partial(_moba_probs_kernel, nq=nq, n_blk=n_blk, past=past), grid=(B,),
        in_specs=[blk,
                  pl.BlockSpec((1, ksum.shape[1], MOBA_H, MOBA_DH), lambda b: (b, 0, 0, 0)),
                  pl.BlockSpec((1, nq, D_MODEL), lambda b: (b, 0, 0)),
                  pl.BlockSpec((1, PAGE_SIZE, MOBA_H, MOBA_DH), lambda b: (b, 0, 0, 0)),
                  pl.BlockSpec((1, HQ), lambda b: (0, 0))],
        out_specs=[blk, pl.BlockSpec((1, PAGE_SIZE, HQ), lambda b: (b, 0, 0))],
        out_shape=[jax.ShapeDtypeStruct((B, n_blk, BK, HQ), BF16), jax.ShapeDtypeStruct((B, PAGE_SIZE, HQ), BF16)],
        scratch_shapes=[pltpu.VMEM((n_blk, HQ), F32)],
        compiler_params=_params(1), name=name,
    )(s, ksum, q, k_new_pad, slopes_hq)


def _moba_pv_kernel(pt_ref, p_ref, m_ref, pn_ref, vn_ref, *refs, nq, n_steps, pp):
    v_refs, (o_ref, acc) = refs[:pp], refs[pp:]
    g = pl.program_id(1)
    HQ = MOBA_H * nq

    def spread(p):
        p2 = jnp.broadcast_to(p.astype(F32)[:, None, :], (PAGE_SIZE, MOBA_H, HQ))
        return (p2.reshape(PAGE_SIZE * MOBA_H, HQ) * m_ref[...]).astype(BF16)

    def values(v3):
        return v3.reshape(PAGE_SIZE * MOBA_H, MOBA_DH).astype(BF16)

    @pl.when(g == 0)
    def _():
        acc[...] = _dg(spread(pn_ref[0]), values(vn_ref[0]), _TN)

    p2s = [spread(p_ref[0, u]) for u in range(pp)]
    v2s = [values(v_ref[0, 0]) for v_ref in v_refs]
    acc[...] += functools.reduce(lambda x, y: x + y, [_dg(p2, v2, _TN) for p2, v2 in zip(p2s, v2s)])

    @pl.when(g == n_steps - 1)
    def _():
        a = acc[...]
        o_ref[0] = jnp.concatenate([a[h * nq:(h + 1) * nq, :] for h in range(MOBA_H)], axis=-1)


def _moba_pv(name, page_table, p, cache_v, pn, v_new_pad, nq, layer, match_f):
    B, n_pg, _, HQ = p.shape
    D = D_MODEL
    pp = MOBA_PAGES_PER_STEP
    assert n_pg % pp == 0
    gs = pltpu.PrefetchScalarGridSpec(
        num_scalar_prefetch=1, grid=(B, n_pg // pp),
        in_specs=[pl.BlockSpec((1, pp, PAGE_SIZE, HQ), lambda b, g, pt: (b, g, 0, 0)),
                  pl.BlockSpec(match_f.shape, lambda b, g, pt: (0, 0)),
                  pl.BlockSpec((1, PAGE_SIZE, HQ), lambda b, g, pt: (b, 0, 0)),
                  pl.BlockSpec((1, PAGE_SIZE, MOBA_H, MOBA_DH), lambda b, g, pt: (b, 0, 0, 0))]
        + _page_specs(layer, pp),
        out_specs=pl.BlockSpec((1, nq, D), lambda b, g, pt: (b, 0, 0)),
        scratch_shapes=[pltpu.VMEM((HQ, MOBA_DH), F32)])
    return pl.pallas_call(
        functools.partial(_moba_pv_kernel, nq=nq, n_steps=n_pg // pp, pp=pp), grid_spec=gs,
        out_shape=jax.ShapeDtypeStruct((B, nq, D), F32), compiler_params=_params(2), name=name,
    )(page_table, p, match_f, pn, v_new_pad, *([cache_v] * pp))


def _ident(x):
    return x


def _mix(x, xp, mu):
    return x + (xp - x) * mu


def _pad_cols(w, n):
    return jnp.pad(w, ((0, 0), (0, n - w.shape[1])))


def _pad_rows(w, n):
    return jnp.pad(w, ((0, n - w.shape[0]), (0, 0)))


def _rwkv_layer(tag, x3, shift0, s0, v_first, P, j, ln_g, ln_b):
    B, T, D = x3.shape
    M = B * T
    x = x3.reshape(M, D)
    wb = lambda w: w.astype(BF16)
    lp = LANES
    w_rkv = P['rwkv_w_rkv'][j]
    w_main = jnp.concatenate([wb(w_rkv[0]), wb(w_rkv[1]), wb(w_rkv[2])], axis=1)
    lora1 = [wb(_pad_cols(P['rwkv_w1'][j], lp)), wb(_pad_cols(P['rwkv_a1'][j], lp)), wb(P['rwkv_g1'][j])]
    lora2 = [wb(_pad_rows(P['rwkv_w2'][j], lp)), wb(_pad_rows(P['rwkv_a2'][j], lp)), wb(P['rwkv_g2'][j])]
    bias = [P['rwkv_w0'][j][None, :], P['rwkv_a0'][j][None, :]]
    vgate = None
    if j > 0:
        vgate = (wb(_pad_cols(P['rwkv_v1'][j - 1], lp)), wb(_pad_rows(P['rwkv_v2'][j - 1], lp)),
                 P['rwkv_v0'][j - 1][None, :], v_first)
    r, k, v, lw, a, g = _rwkv_proj(tag + 'proj', x3, shift0, P['rwkv_mu'][j], w_main, lora1, lora2, bias, vgate)
    if j == 0:
        v_first = v
    Tp = -(-T // WKV_CHUNK) * WKV_CHUNK
    seq = lambda t: jnp.pad(t.reshape(B, T, D), ((0, 0), (0, Tp - T), (0, 0)))
    vecp = lambda name: P[name][j].reshape(1, D)
    z, s_new = _wkv(tag + 'wkv', seq(r), seq(lw), seq(k), seq(v), seq(a), vecp('rwkv_k_k'), vecp('rwkv_k_a'),
                    vecp('rwkv_r_k'), vecp('rwkv_lnx_g'), vecp('rwkv_lnx_b'), s0)
    z = z[:, :T].reshape(M, D)
    xn = _mm_ln(tag + 'wo', [z, g], wb(P['rwkv_wo'][j]), x, ln_g, ln_b, lambda z_, g_: z_ * g_)
    return xn, s_new, x3[:, -1], v_first


def _s5_weights(P, j):
    ns, gs = D_MODEL // S5_SLAB, S5_SLAB // S5_GROUP
    eye = jnp.eye(gs, dtype=F32)

    def bd_in(b):
        bt = jnp.swapaxes(b.reshape(ns, gs, S5_P, S5_GROUP), 2, 3)
        return jnp.einsum('sgcp,gh->sgchp', bt, eye).reshape(ns, S5_SLAB, S5_STATES)

    def bd_out(c):
        ct = jnp.swapaxes(c.reshape(ns, gs, S5_GROUP, S5_P), 2, 3)
        return jnp.einsum('sgpc,gh->sgphc', ct, eye).reshape(ns, S5_STATES, S5_SLAB)

    wb = jnp.concatenate([bd_in(P['s5_b_re'][j]), bd_in(P['s5_b_im'][j])], axis=2)
    wc = jnp.concatenate([bd_out(P['s5_c_re'][j]), -bd_out(P['s5_c_im'][j])], axis=1)
    flat = lambda t: t.reshape(1, 1, S5_G * S5_P)
    dt = jnp.broadcast_to(jnp.exp(P['s5_log_dt'][j])[:, None], (S5_G, S5_P))
    return wb, wc, flat(P['s5_lam_re'][j]), flat(P['s5_lam_im'][j]), flat(dt)


def _s5_layer(tag, x3, h0r, h0i, P, j, ln_g, ln_b):
    B, T, D = x3.shape
    M = B * T
    x = x3.reshape(M, D)
    wb, wc, lr, li, dt = _s5_weights(P, j)
    st = lambda h: h.reshape(B, 1, S5_G * S5_P)
    z, hr, hi = _s5(tag + 'scan', x3, wb, wc, lr, li, dt, P['s5_d'][j][None, :], st(h0r), st(h0i))
    out, = _mm(tag + 'gate', [z.reshape(M, D)], [], [P['s5_w_val'][j].astype(BF16), P['s5_w_gate'][j].astype(BF16)],
               [], [], _ident, lambda val, gate: (val * _sigmoid(gate),), [F32])
    xn = _ln(tag + 'ln', x, out, ln_g, ln_b)
    return xn, hr.reshape(B, S5_G, S5_P), hi.reshape(B, S5_G, S5_P)


def _alibi_slopes():
    return 2.0 ** (-8.0 * jnp.arange(1, MOBA_H + 1, dtype=F32) / MOBA_H)


def _moba_layer(tag, x3, cache_k, cache_v, page_table, P, j, ln_g, ln_b):
    B, T, D = x3.shape
    M = B * T
    x = x3.reshape(M, D)
    wqkv = P['moba_w_qkv'][j]
    ws = [wqkv[:, n * D:(n + 1) * D].astype(BF16) for n in range(3)]
    q, k, v = _mm(tag + 'qkv', [x], [], ws, [], [], _ident, lambda a, b, c: (a, b, c), [F32, F32, F32])
    q3, k3, v3 = (t.reshape(B, T, D) for t in (q, k, v))
    slopes = _alibi_slopes()
    if cache_k is None:
        sl = jnp.broadcast_to(slopes[:, None, None], (MOBA_H, 1, LANES))
        o = _moba_prompt(tag + 'attn', q3, k3, v3, sl)
    else:
        n_pg = page_table.shape[1]
        past = n_pg * PAGE_SIZE
        assert past % MOBA_BLOCK == 0 and T <= PAGE_SIZE and T <= MOBA_BLOCK
        HQ = MOBA_H * T
        match_f = _head_match(PAGE_SIZE, T).astype(F32)
        s, ksum = _moba_scores(tag + 'scores', page_table, q3, cache_k, j, match_f)
        sl = jnp.repeat(slopes, T)[None, :]
        pad = lambda t: jnp.pad(t.reshape(B, T, MOBA_H, MOBA_DH), ((0, 0), (0, PAGE_SIZE - T), (0, 0), (0, 0)))
        p, pn = _moba_probs(tag + 'probs', s.reshape(B, past // MOBA_BLOCK, MOBA_BLOCK, HQ), ksum, q3, pad(k3),
                            sl, past)
        o = _moba_pv(tag + 'pv', page_table, p.reshape(B, n_pg, PAGE_SIZE, HQ), cache_v, pn, pad(v3), T, j, match_f)
    xn = _mm_ln(tag + 'wo', [o.reshape(M, D)], P['moba_wo'][j].astype(BF16), x, ln_g, ln_b, _ident)
    return xn, k3.reshape(B, T, MOBA_H, MOBA_DH), v3.reshape(B, T, MOBA_H, MOBA_DH)


def _ffn_layer(tag, x3, conv0, P, i, ln_g, ln_b):
    B, T, D = x3.shape
    w_in = P['ffn_w_in'][i]
    h, cst = _ffn_in(tag + 'in', x3, w_in[:, :D_FF].astype(BF16), w_in[:, D_FF:].astype(BF16), conv0,
                     P['ffn_conv_w'][i], P['ffn_conv_b'][i][None, :])
    xn = _mm_ln(tag + 'down', [h], P['ffn_w_down'][i].astype(BF16), x3.reshape(B * T, D), ln_g, ln_b, _ident)
    return xn.reshape(B, T, D), cst


def _trunk(grp, x3, wkv0, shift0, s5re0, s5im0, conv0, cache_k, cache_v, page_table, P):
    B, T, D = x3.shape
    wkv_new, shift_new, s5re_new, s5im_new, k_new, v_new, conv_new = [], [], [], [], [], [], []
    v_first = None
    for i in range(DEPTH):
        kind, j = i % 3, i // 3
        tag = f'{grp}{i}_'
        g0, b0 = P['ln_g'][i, 0][None, :], P['ln_b'][i, 0][None, :]
        g1, b1 = P['ln_g'][i, 1][None, :], P['ln_b'][i, 1][None, :]
        if kind == 0:
            xn, S, last, v_first = _rwkv_layer(tag, x3, shift0[j], wkv0[j], v_first, P, j, g0, b0)
            wkv_new.append(S)
            shift_new.append(last)
        elif kind == 1:
            xn, hr, hi = _s5_layer(tag, x3, s5re0[j], s5im0[j], P, j, g0, b0)
            s5re_new.append(hr)
            s5im_new.append(hi)
        else:
            xn, kr, vr = _moba_layer(tag, x3, cache_k, cache_v, page_table, P, j, g0, b0)
            k_new.append(kr)
            v_new.append(vr)
        x3, cst = _ffn_layer(tag + 'ffn_', xn.reshape(B, T, D), conv0[i], P, i, g1, b1)
        conv_new.append(cst)
    return (x3, jnp.stack(wkv_new), jnp.stack(shift_new), jnp.stack(s5re_new), jnp.stack(s5im_new),
            jnp.stack(conv_new), jnp.stack(k_new), jnp.stack(v_new))


def kernel(x_prompt, x_sample, state_rwkv_wkv, state_rwkv_shift, state_s5_re, state_s5_im, state_ffn_conv, cache_k, cache_v, page_table, ln_g, ln_b, rwkv_mu, rwkv_w_rkv, rwkv_w0, rwkv_w1, rwkv_w2, rwkv_a0, rwkv_a1, rwkv_a2, rwkv_v0, rwkv_v1, rwkv_v2, rwkv_g1, rwkv_g2, rwkv_k_k, rwkv_k_a, rwkv_r_k, rwkv_lnx_g, rwkv_lnx_b, rwkv_wo, s5_log_dt, s5_lam_re, s5_lam_im, s5_b_re, s5_b_im, s5_c_re, s5_c_im, s5_d, s5_w_val, s5_w_gate, moba_w_qkv, moba_wo, ffn_w_in, ffn_conv_w, ffn_conv_b, ffn_w_down):
    P = dict(ln_g=ln_g, ln_b=ln_b, rwkv_mu=rwkv_mu, rwkv_w_rkv=rwkv_w_rkv, rwkv_w0=rwkv_w0, rwkv_w1=rwkv_w1,
             rwkv_w2=rwkv_w2, rwkv_a0=rwkv_a0, rwkv_a1=rwkv_a1, rwkv_a2=rwkv_a2, rwkv_v0=rwkv_v0,
             rwkv_v1=rwkv_v1, rwkv_v2=rwkv_v2, rwkv_g1=rwkv_g1, rwkv_g2=rwkv_g2, rwkv_k_k=rwkv_k_k,
             rwkv_k_a=rwkv_k_a, rwkv_r_k=rwkv_r_k.reshape(rwkv_r_k.shape[0], D_MODEL), rwkv_lnx_g=rwkv_lnx_g,
             rwkv_lnx_b=rwkv_lnx_b, rwkv_wo=rwkv_wo, s5_log_dt=s5_log_dt, s5_lam_re=s5_lam_re,
             s5_lam_im=s5_lam_im, s5_b_re=s5_b_re, s5_b_im=s5_b_im, s5_c_re=s5_c_re, s5_c_im=s5_c_im, s5_d=s5_d,
             s5_w_val=s5_w_val, s5_w_gate=s5_w_gate, moba_w_qkv=moba_w_qkv, moba_wo=moba_wo,
             ffn_w_in=ffn_w_in, ffn_conv_w=ffn_conv_w, ffn_conv_b=ffn_conv_b, ffn_w_down=ffn_w_down)
    B = x_prompt.shape[0]
    n_rwkv, n_s5 = state_rwkv_wkv.shape[0], state_s5_re.shape[0]
    zeros = lambda *s: jnp.zeros(s, F32)
    outs_p = _trunk('p', x_prompt, zeros(n_rwkv, B, RWKV_H, RWKV_N, RWKV_N), zeros(n_rwkv, B, D_MODEL),
                    zeros(n_s5, B, S5_G, S5_P), zeros(n_s5, B, S5_G, S5_P), zeros(DEPTH, B, CONV_W - 1, D_FF),
                    None, None, None, P)
    outs_s = _trunk('s', x_sample, state_rwkv_wkv, state_rwkv_shift, state_s5_re, state_s5_im, state_ffn_conv,
                    cache_k, cache_v, page_table, P)
    return (outs_p[0], outs_s[0]) + tuple(outs_p[1:]) + tuple(outs_s[1:])
```

```python
import functools
import math

import jax
import jax.numpy as jnp
from jax import lax
from jax.experimental import pallas as pl
from jax.experimental.pallas import tpu as pltpu

F32 = jnp.float32
BF16 = jnp.bfloat16

D_MODEL = 2048
DEPTH = 4
RWKV_N = 64
RWKV_H = D_MODEL // RWKV_N
RWKV_GN_EPS = 64e-5
S5_GROUP = 16
S5_G = D_MODEL // S5_GROUP
S5_P = 64
MOBA_H = 16
MOBA_DH = D_MODEL // MOBA_H
MOBA_BLOCK = 256
MOBA_TOPK = 3
PAGE_SIZE = 128
D_FF = 11 * D_MODEL // 4
CONV_W = 3
LN_EPS = 1e-5
DN_ALPHA = (2.0 * DEPTH) ** 0.25
NEG = -1e30

LANES = 128
SUBLANES = 8
VMEM_LIMIT_BYTES = 56 * 1024 * 1024

MM_ROW_TILE = 1024
MM_LN_RESIDENT_BYTES = 24 * 1024 * 1024
MM_LN_WEIGHT_TILE_BYTES = 12 * 1024 * 1024
WKV_CHUNK = 64
WKV_CHAINS = 16
WKV_HEADS = 4
MOBA_PAGES_PER_STEP = 8
S5_SLAB = LANES
S5_STATES = (S5_SLAB // S5_GROUP) * S5_P


def _params(n_axes):
    return pltpu.CompilerParams(dimension_semantics=("arbitrary",) * n_axes,
                                vmem_limit_bytes=VMEM_LIMIT_BYTES)


def _sigmoid(x):
    return 1.0 / (1.0 + jnp.exp(-x))


def _softplus(x):
    return jnp.maximum(x, 0.0) + jnp.log1p(jnp.exp(-jnp.abs(x)))


def _dg(a, b, dims):
    return lax.dot_general(a, b, (dims, ((), ())), preferred_element_type=F32)


_NN = ((1,), (0,))
_NT = ((1,), (1,))
_TN = ((0,), (0,))


def _split(a):
    hi = a.astype(BF16)
    lo = (a - hi.astype(F32)).astype(BF16)
    return hi, lo


def _dot3(a, b, dims=_NN):
    ah, al = _split(a)
    bh, bl = _split(b)
    return _dg(ah, bh, dims) + (_dg(al, bh, dims) + _dg(ah, bl, dims))


def _dot1(a, b, dims=_NN):
    return _dg(a.astype(BF16), b.astype(BF16), dims)


def _dot_exact_lhs(lhs_bf16, x):
    x1 = x.astype(BF16)
    r1 = x - x1.astype(F32)
    x2 = r1.astype(BF16)
    x3 = (r1 - x2.astype(F32)).astype(BF16)
    return _dg(lhs_bf16, x1, _NN) + (_dg(lhs_bf16, x2, _NN) + _dg(lhs_bf16, x3, _NN))


def _layer_norm(y, g, b):
    mu = jnp.mean(y, axis=-1, keepdims=True)
    yc = y - mu
    var = jnp.mean(yc * yc, axis=-1, keepdims=True)
    return yc * lax.rsqrt(var + LN_EPS) * g + b


def _mm_kernel(*refs, n_row, n_vec, n_w, n_erow, n_evec, n_out, prologue, epilogue):
    it = iter(refs)
    rows = [next(it) for _ in range(n_row)]
    vecs = [next(it) for _ in range(n_vec)]
    ws = [next(it) for _ in range(n_w)]
    erows = [next(it) for _ in range(n_erow)]
    evecs = [next(it) for _ in range(n_evec)]
    outs = [next(it) for _ in range(n_out)]
    a_scr = next(it)

    @pl.when(pl.program_id(1) == 0)
    def _():
        a_scr[...] = prologue(*[r[...] for r in rows], *[v[...] for v in vecs]).astype(BF16)

    a = a_scr[...]
    accs = [jnp.dot(a, w[...], preferred_element_type=F32) for w in ws]
    res = epilogue(*accs, *[e[...] for e in erows], *[e[...] for e in evecs])
    for o_ref, o in zip(outs, res):
        o_ref[...] = o.astype(o_ref.dtype)


def _mm(name, rows, vecs, ws, erows, evecs, prologue, epilogue, out_dtypes):
    M, K = rows[0].shape
    N = ws[0].shape[1]
    tm = MM_ROW_TILE if (M % MM_ROW_TILE == 0 and len(rows) == 1) else min(512, M)
    tn = 512 if N % 512 == 0 else N
    assert M % tm == 0 and N % tn == 0
    in_specs = ([pl.BlockSpec((tm, K), lambda i, j: (i, 0)) for _ in rows]
                + [pl.BlockSpec((1, K), lambda i, j: (0, 0)) for _ in vecs]
                + [pl.BlockSpec((K, tn), lambda i, j: (0, j)) for _ in ws]
                + [pl.BlockSpec((tm, tn), lambda i, j: (i, j)) for _ in erows]
                + [pl.BlockSpec((1, tn), lambda i, j: (0, j)) for _ in evecs])
    out_specs = [pl.BlockSpec((tm, tn), lambda i, j: (i, j)) for _ in out_dtypes]
    out_shape = [jax.ShapeDtypeStruct((M, N), dt) for dt in out_dtypes]
    kern = functools.partial(_mm_kernel, n_row=len(rows), n_vec=len(vecs), n_w=len(ws), n_erow=len(erows),
                             n_evec=len(evecs), n_out=len(out_dtypes), prologue=prologue, epilogue=epilogue)
    return pl.pallas_call(
        kern, grid=(M // tm, N // tn), in_specs=in_specs, out_specs=out_specs, out_shape=out_shape,
        scratch_shapes=[pltpu.VMEM((tm, K), BF16)], compiler_params=_params(2), name=name,
    )(*rows, *vecs, *ws, *erows, *evecs)


def _mm_ln_kernel(*refs, n_row, prologue, n_k):
    it = iter(refs)
    rows = [next(it) for _ in range(n_row)]
    w_ref, x_ref, g_ref, b_ref, o_ref, acc = (next(it) for _ in range(6))
    k = pl.program_id(1)
    a = prologue(*[r[...] for r in rows]).astype(BF16)
    p = jnp.dot(a, w_ref[...], preferred_element_type=F32)
    if n_k == 1:
        o_ref[...] = _layer_norm(DN_ALPHA * x_ref[...] + p, g_ref[...], b_ref[...])
        return

    @pl.when(k == 0)
    def _():
        acc[...] = p

    @pl.when(jnp.logical_and(k > 0, k < n_k - 1))
    def _():
        acc[...] += p

    @pl.when(k == n_k - 1)
    def _():
        o_ref[...] = _layer_norm(DN_ALPHA * x_ref[...] + (acc[...] + p), g_ref[...], b_ref[...])


def _mm_ln(name, rows, w, x, g, b, prologue):
    M, K = rows[0].shape
    N = w.shape[1]
    w_bytes = K * N * 2
    if w_bytes <= MM_LN_WEIGHT_TILE_BYTES:
        n_k, tm, w_mode = 1, min(512, M), None
    elif w_bytes <= MM_LN_RESIDENT_BYTES:
        n_k, tm, w_mode = 1, min(256, M), pl.Buffered(1)
    else:
        n_k, tm, w_mode = -(-w_bytes // MM_LN_WEIGHT_TILE_BYTES), min(512, M), None
    tk = K // n_k
    assert M % tm == 0 and K % n_k == 0 and tk % LANES == 0
    in_specs = ([pl.BlockSpec((tm, tk), lambda i, k: (i, k)) for _ in rows]
                + [pl.BlockSpec((tk, N), lambda i, k: (k, 0), pipeline_mode=w_mode),
                   pl.BlockSpec((tm, N), lambda i, k: (i, 0)),
                   pl.BlockSpec((1, N), lambda i, k: (0, 0)),
                   pl.BlockSpec((1, N), lambda i, k: (0, 0))])
    kern = functools.partial(_mm_ln_kernel, n_row=len(rows), prologue=prologue, n_k=n_k)
    return pl.pallas_call(
        kern, grid=(M // tm, n_k), in_specs=in_specs,
        out_specs=pl.BlockSpec((tm, N), lambda i, k: (i, 0)),
        out_shape=jax.ShapeDtypeStruct((M, N), F32),
        scratch_shapes=[pltpu.VMEM((tm, N), F32)], compiler_params=_params(2), name=name,
    )(*rows, w, x, g, b)


def _ln_kernel(x_ref, h_ref, g_ref, b_ref, o_ref):
    o_ref[...] = _layer_norm(DN_ALPHA * x_ref[...] + h_ref[...], g_ref[...], b_ref[...])


def _ln(name, x, h, g, b):
    M, N = x.shape
    tm = min(512, M)
    row = pl.BlockSpec((tm, N), lambda i: (i, 0))
    vec = pl.BlockSpec((1, N), lambda i: (0, 0))
    return pl.pallas_call(_ln_kernel, grid=(M // tm,), in_specs=[row, row, vec, vec], out_specs=row,
                          out_shape=jax.ShapeDtypeStruct((M, N), F32), compiler_params=_params(1), name=name)(x, h, g, b)


def _ffn_in_kernel(x_ref, wg_ref, wv_ref, c0_ref, cw_ref, cb_ref, h_ref, cs_ref, a_scr, carry, *, bt, tt):
    t = pl.program_id(1)
    j = pl.program_id(2)
    rows = bt * tt
    tn = wg_ref.shape[1]

    @pl.when(j == 0)
    def _():
        a_scr[...] = x_ref[...].reshape(rows, x_ref.shape[2]).astype(BF16)

    @pl.when(t == 0)
    def _():
        carry[j] = c0_ref[...]

    a = a_scr[...]
    hg = jnp.dot(a, wg_ref[...], preferred_element_type=F32)
    hv = jnp.dot(a, wv_ref[...], preferred_element_type=F32)
    prev = carry[j]
    p2 = jnp.broadcast_to(prev[:, 0:1, :], (bt, tt, tn)).reshape(rows, tn)
    p1 = jnp.broadcast_to(prev[:, 1:2, :], (bt, tt, tn)).reshape(rows, tn)
    r = lax.broadcasted_iota(jnp.int32, (rows, tn), 0) % tt
    s1 = jnp.where(r == 0, p1, pltpu.roll(hg, 1, axis=0))
    s2 = jnp.where(r == 0, p2, jnp.where(r == 1, p1, pltpu.roll(hg, 2, axis=0)))
    cw = cw_ref[...]
    c = cb_ref[...] + cw[0:1, :] * s2 + cw[1:2, :] * s1 + cw[2:3, :] * hg
    h_ref[...] = (c * _sigmoid(c) * hv).astype(h_ref.dtype)
    last = hg.reshape(bt, tt, tn)[:, tt - 2:, :]
    carry[j] = last
    cs_ref[0] = last


def _ffn_in(name, x3, wg, wv, conv0, cw, cb):
    B, T, K = x3.shape
    F = wg.shape[1]
    bt, tt = (1, MM_ROW_TILE) if T % MM_ROW_TILE == 0 else (B, T)
    tn = 512
    assert B % bt == 0 and T % tt == 0 and F % tn == 0 and tt >= CONV_W - 1
    n_j = F // tn
    kern = functools.partial(_ffn_in_kernel, bt=bt, tt=tt)
    n_t = T // tt
    h, cst = pl.pallas_call(
        kern, grid=(B // bt, n_t, n_j),
        in_specs=[pl.BlockSpec((bt, tt, K), lambda b, t, j: (b, t, 0)),
                  pl.BlockSpec((K, tn), lambda b, t, j: (0, j)),
                  pl.BlockSpec((K, tn), lambda b, t, j: (0, j)),
                  pl.BlockSpec((bt, CONV_W - 1, tn), lambda b, t, j: (b, 0, j)),
                  pl.BlockSpec((CONV_W, tn), lambda b, t, j: (0, j)),
                  pl.BlockSpec((1, tn), lambda b, t, j: (0, j))],
        out_specs=[pl.BlockSpec((bt * tt, tn), lambda b, t, j: (b * n_t + t, j)),
                   pl.BlockSpec((1, bt, CONV_W - 1, tn), lambda b, t, j: (t, b, 0, j))],
        out_shape=[jax.ShapeDtypeStruct((B * T, F), BF16),
                   jax.ShapeDtypeStruct((n_t, B, CONV_W - 1, F), F32)],
        scratch_shapes=[pltpu.VMEM((bt * tt, K), BF16), pltpu.VMEM((n_j, bt, CONV_W - 1, tn), F32)],
        compiler_params=_params(3), name=name,
    )(x3, wg, wv, conv0, cw, cb)
    return h, cst[n_t - 1]


RWKV_PROJ_TN = 1024
RWKV_PROJ_TILES = D_MODEL // RWKV_PROJ_TN


def _rwkv_proj_kernel(*refs, has_vgate, seq_len, tiles_per_seq):
    it = iter(refs)
    x_ref, sh_ref, mu_ref, wm_ref, w1_ref, a1_ref, g1_ref, w2_ref, a2_ref, g2_ref, w0_ref, a0_ref = (
        next(it) for _ in range(12))
    if has_vgate:
        v1_ref, v2_ref, v0_ref, vf_ref = (next(it) for _ in range(4))
    r_ref, k_ref, v_ref, lw_ref, a_ref, g_ref = (next(it) for _ in range(6))
    mix_scr, hw_scr, ha_scr, hg_scr, last_scr = (next(it) for _ in range(5))
    if has_vgate:
        hv_scr = next(it)
    i = pl.program_id(0)
    j = pl.program_id(1)
    dot = lambda a, b: jnp.dot(a, b, preferred_element_type=F32)

    @pl.when(j == 0)
    def _():
        x = x_ref[...]
        tm, K = x.shape
        row = lax.broadcasted_iota(jnp.int32, (tm, K), 0)
        prev = pltpu.roll(x, 1, axis=0)
        if tiles_per_seq > 1:
            first = jnp.where(i % tiles_per_seq == 0, sh_ref[0], last_scr[0:1, :])
            prev = jnp.where(row == 0, first, prev)
            last_scr[0:1, :] = x[tm - 1:tm, :]
        else:
            n_seq = tm // seq_len
            sh = jnp.broadcast_to(sh_ref[...], (n_seq, seq_len, K)).reshape(tm, K)
            prev = jnp.where(row % seq_len == 0, sh, prev)
        d = prev - x
        mix = lambda n: (x + d * mu_ref[n:n + 1, :]).astype(BF16)
        mix_scr[0] = mix(0)
        mix_scr[1] = mix(2)
        xv = mix(3)
        mix_scr[2] = xv
        hw_scr[...] = jnp.tanh(dot(mix(1), w1_ref[...])).astype(BF16)
        ha_scr[...] = dot(mix(4), a1_ref[...]).astype(BF16)
        hg_scr[...] = _sigmoid(dot(mix(5), g1_ref[...])).astype(BF16)
        if has_vgate:
            hv_scr[...] = dot(xv, v1_ref[...]).astype(BF16)

    grp = j // RWKV_PROJ_TILES

    @pl.when(grp == 0)
    def _():
        r_ref[...] = dot(mix_scr[0], wm_ref[...]).astype(r_ref.dtype)
        lw_ref[...] = -jnp.exp(-_softplus(-(w0_ref[...] + dot(hw_scr[...], w2_ref[...]))) - 0.5)

    @pl.when(grp == 1)
    def _():
        k_ref[...] = dot(mix_scr[1], wm_ref[...]).astype(k_ref.dtype)
        a_ref[...] = _sigmoid(a0_ref[...] + dot(ha_scr[...], a2_ref[...])).astype(a_ref.dtype)

    @pl.when(grp == 2)
    def _():
        v = dot(mix_scr[2], wm_ref[...])
        if has_vgate:
            v = v + (vf_ref[...] - v) * _sigmoid(v0_ref[...] + dot(hv_scr[...], v2_ref[...]))
        v_ref[...] = v.astype(v_ref.dtype)
        g_ref[...] = dot(hg_scr[...], g2_ref[...]).astype(g_ref.dtype)


def _rwkv_proj(name, x3, shift0, mu, w_main, lora1, lora2, bias, vgate):
    B, T, K = x3.shape
    M = B * T
    tm, tn, nt = min(512, M), RWKV_PROJ_TN, RWKV_PROJ_TILES
    assert M % tm == 0 and w_main.shape == (K, 3 * D_MODEL) and (T % tm == 0 or tm % T == 0)
    tiles_per_seq = max(1, T // tm)
    seqs_per_tile = max(1, tm // T)
    row = pl.BlockSpec((tm, K), lambda i, j: (i, 0))
    full = lambda a: pl.BlockSpec(a.shape, lambda i, j: (0, 0))
    col = lambda a: pl.BlockSpec((a.shape[0], tn), lambda i, j: (0, j % nt))
    shift = pl.BlockSpec((seqs_per_tile, 1, K), lambda i, j: (i // tiles_per_seq, 0, 0))
    ins = [x3.reshape(M, K), shift0.reshape(B, 1, K), mu, w_main, *lora1, *lora2, *bias]
    specs = ([row, shift, full(mu), pl.BlockSpec((K, tn), lambda i, j: (0, j))] + [full(w) for w in lora1]
             + [col(w) for w in lora2] + [col(b) for b in bias])
    scratch = ([pltpu.VMEM((3, tm, K), BF16)] + [pltpu.VMEM((tm, w.shape[1]), BF16) for w in lora1]
               + [pltpu.VMEM((SUBLANES, K), F32)])
    if vgate is not None:
        v1, v2, v0, v_first = vgate
        ins += [v1, v2, v0, v_first]
        specs += [full(v1), col(v2), col(v0), pl.BlockSpec((tm, tn), lambda i, j: (i, j % nt))]
        scratch.append(pltpu.VMEM((tm, v1.shape[1]), BF16))
    out = lambda grp: pl.BlockSpec((tm, tn), lambda i, j: (i, jnp.clip(j - grp * nt, 0, nt - 1)))
    return pl.pallas_call(
        functools.partial(_rwkv_proj_kernel, has_vgate=vgate is not None, seq_len=T, tiles_per_seq=tiles_per_seq),
        grid=(M // tm, 3 * nt),
        in_specs=specs, out_specs=[out(0), out(1), out(2), out(0), out(1), out(2)],
        out_shape=[jax.ShapeDtypeStruct((M, D_MODEL), dt) for dt in (BF16, BF16, BF16, F32, BF16, BF16)],
        scratch_shapes=scratch, compiler_params=_params(2), name=name,
    )(*ins)


def _wkv_chunks(r, lw, k, v, a, prm, S, consts):
    C = WKV_CHUNK
    bdmask, l_incl, strict, incl, eye = consts

    def bd(xp):
        return jnp.concatenate([xp.astype(BF16)] * WKV_HEADS, axis=0) * bdmask

    def headsum(xp):
        hi, lo = _split(xp)
        return _dg(hi, bdmask, _NN) + _dg(lo, bdmask, _NN)

    def each(f, *lists):
        return [f(*args) for args in zip(*lists)]

    kkp, kap, rkp, lng, lnb = (list(t) for t in zip(*prm))
    kkn = each(lambda k_, p_: k_ * p_, k, kkp)
    kp = each(lambda k_, a_, p_: k_ * (1.0 + (a_ - 1.0) * p_), k, a, kap)
    sums = each(lambda kkn_, r_, kp_, p_: headsum(jnp.concatenate([kkn_ * kkn_, r_ * kp_ * p_], axis=0)),
                kkn, r, kp, rkp)
    kk = each(lambda kkn_, s_: kkn_ / jnp.maximum(jnp.sqrt(s_[:C]), 1e-12), kkn, sums)
    bonus = each(lambda s_, v_: s_[C:] * v_, sums, v)
    b = each(lambda kk_, a_: kk_ * a_, kk, a)
    cum = each(lambda lw_: _dot_exact_lhs(l_incl, lw_), lw)
    e_neg = each(lambda c_: jnp.exp(-c_), cum)
    x = each(lambda kk_, c_, lw_, r_: jnp.concatenate([kk_ * jnp.exp(c_ - lw_), r_ * jnp.exp(c_)],
                                                       axis=0).astype(BF16), kk, cum, lw, r)
    g_b = each(lambda x_, b_, e_: _dg(x_, bd(b_ * e_), _NT), x, b, e_neg)
    g_k = each(lambda x_, k_, e_: _dg(x_, bd(k_ * e_), _NT), x, kp, e_neg)
    xs = each(lambda x_, s_: _dg(x_, s_.astype(BF16), _NT), x, S)
    mb = each(lambda g_: jnp.where(incl, g_[C:], 0.0), g_b)
    kv = each(lambda g_, v_: _dg(jnp.concatenate([jnp.where(strict, g_[:C], 0.0), jnp.where(incl, g_[C:], 0.0)],
                                                 axis=0).astype(BF16), bd(v_), _NN), g_k, v)
    rhs = each(lambda xs_, kv_: -(xs_[:C] + kv_[:C]), xs, kv)
    npow = each(lambda g_: -jnp.where(strict, g_[:C], 0.0), g_b)
    tinv = each(lambda n_: eye + n_, npow)
    npow = each(lambda n_: _dg(n_.astype(BF16), bd(n_), _NN), npow)
    for _ in range(int(math.log2(C)) - 2):
        both = each(lambda t_, n_: _dg(jnp.concatenate([t_, n_], axis=0).astype(BF16), bd(n_), _NN), tinv, npow)
        tinv = each(lambda t_, b_: t_ + b_[:C], tinv, both)
        npow = each(lambda b_: b_[C:], both)
    tinv = each(lambda t_, n_: t_ + _dg(t_.astype(BF16), bd(n_), _NN), tinv, npow)
    u = each(lambda t_, r_: _dg(t_.astype(BF16), bd(r_), _NN), tinv, rhs)
    y = each(lambda xs_, kv_, mb_, u_: xs_[C:] + kv_[C:] + _dg(mb_.astype(BF16), bd(u_), _NN), xs, kv, mb, u)
    upd = each(lambda u_, v_, b_, k_, c_: _dg(
        jnp.concatenate([u_, v_], axis=0).astype(BF16),
        (jnp.concatenate([b_, k_], axis=0) * jnp.exp(c_[C - 1:C, :] - jnp.concatenate([c_, c_], axis=0))).astype(BF16),
        _TN), u, v, b, kp, cum)
    bdmask_f = bdmask.astype(F32)
    s_new = each(lambda s_, c_, u_: s_ * jnp.exp(c_[C - 1:C, :]) + u_ * bdmask_f, S, cum, upd)
    mu = each(lambda y_: _dg(y_.astype(BF16), bdmask, _NN) * (1.0 / RWKV_N), y)
    yc = each(lambda y_, m_: y_ - m_, y, mu)
    var = each(lambda yc_: _dg((yc_ * yc_).astype(BF16), bdmask, _NN) * (1.0 / RWKV_N), yc)
    z = each(lambda yc_, var_, g_, b_, bo_: yc_ * lax.rsqrt(var_ + RWKV_GN_EPS) * g_ + b_ + bo_,
             yc, var, lng, lnb, bonus)
    return z, s_new


def _wkv_kernel(r_ref, lw_ref, k_ref, v_ref, a_ref, kk_ref, ka_ref, rk_ref, g_ref, b_ref, s0_ref,
                z_ref, so_ref, s_scr, *, n_t, nb, n_grp):
    t = pl.program_id(1)
    C, N, HW = WKV_CHUNK, RWKV_N, WKV_HEADS * RWKV_N
    ri = lax.broadcasted_iota(jnp.int32, (HW, HW), 0) // N
    ci = lax.broadcasted_iota(jnp.int32, (HW, HW), 1) // N
    bdmask_f = (ri == ci).astype(F32)

    chains = [(bi, g) for bi in range(nb) for g in range(n_grp)]

    @pl.when(t == 0)
    def _():
        for n, (bi, g) in enumerate(chains):
            rows = s0_ref[bi, g * WKV_HEADS:(g + 1) * WKV_HEADS].reshape(HW, N)
            s_scr[n] = jnp.concatenate([rows] * WKV_HEADS, axis=1) * bdmask_f

    tt = lax.broadcasted_iota(jnp.int32, (C, C), 0)
    ss = lax.broadcasted_iota(jnp.int32, (C, C), 1)
    tp = lax.broadcasted_iota(jnp.int32, (C, HW), 0)
    sp = lax.broadcasted_iota(jnp.int32, (C, HW), 1) % N
    consts = (bdmask_f.astype(BF16), (tt >= ss).astype(BF16), tp > sp, tp >= sp, (tp == sp).astype(F32))
    sls = [slice(g * HW, (g + 1) * HW) for _, g in chains]
    tok = lambda ref: [ref[bi, :, sl].astype(F32) for (bi, _), sl in zip(chains, sls)]
    prm = [tuple(ref[:, sl] for ref in (kk_ref, ka_ref, rk_ref, g_ref, b_ref)) for sl in sls]
    zs, s_news = _wkv_chunks(tok(r_ref), tok(lw_ref), tok(k_ref), tok(v_ref), tok(a_ref), prm,
                             [s_scr[n] for n in range(len(chains))], consts)
    for n, ((bi, _), sl) in enumerate(zip(chains, sls)):
        s_scr[n] = s_news[n]
        z_ref[bi, :, sl] = zs[n]

    @pl.when(t == n_t - 1)
    def _():
        for n, (bi, g) in enumerate(chains):
            for h in range(WKV_HEADS):
                so_ref[bi, g * WKV_HEADS + h] = s_scr[n, h * N:(h + 1) * N, h * N:(h + 1) * N]


def _wkv(name, r, lw, k, v, a, kkp, kap, rkp, lng, lnb, s0):
    B, T, D = r.shape
    C, HW = WKV_CHUNK, WKV_HEADS * RWKV_N
    n_grp = min(max(1, WKV_CHAINS // B), D // HW)
    W = n_grp * HW
    assert T % C == 0 and D % W == 0
    n_t = T // C
    tok = pl.BlockSpec((B, C, W), lambda h, t: (0, t, h))
    vec = pl.BlockSpec((1, W), lambda h, t: (0, h))
    st = pl.BlockSpec((B, n_grp * WKV_HEADS, RWKV_N, RWKV_N), lambda h, t: (0, h, 0, 0))
    return pl.pallas_call(
        functools.partial(_wkv_kernel, n_t=n_t, nb=B, n_grp=n_grp), grid=(D // W, n_t),
        in_specs=[tok] * 5 + [vec] * 5 + [st], out_specs=[tok, st],
        out_shape=[jax.ShapeDtypeStruct((B, T, D), F32), jax.ShapeDtypeStruct(s0.shape, F32)],
        scratch_shapes=[pltpu.VMEM((B * n_grp, HW, HW), F32)], compiler_params=_params(2), name=name,
    )(r, lw, k, v, a, kkp, kap, rkp, lng, lnb, s0)


def _cmul(ar, ai, br, bi):
    return ar * br - ai * bi, ar * bi + ai * br


def _dot3w(a, w_hi, w_lo, dims=_NN):
    ah, al = _split(a)
    return _dg(ah, w_hi, dims) + (_dg(al, w_hi, dims) + _dg(ah, w_lo, dims))


S5_SHIFTS = (1, 2, 4)


def _s5_kernel(x_ref, wbh_ref, wbl_ref, wc_ref, lr_ref, li_ref, dt_ref, d_ref, h0r_ref, h0i_ref,
               z_ref, hr_ref, hi_ref, pw_re, pw_im, sh_re, sh_im, cf, hc, *, n_t, tt):
    t = pl.program_id(2)
    R = SUBLANES

    @pl.when(t == 0)
    def _():
        hc[0:1, :] = h0r_ref[0]
        hc[1:2, :] = h0i_ref[0]

    @pl.when(jnp.logical_and(t == 0, pl.program_id(1) == 0))
    def _():
        lr, li, dt = lr_ref[0], li_ref[0], dt_ref[0]
        rid = lax.broadcasted_iota(jnp.int32, (R, S5_STATES), 0)
        n = (rid + 1).astype(F32)
        mag = jnp.exp(lr * dt * n)
        ang = li * dt * n
        pw_re[...] = mag * jnp.cos(ang)
        pw_im[...] = mag * jnp.sin(ang)
        for m, sh in enumerate(S5_SHIFTS):
            sh_re[m] = jnp.where(rid >= sh, pw_re[sh - 1:sh, :], 0.0)
            sh_im[m] = jnp.where(rid >= sh, pw_im[sh - 1:sh, :], 0.0)
        ar, ai = pw_re[0:1, :], pw_im[0:1, :]
        den = lr * lr + li * li
        cf[0:1, :] = ((ar - 1.0) * lr + ai * li) / den
        cf[1:2, :] = (ai * lr - (ar - 1.0) * li) / den

    x = x_ref[0]
    bu = _dot3w(x, wbh_ref[0], wbl_ref[0])
    cr, ci = cf[0:1, :], cf[1:2, :]
    br, bi = _cmul(cr, ci, bu[:, :S5_STATES], bu[:, S5_STATES:])
    pr, pi = pw_re[...], pw_im[...]
    hr, hi = hc[0:1, :], hc[1:2, :]
    xrs, xis = [], []
    for i in range(tt // R):
        vr, vi = br[i * R:(i + 1) * R, :], bi[i * R:(i + 1) * R, :]
        for m, sh in enumerate(S5_SHIFTS):
            ur, ui = _cmul(sh_re[m], sh_im[m], pltpu.roll(vr, sh, axis=0), pltpu.roll(vi, sh, axis=0))
            vr, vi = vr + ur, vi + ui
        ur, ui = _cmul(pr, pi, hr, hi)
        vr, vi = vr + ur, vi + ui
        xrs.append(vr)
        xis.append(vi)
        hr, hi = vr[R - 1:R, :], vi[R - 1:R, :]
    hc[0:1, :] = hr
    hc[1:2, :] = hi
    xcat = jnp.concatenate([jnp.concatenate(xrs, axis=0), jnp.concatenate(xis, axis=0)], axis=-1)
    y = _dg(xcat.astype(BF16), wc_ref[0], _NN) + d_ref[...] * x
    z_ref[0] = 0.5 * y * (1.0 + jnp.tanh(math.sqrt(2.0 / math.pi) * (y + 0.044715 * (y * y * y))))

    @pl.when(t == n_t - 1)
    def _():
        hr_ref[0] = hr
        hi_ref[0] = hi


def _s5(name, x3, wb, wc, lr, li, dt, d, h0r, h0i):
    B, T, D = x3.shape
    tt = min(256, T)
    assert T % tt == 0 and tt % SUBLANES == 0
    n_t = T // tt
    n_s = D // S5_SLAB
    S = S5_STATES
    st = pl.BlockSpec((1, 1, S), lambda s, b, t: (b, 0, s))
    pv = pl.BlockSpec((1, 1, S), lambda s, b, t: (0, 0, s))
    w_in = pl.BlockSpec((1, S5_SLAB, 2 * S), lambda s, b, t: (s, 0, 0))
    w_out = pl.BlockSpec((1, 2 * S, S5_SLAB), lambda s, b, t: (s, 0, 0))
    split = lambda w: (w.astype(BF16), (w - w.astype(BF16).astype(F32)).astype(BF16))
    return pl.pallas_call(
        functools.partial(_s5_kernel, n_t=n_t, tt=tt), grid=(n_s, B, n_t),
        in_specs=[pl.BlockSpec((1, tt, S5_SLAB), lambda s, b, t: (b, t, s)),
                  w_in, w_in, w_out,
                  pv, pv, pv,
                  pl.BlockSpec((1, S5_SLAB), lambda s, b, t: (0, s)),
                  st, st],
        out_specs=[pl.BlockSpec((1, tt, S5_SLAB), lambda s, b, t: (b, t, s)), st, st],
        out_shape=[jax.ShapeDtypeStruct((B, T, D), F32),
                   jax.ShapeDtypeStruct((B, 1, n_s * S), F32), jax.ShapeDtypeStruct((B, 1, n_s * S), F32)],
        scratch_shapes=[pltpu.VMEM((SUBLANES, S), F32), pltpu.VMEM((SUBLANES, S), F32),
                        pltpu.VMEM((len(S5_SHIFTS), SUBLANES, S), F32),
                        pltpu.VMEM((len(S5_SHIFTS), SUBLANES, S), F32),
                        pltpu.VMEM((SUBLANES, S), F32), pltpu.VMEM((SUBLANES, S), F32)],
        compiler_params=_params(3), name=name,
    )(x3, *split(wb), wc.astype(BF16), lr, li, dt, d, h0r, h0i)


def _select_topk(gate, idx, valid, axis):
    g = jnp.where(valid, gate, NEG)
    big = jnp.int32(1 << 30)
    sel = jnp.zeros(gate.shape, jnp.bool_)
    for _ in range(MOBA_TOPK):
        m = jnp.max(g, axis=axis, keepdims=True)
        first = jnp.min(jnp.where(g == m, idx, big), axis=axis, keepdims=True)
        pick = idx == first
        sel = jnp.logical_or(sel, pick)
        g = jnp.where(pick, -jnp.inf, g)
    return jnp.logical_and(sel, valid)


def _moba_prompt_kernel(q_ref, k_ref, v_ref, sl_ref, o_ref, kmean_scr, *, n_blk):
    i = pl.program_id(2)
    BK = MOBA_BLOCK

    @pl.when(i == 0)
    def _():
        kmean_scr[...] = jnp.concatenate([jnp.sum(k_ref[0, n * BK:(n + 1) * BK, :], axis=0, keepdims=True)
                                          for n in range(n_blk)], axis=0) * (1.0 / BK)

    q = q_ref[0]
    slope = sl_ref[0][:, 0:1]
    qi = lax.broadcasted_iota(jnp.int32, (BK, BK), 0)
    ki = lax.broadcasted_iota(jnp.int32, (BK, BK), 1)
    causal = qi >= ki
    bias0 = slope * (qi - ki).astype(F32)
    qb = q.astype(BF16)
    scale = MOBA_DH ** -0.5

    def tile(c):
        ss = []
        if c > 0:
            gate_t = _dot3(kmean_scr[...], q, _NT)
            nidx = lax.broadcasted_iota(jnp.int32, (n_blk, BK), 0)
            sel_t = jnp.where(_select_topk(gate_t, nidx, nidx < c, axis=0), 1.0, 0.0).astype(BF16)
            sel = _dg(jnp.where(qi == ki, 1.0, 0.0).astype(BF16), sel_t, _NT) > 0.5
        for n in range(c + 1):
            s = _dg(qb, k_ref[0, n * BK:(n + 1) * BK, :].astype(BF16), _NT) * scale
            s = s - (bias0 + slope * float((c - n) * BK))
            ss.append(jnp.where(causal if n == c else sel[:, n:n + 1], s, NEG))
        m = jnp.max(functools.reduce(jnp.maximum, ss), axis=-1, keepdims=True)
        ps = [jnp.exp(s - m) for s in ss]
        l = jnp.sum(functools.reduce(lambda x, y: x + y, ps), axis=-1, keepdims=True)
        acc = functools.reduce(lambda x, y: x + y,
                               [_dg(p.astype(BF16), v_ref[0, n * BK:(n + 1) * BK, :].astype(BF16), _NN)
                                for n, p in enumerate(ps)])
        o_ref[0] = acc / l

    for c in range(n_blk):
        pl.when(i == c)(functools.partial(tile, c))


def _moba_prompt(name, q, k, v, slopes):
    B, T, D = q.shape
    n_blk = T // MOBA_BLOCK
    assert T % MOBA_BLOCK == 0
    qs = pl.BlockSpec((1, MOBA_BLOCK, MOBA_DH), lambda b, h, i: (b, i, h))
    kv = pl.BlockSpec((1, T, MOBA_DH), lambda b, h, i: (b, 0, h))
    return pl.pallas_call(
        functools.partial(_moba_prompt_kernel, n_blk=n_blk), grid=(B, MOBA_H, n_blk),
        in_specs=[qs, kv, kv, pl.BlockSpec((1, 1, LANES), lambda b, h, i: (h, 0, 0))], out_specs=qs,
        out_shape=jax.ShapeDtypeStruct((B, T, D), F32),
        scratch_shapes=[pltpu.VMEM((n_blk, MOBA_DH), F32)],
        compiler_params=_params(3), name=name,
    )(q, k, v, slopes)


def _q_rows(q, nq):
    return jnp.concatenate([q[:, h * MOBA_DH:(h + 1) * MOBA_DH] for h in range(MOBA_H)], axis=0)


def _head_match(n_rows, nq):
    shape = (n_rows * MOBA_H, MOBA_H * nq)
    return lax.broadcasted_iota(jnp.int32, shape, 0) % MOBA_H == lax.broadcasted_iota(jnp.int32, shape, 1) // nq


def _token_scores(k3, qr, nq, dot):
    n_rows = k3.shape[0]
    s2 = dot(k3.reshape(n_rows * MOBA_H, MOBA_DH), qr, _NT)
    s2 = jnp.where(_head_match(n_rows, nq), s2, 0.0)
    return jnp.sum(s2.reshape(n_rows, MOBA_H, MOBA_H * nq), axis=1)


def _moba_scores_kernel(pt_ref, q_ref, m_ref, *refs, nq, pp):
    k_refs, (s_ref, ks_ref) = refs[:pp], refs[pp:]
    qr = _q_rows(q_ref[0], nq).astype(BF16)
    kps = [k_ref[0, 0] for k_ref in k_refs]
    s2s = [_dg(kp.reshape(PAGE_SIZE * MOBA_H, MOBA_DH).astype(BF16), qr, _NT) for kp in kps]
    for u in range(pp):
        s_ref[0, u] = jnp.sum((s2s[u] * m_ref[...]).reshape(PAGE_SIZE, MOBA_H, MOBA_H * nq), axis=1)
        ks_ref[0, u] = jnp.sum(kps[u], axis=0)


def _page_specs(layer, pp):
    return [pl.BlockSpec((1, 1, PAGE_SIZE, MOBA_H, MOBA_DH),
                         lambda b, g, pt, u=u: (layer, pt[b, g * pp + u], 0, 0, 0)) for u in range(pp)]


def _moba_scores(name, page_table, q, cache_k, layer, match_f):
    B, nq, D = q.shape
    n_pg = page_table.shape[1]
    pp = MOBA_PAGES_PER_STEP
    assert n_pg % pp == 0
    gs = pltpu.PrefetchScalarGridSpec(
        num_scalar_prefetch=1, grid=(B, n_pg // pp),
        in_specs=[pl.BlockSpec((1, nq, D), lambda b, g, pt: (b, 0, 0)),
                  pl.BlockSpec(match_f.shape, lambda b, g, pt: (0, 0))] + _page_specs(layer, pp),
        out_specs=[pl.BlockSpec((1, pp, PAGE_SIZE, MOBA_H * nq), lambda b, g, pt: (b, g, 0, 0)),
                   pl.BlockSpec((1, pp, MOBA_H, MOBA_DH), lambda b, g, pt: (b, g, 0, 0))])
    return pl.pallas_call(
        functools.partial(_moba_scores_kernel, nq=nq, pp=pp), grid_spec=gs,
        out_shape=[jax.ShapeDtypeStruct((B, n_pg, PAGE_SIZE, MOBA_H * nq), F32),
                   jax.ShapeDtypeStruct((B, n_pg, MOBA_H, MOBA_DH), F32)],
        compiler_params=_params(2), name=name,
    )(page_table, q, match_f, *([cache_k] * pp))


def _moba_probs_kernel(s_ref, ks_ref, q_ref, kn_ref, sl_ref, p_ref, pn_ref, sel_scr, *, nq, n_blk, past):
    BK = MOBA_BLOCK
    HQ = MOBA_H * nq
    qr = _q_rows(q_ref[0], nq)
    ppb = BK // PAGE_SIZE
    kmean = jnp.sum(ks_ref[0].reshape(n_blk, ppb, MOBA_H, MOBA_DH), axis=1) * (1.0 / BK)
    gate = _token_scores(kmean, qr, nq, _dot3)
    nidx = lax.broadcasted_iota(jnp.int32, (n_blk, HQ), 0)
    own = past // BK
    sel = _select_topk(gate, nidx, nidx < own, axis=0)
    sel_scr[...] = sel.astype(F32)
    slope = sl_ref[...]
    scale = MOBA_DH ** -0.5
    qpos = past + lax.broadcasted_iota(jnp.int32, (1, HQ), 1) % nq
    tok = lax.broadcasted_iota(jnp.int32, (BK, HQ), 0)

    def scores(n):
        s = s_ref[0, n] * scale - slope * (qpos - (n * BK + tok)).astype(F32)
        return jnp.where(sel_scr[pl.ds(n, 1), :] > 0.5, s, NEG)

    sn = _token_scores(kn_ref[0], qr, nq, _dot1) * scale
    tn = lax.broadcasted_iota(jnp.int32, (PAGE_SIZE, HQ), 0)
    dn = qpos - (past + tn)
    sn = jnp.where(jnp.logical_and(dn >= 0, tn < nq), sn - slope * dn.astype(F32), NEG)
    m = lax.fori_loop(0, n_blk, lambda n, m_: jnp.maximum(m_, jnp.max(scores(n), axis=0, keepdims=True)),
                      jnp.max(sn, axis=0, keepdims=True))
    pn = jnp.exp(sn - m)
    l = lax.fori_loop(0, n_blk, lambda n, l_: l_ + jnp.sum(jnp.exp(scores(n) - m), axis=0, keepdims=True),
                      jnp.sum(pn, axis=0, keepdims=True))
    pn_ref[0] = (pn / l).astype(pn_ref.dtype)

    def write(n, c):
        p_ref[0, n] = (jnp.exp(scores(n) - m) / l).astype(p_ref.dtype)
        return c

    lax.fori_loop(0, n_blk, write, 0)


def _moba_probs(name, s, ksum, q, k_new_pad, slopes_hq, past):
    B, n_blk, BK, HQ = s.shape
    nq = q.shape[1]
    blk = pl.BlockSpec((1, n_blk, BK, HQ), lambda b: (b, 0, 0, 0))
    return pl.pallas_call(
        functools.partial(_moba_probs_kernel, nq=nq, n_blk=n_blk, past=past), grid=(B,),
        in_specs=[blk,
                  pl.BlockSpec((1, ksum.shape[1], MOBA_H, MOBA_DH), lambda b: (b, 0, 0, 0)),
                  pl.BlockSpec((1, nq, D_MODEL), lambda b: (b, 0, 0)),
                  pl.BlockSpec((1, PAGE_SIZE, MOBA_H, MOBA_DH), lambda b: (b, 0, 0, 0)),
                  pl.BlockSpec((1, HQ), lambda b: (0, 0))],
        out_specs=[blk, pl.BlockSpec((1, PAGE_SIZE, HQ), lambda b: (b, 0, 0))],
        out_shape=[jax.ShapeDtypeStruct((B, n_blk, BK, HQ), BF16), jax.ShapeDtypeStruct((B, PAGE_SIZE, HQ), BF16)],
        scratch_shapes=[pltpu.VMEM((n_blk, HQ), F32)],
        compiler_params=_params(1), name=name,
    )(s, ksum, q, k_new_pad, slopes_hq)


def _moba_pv_kernel(pt_ref, p_ref, m_ref, pn_ref, vn_ref, *refs, nq, n_steps, pp):
    v_refs, (o_ref, acc) = refs[:pp], refs[pp:]
    g = pl.program_id(1)
    HQ = MOBA_H * nq

    def spread(p):
        p2 = jnp.broadcast_to(p.astype(F32)[:, None, :], (PAGE_SIZE, MOBA_H, HQ))
        return (p2.reshape(PAGE_SIZE * MOBA_H, HQ) * m_ref[...]).astype(BF16)

    def values(v3):
        return v3.reshape(PAGE_SIZE * MOBA_H, MOBA_DH).astype(BF16)

    @pl.when(g == 0)
    def _():
        acc[...] = _dg(spread(pn_ref[0]), values(vn_ref[0]), _TN)

    p2s = [spread(p_ref[0, u]) for u in range(pp)]
    v2s = [values(v_ref[0, 0]) for v_ref in v_refs]
    acc[...] += functools.reduce(lambda x, y: x + y, [_dg(p2, v2, _TN) for p2, v2 in zip(p2s, v2s)])

    @pl.when(g == n_steps - 1)
    def _():
        a = acc[...]
        o_ref[0] = jnp.concatenate([a[h * nq:(h + 1) * nq, :] for h in range(MOBA_H)], axis=-1)


def _moba_pv(name, page_table, p, cache_v, pn, v_new_pad, nq, layer, match_f):
    B, n_pg, _, HQ = p.shape
    D = D_MODEL
    pp = MOBA_PAGES_PER_STEP
    assert n_pg % pp == 0
    gs = pltpu.PrefetchScalarGridSpec(
        num_scalar_prefetch=1, grid=(B, n_pg // pp),
        in_specs=[pl.BlockSpec((1, pp, PAGE_SIZE, HQ), lambda b, g, pt: (b, g, 0, 0)),
                  pl.BlockSpec(match_f.shape, lambda b, g, pt: (0, 0)),
                  pl.BlockSpec((1, PAGE_SIZE, HQ), lambda b, g, pt: (b, 0, 0)),
                  pl.BlockSpec((1, PAGE_SIZE, MOBA_H, MOBA_DH), lambda b, g, pt: (b, 0, 0, 0))]
        + _page_specs(layer, pp),
        out_specs=pl.BlockSpec((1, nq, D), lambda b, g, pt: (b, 0, 0)),
        scratch_shapes=[pltpu.VMEM((HQ, MOBA_DH), F32)])
    return pl.pallas_call(
        functools.partial(_moba_pv_kernel, nq=nq, n_steps=n_pg // pp, pp=pp), grid_spec=gs,
        out_shape=jax.ShapeDtypeStruct((B, nq, D), F32), compiler_params=_params(2), name=name,
    )(page_table, p, match_f, pn, v_new_pad, *([cache_v] * pp))


def _ident(x):
    return x


def _mix(x, xp, mu):
    return x + (xp - x) * mu


def _pad_cols(w, n):
    return jnp.pad(w, ((0, 0), (0, n - w.shape[1])))


def _pad_rows(w, n):
    return jnp.pad(w, ((0, n - w.shape[0]), (0, 0)))


def _rwkv_layer(tag, x3, shift0, s0, v_first, P, j, ln_g, ln_b):
    B, T, D = x3.shape
    M = B * T
    x = x3.reshape(M, D)
    wb = lambda w: w.astype(BF16)
    lp = LANES
    w_rkv = P['rwkv_w_rkv'][j]
    w_main = jnp.concatenate([wb(w_rkv[0]), wb(w_rkv[1]), wb(w_rkv[2])], axis=1)
    lora1 = [wb(_pad_cols(P['rwkv_w1'][j], lp)), wb(_pad_cols(P['rwkv_a1'][j], lp)), wb(P['rwkv_g1'][j])]
    lora2 = [wb(_pad_rows(P['rwkv_w2'][j], lp)), wb(_pad_rows(P['rwkv_a2'][j], lp)), wb(P['rwkv_g2'][j])]
    bias = [P['rwkv_w0'][j][None, :], P['rwkv_a0'][j][None, :]]
    vgate = None
    if j > 0:
        vgate = (wb(_pad_cols(P['rwkv_v1'][j - 1], lp)), wb(_pad_rows(P['rwkv_v2'][j - 1], lp)),
                 P['rwkv_v0'][j - 1][None, :], v_first)
    r, k, v, lw, a, g = _rwkv_proj(tag + 'proj', x3, shift0, P['rwkv_mu'][j], w_main, lora1, lora2, bias, vgate)
    if j == 0:
        v_first = v
    Tp = -(-T // WKV_CHUNK) * WKV_CHUNK
    seq = lambda t: jnp.pad(t.reshape(B, T, D), ((0, 0), (0, Tp - T), (0, 0)))
    vecp = lambda name: P[name][j].reshape(1, D)
    z, s_new = _wkv(tag + 'wkv', seq(r), seq(lw), seq(k), seq(v), seq(a), vecp('rwkv_k_k'), vecp('rwkv_k_a'),
                    vecp('rwkv_r_k'), vecp('rwkv_lnx_g'), vecp('rwkv_lnx_b'), s0)
    z = z[:, :T].reshape(M, D)
    xn = _mm_ln(tag + 'wo', [z, g], wb(P['rwkv_wo'][j]), x, ln_g, ln_b, lambda z_, g_: z_ * g_)
    return xn, s_new, x3[:, -1], v_first


def _s5_weights(P, j):
    ns, gs = D_MODEL // S5_SLAB, S5_SLAB // S5_GROUP
    eye = jnp.eye(gs, dtype=F32)

    def bd_in(b):
        bt = jnp.swapaxes(b.reshape(ns, gs, S5_P, S5_GROUP), 2, 3)
        return jnp.einsum('sgcp,gh->sgchp', bt, eye).reshape(ns, S5_SLAB, S5_STATES)

    def bd_out(c):
        ct = jnp.swapaxes(c.reshape(ns, gs, S5_GROUP, S5_P), 2, 3)
        return jnp.einsum('sgpc,gh->sgphc', ct, eye).reshape(ns, S5_STATES, S5_SLAB)

    wb = jnp.concatenate([bd_in(P['s5_b_re'][j]), bd_in(P['s5_b_im'][j])], axis=2)
    wc = jnp.concatenate([bd_out(P['s5_c_re'][j]), -bd_out(P['s5_c_im'][j])], axis=1)
    flat = lambda t: t.reshape(1, 1, S5_G * S5_P)
    dt = jnp.broadcast_to(jnp.exp(P['s5_log_dt'][j])[:, None], (S5_G, S5_P))
    return wb, wc, flat(P['s5_lam_re'][j]), flat(P['s5_lam_im'][j]), flat(dt)


def _s5_layer(tag, x3, h0r, h0i, P, j, ln_g, ln_b):
    B, T, D = x3.shape
    M = B * T
    x = x3.reshape(M, D)
    wb, wc, lr, li, dt = _s5_weights(P, j)
    st = lambda h: h.reshape(B, 1, S5_G * S5_P)
    z, hr, hi = _s5(tag + 'scan', x3, wb, wc, lr, li, dt, P['s5_d'][j][None, :], st(h0r), st(h0i))
    out, = _mm(tag + 'gate', [z.reshape(M, D)], [], [P['s5_w_val'][j].astype(BF16), P['s5_w_gate'][j].astype(BF16)],
               [], [], _ident, lambda val, gate: (val * _sigmoid(gate),), [F32])
    xn = _ln(tag + 'ln', x, out, ln_g, ln_b)
    return xn, hr.reshape(B, S5_G, S5_P), hi.reshape(B, S5_G, S5_P)


def _alibi_slopes():
    return 2.0 ** (-8.0 * jnp.arange(1, MOBA_H + 1, dtype=F32) / MOBA_H)


def _moba_layer(tag, x3, cache_k, cache_v, page_table, P, j, ln_g, ln_b):
    B, T, D = x3.shape
    M = B * T
    x = x3.reshape(M, D)
    wqkv = P['moba_w_qkv'][j]
    ws = [wqkv[:, n * D:(n + 1) * D].astype(BF16) for n in range(3)]
    q, k, v = _mm(tag + 'qkv', [x], [], ws, [], [], _ident, lambda a, b, c: (a, b, c), [F32, F32, F32])
    q3, k3, v3 = (t.reshape(B, T, D) for t in (q, k, v))
    slopes = _alibi_slopes()
    if cache_k is None:
        sl = jnp.broadcast_to(slopes[:, None, None], (MOBA_H, 1, LANES))
        o = _moba_prompt(tag + 'attn', q3, k3, v3, sl)
    else:
        n_pg = page_table.shape[1]
        past = n_pg * PAGE_SIZE
        assert past % MOBA_BLOCK == 0 and T <= PAGE_SIZE and T <= MOBA_BLOCK
        HQ = MOBA_H * T
        match_f = _head_match(PAGE_SIZE, T).astype(F32)
        s, ksum = _moba_scores(tag + 'scores', page_table, q3, cache_k, j, match_f)
        sl = jnp.repeat(slopes, T)[None, :]
        pad = lambda t: jnp.pad(t.reshape(B, T, MOBA_H, MOBA_DH), ((0, 0), (0, PAGE_SIZE - T), (0, 0), (0, 0)))
        p, pn = _moba_probs(tag + 'probs', s.reshape(B, past // MOBA_BLOCK, MOBA_BLOCK, HQ), ksum, q3, pad(k3),
                            sl, past)
        o = _moba_pv(tag + 'pv', page_table, p.reshape(B, n_pg, PAGE_SIZE, HQ), cache_v, pn, pad(v3), T, j, match_f)
    xn = _mm_ln(tag + 'wo', [o.reshape(M, D)], P['moba_wo'][j].astype(BF16), x, ln_g, ln_b, _ident)
    return xn, k3.reshape(B, T, MOBA_H, MOBA_DH), v3.reshape(B, T, MOBA_H, MOBA_DH)


def _ffn_layer(tag, x3, conv0, P, i, ln_g, ln_b):
    B, T, D = x3.shape
    w_in = P['ffn_w_in'][i]
    h, cst = _ffn_in(tag + 'in', x3, w_in[:, :D_FF].astype(BF16), w_in[:, D_FF:].astype(BF16), conv0,
                     P['ffn_conv_w'][i], P['ffn_conv_b'][i][None, :])
    xn = _mm_ln(tag + 'down', [h], P['ffn_w_down'][i].astype(BF16), x3.reshape(B * T, D), ln_g, ln_b, _ident)
    return xn.reshape(B, T, D), cst


def _trunk(grp, x3, wkv0, shift0, s5re0, s5im0, conv0, cache_k, cache_v, page_table, P):
    B, T, D = x3.shape
    wkv_new, shift_new, s5re_new, s5im_new, k_new, v_new, conv_new = [], [], [], [], [], [], []
    v_first = None
    for i in range(DEPTH):
        kind, j = i % 3, i // 3
        tag = f'{grp}{i}_'
        g0, b0 = P['ln_g'][i, 0][None, :], P['ln_b'][i, 0][None, :]
        g1, b1 = P['ln_g'][i, 1][None, :], P['ln_b'][i, 1][None, :]
        if kind == 0:
            xn, S, last, v_first = _rwkv_layer(tag, x3, shift0[j], wkv0[j], v_first, P, j, g0, b0)
            wkv_new.append(S)
            shift_new.append(last)
        elif kind == 1:
            xn, hr, hi = _s5_layer(tag, x3, s5re0[j], s5im0[j], P, j, g0, b0)
            s5re_new.append(hr)
            s5im_new.append(hi)
        else:
            xn, kr, vr = _moba_layer(tag, x3, cache_k, cache_v, page_table, P, j, g0, b0)
            k_new.append(kr)
            v_new.append(vr)
        x3, cst = _ffn_layer(tag + 'ffn_', xn.reshape(B, T, D), conv0[i], P, i, g1, b1)
        conv_new.append(cst)
    return (x3, jnp.stack(wkv_new), jnp.stack(shift_new), jnp.stack(s5re_new), jnp.stack(s5im_new),
            jnp.stack(conv_new), jnp.stack(k_new), jnp.stack(v_new))


def kernel(x_prompt, x_sample, state_rwkv_wkv, state_rwkv_shift, state_s5_re, state_s5_im, state_ffn_conv, cache_k, cache_v, page_table, ln_g, ln_b, rwkv_mu, rwkv_w_rkv, rwkv_w0, rwkv_w1, rwkv_w2, rwkv_a0, rwkv_a1, rwkv_a2, rwkv_v0, rwkv_v1, rwkv_v2, rwkv_g1, rwkv_g2, rwkv_k_k, rwkv_k_a, rwkv_r_k, rwkv_lnx_g, rwkv_lnx_b, rwkv_wo, s5_log_dt, s5_lam_re, s5_lam_im, s5_b_re, s5_b_im, s5_c_re, s5_c_im, s5_d, s5_w_val, s5_w_gate, moba_w_qkv, moba_wo, ffn_w_in, ffn_conv_w, ffn_conv_b, ffn_w_down):
    P = dict(ln_g=ln_g, ln_b=ln_b, rwkv_mu=rwkv_mu, rwkv_w_rkv=rwkv_w_rkv, rwkv_w0=rwkv_w0, rwkv_w1=rwkv_w1,
             rwkv_w2=rwkv_w2, rwkv_a0=rwkv_a0, rwkv_a1=rwkv_a1, rwkv_a2=rwkv_a2, rwkv_v0=rwkv_v0,
             rwkv_v1=rwkv_v1, rwkv_v2=rwkv_v2, rwkv_g1=rwkv_g1, rwkv_g2=rwkv_g2, rwkv_k_k=rwkv_k_k,
             rwkv_k_a=rwkv_k_a, rwkv_r_k=rwkv_r_k.reshape(rwkv_r_k.shape[0], D_MODEL), rwkv_lnx_g=rwkv_lnx_g,
             rwkv_lnx_b=rwkv_lnx_b, rwkv_wo=rwkv_wo, s5_log_dt=s5_log_dt, s5_lam_re=s5_lam_re,
             s5_lam_im=s5_lam_im, s5_b_re=s5_b_re, s5_b_im=s5_b_im, s5_c_re=s5_c_re, s5_c_im=s5_c_im, s5_d=s5_d,
             s5_w_val=s5_w_val, s5_w_gate=s5_w_gate, moba_w_qkv=moba_w_qkv, moba_wo=moba_wo,
             ffn_w_in=ffn_w_in, ffn_conv_w=ffn_conv_w, ffn_conv_b=ffn_conv_b, ffn_w_down=ffn_w_down)
    B = x_prompt.shape[0]
    n_rwkv, n_s5 = state_rwkv_wkv.shape[0], state_s5_re.shape[0]
    zeros = lambda *s: jnp.zeros(s, F32)
    outs_p = _trunk('p', x_prompt, zeros(n_rwkv, B, RWKV_H, RWKV_N, RWKV_N), zeros(n_rwkv, B, D_MODEL),
                    zeros(n_s5, B, S5_G, S5_P), zeros(n_s5, B, S5_G, S5_P), zeros(DEPTH, B, CONV_W - 1, D_FF),
                    None, None, None, P)
    outs_s = _trunk('s', x_sample, state_rwkv_wkv, state_rwkv_shift, state_s5_re, state_s5_im, state_ffn_conv,
                    cache_k, cache_v, page_table, P)
    return (outs_p[0], outs_s[0]) + tuple(outs_p[1:]) + tuple(outs_s[1:])
```

```python
import functools
import math

import jax
import jax.numpy as jnp
from jax import lax
from jax.experimental import pallas as pl
from jax.experimental.pallas import tpu as pltpu

F32 = jnp.float32
BF16 = jnp.bfloat16

D_MODEL = 2048
DEPTH = 4
RWKV_N = 64
RWKV_H = D_MODEL // RWKV_N
RWKV_GN_EPS = 64e-5
S5_GROUP = 16
S5_G = D_MODEL // S5_GROUP
S5_P = 64
MOBA_H = 16
MOBA_DH = D_MODEL // MOBA_H
MOBA_BLOCK = 256
MOBA_TOPK = 3
PAGE_SIZE = 128
D_FF = 11 * D_MODEL // 4
CONV_W = 3
LN_EPS = 1e-5
DN_ALPHA = (2.0 * DEPTH) ** 0.25
NEG = -1e30

LANES = 128
SUBLANES = 8
VMEM_LIMIT_BYTES = 56 * 1024 * 1024

MM_ROW_TILE = 1024
MM_LN_RESIDENT_BYTES = 24 * 1024 * 1024
MM_LN_WEIGHT_TILE_BYTES = 12 * 1024 * 1024
WKV_CHUNK = 64
WKV_CHAINS = 16
WKV_HEADS = 4
MOBA_PAGES_PER_STEP = 8
S5_SLAB = LANES
S5_STATES = (S5_SLAB // S5_GROUP) * S5_P


def _params(n_axes):
    return pltpu.CompilerParams(dimension_semantics=("arbitrary",) * n_axes,
                                vmem_limit_bytes=VMEM_LIMIT_BYTES)


def _sigmoid(x):
    return 1.0 / (1.0 + jnp.exp(-x))


def _softplus(x):
    return jnp.maximum(x, 0.0) + jnp.log1p(jnp.exp(-jnp.abs(x)))


def _dg(a, b, dims):
    return lax.dot_general(a, b, (dims, ((), ())), preferred_element_type=F32)


_NN = ((1,), (0,))
_NT = ((1,), (1,))
_TN = ((0,), (0,))


def _split(a):
    hi = a.astype(BF16)
    lo = (a - hi.astype(F32)).astype(BF16)
    return hi, lo


def _dot3(a, b, dims=_NN):
    ah, al = _split(a)
    bh, bl = _split(b)
    return _dg(ah, bh, dims) + (_dg(al, bh, dims) + _dg(ah, bl, dims))


def _dot1(a, b, dims=_NN):
    return _dg(a.astype(BF16), b.astype(BF16), dims)


def _dot_exact_lhs(lhs_bf16, x):
    x1 = x.astype(BF16)
    r1 = x - x1.astype(F32)
    x2 = r1.astype(BF16)
    x3 = (r1 - x2.astype(F32)).astype(BF16)
    return _dg(lhs_bf16, x1, _NN) + (_dg(lhs_bf16, x2, _NN) + _dg(lhs_bf16, x3, _NN))


def _layer_norm(y, g, b):
    mu = jnp.mean(y, axis=-1, keepdims=True)
    yc = y - mu
    var = jnp.mean(yc * yc, axis=-1, keepdims=True)
    return yc * lax.rsqrt(var + LN_EPS) * g + b


def _mm_kernel(*refs, n_row, n_vec, n_w, n_erow, n_evec, n_out, prologue, epilogue):
    it = iter(refs)
    rows = [next(it) for _ in range(n_row)]
    vecs = [next(it) for _ in range(n_vec)]
    ws = [next(it) for _ in range(n_w)]
    erows = [next(it) for _ in range(n_erow)]
    evecs = [next(it) for _ in range(n_evec)]
    outs = [next(it) for _ in range(n_out)]
    a_scr = next(it)

    @pl.when(pl.program_id(1) == 0)
    def _():
        a_scr[...] = prologue(*[r[...] for r in rows], *[v[...] for v in vecs]).astype(BF16)

    a = a_scr[...]
    accs = [jnp.dot(a, w[...], preferred_element_type=F32) for w in ws]
    res = epilogue(*accs, *[e[...] for e in erows], *[e[...] for e in evecs])
    for o_ref, o in zip(outs, res):
        o_ref[...] = o.astype(o_ref.dtype)


def _mm(name, rows, vecs, ws, erows, evecs, prologue, epilogue, out_dtypes):
    M, K = rows[0].shape
    N = ws[0].shape[1]
    tm = MM_ROW_TILE if (M % MM_ROW_TILE == 0 and len(rows) == 1) else min(512, M)
    tn = 512 if N % 512 == 0 else N
    assert M % tm == 0 and N % tn == 0
    in_specs = ([pl.BlockSpec((tm, K), lambda i, j: (i, 0)) for _ in rows]
                + [pl.BlockSpec((1, K), lambda i, j: (0, 0)) for _ in vecs]
                + [pl.BlockSpec((K, tn), lambda i, j: (0, j)) for _ in ws]
                + [pl.BlockSpec((tm, tn), lambda i, j: (i, j)) for _ in erows]
                + [pl.BlockSpec((1, tn), lambda i, j: (0, j)) for _ in evecs])
    out_specs = [pl.BlockSpec((tm, tn), lambda i, j: (i, j)) for _ in out_dtypes]
    out_shape = [jax.ShapeDtypeStruct((M, N), dt) for dt in out_dtypes]
    kern = functools.partial(_mm_kernel, n_row=len(rows), n_vec=len(vecs), n_w=len(ws), n_erow=len(erows),
                             n_evec=len(evecs), n_out=len(out_dtypes), prologue=prologue, epilogue=epilogue)
    return pl.pallas_call(
        kern, grid=(M // tm, N // tn), in_specs=in_specs, out_specs=out_specs, out_shape=out_shape,
        scratch_shapes=[pltpu.VMEM((tm, K), BF16)], compiler_params=_params(2), name=name,
    )(*rows, *vecs, *ws, *erows, *evecs)


def _mm_ln_kernel(*refs, n_row, prologue, n_k):
    it = iter(refs)
    rows = [next(it) for _ in range(n_row)]
    w_ref, x_ref, g_ref, b_ref, o_ref, acc = (next(it) for _ in range(6))
    k = pl.program_id(1)
    a = prologue(*[r[...] for r in rows]).astype(BF16)
    p = jnp.dot(a, w_ref[...], preferred_element_type=F32)
    if n_k == 1:
        o_ref[...] = _layer_norm(DN_ALPHA * x_ref[...] + p, g_ref[...], b_ref[...])
        return

    @pl.when(k == 0)
    def _():
        acc[...] = p

    @pl.when(jnp.logical_and(k > 0, k < n_k - 1))
    def _():
        acc[...] += p

    @pl.when(k == n_k - 1)
    def _():
        o_ref[...] = _layer_norm(DN_ALPHA * x_ref[...] + (acc[...] + p), g_ref[...], b_ref[...])


def _mm_ln(name, rows, w, x, g, b, prologue):
    M, K = rows[0].shape
    N = w.shape[1]
    w_bytes = K * N * 2
    if w_bytes <= MM_LN_WEIGHT_TILE_BYTES:
        n_k, tm, w_mode = 1, min(512, M), None
    elif w_bytes <= MM_LN_RESIDENT_BYTES:
        n_k, tm, w_mode = 1, min(256, M), pl.Buffered(1)
    else:
        n_k, tm, w_mode = -(-w_bytes // MM_LN_WEIGHT_TILE_BYTES), min(512, M), None
    tk = K // n_k
    assert M % tm == 0 and K % n_k == 0 and tk % LANES == 0
    in_specs = ([pl.BlockSpec((tm, tk), lambda i, k: (i, k)) for _ in rows]
                + [pl.BlockSpec((tk, N), lambda i, k: (k, 0), pipeline_mode=w_mode),
                   pl.BlockSpec((tm, N), lambda i, k: (i, 0)),
                   pl.BlockSpec((1, N), lambda i, k: (0, 0)),
                   pl.BlockSpec((1, N), lambda i, k: (0, 0))])
    kern = functools.partial(_mm_ln_kernel, n_row=len(rows), prologue=prologue, n_k=n_k)
    return pl.pallas_call(
        kern, grid=(M // tm, n_k), in_specs=in_specs,
        out_specs=pl.BlockSpec((tm, N), lambda i, k: (i, 0)),
        out_shape=jax.ShapeDtypeStruct((M, N), F32),
        scratch_shapes=[pltpu.VMEM((tm, N), F32)], compiler_params=_params(2), name=name,
    )(*rows, w, x, g, b)


def _ln_kernel(x_ref, h_ref, g_ref, b_ref, o_ref):
    o_ref[...] = _layer_norm(DN_ALPHA * x_ref[...] + h_ref[...], g_ref[...], b_ref[...])


def _ln(name, x, h, g, b):
    M, N = x.shape
    tm = min(512, M)
    row = pl.BlockSpec((tm, N), lambda i: (i, 0))
    vec = pl.BlockSpec((1, N), lambda i: (0, 0))
    return pl.pallas_call(_ln_kernel, grid=(M // tm,), in_specs=[row, row, vec, vec], out_specs=row,
                          out_shape=jax.ShapeDtypeStruct((M, N), F32), compiler_params=_params(1), name=name)(x, h, g, b)


def _ffn_in_kernel(x_ref, wg_ref, wv_ref, c0_ref, cw_ref, cb_ref, h_ref, cs_ref, a_scr, carry, *, bt, tt):
    t = pl.program_id(1)
    j = pl.program_id(2)
    rows = bt * tt
    tn = wg_ref.shape[1]

    @pl.when(j == 0)
    def _():
        a_scr[...] = x_ref[...].reshape(rows, x_ref.shape[2]).astype(BF16)

    @pl.when(t == 0)
    def _():
        carry[j] = c0_ref[...]

    a = a_scr[...]
    hg = jnp.dot(a, wg_ref[...], preferred_element_type=F32)
    hv = jnp.dot(a, wv_ref[...], preferred_element_type=F32)
    prev = carry[j]
    p2 = jnp.broadcast_to(prev[:, 0:1, :], (bt, tt, tn)).reshape(rows, tn)
    p1 = jnp.broadcast_to(prev[:, 1:2, :], (bt, tt, tn)).reshape(rows, tn)
    r = lax.broadcasted_iota(jnp.int32, (rows, tn), 0) % tt
    s1 = jnp.where(r == 0, p1, pltpu.roll(hg, 1, axis=0))
    s2 = jnp.where(r == 0, p2, jnp.where(r == 1, p1, pltpu.roll(hg, 2, axis=0)))
    cw = cw_ref[...]
    c = cb_ref[...] + cw[0:1, :] * s2 + cw[1:2, :] * s1 + cw[2:3, :] * hg
    h_ref[...] = (c * _sigmoid(c) * hv).astype(h_ref.dtype)
    last = hg.reshape(bt, tt, tn)[:, tt - 2:, :]
    carry[j] = last
    cs_ref[0] = last


def _ffn_in(name, x3, wg, wv, conv0, cw, cb):
    B, T, K = x3.shape
    F = wg.shape[1]
    bt, tt = (1, MM_ROW_TILE) if T % MM_ROW_TILE == 0 else (B, T)
    tn = 512
    assert B % bt == 0 and T % tt == 0 and F % tn == 0 and tt >= CONV_W - 1
    n_j = F // tn
    kern = functools.partial(_ffn_in_kernel, bt=bt, tt=tt)
    n_t = T // tt
    h, cst = pl.pallas_call(
        kern, grid=(B // bt, n_t, n_j),
        in_specs=[pl.BlockSpec((bt, tt, K), lambda b, t, j: (b, t, 0)),
                  pl.BlockSpec((K, tn), lambda b, t, j: (0, j)),
                  pl.BlockSpec((K, tn), lambda b, t, j: (0, j)),
                  pl.BlockSpec((bt, CONV_W - 1, tn), lambda b, t, j: (b, 0, j)),
                  pl.BlockSpec((CONV_W, tn), lambda b, t, j: (0, j)),
                  pl.BlockSpec((1, tn), lambda b, t, j: (0, j))],
        out_specs=[pl.BlockSpec((bt * tt, tn), lambda b, t, j: (b * n_t + t, j)),
                   pl.BlockSpec((1, bt, CONV_W - 1, tn), lambda b, t, j: (t, b, 0, j))],
        out_shape=[jax.ShapeDtypeStruct((B * T, F), BF16),
                   jax.ShapeDtypeStruct((n_t, B, CONV_W - 1, F), F32)],
        scratch_shapes=[pltpu.VMEM((bt * tt, K), BF16), pltpu.VMEM((n_j, bt, CONV_W - 1, tn), F32)],
        compiler_params=_params(3), name=name,
    )(x3, wg, wv, conv0, cw, cb)
    return h, cst[n_t - 1]


RWKV_PROJ_TN = 1024
RWKV_PROJ_TILES = D_MODEL // RWKV_PROJ_TN


def _rwkv_proj_kernel(*refs, has_vgate, seq_len, tiles_per_seq):
    it = iter(refs)
    x_ref, sh_ref, mu_ref, wm_ref, w1_ref, a1_ref, g1_ref, w2_ref, a2_ref, g2_ref, w0_ref, a0_ref = (
        next(it) for _ in range(12))
    if has_vgate:
        v1_ref, v2_ref, v0_ref, vf_ref = (next(it) for _ in range(4))
    r_ref, k_ref, v_ref, lw_ref, a_ref, g_ref = (next(it) for _ in range(6))
    mix_scr, hw_scr, ha_scr, hg_scr, last_scr = (next(it) for _ in range(5))
    if has_vgate:
        hv_scr = next(it)
    i = pl.program_id(0)
    j = pl.program_id(1)
    dot = lambda a, b: jnp.dot(a, b, preferred_element_type=F32)

    @pl.when(j == 0)
    def _():
        x = x_ref[...]
        tm, K = x.shape
        row = lax.broadcasted_iota(jnp.int32, (tm, K), 0)
        prev = pltpu.roll(x, 1, axis=0)
        if tiles_per_seq > 1:
            first = jnp.where(i % tiles_per_seq == 0, sh_ref[0], last_scr[0:1, :])
            prev = jnp.where(row == 0, first, prev)
            last_scr[0:1, :] = x[tm - 1:tm, :]
        else:
            n_seq = tm // seq_len
            sh = jnp.broadcast_to(sh_ref[...], (n_seq, seq_len, K)).reshape(tm, K)
            prev = jnp.where(row % seq_len == 0, sh, prev)
        d = prev - x
        mix = lambda n: (x + d * mu_ref[n:n + 1, :]).astype(BF16)
        mix_scr[0] = mix(0)
        mix_scr[1] = mix(2)
        xv = mix(3)
        mix_scr[2] = xv
        hw_scr[...] = jnp.tanh(dot(mix(1), w1_ref[...])).astype(BF16)
        ha_scr[...] = dot(mix(4), a1_ref[...]).astype(BF16)
        hg_scr[...] = _sigmoid(dot(mix(5), g1_ref[...])).astype(BF16)
        if has_vgate:
            hv_scr[...] = dot(xv, v1_ref[...]).astype(BF16)

    grp = j // RWKV_PROJ_TILES

    @pl.when(grp == 0)
    def _():
        r_ref[...] = dot(mix_scr[0], wm_ref[...]).astype(r_ref.dtype)
        lw_ref[...] = -jnp.exp(-_softplus(-(w0_ref[...] + dot(hw_scr[...], w2_ref[...]))) - 0.5)

    @pl.when(grp == 1)
    def _():
        k_ref[...] = dot(mix_scr[1], wm_ref[...]).astype(k_ref.dtype)
        a_ref[...] = _sigmoid(a0_ref[...] + dot(ha_scr[...], a2_ref[...])).astype(a_ref.dtype)

    @pl.when(grp == 2)
    def _():
        v = dot(mix_scr[2], wm_ref[...])
        if has_vgate:
            v = v + (vf_ref[...] - v) * _sigmoid(v0_ref[...] + dot(hv_scr[...], v2_ref[...]))
        v_ref[...] = v.astype(v_ref.dtype)
        g_ref[...] = dot(hg_scr[...], g2_ref[...]).astype(g_ref.dtype)


def _rwkv_proj(name, x3, shift0, mu, w_main, lora1, lora2, bias, vgate):
    B, T, K = x3.shape
    M = B * T
    tm, tn, nt = min(512, M), RWKV_PROJ_TN, RWKV_PROJ_TILES
    assert M % tm == 0 and w_main.shape == (K, 3 * D_MODEL) and (T % tm == 0 or tm % T == 0)
    tiles_per_seq = max(1, T // tm)
    seqs_per_tile = max(1, tm // T)
    row = pl.BlockSpec((tm, K), lambda i, j: (i, 0))
    full = lambda a: pl.BlockSpec(a.shape, lambda i, j: (0, 0))
    col = lambda a: pl.BlockSpec((a.shape[0], tn), lambda i, j: (0, j % nt))
    shift = pl.BlockSpec((seqs_per_tile, 1, K), lambda i, j: (i // tiles_per_seq, 0, 0))
    ins = [x3.reshape(M, K), shift0.reshape(B, 1, K), mu, w_main, *lora1, *lora2, *bias]
    specs = ([row, shift, full(mu), pl.BlockSpec((K, tn), lambda i, j: (0, j))] + [full(w) for w in lora1]
             + [col(w) for w in lora2] + [col(b) for b in bias])
    scratch = ([pltpu.VMEM((3, tm, K), BF16)] + [pltpu.VMEM((tm, w.shape[1]), BF16) for w in lora1]
               + [pltpu.VMEM((SUBLANES, K), F32)])
    if vgate is not None:
        v1, v2, v0, v_first = vgate
        ins += [v1, v2, v0, v_first]
        specs += [full(v1), col(v2), col(v0), pl.BlockSpec((tm, tn), lambda i, j: (i, j % nt))]
        scratch.append(pltpu.VMEM((tm, v1.shape[1]), BF16))
    out = lambda grp: pl.BlockSpec((tm, tn), lambda i, j: (i, jnp.clip(j - grp * nt, 0, nt - 1)))
    return pl.pallas_call(
        functools.partial(_rwkv_proj_kernel, has_vgate=vgate is not None, seq_len=T, tiles_per_seq=tiles_per_seq),
        grid=(M // tm, 3 * nt),
        in_specs=specs, out_specs=[out(0), out(1), out(2), out(0), out(1), out(2)],
        out_shape=[jax.ShapeDtypeStruct((M, D_MODEL), dt) for dt in (BF16, BF16, BF16, F32, BF16, BF16)],
        scratch_shapes=scratch, compiler_params=_params(2), name=name,
    )(*ins)


def _wkv_chunks(r, lw, k, v, a, prm, S, consts):
    C = WKV_CHUNK
    bdmask, l_incl, strict, incl, eye = consts

    def bd(xp):
        return jnp.concatenate([xp.astype(BF16)] * WKV_HEADS, axis=0) * bdmask

    def headsum(xp):
        hi, lo = _split(xp)
        return _dg(hi, bdmask, _NN) + _dg(lo, bdmask, _NN)

    def each(f, *lists):
        return [f(*args) for args in zip(*lists)]

    kkp, kap, rkp, lng, lnb = (list(t) for t in zip(*prm))
    kkn = each(lambda k_, p_: k_ * p_, k, kkp)
    kp = each(lambda k_, a_, p_: k_ * (1.0 + (a_ - 1.0) * p_), k, a, kap)
    sums = each(lambda kkn_, r_, kp_, p_: headsum(jnp.concatenate([kkn_ * kkn_, r_ * kp_ * p_], axis=0)),
                kkn, r, kp, rkp)
    kk = each(lambda kkn_, s_: kkn_ / jnp.maximum(jnp.sqrt(s_[:C]), 1e-12), kkn, sums)
    bonus = each(lambda s_, v_: s_[C:] * v_, sums, v)
    b = each(lambda kk_, a_: kk_ * a_, kk, a)
    cum = each(lambda lw_: _dot_exact_lhs(l_incl, lw_), lw)
    e_neg = each(lambda c_: jnp.exp(-c_), cum)
    x = each(lambda kk_, c_, lw_, r_: jnp.concatenate([kk_ * jnp.exp(c_ - lw_), r_ * jnp.exp(c_)],
                                                       axis=0).astype(BF16), kk, cum, lw, r)
    g_b = each(lambda x_, b_, e_: _dg(x_, bd(b_ * e_), _NT), x, b, e_neg)
    g_k = each(lambda x_, k_, e_: _dg(x_, bd(k_ * e_), _NT), x, kp, e_neg)
    xs = each(lambda x_, s_: _dg(x_, s_.astype(BF16), _NT), x, S)
    mb = each(lambda g_: jnp.where(incl, g_[C:], 0.0), g_b)
    kv = each(lambda g_, v_: _dg(jnp.concatenate([jnp.where(strict, g_[:C], 0.0), jnp.where(incl, g_[C:], 0.0)],
                                                 axis=0).astype(BF16), bd(v_), _NN), g_k, v)
    rhs = each(lambda xs_, kv_: -(xs_[:C] + kv_[:C]), xs, kv)
    npow = each(lambda g_: -jnp.where(strict, g_[:C], 0.0), g_b)
    tinv = each(lambda n_: eye + n_, npow)
    npow = each(lambda n_: _dg(n_.astype(BF16), bd(n_), _NN), npow)
    for _ in range(int(math.log2(C)) - 2):
        both = each(lambda t_, n_: _dg(jnp.concatenate([t_, n_], axis=0).astype(BF16), bd(n_), _NN), tinv, npow)
        tinv = each(lambda t_, b_: t_ + b_[:C], tinv, both)
        npow = each(lambda b_: b_[C:], both)
    tinv = each(lambda t_, n_: t_ + _dg(t_.astype(BF16), bd(n_), _NN), tinv, npow)
    u = each(lambda t_, r_: _dg(t_.astype(BF16), bd(r_), _NN), tinv, rhs)
    y = each(lambda xs_, kv_, mb_, u_: xs_[C:] + kv_[C:] + _dg(mb_.astype(BF16), bd(u_), _NN), xs, kv, mb, u)
    upd = each(lambda u_, v_, b_, k_, c_: _dg(
        jnp.concatenate([u_, v_], axis=0).astype(BF16),
        (jnp.concatenate([b_, k_], axis=0) * jnp.exp(c_[C - 1:C, :] - jnp.concatenate([c_, c_], axis=0))).astype(BF16),
        _TN), u, v, b, kp, cum)
    bdmask_f = bdmask.astype(F32)
    s_new = each(lambda s_, c_, u_: s_ * jnp.exp(c_[C - 1:C, :]) + u_ * bdmask_f, S, cum, upd)
    mu = each(lambda y_: _dg(y_.astype(BF16), bdmask, _NN) * (1.0 / RWKV_N), y)
    yc = each(lambda y_, m_: y_ - m_, y, mu)
    var = each(lambda yc_: _dg((yc_ * yc_).astype(BF16), bdmask, _NN) * (1.0 / RWKV_N), yc)
    z = each(lambda yc_, var_, g_, b_, bo_: yc_ * lax.rsqrt(var_ + RWKV_GN_EPS) * g_ + b_ + bo_,
             yc, var, lng, lnb, bonus)
    return z, s_new


def _wkv_kernel(r_ref, lw_ref, k_ref, v_ref, a_ref, kk_ref, ka_ref, rk_ref, g_ref, b_ref, s0_ref,
                z_ref, so_ref, s_scr, *, n_t, nb, n_grp):
    t = pl.program_id(1)
    C, N, HW = WKV_CHUNK, RWKV_N, WKV_HEADS * RWKV_N
    ri = lax.broadcasted_iota(jnp.int32, (HW, HW), 0) // N
    ci = lax.broadcasted_iota(jnp.int32, (HW, HW), 1) // N
    bdmask_f = (ri == ci).astype(F32)

    chains = [(bi, g) for bi in range(nb) for g in range(n_grp)]

    @pl.when(t == 0)
    def _():
        for n, (bi, g) in enumerate(chains):
            rows = s0_ref[bi, g * WKV_HEADS:(g + 1) * WKV_HEADS].reshape(HW, N)
            s_scr[n] = jnp.concatenate([rows] * WKV_HEADS, axis=1) * bdmask_f

    tt = lax.broadcasted_iota(jnp.int32, (C, C), 0)
    ss = lax.broadcasted_iota(jnp.int32, (C, C), 1)
    tp = lax.broadcasted_iota(jnp.int32, (C, HW), 0)
    sp = lax.broadcasted_iota(jnp.int32, (C, HW), 1) % N
    consts = (bdmask_f.astype(BF16), (tt >= ss).astype(BF16), tp > sp, tp >= sp, (tp == sp).astype(F32))
    sls = [slice(g * HW, (g + 1) * HW) for _, g in chains]
    tok = lambda ref: [ref[bi, :, sl].astype(F32) for (bi, _), sl in zip(chains, sls)]
    prm = [tuple(ref[:, sl] for ref in (kk_ref, ka_ref, rk_ref, g_ref, b_ref)) for sl in sls]
    zs, s_news = _wkv_chunks(tok(r_ref), tok(lw_ref), tok(k_ref), tok(v_ref), tok(a_ref), prm,
                             [s_scr[n] for n in range(len(chains))], consts)
    for n, ((bi, _), sl) in enumerate(zip(chains, sls)):
        s_scr[n] = s_news[n]
        z_ref[bi, :, sl] = zs[n]

    @pl.when(t == n_t - 1)
    def _():
        for n, (bi, g) in enumerate(chains):
            for h in range(WKV_HEADS):
                so_ref[bi, g * WKV_HEADS + h] = s_scr[n, h * N:(h + 1) * N, h * N:(h + 1) * N]


def _wkv(name, r, lw, k, v, a, kkp, kap, rkp, lng, lnb, s0):
    B, T, D = r.shape
    C, HW = WKV_CHUNK, WKV_HEADS * RWKV_N
    n_grp = min(max(1, WKV_CHAINS // B), D // HW)
    W = n_grp * HW
    assert T % C == 0 and D % W == 0
    n_t = T // C
    tok = pl.BlockSpec((B, C, W), lambda h, t: (0, t, h))
    vec = pl.BlockSpec((1, W), lambda h, t: (0, h))
    st = pl.BlockSpec((B, n_grp * WKV_HEADS, RWKV_N, RWKV_N), lambda h, t: (0, h, 0, 0))
    return pl.pallas_call(
        functools.partial(_wkv_kernel, n_t=n_t, nb=B, n_grp=n_grp), grid=(D // W, n_t),
        in_specs=[tok] * 5 + [vec] * 5 + [st], out_specs=[tok, st],
        out_shape=[jax.ShapeDtypeStruct((B, T, D), F32), jax.ShapeDtypeStruct(s0.shape, F32)],
        scratch_shapes=[pltpu.VMEM((B * n_grp, HW, HW), F32)], compiler_params=_params(2), name=name,
    )(r, lw, k, v, a, kkp, kap, rkp, lng, lnb, s0)


def _cmul(ar, ai, br, bi):
    return ar * br - ai * bi, ar * bi + ai * br


def _dot3w(a, w_hi, w_lo, dims=_NN):
    ah, al = _split(a)
    return _dg(ah, w_hi, dims) + (_dg(al, w_hi, dims) + _dg(ah, w_lo, dims))


S5_SHIFTS = (1, 2, 4)


def _s5_kernel(x_ref, wbh_ref, wbl_ref, wc_ref, lr_ref, li_ref, dt_ref, d_ref, h0r_ref, h0i_ref,
               z_ref, hr_ref, hi_ref, pw_re, pw_im, sh_re, sh_im, cf, hc, *, n_t, tt):
    t = pl.program_id(2)
    R = SUBLANES

    @pl.when(t == 0)
    def _():
        hc[0:1, :] = h0r_ref[0]
        hc[1:2, :] = h0i_ref[0]

    @pl.when(jnp.logical_and(t == 0, pl.program_id(1) == 0))
    def _():
        lr, li, dt = lr_ref[0], li_ref[0], dt_ref[0]
        rid = lax.broadcasted_iota(jnp.int32, (R, S5_STATES), 0)
        n = (rid + 1).astype(F32)
        mag = jnp.exp(lr * dt * n)
        ang = li * dt * n
        pw_re[...] = mag * jnp.cos(ang)
        pw_im[...] = mag * jnp.sin(ang)
        for m, sh in enumerate(S5_SHIFTS):
            sh_re[m] = jnp.where(rid >= sh, pw_re[sh - 1:sh, :], 0.0)
            sh_im[m] = jnp.where(rid >= sh, pw_im[sh - 1:sh, :], 0.0)
        ar, ai = pw_re[0:1, :], pw_im[0:1, :]
        den = lr * lr + li * li
        cf[0:1, :] = ((ar - 1.0) * lr + ai * li) / den
        cf[1:2, :] = (ai * lr - (ar - 1.0) * li) / den

    x = x_ref[0]
    bu = _dot3w(x, wbh_ref[0], wbl_ref[0])
    cr, ci = cf[0:1, :], cf[1:2, :]
    br, bi = _cmul(cr, ci, bu[:, :S5_STATES], bu[:, S5_STATES:])
    pr, pi = pw_re[...], pw_im[...]
    hr, hi = hc[0:1, :], hc[1:2, :]
    xrs, xis = [], []
    for i in range(tt // R):
        vr, vi = br[i * R:(i + 1) * R, :], bi[i * R:(i + 1) * R, :]
        for m, sh in enumerate(S5_SHIFTS):
            ur, ui = _cmul(sh_re[m], sh_im[m], pltpu.roll(vr, sh, axis=0), pltpu.roll(vi, sh, axis=0))
            vr, vi = vr + ur, vi + ui
        ur, ui = _cmul(pr, pi, hr, hi)
        vr, vi = vr + ur, vi + ui
        xrs.append(vr)
        xis.append(vi)
        hr, hi = vr[R - 1:R, :], vi[R - 1:R, :]
    hc[0:1, :] = hr
    hc[1:2, :] = hi
    xcat = jnp.concatenate([jnp.concatenate(xrs, axis=0), jnp.concatenate(xis, axis=0)], axis=-1)
    y = _dg(xcat.astype(BF16), wc_ref[0], _NN) + d_ref[...] * x
    z_ref[0] = 0.5 * y * (1.0 + jnp.tanh(math.sqrt(2.0 / math.pi) * (y + 0.044715 * (y * y * y))))

    @pl.when(t == n_t - 1)
    def _():
        hr_ref[0] = hr
        hi_ref[0] = hi


def _s5(name, x3, wb, wc, lr, li, dt, d, h0r, h0i):
    B, T, D = x3.shape
    tt = min(256, T)
    assert T % tt == 0 and tt % SUBLANES == 0
    n_t = T // tt
    n_s = D // S5_SLAB
    S = S5_STATES
    st = pl.BlockSpec((1, 1, S), lambda s, b, t: (b, 0, s))
    pv = pl.BlockSpec((1, 1, S), lambda s, b, t: (0, 0, s))
    w_in = pl.BlockSpec((1, S5_SLAB, 2 * S), lambda s, b, t: (s, 0, 0))
    w_out = pl.BlockSpec((1, 2 * S, S5_SLAB), lambda s, b, t: (s, 0, 0))
    split = lambda w: (w.astype(BF16), (w - w.astype(BF16).astype(F32)).astype(BF16))
    return pl.pallas_call(
        functools.partial(_s5_kernel, n_t=n_t, tt=tt), grid=(n_s, B, n_t),
        in_specs=[pl.BlockSpec((1, tt, S5_SLAB), lambda s, b, t: (b, t, s)),
                  w_in, w_in, w_out,
                  pv, pv, pv,
                  pl.BlockSpec((1, S5_SLAB), lambda s, b, t: (0, s)),
                  st, st],
        out_specs=[pl.BlockSpec((1, tt, S5_SLAB), lambda s, b, t: (b, t, s)), st, st],
        out_shape=[jax.ShapeDtypeStruct((B, T, D), F32),
                   jax.ShapeDtypeStruct((B, 1, n_s * S), F32), jax.ShapeDtypeStruct((B, 1, n_s * S), F32)],
        scratch_shapes=[pltpu.VMEM((SUBLANES, S), F32), pltpu.VMEM((SUBLANES, S), F32),
                        pltpu.VMEM((len(S5_SHIFTS), SUBLANES, S), F32),
                        pltpu.VMEM((len(S5_SHIFTS), SUBLANES, S), F32),
                        pltpu.VMEM((SUBLANES, S), F32), pltpu.VMEM((SUBLANES, S), F32)],
        compiler_params=_params(3), name=name,
    )(x3, *split(wb), wc.astype(BF16), lr, li, dt, d, h0r, h0i)


def _select_topk(gate, idx, valid, axis):
    g = jnp.where(valid, gate, NEG)
    big = jnp.int32(1 << 30)
    sel = jnp.zeros(gate.shape, jnp.bool_)
    for _ in range(MOBA_TOPK):
        m = jnp.max(g, axis=axis, keepdims=True)
        first = jnp.min(jnp.where(g == m, idx, big), axis=axis, keepdims=True)
        pick = idx == first
        sel = jnp.logical_or(sel, pick)
        g = jnp.where(pick, -jnp.inf, g)
    return jnp.logical_and(sel, valid)


def _moba_prompt_kernel(qa_ref, qb_ref, k_ref, v_ref, sl_ref, oa_ref, ob_ref, kmean_scr, *, n_blk):
    i = pl.program_id(2)
    BK = MOBA_BLOCK

    @pl.when(i == 0)
    def _():
        kmean_scr[...] = jnp.concatenate([jnp.sum(k_ref[0, n * BK:(n + 1) * BK, :], axis=0, keepdims=True)
                                          for n in range(n_blk)], axis=0) * (1.0 / BK)

    slope = sl_ref[0][:, 0:1]
    qi = lax.broadcasted_iota(jnp.int32, (BK, BK), 0)
    ki = lax.broadcasted_iota(jnp.int32, (BK, BK), 1)
    causal = qi >= ki
    bias0 = slope * (qi - ki).astype(F32)
    scale = MOBA_DH ** -0.5
    add = lambda x, y: x + y

    def select(c, q):
        if c == 0:
            return None
        gate_t = _dot3(kmean_scr[...], q, _NT)
        nidx = lax.broadcasted_iota(jnp.int32, (n_blk, BK), 0)
        sel_t = jnp.where(_select_topk(gate_t, nidx, nidx < c, axis=0), 1.0, 0.0).astype(BF16)
        return _dg(jnp.where(qi == ki, 1.0, 0.0).astype(BF16), sel_t, _NT) > 0.5

    def scores(c, qb, sel):
        ss = []
        for n in range(c + 1):
            s = _dg(qb, k_ref[0, n * BK:(n + 1) * BK, :].astype(BF16), _NT) * scale
            s = s - (bias0 + slope * float((c - n) * BK))
            ss.append(jnp.where(causal if n == c else sel[:, n:n + 1], s, NEG))
        return ss

    def pair(cs):
        qs = [qa_ref[0], qb_ref[0]]
        sels = [select(c, q) for c, q in zip(cs, qs)]
        sss = [scores(c, q.astype(BF16), sel) for c, q, sel in zip(cs, qs, sels)]
        ms = [jnp.max(functools.reduce(jnp.maximum, ss), axis=-1, keepdims=True) for ss in sss]
        pss = [[jnp.exp(s - m) for s in ss] for ss, m in zip(sss, ms)]
        ls = [jnp.sum(functools.reduce(add, ps), axis=-1, keepdims=True) for ps in pss]
        accs = [functools.reduce(add, [_dg(p.astype(BF16), v_ref[0, n * BK:(n + 1) * BK, :].astype(BF16), _NN)
                                       for n, p in enumerate(ps)]) for ps in pss]
        for o_ref, acc, l in zip((oa_ref, ob_ref), accs, ls):
            o_ref[0] = acc / l

    for c in range(n_blk // 2):
        pl.when(i == c)(functools.partial(pair, (c, n_blk - 1 - c)))


def _moba_prompt(name, q, k, v, slopes):
    B, T, D = q.shape
    n_blk = T // MOBA_BLOCK
    assert T % MOBA_BLOCK == 0 and n_blk % 2 == 0
    q_lo = pl.BlockSpec((1, MOBA_BLOCK, MOBA_DH), lambda b, h, i: (b, i, h))
    q_hi = pl.BlockSpec((1, MOBA_BLOCK, MOBA_DH), lambda b, h, i: (b, n_blk - 1 - i, h))
    kv = pl.BlockSpec((1, T, MOBA_DH), lambda b, h, i: (b, 0, h))
    o_lo, o_hi = pl.pallas_call(
        functools.partial(_moba_prompt_kernel, n_blk=n_blk), grid=(B, MOBA_H, n_blk // 2),
        in_specs=[q_lo, q_hi, kv, kv, pl.BlockSpec((1, 1, LANES), lambda b, h, i: (h, 0, 0))],
        out_specs=[q_lo, pl.BlockSpec((1, MOBA_BLOCK, MOBA_DH), lambda b, h, i: (b, n_blk // 2 - 1 - i, h))],
        out_shape=[jax.ShapeDtypeStruct((B, T // 2, D), F32)] * 2,
        scratch_shapes=[pltpu.VMEM((n_blk, MOBA_DH), F32)],
        compiler_params=_params(3), name=name,
    )(q, q, k, v, slopes)
    return jnp.concatenate([o_lo, o_hi], axis=1)


def _q_rows(q, nq):
    return jnp.concatenate([q[:, h * MOBA_DH:(h + 1) * MOBA_DH] for h in range(MOBA_H)], axis=0)


def _head_match(n_rows, nq):
    shape = (n_rows * MOBA_H, MOBA_H * nq)
    return lax.broadcasted_iota(jnp.int32, shape, 0) % MOBA_H == lax.broadcasted_iota(jnp.int32, shape, 1) // nq


def _token_scores(k3, qr, nq, dot):
    n_rows = k3.shape[0]
    s2 = dot(k3.reshape(n_rows * MOBA_H, MOBA_DH), qr, _NT)
    s2 = jnp.where(_head_match(n_rows, nq), s2, 0.0)
    return jnp.sum(s2.reshape(n_rows, MOBA_H, MOBA_H * nq), axis=1)


def _moba_scores_kernel(pt_ref, q_ref, m_ref, *refs, nq, pp):
    k_refs, (s_ref, ks_ref) = refs[:pp], refs[pp:]
    qr = _q_rows(q_ref[0], nq).astype(BF16)
    kps = [k_ref[0, 0] for k_ref in k_refs]
    s2s = [_dg(kp.reshape(PAGE_SIZE * MOBA_H, MOBA_DH).astype(BF16), qr, _NT) for kp in kps]
    for u in range(pp):
        s_ref[0, u] = jnp.sum((s2s[u] * m_ref[...]).reshape(PAGE_SIZE, MOBA_H, MOBA_H * nq), axis=1)
        ks_ref[0, u] = jnp.sum(kps[u], axis=0)


def _page_specs(layer, pp):
    return [pl.BlockSpec((1, 1, PAGE_SIZE, MOBA_H, MOBA_DH),
                         lambda b, g, pt, u=u: (layer, pt[b, g * pp + u], 0, 0, 0)) for u in range(pp)]


def _moba_scores(name, page_table, q, cache_k, layer, match_f):
    B, nq, D = q.shape
    n_pg = page_table.shape[1]
    pp = MOBA_PAGES_PER_STEP
    assert n_pg % pp == 0
    gs = pltpu.PrefetchScalarGridSpec(
        num_scalar_prefetch=1, grid=(B, n_pg // pp),
        in_specs=[pl.BlockSpec((1, nq, D), lambda b, g, pt: (b, 0, 0)),
                  pl.BlockSpec(match_f.shape, lambda b, g, pt: (0, 0))] + _page_specs(layer, pp),
        out_specs=[pl.BlockSpec((1, pp, PAGE_SIZE, MOBA_H * nq), lambda b, g, pt: (b, g, 0, 0)),
                   pl.BlockSpec((1, pp, MOBA_H, MOBA_DH), lambda b, g, pt: (b, g, 0, 0))])
    return pl.pallas_call(
        functools.partial(_moba_scores_kernel, nq=nq, pp=pp), grid_spec=gs,
        out_shape=[jax.ShapeDtypeStruct((B, n_pg, PAGE_SIZE, MOBA_H * nq), F32),
                   jax.ShapeDtypeStruct((B, n_pg, MOBA_H, MOBA_DH), F32)],
        compiler_params=_params(2), name=name,
    )(page_table, q, match_f, *([cache_k] * pp))


def _moba_probs_kernel(s_ref, ks_ref, q_ref, kn_ref, sl_ref, p_ref, pn_ref, sel_scr, *, nq, n_blk, past):
    BK = MOBA_BLOCK
    HQ = MOBA_H * nq
    qr = _q_rows(q_ref[0], nq)
    ppb = BK // PAGE_SIZE
    kmean = jnp.sum(ks_ref[0].reshape(n_blk, ppb, MOBA_H, MOBA_DH), axis=1) * (1.0 / BK)
    gate = _token_scores(kmean, qr, nq, _dot3)
    nidx = lax.broadcasted_iota(jnp.int32, (n_blk, HQ), 0)
    own = past // BK
    sel = _select_topk(gate, nidx, nidx < own, axis=0)
    sel_scr[...] = sel.astype(F32)
    slope = sl_ref[...]
    scale = MOBA_DH ** -0.5
    qpos = past + lax.broadcasted_iota(jnp.int32, (1, HQ), 1) % nq
    tok = lax.broadcasted_iota(jnp.int32, (BK, HQ), 0)

    def scores(n):
        s = s_ref[0, n] * scale - slope * (qpos - (n * BK + tok)).astype(F32)
        return jnp.where(sel_scr[pl.ds(n, 1), :] > 0.5, s, NEG)

    sn = _token_scores(kn_ref[0], qr, nq, _dot1) * scale
    tn = lax.broadcasted_iota(jnp.int32, (PAGE_SIZE, HQ), 0)
    dn = qpos - (past + tn)
    sn = jnp.where(jnp.logical_and(dn >= 0, tn < nq), sn - slope * dn.astype(F32), NEG)
    m = lax.fori_loop(0, n_blk, lambda n, m_: jnp.maximum(m_, jnp.max(scores(n), axis=0, keepdims=True)),
                      jnp.max(sn, axis=0, keepdims=True))
    pn = jnp.exp(sn - m)
    l = lax.fori_loop(0, n_blk, lambda n, l_: l_ + jnp.sum(jnp.exp(scores(n) - m), axis=0, keepdims=True),
                      jnp.sum(pn, axis=0, keepdims=True))
    pn_ref[0] = (pn / l).astype(pn_ref.dtype)

    def write(n, c):
        p_ref[0, n] = (jnp.exp(scores(n) - m) / l).astype(p_ref.dtype)
        return c

    lax.fori_loop(0, n_blk, write, 0)


def _moba_probs(name, s, ksum, q, k_new_pad, slopes_hq, past):
    B, n_blk, BK, HQ = s.shape
    nq = q.shape[1]
    blk = pl.BlockSpec((1, n_blk, BK, HQ), lambda b: (b, 0, 0, 0))
    return pl.pallas_call(
        functools.partial(_moba_probs_kernel, nq=nq, n_blk=n_blk, past=past), grid=(B,),
        in_specs=[blk,
                  pl.BlockSpec((1, ksum.shape[1], MOBA_H, MOBA_DH), lambda b: (b, 0, 0, 0)),
                  pl.BlockSpec((1, nq, D_MODEL), lambda b: (b, 0, 0)),
                  pl.BlockSpec((1, PAGE_SIZE, MOBA_H, MOBA_DH), lambda b: (b, 0, 0, 0)),
                  pl.BlockSpec((1, HQ), lambda b: (0, 0))],
        out_specs=[blk, pl.BlockSpec((1, PAGE_SIZE, HQ), lambda b: (b, 0, 0))],
        out_shape=[jax.ShapeDtypeStruct((B, n_blk, BK, HQ), BF16), jax.ShapeDtypeStruct((B, PAGE_SIZE, HQ), BF16)],
        scratch_shapes=[pltpu.VMEM((n_blk, HQ), F32)],
        compiler_params=_params(1), name=name,
    )(s, ksum, q, k_new_pad, slopes_hq)


def _moba_pv_kernel(pt_ref, p_ref, m_ref, pn_ref, vn_ref, *refs, nq, n_steps, pp):
    v_refs, (o_ref, acc) = refs[:pp], refs[pp:]
    g = pl.program_id(1)
    HQ = MOBA_H * nq

    def spread(p):
        p2 = jnp.broadcast_to(p.astype(F32)[:, None, :], (PAGE_SIZE, MOBA_H, HQ))
        return (p2.reshape(PAGE_SIZE * MOBA_H, HQ) * m_ref[...]).astype(BF16)

    def values(v3):
        return v3.reshape(PAGE_SIZE * MOBA_H, MOBA_DH).astype(BF16)

    @pl.when(g == 0)
    def _():
        acc[...] = _dg(spread(pn_ref[0]), values(vn_ref[0]), _TN)

    p2s = [spread(p_ref[0, u]) for u in range(pp)]
    v2s = [values(v_ref[0, 0]) for v_ref in v_refs]
    acc[...] += functools.reduce(lambda x, y: x + y, [_dg(p2, v2, _TN) for p2, v2 in zip(p2s, v2s)])

    @pl.when(g == n_steps - 1)
    def _():
        a = acc[...]
        o_ref[0] = jnp.concatenate([a[h * nq:(h + 1) * nq, :] for h in range(MOBA_H)], axis=-1)


def _moba_pv(name, page_table, p, cache_v, pn, v_new_pad, nq, layer, match_f):
    B, n_pg, _, HQ = p.shape
    D = D_MODEL
    pp = MOBA_PAGES_PER_STEP
    assert n_pg % pp == 0
    gs = pltpu.PrefetchScalarGridSpec(
        num_scalar_prefetch=1, grid=(B, n_pg // pp),
        in_specs=[pl.BlockSpec((1, pp, PAGE_SIZE, HQ), lambda b, g, pt: (b, g, 0, 0)),
                  pl.BlockSpec(match_f.shape, lambda b, g, pt: (0, 0)),
                  pl.BlockSpec((1, PAGE_SIZE, HQ), lambda b, g, pt: (b, 0, 0)),
                  pl.BlockSpec((1, PAGE_SIZE, MOBA_H, MOBA_DH), lambda b, g, pt: (b, 0, 0, 0))]
        + _page_specs(layer, pp),
        out_specs=pl.BlockSpec((1, nq, D), lambda b, g, pt: (b, 0, 0)),
        scratch_shapes=[pltpu.VMEM((HQ, MOBA_DH), F32)])
    return pl.pallas_call(
        functools.partial(_moba_pv_kernel, nq=nq, n_steps=n_pg // pp, pp=pp), grid_spec=gs,
        out_shape=jax.ShapeDtypeStruct((B, nq, D), F32), compiler_params=_params(2), name=name,
    )(page_table, p, match_f, pn, v_new_pad, *([cache_v] * pp))


def _ident(x):
    return x


def _mix(x, xp, mu):
    return x + (xp - x) * mu


def _pad_cols(w, n):
    return jnp.pad(w, ((0, 0), (0, n - w.shape[1])))


def _pad_rows(w, n):
    return jnp.pad(w, ((0, n - w.shape[0]), (0, 0)))


def _rwkv_layer(tag, x3, shift0, s0, v_first, P, j, ln_g, ln_b):
    B, T, D = x3.shape
    M = B * T
    x = x3.reshape(M, D)
    wb = lambda w: w.astype(BF16)
    lp = LANES
    w_rkv = P['rwkv_w_rkv'][j]
    w_main = jnp.concatenate([wb(w_rkv[0]), wb(w_rkv[1]), wb(w_rkv[2])], axis=1)
    lora1 = [wb(_pad_cols(P['rwkv_w1'][j], lp)), wb(_pad_cols(P['rwkv_a1'][j], lp)), wb(P['rwkv_g1'][j])]
    lora2 = [wb(_pad_rows(P['rwkv_w2'][j], lp)), wb(_pad_rows(P['rwkv_a2'][j], lp)), wb(P['rwkv_g2'][j])]
    bias = [P['rwkv_w0'][j][None, :], P['rwkv_a0'][j][None, :]]
    vgate = None
    if j > 0:
        vgate = (wb(_pad_cols(P['rwkv_v1'][j - 1], lp)), wb(_pad_rows(P['rwkv_v2'][j - 1], lp)),
                 P['rwkv_v0'][j - 1][None, :], v_first)
    r, k, v, lw, a, g = _rwkv_proj(tag + 'proj', x3, shift0, P['rwkv_mu'][j], w_main, lora1, lora2, bias, vgate)
    if j == 0:
        v_first = v
    Tp = -(-T // WKV_CHUNK) * WKV_CHUNK
    seq = lambda t: jnp.pad(t.reshape(B, T, D), ((0, 0), (0, Tp - T), (0, 0)))
    vecp = lambda name: P[name][j].reshape(1, D)
    z, s_new = _wkv(tag + 'wkv', seq(r), seq(lw), seq(k), seq(v), seq(a), vecp('rwkv_k_k'), vecp('rwkv_k_a'),
                    vecp('rwkv_r_k'), vecp('rwkv_lnx_g'), vecp('rwkv_lnx_b'), s0)
    z = z[:, :T].reshape(M, D)
    xn = _mm_ln(tag + 'wo', [z, g], wb(P['rwkv_wo'][j]), x, ln_g, ln_b, lambda z_, g_: z_ * g_)
    return xn, s_new, x3[:, -1], v_first


def _s5_weights(P, j):
    ns, gs = D_MODEL // S5_SLAB, S5_SLAB // S5_GROUP
    eye = jnp.eye(gs, dtype=F32)

    def bd_in(b):
        bt = jnp.swapaxes(b.reshape(ns, gs, S5_P, S5_GROUP), 2, 3)
        return jnp.einsum('sgcp,gh->sgchp', bt, eye).reshape(ns, S5_SLAB, S5_STATES)

    def bd_out(c):
        ct = jnp.swapaxes(c.reshape(ns, gs, S5_GROUP, S5_P), 2, 3)
        return jnp.einsum('sgpc,gh->sgphc', ct, eye).reshape(ns, S5_STATES, S5_SLAB)

    wb = jnp.concatenate([bd_in(P['s5_b_re'][j]), bd_in(P['s5_b_im'][j])], axis=2)
    wc = jnp.concatenate([bd_out(P['s5_c_re'][j]), -bd_out(P['s5_c_im'][j])], axis=1)
    flat = lambda t: t.reshape(1, 1, S5_G * S5_P)
    dt = jnp.broadcast_to(jnp.exp(P['s5_log_dt'][j])[:, None], (S5_G, S5_P))
    return wb, wc, flat(P['s5_lam_re'][j]), flat(P['s5_lam_im'][j]), flat(dt)


def _s5_layer(tag, x3, h0r, h0i, P, j, ln_g, ln_b):
    B, T, D = x3.shape
    M = B * T
    x = x3.reshape(M, D)
    wb, wc, lr, li, dt = _s5_weights(P, j)
    st = lambda h: h.reshape(B, 1, S5_G * S5_P)
    z, hr, hi = _s5(tag + 'scan', x3, wb, wc, lr, li, dt, P['s5_d'][j][None, :], st(h0r), st(h0i))
    out, = _mm(tag + 'gate', [z.reshape(M, D)], [], [P['s5_w_val'][j].astype(BF16), P['s5_w_gate'][j].astype(BF16)],
               [], [], _ident, lambda val, gate: (val * _sigmoid(gate),), [F32])
    xn = _ln(tag + 'ln', x, out, ln_g, ln_b)
    return xn, hr.reshape(B, S5_G, S5_P), hi.reshape(B, S5_G, S5_P)


def _alibi_slopes():
    return 2.0 ** (-8.0 * jnp.arange(1, MOBA_H + 1, dtype=F32) / MOBA_H)


def _moba_layer(tag, x3, cache_k, cache_v, page_table, P, j, ln_g, ln_b):
    B, T, D = x3.shape
    M = B * T
    x = x3.reshape(M, D)
    wqkv = P['moba_w_qkv'][j]
    ws = [wqkv[:, n * D:(n + 1) * D].astype(BF16) for n in range(3)]
    q, k, v = _mm(tag + 'qkv', [x], [], ws, [], [], _ident, lambda a, b, c: (a, b, c), [F32, F32, F32])
    q3, k3, v3 = (t.reshape(B, T, D) for t in (q, k, v))
    slopes = _alibi_slopes()
    if cache_k is None:
        sl = jnp.broadcast_to(slopes[:, None, None], (MOBA_H, 1, LANES))
        o = _moba_prompt(tag + 'attn', q3, k3, v3, sl)
    else:
        n_pg = page_table.shape[1]
        past = n_pg * PAGE_SIZE
        assert past % MOBA_BLOCK == 0 and T <= PAGE_SIZE and T <= MOBA_BLOCK
        HQ = MOBA_H * T
        match_f = _head_match(PAGE_SIZE, T).astype(F32)
        s, ksum = _moba_scores(tag + 'scores', page_table, q3, cache_k, j, match_f)
        sl = jnp.repeat(slopes, T)[None, :]
        pad = lambda t: jnp.pad(t.reshape(B, T, MOBA_H, MOBA_DH), ((0, 0), (0, PAGE_SIZE - T), (0, 0), (0, 0)))
        p, pn = _moba_probs(tag + 'probs', s.reshape(B, past // MOBA_BLOCK, MOBA_BLOCK, HQ), ksum, q3, pad(k3),
                            sl, past)
        o = _moba_pv(tag + 'pv', page_table, p.reshape(B, n_pg, PAGE_SIZE, HQ), cache_v, pn, pad(v3), T, j, match_f)
    xn = _mm_ln(tag + 'wo', [o.reshape(M, D)], P['moba_wo'][j].astype(BF16), x, ln_g, ln_b, _ident)
    return xn, k3.reshape(B, T, MOBA_H, MOBA_DH), v3.reshape(B, T, MOBA_H, MOBA_DH)


def _ffn_layer(tag, x3, conv0, P, i, ln_g, ln_b):
    B, T, D = x3.shape
    w_in = P['ffn_w_in'][i]
    h, cst = _ffn_in(tag + 'in', x3, w_in[:, :D_FF].astype(BF16), w_in[:, D_FF:].astype(BF16), conv0,
                     P['ffn_conv_w'][i], P['ffn_conv_b'][i][None, :])
    xn = _mm_ln(tag + 'down', [h], P['ffn_w_down'][i].astype(BF16), x3.reshape(B * T, D), ln_g, ln_b, _ident)
    return xn.reshape(B, T, D), cst


def _trunk(grp, x3, wkv0, shift0, s5re0, s5im0, conv0, cache_k, cache_v, page_table, P):
    B, T, D = x3.shape
    wkv_new, shift_new, s5re_new, s5im_new, k_new, v_new, conv_new = [], [], [], [], [], [], []
    v_first = None
    for i in range(DEPTH):
        kind, j = i % 3, i // 3
        tag = f'{grp}{i}_'
        g0, b0 = P['ln_g'][i, 0][None, :], P['ln_b'][i, 0][None, :]
        g1, b1 = P['ln_g'][i, 1][None, :], P['ln_b'][i, 1][None, :]
        if kind == 0:
            xn, S, last, v_first = _rwkv_layer(tag, x3, shift0[j], wkv0[j], v_first, P, j, g0, b0)
            wkv_new.append(S)
            shift_new.append(last)
        elif kind == 1:
            xn, hr, hi = _s5_layer(tag, x3, s5re0[j], s5im0[j], P, j, g0, b0)
            s5re_new.append(hr)
            s5im_new.append(hi)
        else:
            xn, kr, vr = _moba_layer(tag, x3, cache_k, cache_v, page_table, P, j, g0, b0)
            k_new.append(kr)
            v_new.append(vr)
        x3, cst = _ffn_layer(tag + 'ffn_', xn.reshape(B, T, D), conv0[i], P, i, g1, b1)
        conv_new.append(cst)
    return (x3, jnp.stack(wkv_new), jnp.stack(shift_new), jnp.stack(s5re_new), jnp.stack(s5im_new),
            jnp.stack(conv_new), jnp.stack(k_new), jnp.stack(v_new))


def kernel(x_prompt, x_sample, state_rwkv_wkv, state_rwkv_shift, state_s5_re, state_s5_im, state_ffn_conv, cache_k, cache_v, page_table, ln_g, ln_b, rwkv_mu, rwkv_w_rkv, rwkv_w0, rwkv_w1, rwkv_w2, rwkv_a0, rwkv_a1, rwkv_a2, rwkv_v0, rwkv_v1, rwkv_v2, rwkv_g1, rwkv_g2, rwkv_k_k, rwkv_k_a, rwkv_r_k, rwkv_lnx_g, rwkv_lnx_b, rwkv_wo, s5_log_dt, s5_lam_re, s5_lam_im, s5_b_re, s5_b_im, s5_c_re, s5_c_im, s5_d, s5_w_val, s5_w_gate, moba_w_qkv, moba_wo, ffn_w_in, ffn_conv_w, ffn_conv_b, ffn_w_down):
    P = dict(ln_g=ln_g, ln_b=ln_b, rwkv_mu=rwkv_mu, rwkv_w_rkv=rwkv_w_rkv, rwkv_w0=rwkv_w0, rwkv_w1=rwkv_w1,
             rwkv_w2=rwkv_w2, rwkv_a0=rwkv_a0, rwkv_a1=rwkv_a1, rwkv_a2=rwkv_a2, rwkv_v0=rwkv_v0,
             rwkv_v1=rwkv_v1, rwkv_v2=rwkv_v2, rwkv_g1=rwkv_g1, rwkv_g2=rwkv_g2, rwkv_k_k=rwkv_k_k,
             rwkv_k_a=rwkv_k_a, rwkv_r_k=rwkv_r_k.reshape(rwkv_r_k.shape[0], D_MODEL), rwkv_lnx_g=rwkv_lnx_g,
             rwkv_lnx_b=rwkv_lnx_b, rwkv_wo=rwkv_wo, s5_log_dt=s5_log_dt, s5_lam_re=s5_lam_re,
             s5_lam_im=s5_lam_im, s5_b_re=s5_b_re, s5_b_im=s5_b_im, s5_c_re=s5_c_re, s5_c_im=s5_c_im, s5_d=s5_d,
             s5_w_val=s5_w_val, s5_w_gate=s5_w_gate, moba_w_qkv=moba_w_qkv, moba_wo=moba_wo,
             ffn_w_in=ffn_w_in, ffn_conv_w=ffn_conv_w, ffn_conv_b=ffn_conv_b, ffn_w_down=ffn_w_down)
    B = x_prompt.shape[0]
    n_rwkv, n_s5 = state_rwkv_wkv.shape[0], state_s5_re.shape[0]
    zeros = lambda *s: jnp.zeros(s, F32)
    outs_p = _trunk('p', x_prompt, zeros(n_rwkv, B, RWKV_H, RWKV_N, RWKV_N), zeros(n_rwkv, B, D_MODEL),
                    zeros(n_s5, B, S5_G, S5_P), zeros(n_s5, B, S5_G, S5_P), zeros(DEPTH, B, CONV_W - 1, D_FF),
                    None, None, None, P)
    outs_s = _trunk('s', x_sample, state_rwkv_wkv, state_rwkv_shift, state_s5_re, state_s5_im, state_ffn_conv,
                    cache_k, cache_v, page_table, P)
    return (outs_p[0], outs_s[0]) + tuple(outs_p[1:]) + tuple(outs_s[1:])
```

```python
import functools
import math

import jax
import jax.numpy as jnp
from jax import lax
from jax.experimental import pallas as pl
from jax.experimental.pallas import tpu as pltpu

F32 = jnp.float32
BF16 = jnp.bfloat16

D_MODEL = 2048
DEPTH = 4
RWKV_N = 64
RWKV_H = D_MODEL // RWKV_N
RWKV_GN_EPS = 64e-5
S5_GROUP = 16
S5_G = D_MODEL // S5_GROUP
S5_P = 64
MOBA_H = 16
MOBA_DH = D_MODEL // MOBA_H
MOBA_BLOCK = 256
MOBA_TOPK = 3
PAGE_SIZE = 128
D_FF = 11 * D_MODEL // 4
CONV_W = 3
LN_EPS = 1e-5
DN_ALPHA = (2.0 * DEPTH) ** 0.25
NEG = -1e30

LANES = 128
SUBLANES = 8
VMEM_LIMIT_BYTES = 56 * 1024 * 1024

MM_ROW_TILE = 1024
MM_LN_RESIDENT_BYTES = 24 * 1024 * 1024
MM_LN_WEIGHT_TILE_BYTES = 12 * 1024 * 1024
WKV_CHUNK = 64
WKV_CHAINS = 16
WKV_HEADS = 4
MOBA_PAGES_PER_STEP = 8
S5_SLAB = LANES
S5_STATES = (S5_SLAB // S5_GROUP) * S5_P


def _params(n_axes):
    return pltpu.CompilerParams(dimension_semantics=("arbitrary",) * n_axes,
                                vmem_limit_bytes=VMEM_LIMIT_BYTES)


def _sigmoid(x):
    return 1.0 / (1.0 + jnp.exp(-x))


def _softplus(x):
    return jnp.maximum(x, 0.0) + jnp.log1p(jnp.exp(-jnp.abs(x)))


def _dg(a, b, dims):
    return lax.dot_general(a, b, (dims, ((), ())), preferred_element_type=F32)


_NN = ((1,), (0,))
_NT = ((1,), (1,))
_TN = ((0,), (0,))


def _split(a):
    hi = a.astype(BF16)
    lo = (a - hi.astype(F32)).astype(BF16)
    return hi, lo


def _dot3(a, b, dims=_NN):
    ah, al = _split(a)
    bh, bl = _split(b)
    return _dg(ah, bh, dims) + (_dg(al, bh, dims) + _dg(ah, bl, dims))


def _dot1(a, b, dims=_NN):
    return _dg(a.astype(BF16), b.astype(BF16), dims)


def _dot_exact_lhs(lhs_bf16, x):
    x1 = x.astype(BF16)
    r1 = x - x1.astype(F32)
    x2 = r1.astype(BF16)
    x3 = (r1 - x2.astype(F32)).astype(BF16)
    return _dg(lhs_bf16, x1, _NN) + (_dg(lhs_bf16, x2, _NN) + _dg(lhs_bf16, x3, _NN))


def _layer_norm(y, g, b):
    mu = jnp.mean(y, axis=-1, keepdims=True)
    yc = y - mu
    var = jnp.mean(yc * yc, axis=-1, keepdims=True)
    return yc * lax.rsqrt(var + LN_EPS) * g + b


def _mm_kernel(*refs, n_row, n_vec, n_w, n_erow, n_evec, n_out, prologue, epilogue):
    it = iter(refs)
    rows = [next(it) for _ in range(n_row)]
    vecs = [next(it) for _ in range(n_vec)]
    ws = [next(it) for _ in range(n_w)]
    erows = [next(it) for _ in range(n_erow)]
    evecs = [next(it) for _ in range(n_evec)]
    outs = [next(it) for _ in range(n_out)]
    a_scr = next(it)

    @pl.when(pl.program_id(1) == 0)
    def _():
        a_scr[...] = prologue(*[r[...] for r in rows], *[v[...] for v in vecs]).astype(BF16)

    a = a_scr[...]
    accs = [jnp.dot(a, w[...], preferred_element_type=F32) for w in ws]
    res = epilogue(*accs, *[e[...] for e in erows], *[e[...] for e in evecs])
    for o_ref, o in zip(outs, res):
        o_ref[...] = o.astype(o_ref.dtype)


def _mm(name, rows, vecs, ws, erows, evecs, prologue, epilogue, out_dtypes):
    M, K = rows[0].shape
    N = ws[0].shape[1]
    tm = MM_ROW_TILE if (M % MM_ROW_TILE == 0 and len(rows) == 1) else min(512, M)
    tn = 512 if N % 512 == 0 else N
    assert M % tm == 0 and N % tn == 0
    in_specs = ([pl.BlockSpec((tm, K), lambda i, j: (i, 0)) for _ in rows]
                + [pl.BlockSpec((1, K), lambda i, j: (0, 0)) for _ in vecs]
                + [pl.BlockSpec((K, tn), lambda i, j: (0, j)) for _ in ws]
                + [pl.BlockSpec((tm, tn), lambda i, j: (i, j)) for _ in erows]
                + [pl.BlockSpec((1, tn), lambda i, j: (0, j)) for _ in evecs])
    out_specs = [pl.BlockSpec((tm, tn), lambda i, j: (i, j)) for _ in out_dtypes]
    out_shape = [jax.ShapeDtypeStruct((M, N), dt) for dt in out_dtypes]
    kern = functools.partial(_mm_kernel, n_row=len(rows), n_vec=len(vecs), n_w=len(ws), n_erow=len(erows),
                             n_evec=len(evecs), n_out=len(out_dtypes), prologue=prologue, epilogue=epilogue)
    return pl.pallas_call(
        kern, grid=(M // tm, N // tn), in_specs=in_specs, out_specs=out_specs, out_shape=out_shape,
        scratch_shapes=[pltpu.VMEM((tm, K), BF16)], compiler_params=_params(2), name=name,
    )(*rows, *vecs, *ws, *erows, *evecs)


def _mm_ln_kernel(*refs, n_row, prologue, n_k):
    it = iter(refs)
    rows = [next(it) for _ in range(n_row)]
    w_ref, x_ref, g_ref, b_ref, o_ref, acc = (next(it) for _ in range(6))
    k = pl.program_id(1)
    a = prologue(*[r[...] for r in rows]).astype(BF16)
    p = jnp.dot(a, w_ref[...], preferred_element_type=F32)
    if n_k == 1:
        o_ref[...] = _layer_norm(DN_ALPHA * x_ref[...] + p, g_ref[...], b_ref[...])
        return

    @pl.when(k == 0)
    def _():
        acc[...] = p

    @pl.when(jnp.logical_and(k > 0, k < n_k - 1))
    def _():
        acc[...] += p

    @pl.when(k == n_k - 1)
    def _():
        o_ref[...] = _layer_norm(DN_ALPHA * x_ref[...] + (acc[...] + p), g_ref[...], b_ref[...])


def _mm_ln(name, rows, w, x, g, b, prologue):
    M, K = rows[0].shape
    N = w.shape[1]
    w_bytes = K * N * 2
    if w_bytes <= MM_LN_WEIGHT_TILE_BYTES:
        n_k, tm, w_mode = 1, min(512, M), None
    elif w_bytes <= MM_LN_RESIDENT_BYTES:
        n_k, tm, w_mode = 1, min(256, M), pl.Buffered(1)
    else:
        n_k, tm, w_mode = -(-w_bytes // MM_LN_WEIGHT_TILE_BYTES), min(512, M), None
    tk = K // n_k
    assert M % tm == 0 and K % n_k == 0 and tk % LANES == 0
    in_specs = ([pl.BlockSpec((tm, tk), lambda i, k: (i, k)) for _ in rows]
                + [pl.BlockSpec((tk, N), lambda i, k: (k, 0), pipeline_mode=w_mode),
                   pl.BlockSpec((tm, N), lambda i, k: (i, 0)),
                   pl.BlockSpec((1, N), lambda i, k: (0, 0)),
                   pl.BlockSpec((1, N), lambda i, k: (0, 0))])
    kern = functools.partial(_mm_ln_kernel, n_row=len(rows), prologue=prologue, n_k=n_k)
    return pl.pallas_call(
        kern, grid=(M // tm, n_k), in_specs=in_specs,
        out_specs=pl.BlockSpec((tm, N), lambda i, k: (i, 0)),
        out_shape=jax.ShapeDtypeStruct((M, N), F32),
        scratch_shapes=[pltpu.VMEM((tm, N), F32)], compiler_params=_params(2), name=name,
    )(*rows, w, x, g, b)


def _ln_kernel(x_ref, h_ref, g_ref, b_ref, o_ref):
    o_ref[...] = _layer_norm(DN_ALPHA * x_ref[...] + h_ref[...], g_ref[...], b_ref[...])


def _ln(name, x, h, g, b):
    M, N = x.shape
    tm = min(512, M)
    row = pl.BlockSpec((tm, N), lambda i: (i, 0))
    vec = pl.BlockSpec((1, N), lambda i: (0, 0))
    return pl.pallas_call(_ln_kernel, grid=(M // tm,), in_specs=[row, row, vec, vec], out_specs=row,
                          out_shape=jax.ShapeDtypeStruct((M, N), F32), compiler_params=_params(1), name=name)(x, h, g, b)


def _ffn_in_kernel(x_ref, wg_ref, wv_ref, c0_ref, cw_ref, cb_ref, h_ref, cs_ref, a_scr, carry, *, bt, tt):
    t = pl.program_id(1)
    j = pl.program_id(2)
    rows = bt * tt
    tn = wg_ref.shape[1]

    @pl.when(j == 0)
    def _():
        a_scr[...] = x_ref[...].reshape(rows, x_ref.shape[2]).astype(BF16)

    @pl.when(t == 0)
    def _():
        carry[j] = c0_ref[...]

    a = a_scr[...]
    hg = jnp.dot(a, wg_ref[...], preferred_element_type=F32)
    hv = jnp.dot(a, wv_ref[...], preferred_element_type=F32)
    prev = carry[j]
    p2 = jnp.broadcast_to(prev[:, 0:1, :], (bt, tt, tn)).reshape(rows, tn)
    p1 = jnp.broadcast_to(prev[:, 1:2, :], (bt, tt, tn)).reshape(rows, tn)
    r = lax.broadcasted_iota(jnp.int32, (rows, tn), 0) % tt
    s1 = jnp.where(r == 0, p1, pltpu.roll(hg, 1, axis=0))
    s2 = jnp.where(r == 0, p2, jnp.where(r == 1, p1, pltpu.roll(hg, 2, axis=0)))
    cw = cw_ref[...]
    c = cb_ref[...] + cw[0:1, :] * s2 + cw[1:2, :] * s1 + cw[2:3, :] * hg
    h_ref[...] = (c * _sigmoid(c) * hv).astype(h_ref.dtype)
    last = hg.reshape(bt, tt, tn)[:, tt - 2:, :]
    carry[j] = last
    cs_ref[0] = last


def _ffn_in(name, x3, wg, wv, conv0, cw, cb):
    B, T, K = x3.shape
    F = wg.shape[1]
    bt, tt = (1, MM_ROW_TILE) if T % MM_ROW_TILE == 0 else (B, T)
    tn = 512
    assert B % bt == 0 and T % tt == 0 and F % tn == 0 and tt >= CONV_W - 1
    n_j = F // tn
    kern = functools.partial(_ffn_in_kernel, bt=bt, tt=tt)
    n_t = T // tt
    h, cst = pl.pallas_call(
        kern, grid=(B // bt, n_t, n_j),
        in_specs=[pl.BlockSpec((bt, tt, K), lambda b, t, j: (b, t, 0)),
                  pl.BlockSpec((K, tn), lambda b, t, j: (0, j)),
                  pl.BlockSpec((K, tn), lambda b, t, j: (0, j)),
                  pl.BlockSpec((bt, CONV_W - 1, tn), lambda b, t, j: (b, 0, j)),
                  pl.BlockSpec((CONV_W, tn), lambda b, t, j: (0, j)),
                  pl.BlockSpec((1, tn), lambda b, t, j: (0, j))],
        out_specs=[pl.BlockSpec((bt * tt, tn), lambda b, t, j: (b * n_t + t, j)),
                   pl.BlockSpec((1, bt, CONV_W - 1, tn), lambda b, t, j: (t, b, 0, j))],
        out_shape=[jax.ShapeDtypeStruct((B * T, F), BF16),
                   jax.ShapeDtypeStruct((n_t, B, CONV_W - 1, F), F32)],
        scratch_shapes=[pltpu.VMEM((bt * tt, K), BF16), pltpu.VMEM((n_j, bt, CONV_W - 1, tn), F32)],
        compiler_params=_params(3), name=name,
    )(x3, wg, wv, conv0, cw, cb)
    return h, cst[n_t - 1]


RWKV_PROJ_TN = 1024
RWKV_PROJ_TILES = D_MODEL // RWKV_PROJ_TN


def _rwkv_proj_kernel(*refs, has_vgate, seq_len, tiles_per_seq):
    it = iter(refs)
    x_ref, sh_ref, mu_ref, wm_ref, w1_ref, a1_ref, g1_ref, w2_ref, a2_ref, g2_ref, w0_ref, a0_ref = (
        next(it) for _ in range(12))
    if has_vgate:
        v1_ref, v2_ref, v0_ref, vf_ref = (next(it) for _ in range(4))
    r_ref, k_ref, v_ref, lw_ref, a_ref, g_ref = (next(it) for _ in range(6))
    mix_scr, hw_scr, ha_scr, hg_scr, last_scr = (next(it) for _ in range(5))
    if has_vgate:
        hv_scr = next(it)
    i = pl.program_id(0)
    j = pl.program_id(1)
    dot = lambda a, b: jnp.dot(a, b, preferred_element_type=F32)

    @pl.when(j == 0)
    def _():
        x = x_ref[...]
        tm, K = x.shape
        row = lax.broadcasted_iota(jnp.int32, (tm, K), 0)
        prev = pltpu.roll(x, 1, axis=0)
        if tiles_per_seq > 1:
            first = jnp.where(i % tiles_per_seq == 0, sh_ref[0], last_scr[0:1, :])
            prev = jnp.where(row == 0, first, prev)
            last_scr[0:1, :] = x[tm - 1:tm, :]
        else:
            n_seq = tm // seq_len
            sh = jnp.broadcast_to(sh_ref[...], (n_seq, seq_len, K)).reshape(tm, K)
            prev = jnp.where(row % seq_len == 0, sh, prev)
        d = prev - x
        mix = lambda n: (x + d * mu_ref[n:n + 1, :]).astype(BF16)
        mix_scr[0] = mix(0)
        mix_scr[1] = mix(2)
        xv = mix(3)
        mix_scr[2] = xv
        hw_scr[...] = jnp.tanh(dot(mix(1), w1_ref[...])).astype(BF16)
        ha_scr[...] = dot(mix(4), a1_ref[...]).astype(BF16)
        hg_scr[...] = _sigmoid(dot(mix(5), g1_ref[...])).astype(BF16)
        if has_vgate:
            hv_scr[...] = dot(xv, v1_ref[...]).astype(BF16)

    grp = j // RWKV_PROJ_TILES

    @pl.when(grp == 0)
    def _():
        r_ref[...] = dot(mix_scr[0], wm_ref[...]).astype(r_ref.dtype)
        lw_ref[...] = -jnp.exp(-_softplus(-(w0_ref[...] + dot(hw_scr[...], w2_ref[...]))) - 0.5)

    @pl.when(grp == 1)
    def _():
        k_ref[...] = dot(mix_scr[1], wm_ref[...]).astype(k_ref.dtype)
        a_ref[...] = _sigmoid(a0_ref[...] + dot(ha_scr[...], a2_ref[...])).astype(a_ref.dtype)

    @pl.when(grp == 2)
    def _():
        v = dot(mix_scr[2], wm_ref[...])
        if has_vgate:
            v = v + (vf_ref[...] - v) * _sigmoid(v0_ref[...] + dot(hv_scr[...], v2_ref[...]))
        v_ref[...] = v.astype(v_ref.dtype)
        g_ref[...] = dot(hg_scr[...], g2_ref[...]).astype(g_ref.dtype)


def _rwkv_proj(name, x3, shift0, mu, w_main, lora1, lora2, bias, vgate):
    B, T, K = x3.shape
    M = B * T
    tm, tn, nt = min(512, M), RWKV_PROJ_TN, RWKV_PROJ_TILES
    assert M % tm == 0 and w_main.shape == (K, 3 * D_MODEL) and (T % tm == 0 or tm % T == 0)
    tiles_per_seq = max(1, T // tm)
    seqs_per_tile = max(1, tm // T)
    row = pl.BlockSpec((tm, K), lambda i, j: (i, 0))
    full = lambda a: pl.BlockSpec(a.shape, lambda i, j: (0, 0))
    col = lambda a: pl.BlockSpec((a.shape[0], tn), lambda i, j: (0, j % nt))
    shift = pl.BlockSpec((seqs_per_tile, 1, K), lambda i, j: (i // tiles_per_seq, 0, 0))
    ins = [x3.reshape(M, K), shift0.reshape(B, 1, K), mu, w_main, *lora1, *lora2, *bias]
    specs = ([row, shift, full(mu), pl.BlockSpec((K, tn), lambda i, j: (0, j))] + [full(w) for w in lora1]
             + [col(w) for w in lora2] + [col(b) for b in bias])
    scratch = ([pltpu.VMEM((3, tm, K), BF16)] + [pltpu.VMEM((tm, w.shape[1]), BF16) for w in lora1]
               + [pltpu.VMEM((SUBLANES, K), F32)])
    if vgate is not None:
        v1, v2, v0, v_first = vgate
        ins += [v1, v2, v0, v_first]
        specs += [full(v1), col(v2), col(v0), pl.BlockSpec((tm, tn), lambda i, j: (i, j % nt))]
        scratch.append(pltpu.VMEM((tm, v1.shape[1]), BF16))
    out = lambda grp: pl.BlockSpec((tm, tn), lambda i, j: (i, jnp.clip(j - grp * nt, 0, nt - 1)))
    return pl.pallas_call(
        functools.partial(_rwkv_proj_kernel, has_vgate=vgate is not None, seq_len=T, tiles_per_seq=tiles_per_seq),
        grid=(M // tm, 3 * nt),
        in_specs=specs, out_specs=[out(0), out(1), out(2), out(0), out(1), out(2)],
        out_shape=[jax.ShapeDtypeStruct((M, D_MODEL), dt) for dt in (BF16, BF16, BF16, F32, BF16, BF16)],
        scratch_shapes=scratch, compiler_params=_params(2), name=name,
    )(*ins)


def _wkv_chunks(r, lw, k, v, a, prm, S, consts):
    C = WKV_CHUNK
    bdmask, l_incl, strict, incl, eye = consts

    def bd(xp):
        return jnp.concatenate([xp.astype(BF16)] * WKV_HEADS, axis=0) * bdmask

    def headsum(xp):
        hi, lo = _split(xp)
        return _dg(hi, bdmask, _NN) + _dg(lo, bdmask, _NN)

    def each(f, *lists):
        return [f(*args) for args in zip(*lists)]

    kkp, kap, rkp, lng, lnb = (list(t) for t in zip(*prm))
    kkn = each(lambda k_, p_: k_ * p_, k, kkp)
    kp = each(lambda k_, a_, p_: k_ * (1.0 + (a_ - 1.0) * p_), k, a, kap)
    sums = each(lambda kkn_, r_, kp_, p_: headsum(jnp.concatenate([kkn_ * kkn_, r_ * kp_ * p_], axis=0)),
                kkn, r, kp, rkp)
    kk = each(lambda kkn_, s_: kkn_ / jnp.maximum(jnp.sqrt(s_[:C]), 1e-12), kkn, sums)
    bonus = each(lambda s_, v_: s_[C:] * v_, sums, v)
    b = each(lambda kk_, a_: kk_ * a_, kk, a)
    cum = each(lambda lw_: _dot_exact_lhs(l_incl, lw_), lw)
    e_neg = each(lambda c_: jnp.exp(-c_), cum)
    x = each(lambda kk_, c_, lw_, r_: jnp.concatenate([kk_ * jnp.exp(c_ - lw_), r_ * jnp.exp(c_)],
                                                       axis=0).astype(BF16), kk, cum, lw, r)
    g_b = each(lambda x_, b_, e_: _dg(x_, bd(b_ * e_), _NT), x, b, e_neg)
    g_k = each(lambda x_, k_, e_: _dg(x_, bd(k_ * e_), _NT), x, kp, e_neg)
    xs = each(lambda x_, s_: _dg(x_, s_.astype(BF16), _NT), x, S)
    mb = each(lambda g_: jnp.where(incl, g_[C:], 0.0), g_b)
    kv = each(lambda g_, v_: _dg(jnp.concatenate([jnp.where(strict, g_[:C], 0.0), jnp.where(incl, g_[C:], 0.0)],
                                                 axis=0).astype(BF16), bd(v_), _NN), g_k, v)
    rhs = each(lambda xs_, kv_: -(xs_[:C] + kv_[:C]), xs, kv)
    npow = each(lambda g_: -jnp.where(strict, g_[:C], 0.0), g_b)
    tinv = each(lambda n_: eye + n_, npow)
    npow = each(lambda n_: _dg(n_.astype(BF16), bd(n_), _NN), npow)
    for _ in range(int(math.log2(C)) - 2):
        both = each(lambda t_, n_: _dg(jnp.concatenate([t_, n_], axis=0).astype(BF16), bd(n_), _NN), tinv, npow)
        tinv = each(lambda t_, b_: t_ + b_[:C], tinv, both)
        npow = each(lambda b_: b_[C:], both)
    tinv = each(lambda t_, n_: t_ + _dg(t_.astype(BF16), bd(n_), _NN), tinv, npow)
    u = each(lambda t_, r_: _dg(t_.astype(BF16), bd(r_), _NN), tinv, rhs)
    y = each(lambda xs_, kv_, mb_, u_: xs_[C:] + kv_[C:] + _dg(mb_.astype(BF16), bd(u_), _NN), xs, kv, mb, u)
    upd = each(lambda u_, v_, b_, k_, c_: _dg(
        jnp.concatenate([u_, v_], axis=0).astype(BF16),
        (jnp.concatenate([b_, k_], axis=0) * jnp.exp(c_[C - 1:C, :] - jnp.concatenate([c_, c_], axis=0))).astype(BF16),
        _TN), u, v, b, kp, cum)
    bdmask_f = bdmask.astype(F32)
    s_new = each(lambda s_, c_, u_: s_ * jnp.exp(c_[C - 1:C, :]) + u_ * bdmask_f, S, cum, upd)
    mu = each(lambda y_: _dg(y_.astype(BF16), bdmask, _NN) * (1.0 / RWKV_N), y)
    yc = each(lambda y_, m_: y_ - m_, y, mu)
    var = each(lambda yc_: _dg((yc_ * yc_).astype(BF16), bdmask, _NN) * (1.0 / RWKV_N), yc)
    z = each(lambda yc_, var_, g_, b_, bo_: yc_ * lax.rsqrt(var_ + RWKV_GN_EPS) * g_ + b_ + bo_,
             yc, var, lng, lnb, bonus)
    return z, s_new


def _wkv_kernel(r_ref, lw_ref, k_ref, v_ref, a_ref, kk_ref, ka_ref, rk_ref, g_ref, b_ref, s0_ref,
                z_ref, so_ref, s_scr, *, n_t, nb, n_grp):
    t = pl.program_id(1)
    C, N, HW = WKV_CHUNK, RWKV_N, WKV_HEADS * RWKV_N
    ri = lax.broadcasted_iota(jnp.int32, (HW, HW), 0) // N
    ci = lax.broadcasted_iota(jnp.int32, (HW, HW), 1) // N
    bdmask_f = (ri == ci).astype(F32)

    chains = [(bi, g) for bi in range(nb) for g in range(n_grp)]

    @pl.when(t == 0)
    def _():
        for n, (bi, g) in enumerate(chains):
            rows = s0_ref[bi, g * WKV_HEADS:(g + 1) * WKV_HEADS].reshape(HW, N)
            s_scr[n] = jnp.concatenate([rows] * WKV_HEADS, axis=1) * bdmask_f

    tt = lax.broadcasted_iota(jnp.int32, (C, C), 0)
    ss = lax.broadcasted_iota(jnp.int32, (C, C), 1)
    tp = lax.broadcasted_iota(jnp.int32, (C, HW), 0)
    sp = lax.broadcasted_iota(jnp.int32, (C, HW), 1) % N
    consts = (bdmask_f.astype(BF16), (tt >= ss).astype(BF16), tp > sp, tp >= sp, (tp == sp).astype(F32))
    sls = [slice(g * HW, (g + 1) * HW) for _, g in chains]
    tok = lambda ref: [ref[bi, :, sl].astype(F32) for (bi, _), sl in zip(chains, sls)]
    prm = [tuple(ref[:, sl] for ref in (kk_ref, ka_ref, rk_ref, g_ref, b_ref)) for sl in sls]
    zs, s_news = _wkv_chunks(tok(r_ref), tok(lw_ref), tok(k_ref), tok(v_ref), tok(a_ref), prm,
                             [s_scr[n] for n in range(len(chains))], consts)
    for n, ((bi, _), sl) in enumerate(zip(chains, sls)):
        s_scr[n] = s_news[n]
        z_ref[bi, :, sl] = zs[n]

    @pl.when(t == n_t - 1)
    def _():
        for n, (bi, g) in enumerate(chains):
            for h in range(WKV_HEADS):
                so_ref[bi, g * WKV_HEADS + h] = s_scr[n, h * N:(h + 1) * N, h * N:(h + 1) * N]


def _wkv(name, r, lw, k, v, a, kkp, kap, rkp, lng, lnb, s0):
    B, T, D = r.shape
    C, HW = WKV_CHUNK, WKV_HEADS * RWKV_N
    n_grp = min(max(1, WKV_CHAINS // B), D // HW)
    W = n_grp * HW
    assert T % C == 0 and D % W == 0
    n_t = T // C
    tok = pl.BlockSpec((B, C, W), lambda h, t: (0, t, h))
    vec = pl.BlockSpec((1, W), lambda h, t: (0, h))
    st = pl.BlockSpec((B, n_grp * WKV_HEADS, RWKV_N, RWKV_N), lambda h, t: (0, h, 0, 0))
    return pl.pallas_call(
        functools.partial(_wkv_kernel, n_t=n_t, nb=B, n_grp=n_grp), grid=(D // W, n_t),
        in_specs=[tok] * 5 + [vec] * 5 + [st], out_specs=[tok, st],
        out_shape=[jax.ShapeDtypeStruct((B, T, D), F32), jax.ShapeDtypeStruct(s0.shape, F32)],
        scratch_shapes=[pltpu.VMEM((B * n_grp, HW, HW), F32)], compiler_params=_params(2), name=name,
    )(r, lw, k, v, a, kkp, kap, rkp, lng, lnb, s0)


def _cmul(ar, ai, br, bi):
    return ar * br - ai * bi, ar * bi + ai * br


def _dot3w(a, w_hi, w_lo, dims=_NN):
    ah, al = _split(a)
    return _dg(ah, w_hi, dims) + (_dg(al, w_hi, dims) + _dg(ah, w_lo, dims))


S5_SHIFTS = (1, 2, 4)


def _s5_kernel(x_ref, wbh_ref, wbl_ref, wc_ref, lr_ref, li_ref, dt_ref, d_ref, h0r_ref, h0i_ref,
               z_ref, hr_ref, hi_ref, pw_re, pw_im, sh_re, sh_im, cf, hc, *, n_t, tt):
    t = pl.program_id(2)
    R = SUBLANES

    @pl.when(t == 0)
    def _():
        hc[0:1, :] = h0r_ref[0]
        hc[1:2, :] = h0i_ref[0]

    @pl.when(jnp.logical_and(t == 0, pl.program_id(1) == 0))
    def _():
        lr, li, dt = lr_ref[0], li_ref[0], dt_ref[0]
        rid = lax.broadcasted_iota(jnp.int32, (R, S5_STATES), 0)
        n = (rid + 1).astype(F32)
        mag = jnp.exp(lr * dt * n)
        ang = li * dt * n
        pw_re[...] = mag * jnp.cos(ang)
        pw_im[...] = mag * jnp.sin(ang)
        for m, sh in enumerate(S5_SHIFTS):
            sh_re[m] = jnp.where(rid >= sh, pw_re[sh - 1:sh, :], 0.0)
            sh_im[m] = jnp.where(rid >= sh, pw_im[sh - 1:sh, :], 0.0)
        ar, ai = pw_re[0:1, :], pw_im[0:1, :]
        den = lr * lr + li * li
        cf[0:1, :] = ((ar - 1.0) * lr + ai * li) / den
        cf[1:2, :] = (ai * lr - (ar - 1.0) * li) / den

    x = x_ref[0]
    bu = _dot3w(x, wbh_ref[0], wbl_ref[0])
    cr, ci = cf[0:1, :], cf[1:2, :]
    br, bi = _cmul(cr, ci, bu[:, :S5_STATES], bu[:, S5_STATES:])
    pr, pi = pw_re[...], pw_im[...]
    hr, hi = hc[0:1, :], hc[1:2, :]
    xrs, xis = [], []
    for i in range(tt // R):
        vr, vi = br[i * R:(i + 1) * R, :], bi[i * R:(i + 1) * R, :]
        for m, sh in enumerate(S5_SHIFTS):
            ur, ui = _cmul(sh_re[m], sh_im[m], pltpu.roll(vr, sh, axis=0), pltpu.roll(vi, sh, axis=0))
            vr, vi = vr + ur, vi + ui
        ur, ui = _cmul(pr, pi, hr, hi)
        vr, vi = vr + ur, vi + ui
        xrs.append(vr)
        xis.append(vi)
        hr, hi = vr[R - 1:R, :], vi[R - 1:R, :]
    hc[0:1, :] = hr
    hc[1:2, :] = hi
    xcat = jnp.concatenate([jnp.concatenate(xrs, axis=0), jnp.concatenate(xis, axis=0)], axis=-1)
    y = _dg(xcat.astype(BF16), wc_ref[0], _NN) + d_ref[...] * x
    z_ref[0] = 0.5 * y * (1.0 + jnp.tanh(math.sqrt(2.0 / math.pi) * (y + 0.044715 * (y * y * y))))

    @pl.when(t == n_t - 1)
    def _():
        hr_ref[0] = hr
        hi_ref[0] = hi


def _s5(name, x3, wb, wc, lr, li, dt, d, h0r, h0i):
    B, T, D = x3.shape
    tt = min(512, T)
    assert T % tt == 0 and tt % SUBLANES == 0
    n_t = T // tt
    n_s = D // S5_SLAB
    S = S5_STATES
    st = pl.BlockSpec((1, 1, S), lambda s, b, t: (b, 0, s))
    pv = pl.BlockSpec((1, 1, S), lambda s, b, t: (0, 0, s))
    w_in = pl.BlockSpec((1, S5_SLAB, 2 * S), lambda s, b, t: (s, 0, 0))
    w_out = pl.BlockSpec((1, 2 * S, S5_SLAB), lambda s, b, t: (s, 0, 0))
    split = lambda w: (w.astype(BF16), (w - w.astype(BF16).astype(F32)).astype(BF16))
    return pl.pallas_call(
        functools.partial(_s5_kernel, n_t=n_t, tt=tt), grid=(n_s, B, n_t),
        in_specs=[pl.BlockSpec((1, tt, S5_SLAB), lambda s, b, t: (b, t, s)),
                  w_in, w_in, w_out,
                  pv, pv, pv,
                  pl.BlockSpec((1, S5_SLAB), lambda s, b, t: (0, s)),
                  st, st],
        out_specs=[pl.BlockSpec((1, tt, S5_SLAB), lambda s, b, t: (b, t, s)), st, st],
        out_shape=[jax.ShapeDtypeStruct((B, T, D), F32),
                   jax.ShapeDtypeStruct((B, 1, n_s * S), F32), jax.ShapeDtypeStruct((B, 1, n_s * S), F32)],
        scratch_shapes=[pltpu.VMEM((SUBLANES, S), F32), pltpu.VMEM((SUBLANES, S), F32),
                        pltpu.VMEM((len(S5_SHIFTS), SUBLANES, S), F32),
                        pltpu.VMEM((len(S5_SHIFTS), SUBLANES, S), F32),
                        pltpu.VMEM((SUBLANES, S), F32), pltpu.VMEM((SUBLANES, S), F32)],
        compiler_params=_params(3), name=name,
    )(x3, *split(wb), wc.astype(BF16), lr, li, dt, d, h0r, h0i)


def _select_topk(gate, idx, valid, axis):
    g = jnp.where(valid, gate, NEG)
    big = jnp.int32(1 << 30)
    sel = jnp.zeros(gate.shape, jnp.bool_)
    for _ in range(MOBA_TOPK):
        m = jnp.max(g, axis=axis, keepdims=True)
        first = jnp.min(jnp.where(g == m, idx, big), axis=axis, keepdims=True)
        pick = idx == first
        sel = jnp.logical_or(sel, pick)
        g = jnp.where(pick, -jnp.inf, g)
    return jnp.logical_and(sel, valid)


def _moba_prompt_kernel(qa_ref, qb_ref, k_ref, v_ref, sl_ref, oa_ref, ob_ref, kmean_scr, *, n_blk):
    i = pl.program_id(2)
    BK = MOBA_BLOCK

    @pl.when(i == 0)
    def _():
        kmean_scr[...] = jnp.concatenate([jnp.sum(k_ref[0, n * BK:(n + 1) * BK, :], axis=0, keepdims=True)
                                          for n in range(n_blk)], axis=0) * (1.0 / BK)

    slope = sl_ref[0][:, 0:1]
    qi = lax.broadcasted_iota(jnp.int32, (BK, BK), 0)
    ki = lax.broadcasted_iota(jnp.int32, (BK, BK), 1)
    causal = qi >= ki
    bias0 = slope * (qi - ki).astype(F32)
    scale = MOBA_DH ** -0.5
    add = lambda x, y: x + y

    def select(c, q):
        if c == 0:
            return None
        gate_t = _dot3(kmean_scr[...], q, _NT)
        nidx = lax.broadcasted_iota(jnp.int32, (n_blk, BK), 0)
        sel_t = jnp.where(_select_topk(gate_t, nidx, nidx < c, axis=0), 1.0, 0.0).astype(BF16)
        return _dg(jnp.where(qi == ki, 1.0, 0.0).astype(BF16), sel_t, _NT) > 0.5

    def scores(c, qb, sel):
        ss = []
        for n in range(c + 1):
            s = _dg(qb, k_ref[0, n * BK:(n + 1) * BK, :].astype(BF16), _NT) * scale
            s = s - (bias0 + slope * float((c - n) * BK))
            ss.append(jnp.where(causal if n == c else sel[:, n:n + 1], s, NEG))
        return ss

    def pair(cs):
        qs = [qa_ref[0], qb_ref[0]]
        sels = [select(c, q) for c, q in zip(cs, qs)]
        sss = [scores(c, q.astype(BF16), sel) for c, q, sel in zip(cs, qs, sels)]
        ms = [jnp.max(functools.reduce(jnp.maximum, ss), axis=-1, keepdims=True) for ss in sss]
        pss = [[jnp.exp(s - m) for s in ss] for ss, m in zip(sss, ms)]
        ls = [jnp.sum(functools.reduce(add, ps), axis=-1, keepdims=True) for ps in pss]
        accs = [functools.reduce(add, [_dg(p.astype(BF16), v_ref[0, n * BK:(n + 1) * BK, :].astype(BF16), _NN)
                                       for n, p in enumerate(ps)]) for ps in pss]
        for o_ref, acc, l in zip((oa_ref, ob_ref), accs, ls):
            o_ref[0] = acc / l

    for c in range(n_blk // 2):
        pl.when(i == c)(functools.partial(pair, (c, n_blk - 1 - c)))


def _moba_prompt(name, q, k, v, slopes):
    B, T, D = q.shape
    n_blk = T // MOBA_BLOCK
    assert T % MOBA_BLOCK == 0 and n_blk % 2 == 0
    q_lo = pl.BlockSpec((1, MOBA_BLOCK, MOBA_DH), lambda b, h, i: (b, i, h))
    q_hi = pl.BlockSpec((1, MOBA_BLOCK, MOBA_DH), lambda b, h, i: (b, n_blk - 1 - i, h))
    kv = pl.BlockSpec((1, T, MOBA_DH), lambda b, h, i: (b, 0, h))
    o_lo, o_hi = pl.pallas_call(
        functools.partial(_moba_prompt_kernel, n_blk=n_blk), grid=(B, MOBA_H, n_blk // 2),
        in_specs=[q_lo, q_hi, kv, kv, pl.BlockSpec((1, 1, LANES), lambda b, h, i: (h, 0, 0))],
        out_specs=[q_lo, pl.BlockSpec((1, MOBA_BLOCK, MOBA_DH), lambda b, h, i: (b, n_blk // 2 - 1 - i, h))],
        out_shape=[jax.ShapeDtypeStruct((B, T // 2, D), F32)] * 2,
        scratch_shapes=[pltpu.VMEM((n_blk, MOBA_DH), F32)],
        compiler_params=_params(3), name=name,
    )(q, q, k, v, slopes)
    return jnp.concatenate([o_lo, o_hi], axis=1)


def _q_rows(q, nq):
    return jnp.concatenate([q[:, h * MOBA_DH:(h + 1) * MOBA_DH] for h in range(MOBA_H)], axis=0)


def _head_match(n_rows, nq):
    shape = (n_rows * MOBA_H, MOBA_H * nq)
    return lax.broadcasted_iota(jnp.int32, shape, 0) % MOBA_H == lax.broadcasted_iota(jnp.int32, shape, 1) // nq


def _token_scores(k3, qr, nq, dot):
    n_rows = k3.shape[0]
    s2 = dot(k3.reshape(n_rows * MOBA_H, MOBA_DH), qr, _NT)
    s2 = jnp.where(_head_match(n_rows, nq), s2, 0.0)
    return jnp.sum(s2.reshape(n_rows, MOBA_H, MOBA_H * nq), axis=1)


def _moba_scores_kernel(pt_ref, q_ref, m_ref, *refs, nq, pp):
    k_refs, (s_ref, ks_ref) = refs[:pp], refs[pp:]
    qr = _q_rows(q_ref[0], nq).astype(BF16)
    kps = [k_ref[0, 0] for k_ref in k_refs]
    s2s = [_dg(kp.reshape(PAGE_SIZE * MOBA_H, MOBA_DH).astype(BF16), qr, _NT) for kp in kps]
    for u in range(pp):
        s_ref[0, u] = jnp.sum((s2s[u] * m_ref[...]).reshape(PAGE_SIZE, MOBA_H, MOBA_H * nq), axis=1)
        ks_ref[0, u] = jnp.sum(kps[u], axis=0)


def _page_specs(layer, pp):
    return [pl.BlockSpec((1, 1, PAGE_SIZE, MOBA_H, MOBA_DH),
                         lambda b, g, pt, u=u: (layer, pt[b, g * pp + u], 0, 0, 0)) for u in range(pp)]


def _moba_scores(name, page_table, q, cache_k, layer, match_f):
    B, nq, D = q.shape
    n_pg = page_table.shape[1]
    pp = MOBA_PAGES_PER_STEP
    assert n_pg % pp == 0
    gs = pltpu.PrefetchScalarGridSpec(
        num_scalar_prefetch=1, grid=(B, n_pg // pp),
        in_specs=[pl.BlockSpec((1, nq, D), lambda b, g, pt: (b, 0, 0)),
                  pl.BlockSpec(match_f.shape, lambda b, g, pt: (0, 0))] + _page_specs(layer, pp),
        out_specs=[pl.BlockSpec((1, pp, PAGE_SIZE, MOBA_H * nq), lambda b, g, pt: (b, g, 0, 0)),
                   pl.BlockSpec((1, pp, MOBA_H, MOBA_DH), lambda b, g, pt: (b, g, 0, 0))])
    return pl.pallas_call(
        functools.partial(_moba_scores_kernel, nq=nq, pp=pp), grid_spec=gs,
        out_shape=[jax.ShapeDtypeStruct((B, n_pg, PAGE_SIZE, MOBA_H * nq), F32),
                   jax.ShapeDtypeStruct((B, n_pg, MOBA_H, MOBA_DH), F32)],
        compiler_params=_params(2), name=name,
    )(page_table, q, match_f, *([cache_k] * pp))


def _moba_probs_kernel(s_ref, ks_ref, q_ref, kn_ref, sl_ref, p_ref, pn_ref, sel_scr, *, nq, n_blk, past):
    BK = MOBA_BLOCK
    HQ = MOBA_H * nq
    qr = _q_rows(q_ref[0], nq)
    ppb = BK // PAGE_SIZE
    kmean = jnp.sum(ks_ref[0].reshape(n_blk, ppb, MOBA_H, MOBA_DH), axis=1) * (1.0 / BK)
    gate = _token_scores(kmean, qr, nq, _dot3)
    nidx = lax.broadcasted_iota(jnp.int32, (n_blk, HQ), 0)
    own = past // BK
    sel = _select_topk(gate, nidx, nidx < own, axis=0)
    sel_scr[...] = sel.astype(F32)
    slope = sl_ref[...]
    scale = MOBA_DH ** -0.5
    qpos = past + lax.broadcasted_iota(jnp.int32, (1, HQ), 1) % nq
    tok = lax.broadcasted_iota(jnp.int32, (BK, HQ), 0)

    def scores(n):
        s = s_ref[0, n] * scale - slope * (qpos - (n * BK + tok)).astype(F32)
        return jnp.where(sel_scr[pl.ds(n, 1), :] > 0.5, s, NEG)

    sn = _token_scores(kn_ref[0], qr, nq, _dot1) * scale
    tn = lax.broadcasted_iota(jnp.int32, (PAGE_SIZE, HQ), 0)
    dn = qpos - (past + tn)
    sn = jnp.where(jnp.logical_and(dn >= 0, tn < nq), sn - slope * dn.astype(F32), NEG)
    m = lax.fori_loop(0, n_blk, lambda n, m_: jnp.maximum(m_, jnp.max(scores(n), axis=0, keepdims=True)),
                      jnp.max(sn, axis=0, keepdims=True))
    pn = jnp.exp(sn - m)
    l = lax.fori_loop(0, n_blk, lambda n, l_: l_ + jnp.sum(jnp.exp(scores(n) - m), axis=0, keepdims=True),
                      jnp.sum(pn, axis=0, keepdims=True))
    pn_ref[0] = (pn / l).astype(pn_ref.dtype)

    def write(n, c):
        p_ref[0, n] = (jnp.exp(scores(n) - m) / l).astype(p_ref.dtype)
        return c

    lax.fori_loop(0, n_blk, write, 0)


def _moba_probs(name, s, ksum, q, k_new_pad, slopes_hq, past):
    B, n_blk, BK, HQ = s.shape
    nq = q.shape[1]
    blk = pl.BlockSpec((1, n_blk, BK, HQ), lambda b: (b, 0, 0, 0))
    return pl.pallas_call(
        functools.partial(_moba_probs_kernel, nq=nq, n_blk=n_blk, past=past), grid=(B,),
        in_specs=[blk,
                  pl.BlockSpec((1, ksum.shape[1], MOBA_H, MOBA_DH), lambda b: (b, 0, 0, 0)),
                  pl.BlockSpec((1, nq, D_MODEL), lambda b: (b, 0, 0)),
                  pl.BlockSpec((1, PAGE_SIZE, MOBA_H, MOBA_DH), lambda b: (b, 0, 0, 0)),
                  pl.BlockSpec((1, HQ), lambda b: (0, 0))],
        out_specs=[blk, pl.BlockSpec((1, PAGE_SIZE, HQ), lambda b: (b, 0, 0))],
        out_shape=[jax.ShapeDtypeStruct((B, n_blk, BK, HQ), BF16), jax.ShapeDtypeStruct((B, PAGE_SIZE, HQ), BF16)],
        scratch_shapes=[pltpu.VMEM((n_blk, HQ), F32)],
        compiler_params=_params(1), name=name,
    )(s, ksum, q, k_new_pad, slopes_hq)


def _moba_pv_kernel(pt_ref, p_ref, m_ref, pn_ref, vn_ref, *refs, nq, n_steps, pp):
    v_refs, (o_ref, acc) = refs[:pp], refs[pp:]
    g = pl.program_id(1)
    HQ = MOBA_H * nq

    def spread(p):
        p2 = jnp.broadcast_to(p.astype(F32)[:, None, :], (PAGE_SIZE, MOBA_H, HQ))
        return (p2.reshape(PAGE_SIZE * MOBA_H, HQ) * m_ref[...]).astype(BF16)

    def values(v3):
        return v3.reshape(PAGE_SIZE * MOBA_H, MOBA_DH).astype(BF16)

    @pl.when(g == 0)
    def _():
        acc[...] = _dg(spread(pn_ref[0]), values(vn_ref[0]), _TN)

    p2s = [spread(p_ref[0, u]) for u in range(pp)]
    v2s = [values(v_ref[0, 0]) for v_ref in v_refs]
    acc[...] += functools.reduce(lambda x, y: x + y, [_dg(p2, v2, _TN) for p2, v2 in zip(p2s, v2s)])

    @pl.when(g == n_steps - 1)
    def _():
        a = acc[...]
        o_ref[0] = jnp.concatenate([a[h * nq:(h + 1) * nq, :] for h in range(MOBA_H)], axis=-1)


def _moba_pv(name, page_table, p, cache_v, pn, v_new_pad, nq, layer, match_f):
    B, n_pg, _, HQ = p.shape
    D = D_MODEL
    pp = MOBA_PAGES_PER_STEP
    assert n_pg % pp == 0
    gs = pltpu.PrefetchScalarGridSpec(
        num_scalar_prefetch=1, grid=(B, n_pg // pp),
        in_specs=[pl.BlockSpec((1, pp, PAGE_SIZE, HQ), lambda b, g, pt: (b, g, 0, 0)),
                  pl.BlockSpec(match_f.shape, lambda b, g, pt: (0, 0)),
                  pl.BlockSpec((1, PAGE_SIZE, HQ), lambda b, g, pt: (b, 0, 0)),
                  pl.BlockSpec((1, PAGE_SIZE, MOBA_H, MOBA_DH), lambda b, g, pt: (b, 0, 0, 0))]
        + _page_specs(layer, pp),
        out_specs=pl.BlockSpec((1, nq, D), lambda b, g, pt: (b, 0, 0)),
        scratch_shapes=[pltpu.VMEM((HQ, MOBA_DH), F32)])
    return pl.pallas_call(
        functools.partial(_moba_pv_kernel, nq=nq, n_steps=n_pg // pp, pp=pp), grid_spec=gs,
        out_shape=jax.ShapeDtypeStruct((B, nq, D), F32), compiler_params=_params(2), name=name,
    )(page_table, p, match_f, pn, v_new_pad, *([cache_v] * pp))


def _ident(x):
    return x


def _mix(x, xp, mu):
    return x + (xp - x) * mu


def _pad_cols(w, n):
    return jnp.pad(w, ((0, 0), (0, n - w.shape[1])))


def _pad_rows(w, n):
    return jnp.pad(w, ((0, n - w.shape[0]), (0, 0)))


def _rwkv_layer(tag, x3, shift0, s0, v_first, P, j, ln_g, ln_b):
    B, T, D = x3.shape
    M = B * T
    x = x3.reshape(M, D)
    wb = lambda w: w.astype(BF16)
    lp = LANES
    w_rkv = P['rwkv_w_rkv'][j]
    w_main = jnp.concatenate([wb(w_rkv[0]), wb(w_rkv[1]), wb(w_rkv[2])], axis=1)
    lora1 = [wb(_pad_cols(P['rwkv_w1'][j], lp)), wb(_pad_cols(P['rwkv_a1'][j], lp)), wb(P['rwkv_g1'][j])]
    lora2 = [wb(_pad_rows(P['rwkv_w2'][j], lp)), wb(_pad_rows(P['rwkv_a2'][j], lp)), wb(P['rwkv_g2'][j])]
    bias = [P['rwkv_w0'][j][None, :], P['rwkv_a0'][j][None, :]]
    vgate = None
    if j > 0:
        vgate = (wb(_pad_cols(P['rwkv_v1'][j - 1], lp)), wb(_pad_rows(P['rwkv_v2'][j - 1], lp)),
                 P['rwkv_v0'][j - 1][None, :], v_first)
    r, k, v, lw, a, g = _rwkv_proj(tag + 'proj', x3, shift0, P['rwkv_mu'][j], w_main, lora1, lora2, bias, vgate)
    if j == 0:
        v_first = v
    Tp = -(-T // WKV_CHUNK) * WKV_CHUNK
    seq = lambda t: jnp.pad(t.reshape(B, T, D), ((0, 0), (0, Tp - T), (0, 0)))
    vecp = lambda name: P[name][j].reshape(1, D)
    z, s_new = _wkv(tag + 'wkv', seq(r), seq(lw), seq(k), seq(v), seq(a), vecp('rwkv_k_k'), vecp('rwkv_k_a'),
                    vecp('rwkv_r_k'), vecp('rwkv_lnx_g'), vecp('rwkv_lnx_b'), s0)
    z = z[:, :T].reshape(M, D)
    xn = _mm_ln(tag + 'wo', [z, g], wb(P['rwkv_wo'][j]), x, ln_g, ln_b, lambda z_, g_: z_ * g_)
    return xn, s_new, x3[:, -1], v_first


def _s5_weights(P, j):
    ns, gs = D_MODEL // S5_SLAB, S5_SLAB // S5_GROUP
    eye = jnp.eye(gs, dtype=F32)

    def bd_in(b):
        bt = jnp.swapaxes(b.reshape(ns, gs, S5_P, S5_GROUP), 2, 3)
        return jnp.einsum('sgcp,gh->sgchp', bt, eye).reshape(ns, S5_SLAB, S5_STATES)

    def bd_out(c):
        ct = jnp.swapaxes(c.reshape(ns, gs, S5_GROUP, S5_P), 2, 3)
        return jnp.einsum('sgpc,gh->sgphc', ct, eye).reshape(ns, S5_STATES, S5_SLAB)

    wb = jnp.concatenate([bd_in(P['s5_b_re'][j]), bd_in(P['s5_b_im'][j])], axis=2)
    wc = jnp.concatenate([bd_out(P['s5_c_re'][j]), -bd_out(P['s5_c_im'][j])], axis=1)
    flat = lambda t: t.reshape(1, 1, S5_G * S5_P)
    dt = jnp.broadcast_to(jnp.exp(P['s5_log_dt'][j])[:, None], (S5_G, S5_P))
    return wb, wc, flat(P['s5_lam_re'][j]), flat(P['s5_lam_im'][j]), flat(dt)


def _s5_layer(tag, x3, h0r, h0i, P, j, ln_g, ln_b):
    B, T, D = x3.shape
    M = B * T
    x = x3.reshape(M, D)
    wb, wc, lr, li, dt = _s5_weights(P, j)
    st = lambda h: h.reshape(B, 1, S5_G * S5_P)
    z, hr, hi = _s5(tag + 'scan', x3, wb, wc, lr, li, dt, P['s5_d'][j][None, :], st(h0r), st(h0i))
    out, = _mm(tag + 'gate', [z.reshape(M, D)], [], [P['s5_w_val'][j].astype(BF16), P['s5_w_gate'][j].astype(BF16)],
               [], [], _ident, lambda val, gate: (val * _sigmoid(gate),), [F32])
    xn = _ln(tag + 'ln', x, out, ln_g, ln_b)
    return xn, hr.reshape(B, S5_G, S5_P), hi.reshape(B, S5_G, S5_P)


def _alibi_slopes():
    return 2.0 ** (-8.0 * jnp.arange(1, MOBA_H + 1, dtype=F32) / MOBA_H)


def _moba_layer(tag, x3, cache_k, cache_v, page_table, P, j, ln_g, ln_b):
    B, T, D = x3.shape
    M = B * T
    x = x3.reshape(M, D)
    wqkv = P['moba_w_qkv'][j]
    ws = [wqkv[:, n * D:(n + 1) * D].astype(BF16) for n in range(3)]
    q, k, v = _mm(tag + 'qkv', [x], [], ws, [], [], _ident, lambda a, b, c: (a, b, c), [F32, F32, F32])
    q3, k3, v3 = (t.reshape(B, T, D) for t in (q, k, v))
    slopes = _alibi_slopes()
    if cache_k is None:
        sl = jnp.broadcast_to(slopes[:, None, None], (MOBA_H, 1, LANES))
        o = _moba_prompt(tag + 'attn', q3, k3, v3, sl)
    else:
        n_pg = page_table.shape[1]
        past = n_pg * PAGE_SIZE
        assert past % MOBA_BLOCK == 0 and T <= PAGE_SIZE and T <= MOBA_BLOCK
        HQ = MOBA_H * T
        match_f = _head_match(PAGE_SIZE, T).astype(F32)
        s, ksum = _moba_scores(tag + 'scores', page_table, q3, cache_k, j, match_f)
        sl = jnp.repeat(slopes, T)[None, :]
        pad = lambda t: jnp.pad(t.reshape(B, T, MOBA_H, MOBA_DH), ((0, 0), (0, PAGE_SIZE - T), (0, 0), (0, 0)))
        p, pn = _moba_probs(tag + 'probs', s.reshape(B, past // MOBA_BLOCK, MOBA_BLOCK, HQ), ksum, q3, pad(k3),
                            sl, past)
        o = _moba_pv(tag + 'pv', page_table, p.reshape(B, n_pg, PAGE_SIZE, HQ), cache_v, pn, pad(v3), T, j, match_f)
    xn = _mm_ln(tag + 'wo', [o.reshape(M, D)], P['moba_wo'][j].astype(BF16), x, ln_g, ln_b, _ident)
    return xn, k3.reshape(B, T, MOBA_H, MOBA_DH), v3.reshape(B, T, MOBA_H, MOBA_DH)


def _ffn_layer(tag, x3, conv0, P, i, ln_g, ln_b):
    B, T, D = x3.shape
    w_in = P['ffn_w_in'][i]
    h, cst = _ffn_in(tag + 'in', x3, w_in[:, :D_FF].astype(BF16), w_in[:, D_FF:].astype(BF16), conv0,
                     P['ffn_conv_w'][i], P['ffn_conv_b'][i][None, :])
    xn = _mm_ln(tag + 'down', [h], P['ffn_w_down'][i].astype(BF16), x3.reshape(B * T, D), ln_g, ln_b, _ident)
    return xn.reshape(B, T, D), cst


def _trunk(grp, x3, wkv0, shift0, s5re0, s5im0, conv0, cache_k, cache_v, page_table, P):
    B, T, D = x3.shape
    wkv_new, shift_new, s5re_new, s5im_new, k_new, v_new, conv_new = [], [], [], [], [], [], []
    v_first = None
    for i in range(DEPTH):
        kind, j = i % 3, i // 3
        tag = f'{grp}{i}_'
        g0, b0 = P['ln_g'][i, 0][None, :], P['ln_b'][i, 0][None, :]
        g1, b1 = P['ln_g'][i, 1][None, :], P['ln_b'][i, 1][None, :]
        if kind == 0:
            xn, S, last, v_first = _rwkv_layer(tag, x3, shift0[j], wkv0[j], v_first, P, j, g0, b0)
            wkv_new.append(S)
            shift_new.append(last)
        elif kind == 1:
            xn, hr, hi = _s5_layer(tag, x3, s5re0[j], s5im0[j], P, j, g0, b0)
            s5re_new.append(hr)
            s5im_new.append(hi)
        else:
            xn, kr, vr = _moba_layer(tag, x3, cache_k, cache_v, page_table, P, j, g0, b0)
            k_new.append(kr)
            v_new.append(vr)
        x3, cst = _ffn_layer(tag + 'ffn_', xn.reshape(B, T, D), conv0[i], P, i, g1, b1)
        conv_new.append(cst)
    return (x3, jnp.stack(wkv_new), jnp.stack(shift_new), jnp.stack(s5re_new), jnp.stack(s5im_new),
            jnp.stack(conv_new), jnp.stack(k_new), jnp.stack(v_new))


def kernel(x_prompt, x_sample, state_rwkv_wkv, state_rwkv_shift, state_s5_re, state_s5_im, state_ffn_conv, cache_k, cache_v, page_table, ln_g, ln_b, rwkv_mu, rwkv_w_rkv, rwkv_w0, rwkv_w1, rwkv_w2, rwkv_a0, rwkv_a1, rwkv_a2, rwkv_v0, rwkv_v1, rwkv_v2, rwkv_g1, rwkv_g2, rwkv_k_k, rwkv_k_a, rwkv_r_k, rwkv_lnx_g, rwkv_lnx_b, rwkv_wo, s5_log_dt, s5_lam_re, s5_lam_im, s5_b_re, s5_b_im, s5_c_re, s5_c_im, s5_d, s5_w_val, s5_w_gate, moba_w_qkv, moba_wo, ffn_w_in, ffn_conv_w, ffn_conv_b, ffn_w_down):
    P = dict(ln_g=ln_g, ln_b=ln_b, rwkv_mu=rwkv_mu, rwkv_w_rkv=rwkv_w_rkv, rwkv_w0=rwkv_w0, rwkv_w1=rwkv_w1,
             rwkv_w2=rwkv_w2, rwkv_a0=rwkv_a0, rwkv_a1=rwkv_a1, rwkv_a2=rwkv_a2, rwkv_v0=rwkv_v0,
             rwkv_v1=rwkv_v1, rwkv_v2=rwkv_v2, rwkv_g1=rwkv_g1, rwkv_g2=rwkv_g2, rwkv_k_k=rwkv_k_k,
             rwkv_k_a=rwkv_k_a, rwkv_r_k=rwkv_r_k.reshape(rwkv_r_k.shape[0], D_MODEL), rwkv_lnx_g=rwkv_lnx_g,
             rwkv_lnx_b=rwkv_lnx_b, rwkv_wo=rwkv_wo, s5_log_dt=s5_log_dt, s5_lam_re=s5_lam_re,
             s5_lam_im=s5_lam_im, s5_b_re=s5_b_re, s5_b_im=s5_b_im, s5_c_re=s5_c_re, s5_c_im=s5_c_im, s5_d=s5_d,
             s5_w_val=s5_w_val, s5_w_gate=s5_w_gate, moba_w_qkv=moba_w_qkv, moba_wo=moba_wo,
             ffn_w_in=ffn_w_in, ffn_conv_w=ffn_conv_w, ffn_conv_b=ffn_conv_b, ffn_w_down=ffn_w_down)
    B = x_prompt.shape[0]
    n_rwkv, n_s5 = state_rwkv_wkv.shape[0], state_s5_re.shape[0]
    zeros = lambda *s: jnp.zeros(s, F32)
    outs_p = _trunk('p', x_prompt, zeros(n_rwkv, B, RWKV_H, RWKV_N, RWKV_N), zeros(n_rwkv, B, D_MODEL),
                    zeros(n_s5, B, S5_G, S5_P), zeros(n_s5, B, S5_G, S5_P), zeros(DEPTH, B, CONV_W - 1, D_FF),
                    None, None, None, P)
    outs_s = _trunk('s', x_sample, state_rwkv_wkv, state_rwkv_shift, state_s5_re, state_s5_im, state_ffn_conv,
                    cache_k, cache_v, page_table, P)
    return (outs_p[0], outs_s[0]) + tuple(outs_p[1:]) + tuple(outs_s[1:])
```

```python
import functools
import math

import jax
import jax.numpy as jnp
from jax import lax
from jax.experimental import pallas as pl
from jax.experimental.pallas import tpu as pltpu

F32 = jnp.float32
BF16 = jnp.bfloat16

D_MODEL = 2048
DEPTH = 4
RWKV_N = 64
RWKV_H = D_MODEL // RWKV_N
RWKV_GN_EPS = 64e-5
S5_GROUP = 16
S5_G = D_MODEL // S5_GROUP
S5_P = 64
MOBA_H = 16
MOBA_DH = D_MODEL // MOBA_H
MOBA_BLOCK = 256
MOBA_TOPK = 3
PAGE_SIZE = 128
D_FF = 11 * D_MODEL // 4
CONV_W = 3
LN_EPS = 1e-5
DN_ALPHA = (2.0 * DEPTH) ** 0.25
NEG = -1e30

LANES = 128
SUBLANES = 8
VMEM_LIMIT_BYTES = 56 * 1024 * 1024

MM_ROW_TILE = 1024
MM_LN_RESIDENT_BYTES = 24 * 1024 * 1024
MM_LN_WEIGHT_TILE_BYTES = 12 * 1024 * 1024
WKV_CHUNK = 64
WKV_CHAINS = 16
WKV_HEADS = 4
MOBA_PAGES_PER_STEP = 8
S5_SLAB = LANES
S5_STATES = (S5_SLAB // S5_GROUP) * S5_P


def _params(n_axes):
    return pltpu.CompilerParams(dimension_semantics=("arbitrary",) * n_axes,
                                vmem_limit_bytes=VMEM_LIMIT_BYTES)


def _sigmoid(x):
    return 1.0 / (1.0 + jnp.exp(-x))


def _softplus(x):
    return jnp.maximum(x, 0.0) + jnp.log1p(jnp.exp(-jnp.abs(x)))


def _dg(a, b, dims):
    return lax.dot_general(a, b, (dims, ((), ())), preferred_element_type=F32)


_NN = ((1,), (0,))
_NT = ((1,), (1,))
_TN = ((0,), (0,))


def _split(a):
    hi = a.astype(BF16)
    lo = (a - hi.astype(F32)).astype(BF16)
    return hi, lo


def _dot3(a, b, dims=_NN):
    ah, al = _split(a)
    bh, bl = _split(b)
    return _dg(ah, bh, dims) + (_dg(al, bh, dims) + _dg(ah, bl, dims))


def _dot1(a, b, dims=_NN):
    return _dg(a.astype(BF16), b.astype(BF16), dims)


def _dot_exact_lhs(lhs_bf16, x):
    x1 = x.astype(BF16)
    r1 = x - x1.astype(F32)
    x2 = r1.astype(BF16)
    x3 = (r1 - x2.astype(F32)).astype(BF16)
    return _dg(lhs_bf16, x1, _NN) + (_dg(lhs_bf16, x2, _NN) + _dg(lhs_bf16, x3, _NN))


def _layer_norm(y, g, b):
    mu = jnp.mean(y, axis=-1, keepdims=True)
    yc = y - mu
    var = jnp.mean(yc * yc, axis=-1, keepdims=True)
    return yc * lax.rsqrt(var + LN_EPS) * g + b


def _mm_kernel(*refs, n_row, n_vec, n_w, n_erow, n_evec, n_out, prologue, epilogue):
    it = iter(refs)
    rows = [next(it) for _ in range(n_row)]
    vecs = [next(it) for _ in range(n_vec)]
    ws = [next(it) for _ in range(n_w)]
    erows = [next(it) for _ in range(n_erow)]
    evecs = [next(it) for _ in range(n_evec)]
    outs = [next(it) for _ in range(n_out)]
    a_scr = next(it)

    @pl.when(pl.program_id(1) == 0)
    def _():
        a_scr[...] = prologue(*[r[...] for r in rows], *[v[...] for v in vecs]).astype(BF16)

    a = a_scr[...]
    accs = [jnp.dot(a, w[...], preferred_element_type=F32) for w in ws]
    res = epilogue(*accs, *[e[...] for e in erows], *[e[...] for e in evecs])
    for o_ref, o in zip(outs, res):
        o_ref[...] = o.astype(o_ref.dtype)


def _mm(name, rows, vecs, ws, erows, evecs, prologue, epilogue, out_dtypes):
    M, K = rows[0].shape
    N = ws[0].shape[1]
    tm = MM_ROW_TILE if (M % MM_ROW_TILE == 0 and len(rows) == 1) else min(512, M)
    tn = 512 if N % 512 == 0 else N
    assert M % tm == 0 and N % tn == 0
    in_specs = ([pl.BlockSpec((tm, K), lambda i, j: (i, 0)) for _ in rows]
                + [pl.BlockSpec((1, K), lambda i, j: (0, 0)) for _ in vecs]
                + [pl.BlockSpec((K, tn), lambda i, j: (0, j)) for _ in ws]
                + [pl.BlockSpec((tm, tn), lambda i, j: (i, j)) for _ in erows]
                + [pl.BlockSpec((1, tn), lambda i, j: (0, j)) for _ in evecs])
    out_specs = [pl.BlockSpec((tm, tn), lambda i, j: (i, j)) for _ in out_dtypes]
    out_shape = [jax.ShapeDtypeStruct((M, N), dt) for dt in out_dtypes]
    kern = functools.partial(_mm_kernel, n_row=len(rows), n_vec=len(vecs), n_w=len(ws), n_erow=len(erows),
                             n_evec=len(evecs), n_out=len(out_dtypes), prologue=prologue, epilogue=epilogue)
    return pl.pallas_call(
        kern, grid=(M // tm, N // tn), in_specs=in_specs, out_specs=out_specs, out_shape=out_shape,
        scratch_shapes=[pltpu.VMEM((tm, K), BF16)], compiler_params=_params(2), name=name,
    )(*rows, *vecs, *ws, *erows, *evecs)


def _mm_ln_kernel(*refs, n_row, prologue, n_k):
    it = iter(refs)
    rows = [next(it) for _ in range(n_row)]
    w_ref, x_ref, g_ref, b_ref, o_ref, acc = (next(it) for _ in range(6))
    k = pl.program_id(1)
    a = prologue(*[r[...] for r in rows]).astype(BF16)
    p = jnp.dot(a, w_ref[...], preferred_element_type=F32)
    if n_k == 1:
        o_ref[...] = _layer_norm(DN_ALPHA * x_ref[...] + p, g_ref[...], b_ref[...])
        return

    @pl.when(k == 0)
    def _():
        acc[...] = p

    @pl.when(jnp.logical_and(k > 0, k < n_k - 1))
    def _():
        acc[...] += p

    @pl.when(k == n_k - 1)
    def _():
        o_ref[...] = _layer_norm(DN_ALPHA * x_ref[...] + (acc[...] + p), g_ref[...], b_ref[...])


def _mm_ln(name, rows, w, x, g, b, prologue):
    M, K = rows[0].shape
    N = w.shape[1]
    w_bytes = K * N * 2
    if w_bytes <= MM_LN_WEIGHT_TILE_BYTES:
        n_k, tm, w_mode = 1, min(512, M), None
    elif w_bytes <= MM_LN_RESIDENT_BYTES:
        n_k, tm, w_mode = 1, min(256, M), pl.Buffered(1)
    else:
        n_k, tm, w_mode = -(-w_bytes // MM_LN_WEIGHT_TILE_BYTES), min(512, M), None
    tk = K // n_k
    assert M % tm == 0 and K % n_k == 0 and tk % LANES == 0
    in_specs = ([pl.BlockSpec((tm, tk), lambda i, k: (i, k)) for _ in rows]
                + [pl.BlockSpec((tk, N), lambda i, k: (k, 0), pipeline_mode=w_mode),
                   pl.BlockSpec((tm, N), lambda i, k: (i, 0)),
                   pl.BlockSpec((1, N), lambda i, k: (0, 0)),
                   pl.BlockSpec((1, N), lambda i, k: (0, 0))])
    kern = functools.partial(_mm_ln_kernel, n_row=len(rows), prologue=prologue, n_k=n_k)
    return pl.pallas_call(
        kern, grid=(M // tm, n_k), in_specs=in_specs,
        out_specs=pl.BlockSpec((tm, N), lambda i, k: (i, 0)),
        out_shape=jax.ShapeDtypeStruct((M, N), F32),
        scratch_shapes=[pltpu.VMEM((tm, N), F32)], compiler_params=_params(2), name=name,
    )(*rows, w, x, g, b)


def _ln_kernel(x_ref, h_ref, g_ref, b_ref, o_ref):
    o_ref[...] = _layer_norm(DN_ALPHA * x_ref[...] + h_ref[...], g_ref[...], b_ref[...])


def _gate_ln_kernel(z_ref, wv_ref, wg_ref, x_ref, g_ref, b_ref, o_ref):
    a = z_ref[...].astype(BF16)
    val = jnp.dot(a, wv_ref[...], preferred_element_type=F32)
    gate = jnp.dot(a, wg_ref[...], preferred_element_type=F32)
    o_ref[...] = _layer_norm(DN_ALPHA * x_ref[...] + val * _sigmoid(gate), g_ref[...], b_ref[...])


def _gate_ln(name, z, w_val, w_gate, x, g, b):
    M, K = z.shape
    N = w_val.shape[1]
    tm = min(256, M)
    assert M % tm == 0
    row_k = pl.BlockSpec((tm, K), lambda i: (i, 0))
    row_n = pl.BlockSpec((tm, N), lambda i: (i, 0))
    wsp = pl.BlockSpec((K, N), lambda i: (0, 0), pipeline_mode=pl.Buffered(1))
    vec = pl.BlockSpec((1, N), lambda i: (0, 0))
    return pl.pallas_call(_gate_ln_kernel, grid=(M // tm,), in_specs=[row_k, wsp, wsp, row_n, vec, vec],
                          out_specs=row_n, out_shape=jax.ShapeDtypeStruct((M, N), F32),
                          compiler_params=_params(1), name=name)(z, w_val, w_gate, x, g, b)


def _ln(name, x, h, g, b):
    M, N = x.shape
    tm = min(512, M)
    row = pl.BlockSpec((tm, N), lambda i: (i, 0))
    vec = pl.BlockSpec((1, N), lambda i: (0, 0))
    return pl.pallas_call(_ln_kernel, grid=(M // tm,), in_specs=[row, row, vec, vec], out_specs=row,
                          out_shape=jax.ShapeDtypeStruct((M, N), F32), compiler_params=_params(1), name=name)(x, h, g, b)


def _ffn_in_kernel(x_ref, wg_ref, wv_ref, c0_ref, cw_ref, cb_ref, h_ref, cs_ref, a_scr, carry, *, bt, tt):
    t = pl.program_id(1)
    j = pl.program_id(2)
    rows = bt * tt
    tn = wg_ref.shape[1]

    @pl.when(j == 0)
    def _():
        a_scr[...] = x_ref[...].reshape(rows, x_ref.shape[2]).astype(BF16)

    @pl.when(t == 0)
    def _():
        carry[j] = c0_ref[...]

    a = a_scr[...]
    hg = jnp.dot(a, wg_ref[...], preferred_element_type=F32)
    hv = jnp.dot(a, wv_ref[...], preferred_element_type=F32)
    prev = carry[j]
    p2 = jnp.broadcast_to(prev[:, 0:1, :], (bt, tt, tn)).reshape(rows, tn)
    p1 = jnp.broadcast_to(prev[:, 1:2, :], (bt, tt, tn)).reshape(rows, tn)
    r = lax.broadcasted_iota(jnp.int32, (rows, tn), 0) % tt
    s1 = jnp.where(r == 0, p1, pltpu.roll(hg, 1, axis=0))
    s2 = jnp.where(r == 0, p2, jnp.where(r == 1, p1, pltpu.roll(hg, 2, axis=0)))
    cw = cw_ref[...]
    c = cb_ref[...] + cw[0:1, :] * s2 + cw[1:2, :] * s1 + cw[2:3, :] * hg
    h_ref[...] = (c * _sigmoid(c) * hv).astype(h_ref.dtype)
    last = hg.reshape(bt, tt, tn)[:, tt - 2:, :]
    carry[j] = last
    cs_ref[0] = last


def _ffn_in(name, x3, wg, wv, conv0, cw, cb):
    B, T, K = x3.shape
    F = wg.shape[1]
    bt, tt = (1, MM_ROW_TILE) if T % MM_ROW_TILE == 0 else (B, T)
    tn = 512
    assert B % bt == 0 and T % tt == 0 and F % tn == 0 and tt >= CONV_W - 1
    n_j = F // tn
    kern = functools.partial(_ffn_in_kernel, bt=bt, tt=tt)
    n_t = T // tt
    h, cst = pl.pallas_call(
        kern, grid=(B // bt, n_t, n_j),
        in_specs=[pl.BlockSpec((bt, tt, K), lambda b, t, j: (b, t, 0)),
                  pl.BlockSpec((K, tn), lambda b, t, j: (0, j)),
                  pl.BlockSpec((K, tn), lambda b, t, j: (0, j)),
                  pl.BlockSpec((bt, CONV_W - 1, tn), lambda b, t, j: (b, 0, j)),
                  pl.BlockSpec((CONV_W, tn), lambda b, t, j: (0, j)),
                  pl.BlockSpec((1, tn), lambda b, t, j: (0, j))],
        out_specs=[pl.BlockSpec((bt * tt, tn), lambda b, t, j: (b * n_t + t, j)),
                   pl.BlockSpec((1, bt, CONV_W - 1, tn), lambda b, t, j: (t, b, 0, j))],
        out_shape=[jax.ShapeDtypeStruct((B * T, F), BF16),
                   jax.ShapeDtypeStruct((n_t, B, CONV_W - 1, F), F32)],
        scratch_shapes=[pltpu.VMEM((bt * tt, K), BF16), pltpu.VMEM((n_j, bt, CONV_W - 1, tn), F32)],
        compiler_params=_params(3), name=name,
    )(x3, wg, wv, conv0, cw, cb)
    return h, cst[n_t - 1]


RWKV_PROJ_TN = 1024
RWKV_PROJ_TILES = D_MODEL // RWKV_PROJ_TN


def _rwkv_proj_kernel(*refs, has_vgate, seq_len, tiles_per_seq):
    it = iter(refs)
    x_ref, sh_ref, mu_ref, wm_ref, w1_ref, a1_ref, g1_ref, w2_ref, a2_ref, g2_ref, w0_ref, a0_ref = (
        next(it) for _ in range(12))
    if has_vgate:
        v1_ref, v2_ref, v0_ref, vf_ref = (next(it) for _ in range(4))
    r_ref, k_ref, v_ref, lw_ref, a_ref, g_ref = (next(it) for _ in range(6))
    mix_scr, hw_scr, ha_scr, hg_scr, last_scr = (next(it) for _ in range(5))
    if has_vgate:
        hv_scr = next(it)
    i = pl.program_id(0)
    j = pl.program_id(1)
    dot = lambda a, b: jnp.dot(a, b, preferred_element_type=F32)

    @pl.when(j == 0)
    def _():
        x = x_ref[...]
        tm, K = x.shape
        row = lax.broadcasted_iota(jnp.int32, (tm, K), 0)
        prev = pltpu.roll(x, 1, axis=0)
        if tiles_per_seq > 1:
            first = jnp.where(i % tiles_per_seq == 0, sh_ref[0], last_scr[0:1, :])
            prev = jnp.where(row == 0, first, prev)
            last_scr[0:1, :] = x[tm - 1:tm, :]
        else:
            n_seq = tm // seq_len
            sh = jnp.broadcast_to(sh_ref[...], (n_seq, seq_len, K)).reshape(tm, K)
            prev = jnp.where(row % seq_len == 0, sh, prev)
        d = prev - x
        mix = lambda n: (x + d * mu_ref[n:n + 1, :]).astype(BF16)
        mix_scr[0] = mix(0)
        mix_scr[1] = mix(2)
        xv = mix(3)
        mix_scr[2] = xv
        hw_scr[...] = jnp.tanh(dot(mix(1), w1_ref[...])).astype(BF16)
        ha_scr[...] = dot(mix(4), a1_ref[...]).astype(BF16)
        hg_scr[...] = _sigmoid(dot(mix(5), g1_ref[...])).astype(BF16)
        if has_vgate:
            hv_scr[...] = dot(xv, v1_ref[...]).astype(BF16)

    grp = j // RWKV_PROJ_TILES

    @pl.when(grp == 0)
    def _():
        r_ref[...] = dot(mix_scr[0], wm_ref[...]).astype(r_ref.dtype)
        lw_ref[...] = -jnp.exp(-_softplus(-(w0_ref[...] + dot(hw_scr[...], w2_ref[...]))) - 0.5)

    @pl.when(grp == 1)
    def _():
        k_ref[...] = dot(mix_scr[1], wm_ref[...]).astype(k_ref.dtype)
        a_ref[...] = _sigmoid(a0_ref[...] + dot(ha_scr[...], a2_ref[...])).astype(a_ref.dtype)

    @pl.when(grp == 2)
    def _():
        v = dot(mix_scr[2], wm_ref[...])
        if has_vgate:
            v = v + (vf_ref[...] - v) * _sigmoid(v0_ref[...] + dot(hv_scr[...], v2_ref[...]))
        v_ref[...] = v.astype(v_ref.dtype)
        g_ref[...] = dot(hg_scr[...], g2_ref[...]).astype(g_ref.dtype)


def _rwkv_proj(name, x3, shift0, mu, w_main, lora1, lora2, bias, vgate):
    B, T, K = x3.shape
    M = B * T
    tm, tn, nt = min(512, M), RWKV_PROJ_TN, RWKV_PROJ_TILES
    assert M % tm == 0 and w_main.shape == (K, 3 * D_MODEL) and (T % tm == 0 or tm % T == 0)
    tiles_per_seq = max(1, T // tm)
    seqs_per_tile = max(1, tm // T)
    row = pl.BlockSpec((tm, K), lambda i, j: (i, 0))
    full = lambda a: pl.BlockSpec(a.shape, lambda i, j: (0, 0))
    col = lambda a: pl.BlockSpec((a.shape[0], tn), lambda i, j: (0, j % nt))
    shift = pl.BlockSpec((seqs_per_tile, 1, K), lambda i, j: (i // tiles_per_seq, 0, 0))
    ins = [x3.reshape(M, K), shift0.reshape(B, 1, K), mu, w_main, *lora1, *lora2, *bias]
    specs = ([row, shift, full(mu), pl.BlockSpec((K, tn), lambda i, j: (0, j))] + [full(w) for w in lora1]
             + [col(w) for w in lora2] + [col(b) for b in bias])
    scratch = ([pltpu.VMEM((3, tm, K), BF16)] + [pltpu.VMEM((tm, w.shape[1]), BF16) for w in lora1]
               + [pltpu.VMEM((SUBLANES, K), F32)])
    if vgate is not None:
        v1, v2, v0, v_first = vgate
        ins += [v1, v2, v0, v_first]
        specs += [full(v1), col(v2), col(v0), pl.BlockSpec((tm, tn), lambda i, j: (i, j % nt))]
        scratch.append(pltpu.VMEM((tm, v1.shape[1]), BF16))
    out = lambda grp: pl.BlockSpec((tm, tn), lambda i, j: (i, jnp.clip(j - grp * nt, 0, nt - 1)))
    return pl.pallas_call(
        functools.partial(_rwkv_proj_kernel, has_vgate=vgate is not None, seq_len=T, tiles_per_seq=tiles_per_seq),
        grid=(M // tm, 3 * nt),
        in_specs=specs, out_specs=[out(0), out(1), out(2), out(0), out(1), out(2)],
        out_shape=[jax.ShapeDtypeStruct((M, D_MODEL), dt) for dt in (BF16, BF16, BF16, F32, BF16, BF16)],
        scratch_shapes=scratch, compiler_params=_params(2), name=name,
    )(*ins)


def _wkv_chunks(r, lw, k, v, a, prm, S, consts):
    C = WKV_CHUNK
    bdmask, l_incl, strict, incl, eye = consts

    def bd(xp):
        return jnp.concatenate([xp.astype(BF16)] * WKV_HEADS, axis=0) * bdmask

    def headsum(xp):
        hi, lo = _split(xp)
        return _dg(hi, bdmask, _NN) + _dg(lo, bdmask, _NN)

    def each(f, *lists):
        return [f(*args) for args in zip(*lists)]

    kkp, kap, rkp, lng, lnb = (list(t) for t in zip(*prm))
    kkn = each(lambda k_, p_: k_ * p_, k, kkp)
    kp = each(lambda k_, a_, p_: k_ * (1.0 + (a_ - 1.0) * p_), k, a, kap)
    sums = each(lambda kkn_, r_, kp_, p_: headsum(jnp.concatenate([kkn_ * kkn_, r_ * kp_ * p_], axis=0)),
                kkn, r, kp, rkp)
    kk = each(lambda kkn_, s_: kkn_ / jnp.maximum(jnp.sqrt(s_[:C]), 1e-12), kkn, sums)
    bonus = each(lambda s_, v_: s_[C:] * v_, sums, v)
    b = each(lambda kk_, a_: kk_ * a_, kk, a)
    cum = each(lambda lw_: _dot_exact_lhs(l_incl, lw_), lw)
    e_neg = each(lambda c_: jnp.exp(-c_), cum)
    x = each(lambda kk_, c_, lw_, r_: jnp.concatenate([kk_ * jnp.exp(c_ - lw_), r_ * jnp.exp(c_)],
                                                       axis=0).astype(BF16), kk, cum, lw, r)
    g_b = each(lambda x_, b_, e_: _dg(x_, bd(b_ * e_), _NT), x, b, e_neg)
    g_k = each(lambda x_, k_, e_: _dg(x_, bd(k_ * e_), _NT), x, kp, e_neg)
    xs = each(lambda x_, s_: _dg(x_, s_.astype(BF16), _NT), x, S)
    mb = each(lambda g_: jnp.where(incl, g_[C:], 0.0), g_b)
    kv = each(lambda g_, v_: _dg(jnp.concatenate([jnp.where(strict, g_[:C], 0.0), jnp.where(incl, g_[C:], 0.0)],
                                                 axis=0).astype(BF16), bd(v_), _NN), g_k, v)
    rhs = each(lambda xs_, kv_: -(xs_[:C] + kv_[:C]), xs, kv)
    npow = each(lambda g_: -jnp.where(strict, g_[:C], 0.0), g_b)
    tinv = each(lambda n_: eye + n_, npow)
    npow = each(lambda n_: _dg(n_.astype(BF16), bd(n_), _NN), npow)
    for _ in range(int(math.log2(C)) - 2):
        both = each(lambda t_, n_: _dg(jnp.concatenate([t_, n_], axis=0).astype(BF16), bd(n_), _NN), tinv, npow)
        tinv = each(lambda t_, b_: t_ + b_[:C], tinv, both)
        npow = each(lambda b_: b_[C:], both)
    tinv = each(lambda t_, n_: t_ + _dg(t_.astype(BF16), bd(n_), _NN), tinv, npow)
    u = each(lambda t_, r_: _dg(t_.astype(BF16), bd(r_), _NN), tinv, rhs)
    y = each(lambda xs_, kv_, mb_, u_: xs_[C:] + kv_[C:] + _dg(mb_.astype(BF16), bd(u_), _NN), xs, kv, mb, u)
    upd = each(lambda u_, v_, b_, k_, c_: _dg(
        jnp.concatenate([u_, v_], axis=0).astype(BF16),
        (jnp.concatenate([b_, k_], axis=0) * jnp.exp(c_[C - 1:C, :] - jnp.concatenate([c_, c_], axis=0))).astype(BF16),
        _TN), u, v, b, kp, cum)
    bdmask_f = bdmask.astype(F32)
    s_new = each(lambda s_, c_, u_: s_ * jnp.exp(c_[C - 1:C, :]) + u_ * bdmask_f, S, cum, upd)
    mu = each(lambda y_: _dg(y_.astype(BF16), bdmask, _NN) * (1.0 / RWKV_N), y)
    yc = each(lambda y_, m_: y_ - m_, y, mu)
    var = each(lambda yc_: _dg((yc_ * yc_).astype(BF16), bdmask, _NN) * (1.0 / RWKV_N), yc)
    z = each(lambda yc_, var_, g_, b_, bo_: yc_ * lax.rsqrt(var_ + RWKV_GN_EPS) * g_ + b_ + bo_,
             yc, var, lng, lnb, bonus)
    return z, s_new


def _wkv_kernel(r_ref, lw_ref, k_ref, v_ref, a_ref, kk_ref, ka_ref, rk_ref, g_ref, b_ref, s0_ref,
                z_ref, so_ref, s_scr, *, n_t, nb, n_grp):
    t = pl.program_id(1)
    C, N, HW = WKV_CHUNK, RWKV_N, WKV_HEADS * RWKV_N
    ri = lax.broadcasted_iota(jnp.int32, (HW, HW), 0) // N
    ci = lax.broadcasted_iota(jnp.int32, (HW, HW), 1) // N
    bdmask_f = (ri == ci).astype(F32)

    chains = [(bi, g) for bi in range(nb) for g in range(n_grp)]

    @pl.when(t == 0)
    def _():
        for n, (bi, g) in enumerate(chains):
            rows = s0_ref[bi, g * WKV_HEADS:(g + 1) * WKV_HEADS].reshape(HW, N)
            s_scr[n] = jnp.concatenate([rows] * WKV_HEADS, axis=1) * bdmask_f

    tt = lax.broadcasted_iota(jnp.int32, (C, C), 0)
    ss = lax.broadcasted_iota(jnp.int32, (C, C), 1)
    tp = lax.broadcasted_iota(jnp.int32, (C, HW), 0)
    sp = lax.broadcasted_iota(jnp.int32, (C, HW), 1) % N
    consts = (bdmask_f.astype(BF16), (tt >= ss).astype(BF16), tp > sp, tp >= sp, (tp == sp).astype(F32))
    sls = [slice(g * HW, (g + 1) * HW) for _, g in chains]
    tok = lambda ref: [ref[bi, :, sl].astype(F32) for (bi, _), sl in zip(chains, sls)]
    prm = [tuple(ref[:, sl] for ref in (kk_ref, ka_ref, rk_ref, g_ref, b_ref)) for sl in sls]
    zs, s_news = _wkv_chunks(tok(r_ref), tok(lw_ref), tok(k_ref), tok(v_ref), tok(a_ref), prm,
                             [s_scr[n] for n in range(len(chains))], consts)
    for n, ((bi, _), sl) in enumerate(zip(chains, sls)):
        s_scr[n] = s_news[n]
        z_ref[bi, :, sl] = zs[n]

    @pl.when(t == n_t - 1)
    def _():
        for n, (bi, g) in enumerate(chains):
            for h in range(WKV_HEADS):
                so_ref[bi, g * WKV_HEADS + h] = s_scr[n, h * N:(h + 1) * N, h * N:(h + 1) * N]


def _wkv(name, r, lw, k, v, a, kkp, kap, rkp, lng, lnb, s0):
    B, T, D = r.shape
    C, HW = WKV_CHUNK, WKV_HEADS * RWKV_N
    n_grp = min(max(1, WKV_CHAINS // B), D // HW)
    W = n_grp * HW
    assert T % C == 0 and D % W == 0
    n_t = T // C
    tok = pl.BlockSpec((B, C, W), lambda h, t: (0, t, h))
    vec = pl.BlockSpec((1, W), lambda h, t: (0, h))
    st = pl.BlockSpec((B, n_grp * WKV_HEADS, RWKV_N, RWKV_N), lambda h, t: (0, h, 0, 0))
    return pl.pallas_call(
        functools.partial(_wkv_kernel, n_t=n_t, nb=B, n_grp=n_grp), grid=(D // W, n_t),
        in_specs=[tok] * 5 + [vec] * 5 + [st], out_specs=[tok, st],
        out_shape=[jax.ShapeDtypeStruct((B, T, D), F32), jax.ShapeDtypeStruct(s0.shape, F32)],
        scratch_shapes=[pltpu.VMEM((B * n_grp, HW, HW), F32)], compiler_params=_params(2), name=name,
    )(r, lw, k, v, a, kkp, kap, rkp, lng, lnb, s0)


def _cmul(ar, ai, br, bi):
    return ar * br - ai * bi, ar * bi + ai * br


def _dot3w(a, w_hi, w_lo, dims=_NN):
    ah, al = _split(a)
    return _dg(ah, w_hi, dims) + (_dg(al, w_hi, dims) + _dg(ah, w_lo, dims))


S5_SHIFTS = (1, 2, 4)


def _s5_kernel(x_ref, wbh_ref, wbl_ref, wc_ref, lr_ref, li_ref, dt_ref, d_ref, h0r_ref, h0i_ref,
               z_ref, hr_ref, hi_ref, pw_re, pw_im, sh_re, sh_im, cf, hc, *, n_t, tt):
    t = pl.program_id(2)
    R = SUBLANES

    @pl.when(t == 0)
    def _():
        hc[0:1, :] = h0r_ref[0]
        hc[1:2, :] = h0i_ref[0]

    @pl.when(jnp.logical_and(t == 0, pl.program_id(1) == 0))
    def _():
        lr, li, dt = lr_ref[0], li_ref[0], dt_ref[0]
        rid = lax.broadcasted_iota(jnp.int32, (R, S5_STATES), 0)
        n = (rid + 1).astype(F32)
        mag = jnp.exp(lr * dt * n)
        ang = li * dt * n
        pw_re[...] = mag * jnp.cos(ang)
        pw_im[...] = mag * jnp.sin(ang)
        for m, sh in enumerate(S5_SHIFTS):
            sh_re[m] = jnp.where(rid >= sh, pw_re[sh - 1:sh, :], 0.0)
            sh_im[m] = jnp.where(rid >= sh, pw_im[sh - 1:sh, :], 0.0)
        ar, ai = pw_re[0:1, :], pw_im[0:1, :]
        den = lr * lr + li * li
        cf[0:1, :] = ((ar - 1.0) * lr + ai * li) / den
        cf[1:2, :] = (ai * lr - (ar - 1.0) * li) / den

    x = x_ref[0]
    bu = _dot3w(x, wbh_ref[0], wbl_ref[0])
    cr, ci = cf[0:1, :], cf[1:2, :]
    br, bi = _cmul(cr, ci, bu[:, :S5_STATES], bu[:, S5_STATES:])
    pr, pi = pw_re[...], pw_im[...]
    hr, hi = hc[0:1, :], hc[1:2, :]
    xrs, xis = [], []
    for i in range(tt // R):
        vr, vi = br[i * R:(i + 1) * R, :], bi[i * R:(i + 1) * R, :]
        for m, sh in enumerate(S5_SHIFTS):
            ur, ui = _cmul(sh_re[m], sh_im[m], pltpu.roll(vr, sh, axis=0), pltpu.roll(vi, sh, axis=0))
            vr, vi = vr + ur, vi + ui
        ur, ui = _cmul(pr, pi, hr, hi)
        vr, vi = vr + ur, vi + ui
        xrs.append(vr)
        xis.append(vi)
        hr, hi = vr[R - 1:R, :], vi[R - 1:R, :]
    hc[0:1, :] = hr
    hc[1:2, :] = hi
    xcat = jnp.concatenate([jnp.concatenate(xrs, axis=0), jnp.concatenate(xis, axis=0)], axis=-1)
    y = _dg(xcat.astype(BF16), wc_ref[0], _NN) + d_ref[...] * x
    z_ref[0] = 0.5 * y * (1.0 + jnp.tanh(math.sqrt(2.0 / math.pi) * (y + 0.044715 * (y * y * y))))

    @pl.when(t == n_t - 1)
    def _():
        hr_ref[0] = hr
        hi_ref[0] = hi


def _s5(name, x3, wb, wc, lr, li, dt, d, h0r, h0i):
    B, T, D = x3.shape
    tt = min(512, T)
    assert T % tt == 0 and tt % SUBLANES == 0
    n_t = T // tt
    n_s = D // S5_SLAB
    S = S5_STATES
    st = pl.BlockSpec((1, 1, S), lambda s, b, t: (b, 0, s))
    pv = pl.BlockSpec((1, 1, S), lambda s, b, t: (0, 0, s))
    w_in = pl.BlockSpec((1, S5_SLAB, 2 * S), lambda s, b, t: (s, 0, 0))
    w_out = pl.BlockSpec((1, 2 * S, S5_SLAB), lambda s, b, t: (s, 0, 0))
    split = lambda w: (w.astype(BF16), (w - w.astype(BF16).astype(F32)).astype(BF16))
    return pl.pallas_call(
        functools.partial(_s5_kernel, n_t=n_t, tt=tt), grid=(n_s, B, n_t),
        in_specs=[pl.BlockSpec((1, tt, S5_SLAB), lambda s, b, t: (b, t, s)),
                  w_in, w_in, w_out,
                  pv, pv, pv,
                  pl.BlockSpec((1, S5_SLAB), lambda s, b, t: (0, s)),
                  st, st],
        out_specs=[pl.BlockSpec((1, tt, S5_SLAB), lambda s, b, t: (b, t, s)), st, st],
        out_shape=[jax.ShapeDtypeStruct((B, T, D), F32),
                   jax.ShapeDtypeStruct((B, 1, n_s * S), F32), jax.ShapeDtypeStruct((B, 1, n_s * S), F32)],
        scratch_shapes=[pltpu.VMEM((SUBLANES, S), F32), pltpu.VMEM((SUBLANES, S), F32),
                        pltpu.VMEM((len(S5_SHIFTS), SUBLANES, S), F32),
                        pltpu.VMEM((len(S5_SHIFTS), SUBLANES, S), F32),
                        pltpu.VMEM((SUBLANES, S), F32), pltpu.VMEM((SUBLANES, S), F32)],
        compiler_params=_params(3), name=name,
    )(x3, *split(wb), wc.astype(BF16), lr, li, dt, d, h0r, h0i)


def _select_topk(gate, idx, valid, axis):
    g = jnp.where(valid, gate, NEG)
    big = jnp.int32(1 << 30)
    sel = jnp.zeros(gate.shape, jnp.bool_)
    for _ in range(MOBA_TOPK):
        m = jnp.max(g, axis=axis, keepdims=True)
        first = jnp.min(jnp.where(g == m, idx, big), axis=axis, keepdims=True)
        pick = idx == first
        sel = jnp.logical_or(sel, pick)
        g = jnp.where(pick, -jnp.inf, g)
    return jnp.logical_and(sel, valid)


def _moba_prompt_kernel(qa_ref, qb_ref, k_ref, v_ref, sl_ref, oa_ref, ob_ref, kmean_scr, *, n_blk):
    i = pl.program_id(2)
    BK = MOBA_BLOCK

    @pl.when(i == 0)
    def _():
        kmean_scr[...] = jnp.concatenate([jnp.sum(k_ref[0, n * BK:(n + 1) * BK, :], axis=0, keepdims=True)
                                          for n in range(n_blk)], axis=0) * (1.0 / BK)

    slope = sl_ref[0][:, 0:1]
    qi = lax.broadcasted_iota(jnp.int32, (BK, BK), 0)
    ki = lax.broadcasted_iota(jnp.int32, (BK, BK), 1)
    causal = qi >= ki
    bias0 = slope * (qi - ki).astype(F32)
    scale = MOBA_DH ** -0.5
    add = lambda x, y: x + y

    def select(c, q):
        if c == 0:
            return None
        gate_t = _dot3(kmean_scr[...], q, _NT)
        nidx = lax.broadcasted_iota(jnp.int32, (n_blk, BK), 0)
        sel_t = jnp.where(_select_topk(gate_t, nidx, nidx < c, axis=0), 1.0, 0.0).astype(BF16)
        return _dg(jnp.where(qi == ki, 1.0, 0.0).astype(BF16), sel_t, _NT) > 0.5

    def scores(c, qb, sel):
        ss = []
        for n in range(c + 1):
            s = _dg(qb, k_ref[0, n * BK:(n + 1) * BK, :].astype(BF16), _NT) * scale
            s = s - (bias0 + slope * float((c - n) * BK))
            ss.append(jnp.where(causal if n == c else sel[:, n:n + 1], s, NEG))
        return ss

    def pair(cs):
        qs = [qa_ref[0], qb_ref[0]]
        sels = [select(c, q) for c, q in zip(cs, qs)]
        sss = [scores(c, q.astype(BF16), sel) for c, q, sel in zip(cs, qs, sels)]
        ms = [jnp.max(functools.reduce(jnp.maximum, ss), axis=-1, keepdims=True) for ss in sss]
        pss = [[jnp.exp(s - m) for s in ss] for ss, m in zip(sss, ms)]
        ls = [jnp.sum(functools.reduce(add, ps), axis=-1, keepdims=True) for ps in pss]
        accs = [functools.reduce(add, [_dg(p.astype(BF16), v_ref[0, n * BK:(n + 1) * BK, :].astype(BF16), _NN)
                                       for n, p in enumerate(ps)]) for ps in pss]
        for o_ref, acc, l in zip((oa_ref, ob_ref), accs, ls):
            o_ref[0] = acc / l

    for c in range(n_blk // 2):
        pl.when(i == c)(functools.partial(pair, (c, n_blk - 1 - c)))


def _moba_prompt(name, q, k, v, slopes):
    B, T, D = q.shape
    n_blk = T // MOBA_BLOCK
    assert T % MOBA_BLOCK == 0 and n_blk % 2 == 0
    q_lo = pl.BlockSpec((1, MOBA_BLOCK, MOBA_DH), lambda b, h, i: (b, i, h))
    q_hi = pl.BlockSpec((1, MOBA_BLOCK, MOBA_DH), lambda b, h, i: (b, n_blk - 1 - i, h))
    kv = pl.BlockSpec((1, T, MOBA_DH), lambda b, h, i: (b, 0, h))
    o_lo, o_hi = pl.pallas_call(
        functools.partial(_moba_prompt_kernel, n_blk=n_blk), grid=(B, MOBA_H, n_blk // 2),
        in_specs=[q_lo, q_hi, kv, kv, pl.BlockSpec((1, 1, LANES), lambda b, h, i: (h, 0, 0))],
        out_specs=[q_lo, pl.BlockSpec((1, MOBA_BLOCK, MOBA_DH), lambda b, h, i: (b, n_blk // 2 - 1 - i, h))],
        out_shape=[jax.ShapeDtypeStruct((B, T // 2, D), F32)] * 2,
        scratch_shapes=[pltpu.VMEM((n_blk, MOBA_DH), F32)],
        compiler_params=_params(3), name=name,
    )(q, q, k, v, slopes)
    return jnp.concatenate([o_lo, o_hi], axis=1)


def _q_rows(q, nq):
    return jnp.concatenate([q[:, h * MOBA_DH:(h + 1) * MOBA_DH] for h in range(MOBA_H)], axis=0)


def _head_match(n_rows, nq):
    shape = (n_rows * MOBA_H, MOBA_H * nq)
    return lax.broadcasted_iota(jnp.int32, shape, 0) % MOBA_H == lax.broadcasted_iota(jnp.int32, shape, 1) // nq


def _token_scores(k3, qr, nq, dot):
    n_rows = k3.shape[0]
    s2 = dot(k3.reshape(n_rows * MOBA_H, MOBA_DH), qr, _NT)
    s2 = jnp.where(_head_match(n_rows, nq), s2, 0.0)
    return jnp.sum(s2.reshape(n_rows, MOBA_H, MOBA_H * nq), axis=1)


def _moba_scores_kernel(pt_ref, q_ref, m_ref, *refs, nq, pp):
    k_refs, (s_ref, ks_ref) = refs[:pp], refs[pp:]
    qr = _q_rows(q_ref[0], nq).astype(BF16)
    kps = [k_ref[0, 0] for k_ref in k_refs]
    s2s = [_dg(kp.reshape(PAGE_SIZE * MOBA_H, MOBA_DH).astype(BF16), qr, _NT) for kp in kps]
    for u in range(pp):
        s_ref[0, u] = jnp.sum((s2s[u] * m_ref[...]).reshape(PAGE_SIZE, MOBA_H, MOBA_H * nq), axis=1)
        ks_ref[0, u] = jnp.sum(kps[u], axis=0)


def _page_specs(layer, pp):
    return [pl.BlockSpec((1, 1, PAGE_SIZE, MOBA_H, MOBA_DH),
                         lambda b, g, pt, u=u: (layer, pt[b, g * pp + u], 0, 0, 0)) for u in range(pp)]


def _moba_scores(name, page_table, q, cache_k, layer, match_f):
    B, nq, D = q.shape
    n_pg = page_table.shape[1]
    pp = MOBA_PAGES_PER_STEP
    assert n_pg % pp == 0
    gs = pltpu.PrefetchScalarGridSpec(
        num_scalar_prefetch=1, grid=(B, n_pg // pp),
        in_specs=[pl.BlockSpec((1, nq, D), lambda b, g, pt: (b, 0, 0)),
                  pl.BlockSpec(match_f.shape, lambda b, g, pt: (0, 0))] + _page_specs(layer, pp),
        out_specs=[pl.BlockSpec((1, pp, PAGE_SIZE, MOBA_H * nq), lambda b, g, pt: (b, g, 0, 0)),
                   pl.BlockSpec((1, pp, MOBA_H, MOBA_DH), lambda b, g, pt: (b, g, 0, 0))])
    return pl.pallas_call(
        functools.partial(_moba_scores_kernel, nq=nq, pp=pp), grid_spec=gs,
        out_shape=[jax.ShapeDtypeStruct((B, n_pg, PAGE_SIZE, MOBA_H * nq), F32),
                   jax.ShapeDtypeStruct((B, n_pg, MOBA_H, MOBA_DH), F32)],
        compiler_params=_params(2), name=name,
    )(page_table, q, match_f, *([cache_k] * pp))


def _moba_probs_kernel(s_ref, ks_ref, q_ref, kn_ref, sl_ref, p_ref, pn_ref, sel_scr, *, nq, n_blk, past):
    BK = MOBA_BLOCK
    HQ = MOBA_H * nq
    qr = _q_rows(q_ref[0], nq)
    ppb = BK // PAGE_SIZE
    kmean = jnp.sum(ks_ref[0].reshape(n_blk, ppb, MOBA_H, MOBA_DH), axis=1) * (1.0 / BK)
    gate = _token_scores(kmean, qr, nq, _dot3)
    nidx = lax.broadcasted_iota(jnp.int32, (n_blk, HQ), 0)
    own = past // BK
    sel = _select_topk(gate, nidx, nidx < own, axis=0)
    sel_scr[...] = sel.astype(F32)
    slope = sl_ref[...]
    scale = MOBA_DH ** -0.5
    qpos = past + lax.broadcasted_iota(jnp.int32, (1, HQ), 1) % nq
    tok = lax.broadcasted_iota(jnp.int32, (BK, HQ), 0)

    def scores(n):
        s = s_ref[0, n] * scale - slope * (qpos - (n * BK + tok)).astype(F32)
        return jnp.where(sel_scr[pl.ds(n, 1), :] > 0.5, s, NEG)

    sn = _token_scores(kn_ref[0], qr, nq, _dot1) * scale
    tn = lax.broadcasted_iota(jnp.int32, (PAGE_SIZE, HQ), 0)
    dn = qpos - (past + tn)
    sn = jnp.where(jnp.logical_and(dn >= 0, tn < nq), sn - slope * dn.astype(F32), NEG)
    m = lax.fori_loop(0, n_blk, lambda n, m_: jnp.maximum(m_, jnp.max(scores(n), axis=0, keepdims=True)),
                      jnp.max(sn, axis=0, keepdims=True))
    pn = jnp.exp(sn - m)
    l = lax.fori_loop(0, n_blk, lambda n, l_: l_ + jnp.sum(jnp.exp(scores(n) - m), axis=0, keepdims=True),
                      jnp.sum(pn, axis=0, keepdims=True))
    pn_ref[0] = (pn / l).astype(pn_ref.dtype)

    def write(n, c):
        p_ref[0, n] = (jnp.exp(scores(n) - m) / l).astype(p_ref.dtype)
        return c

    lax.fori_loop(0, n_blk, write, 0)


def _moba_probs(name, s, ksum, q, k_new_pad, slopes_hq, past):
    B, n_blk, BK, HQ = s.shape
    nq = q.shape[1]
    blk = pl.BlockSpec((1, n_blk, BK, HQ), lambda b: (b, 0, 0, 0))
    return pl.pallas_call(
        functools.partial(_moba_probs_kernel, nq=nq, n_blk=n_blk, past=past), grid=(B,),
        in_specs=[blk,
                  pl.BlockSpec((1, ksum.shape[1], MOBA_H, MOBA_DH), lambda b: (b, 0, 0, 0)),
                  pl.BlockSpec((1, nq, D_MODEL), lambda b: (b, 0, 0)),
                  pl.BlockSpec((1, PAGE_SIZE, MOBA_H, MOBA_DH), lambda b: (b, 0, 0, 0)),
                  pl.BlockSpec((1, HQ), lambda b: (0, 0))],
        out_specs=[blk, pl.BlockSpec((1, PAGE_SIZE, HQ), lambda b: (b, 0, 0))],
        out_shape=[jax.ShapeDtypeStruct((B, n_blk, BK, HQ), BF16), jax.ShapeDtypeStruct((B, PAGE_SIZE, HQ), BF16)],
        scratch_shapes=[pltpu.VMEM((n_blk, HQ), F32)],
        compiler_params=_params(1), name=name,
    )(s, ksum, q, k_new_pad, slopes_hq)


def _moba_pv_kernel(pt_ref, p_ref, m_ref, pn_ref, vn_ref, *refs, nq, n_steps, pp):
    v_refs, (o_ref, acc) = refs[:pp], refs[pp:]
    g = pl.program_id(1)
    HQ = MOBA_H * nq

    def spread(p):
        p2 = jnp.broadcast_to(p.astype(F32)[:, None, :], (PAGE_SIZE, MOBA_H, HQ))
        return (p2.reshape(PAGE_SIZE * MOBA_H, HQ) * m_ref[...]).astype(BF16)

    def values(v3):
        return v3.reshape(PAGE_SIZE * MOBA_H, MOBA_DH).astype(BF16)

    @pl.when(g == 0)
    def _():
        acc[...] = _dg(spread(pn_ref[0]), values(vn_ref[0]), _TN)

    p2s = [spread(p_ref[0, u]) for u in range(pp)]
    v2s = [values(v_ref[0, 0]) for v_ref in v_refs]
    acc[...] += functools.reduce(lambda x, y: x + y, [_dg(p2, v2, _TN) for p2, v2 in zip(p2s, v2s)])

    @pl.when(g == n_steps - 1)
    def _():
        a = acc[...]
        o_ref[0] = jnp.concatenate([a[h * nq:(h + 1) * nq, :] for h in range(MOBA_H)], axis=-1)


def _moba_pv(name, page_table, p, cache_v, pn, v_new_pad, nq, layer, match_f):
    B, n_pg, _, HQ = p.shape
    D = D_MODEL
    pp = MOBA_PAGES_PER_STEP
    assert n_pg % pp == 0
    gs = pltpu.PrefetchScalarGridSpec(
        num_scalar_prefetch=1, grid=(B, n_pg // pp),
        in_specs=[pl.BlockSpec((1, pp, PAGE_SIZE, HQ), lambda b, g, pt: (b, g, 0, 0)),
                  pl.BlockSpec(match_f.shape, lambda b, g, pt: (0, 0)),
                  pl.BlockSpec((1, PAGE_SIZE, HQ), lambda b, g, pt: (b, 0, 0)),
                  pl.BlockSpec((1, PAGE_SIZE, MOBA_H, MOBA_DH), lambda b, g, pt: (b, 0, 0, 0))]
        + _page_specs(layer, pp),
        out_specs=pl.BlockSpec((1, nq, D), lambda b, g, pt: (b, 0, 0)),
        scratch_shapes=[pltpu.VMEM((HQ, MOBA_DH), F32)])
    return pl.pallas_call(
        functools.partial(_moba_pv_kernel, nq=nq, n_steps=n_pg // pp, pp=pp), grid_spec=gs,
        out_shape=jax.ShapeDtypeStruct((B, nq, D), F32), compiler_params=_params(2), name=name,
    )(page_table, p, match_f, pn, v_new_pad, *([cache_v] * pp))


def _ident(x):
    return x


def _mix(x, xp, mu):
    return x + (xp - x) * mu


def _pad_cols(w, n):
    return jnp.pad(w, ((0, 0), (0, n - w.shape[1])))


def _pad_rows(w, n):
    return jnp.pad(w, ((0, n - w.shape[0]), (0, 0)))


def _rwkv_layer(tag, x3, shift0, s0, v_first, P, j, ln_g, ln_b):
    B, T, D = x3.shape
    M = B * T
    x = x3.reshape(M, D)
    wb = lambda w: w.astype(BF16)
    lp = LANES
    w_rkv = P['rwkv_w_rkv'][j]
    w_main = jnp.concatenate([wb(w_rkv[0]), wb(w_rkv[1]), wb(w_rkv[2])], axis=1)
    lora1 = [wb(_pad_cols(P['rwkv_w1'][j], lp)), wb(_pad_cols(P['rwkv_a1'][j], lp)), wb(P['rwkv_g1'][j])]
    lora2 = [wb(_pad_rows(P['rwkv_w2'][j], lp)), wb(_pad_rows(P['rwkv_a2'][j], lp)), wb(P['rwkv_g2'][j])]
    bias = [P['rwkv_w0'][j][None, :], P['rwkv_a0'][j][None, :]]
    vgate = None
    if j > 0:
        vgate = (wb(_pad_cols(P['rwkv_v1'][j - 1], lp)), wb(_pad_rows(P['rwkv_v2'][j - 1], lp)),
                 P['rwkv_v0'][j - 1][None, :], v_first)
    r, k, v, lw, a, g = _rwkv_proj(tag + 'proj', x3, shift0, P['rwkv_mu'][j], w_main, lora1, lora2, bias, vgate)
    if j == 0:
        v_first = v
    Tp = -(-T // WKV_CHUNK) * WKV_CHUNK
    seq = lambda t: jnp.pad(t.reshape(B, T, D), ((0, 0), (0, Tp - T), (0, 0)))
    vecp = lambda name: P[name][j].reshape(1, D)
    z, s_new = _wkv(tag + 'wkv', seq(r), seq(lw), seq(k), seq(v), seq(a), vecp('rwkv_k_k'), vecp('rwkv_k_a'),
                    vecp('rwkv_r_k'), vecp('rwkv_lnx_g'), vecp('rwkv_lnx_b'), s0)
    z = z[:, :T].reshape(M, D)
    xn = _mm_ln(tag + 'wo', [z, g], wb(P['rwkv_wo'][j]), x, ln_g, ln_b, lambda z_, g_: z_ * g_)
    return xn, s_new, x3[:, -1], v_first


def _s5_weights(P, j):
    ns, gs = D_MODEL // S5_SLAB, S5_SLAB // S5_GROUP
    eye = jnp.eye(gs, dtype=F32)

    def bd_in(b):
        bt = jnp.swapaxes(b.reshape(ns, gs, S5_P, S5_GROUP), 2, 3)
        return jnp.einsum('sgcp,gh->sgchp', bt, eye).reshape(ns, S5_SLAB, S5_STATES)

    def bd_out(c):
        ct = jnp.swapaxes(c.reshape(ns, gs, S5_GROUP, S5_P), 2, 3)
        return jnp.einsum('sgpc,gh->sgphc', ct, eye).reshape(ns, S5_STATES, S5_SLAB)

    wb = jnp.concatenate([bd_in(P['s5_b_re'][j]), bd_in(P['s5_b_im'][j])], axis=2)
    wc = jnp.concatenate([bd_out(P['s5_c_re'][j]), -bd_out(P['s5_c_im'][j])], axis=1)
    flat = lambda t: t.reshape(1, 1, S5_G * S5_P)
    dt = jnp.broadcast_to(jnp.exp(P['s5_log_dt'][j])[:, None], (S5_G, S5_P))
    return wb, wc, flat(P['s5_lam_re'][j]), flat(P['s5_lam_im'][j]), flat(dt)


def _s5_layer(tag, x3, h0r, h0i, P, j, ln_g, ln_b):
    B, T, D = x3.shape
    M = B * T
    x = x3.reshape(M, D)
    wb, wc, lr, li, dt = _s5_weights(P, j)
    st = lambda h: h.reshape(B, 1, S5_G * S5_P)
    z, hr, hi = _s5(tag + 'scan', x3, wb, wc, lr, li, dt, P['s5_d'][j][None, :], st(h0r), st(h0i))
    xn = _gate_ln(tag + 'gate_ln', z.reshape(M, D), P['s5_w_val'][j].astype(BF16), P['s5_w_gate'][j].astype(BF16),
                  x, ln_g, ln_b)
    return xn, hr.reshape(B, S5_G, S5_P), hi.reshape(B, S5_G, S5_P)


def _alibi_slopes():
    return 2.0 ** (-8.0 * jnp.arange(1, MOBA_H + 1, dtype=F32) / MOBA_H)


def _moba_layer(tag, x3, cache_k, cache_v, page_table, P, j, ln_g, ln_b):
    B, T, D = x3.shape
    M = B * T
    x = x3.reshape(M, D)
    wqkv = P['moba_w_qkv'][j]
    ws = [wqkv[:, n * D:(n + 1) * D].astype(BF16) for n in range(3)]
    q, k, v = _mm(tag + 'qkv', [x], [], ws, [], [], _ident, lambda a, b, c: (a, b, c), [F32, F32, F32])
    q3, k3, v3 = (t.reshape(B, T, D) for t in (q, k, v))
    slopes = _alibi_slopes()
    if cache_k is None:
        sl = jnp.broadcast_to(slopes[:, None, None], (MOBA_H, 1, LANES))
        o = _moba_prompt(tag + 'attn', q3, k3, v3, sl)
    else:
        n_pg = page_table.shape[1]
        past = n_pg * PAGE_SIZE
        assert past % MOBA_BLOCK == 0 and T <= PAGE_SIZE and T <= MOBA_BLOCK
        HQ = MOBA_H * T
        match_f = _head_match(PAGE_SIZE, T).astype(F32)
        s, ksum = _moba_scores(tag + 'scores', page_table, q3, cache_k, j, match_f)
        sl = jnp.repeat(slopes, T)[None, :]
        pad = lambda t: jnp.pad(t.reshape(B, T, MOBA_H, MOBA_DH), ((0, 0), (0, PAGE_SIZE - T), (0, 0), (0, 0)))
        p, pn = _moba_probs(tag + 'probs', s.reshape(B, past // MOBA_BLOCK, MOBA_BLOCK, HQ), ksum, q3, pad(k3),
                            sl, past)
        o = _moba_pv(tag + 'pv', page_table, p.reshape(B, n_pg, PAGE_SIZE, HQ), cache_v, pn, pad(v3), T, j, match_f)
    xn = _mm_ln(tag + 'wo', [o.reshape(M, D)], P['moba_wo'][j].astype(BF16), x, ln_g, ln_b, _ident)
    return xn, k3.reshape(B, T, MOBA_H, MOBA_DH), v3.reshape(B, T, MOBA_H, MOBA_DH)


def _ffn_layer(tag, x3, conv0, P, i, ln_g, ln_b):
    B, T, D = x3.shape
    w_in = P['ffn_w_in'][i]
    h, cst = _ffn_in(tag + 'in', x3, w_in[:, :D_FF].astype(BF16), w_in[:, D_FF:].astype(BF16), conv0,
                     P['ffn_conv_w'][i], P['ffn_conv_b'][i][None, :])
    xn = _mm_ln(tag + 'down', [h], P['ffn_w_down'][i].astype(BF16), x3.reshape(B * T, D), ln_g, ln_b, _ident)
    return xn.reshape(B, T, D), cst


def _trunk(grp, x3, wkv0, shift0, s5re0, s5im0, conv0, cache_k, cache_v, page_table, P):
    B, T, D = x3.shape
    wkv_new, shift_new, s5re_new, s5im_new, k_new, v_new, conv_new = [], [], [], [], [], [], []
    v_first = None
    for i in range(DEPTH):
        kind, j = i % 3, i // 3
        tag = f'{grp}{i}_'
        g0, b0 = P['ln_g'][i, 0][None, :], P['ln_b'][i, 0][None, :]
        g1, b1 = P['ln_g'][i, 1][None, :], P['ln_b'][i, 1][None, :]
        if kind == 0:
            xn, S, last, v_first = _rwkv_layer(tag, x3, shift0[j], wkv0[j], v_first, P, j, g0, b0)
            wkv_new.append(S)
            shift_new.append(last)
        elif kind == 1:
            xn, hr, hi = _s5_layer(tag, x3, s5re0[j], s5im0[j], P, j, g0, b0)
            s5re_new.append(hr)
            s5im_new.append(hi)
        else:
            xn, kr, vr = _moba_layer(tag, x3, cache_k, cache_v, page_table, P, j, g0, b0)
            k_new.append(kr)
            v_new.append(vr)
        x3, cst = _ffn_layer(tag + 'ffn_', xn.reshape(B, T, D), conv0[i], P, i, g1, b1)
        conv_new.append(cst)
    return (x3, jnp.stack(wkv_new), jnp.stack(shift_new), jnp.stack(s5re_new), jnp.stack(s5im_new),
            jnp.stack(conv_new), jnp.stack(k_new), jnp.stack(v_new))


def kernel(x_prompt, x_sample, state_rwkv_wkv, state_rwkv_shift, state_s5_re, state_s5_im, state_ffn_conv, cache_k, cache_v, page_table, ln_g, ln_b, rwkv_mu, rwkv_w_rkv, rwkv_w0, rwkv_w1, rwkv_w2, rwkv_a0, rwkv_a1, rwkv_a2, rwkv_v0, rwkv_v1, rwkv_v2, rwkv_g1, rwkv_g2, rwkv_k_k, rwkv_k_a, rwkv_r_k, rwkv_lnx_g, rwkv_lnx_b, rwkv_wo, s5_log_dt, s5_lam_re, s5_lam_im, s5_b_re, s5_b_im, s5_c_re, s5_c_im, s5_d, s5_w_val, s5_w_gate, moba_w_qkv, moba_wo, ffn_w_in, ffn_conv_w, ffn_conv_b, ffn_w_down):
    P = dict(ln_g=ln_g, ln_b=ln_b, rwkv_mu=rwkv_mu, rwkv_w_rkv=rwkv_w_rkv, rwkv_w0=rwkv_w0, rwkv_w1=rwkv_w1,
             rwkv_w2=rwkv_w2, rwkv_a0=rwkv_a0, rwkv_a1=rwkv_a1, rwkv_a2=rwkv_a2, rwkv_v0=rwkv_v0,
             rwkv_v1=rwkv_v1, rwkv_v2=rwkv_v2, rwkv_g1=rwkv_g1, rwkv_g2=rwkv_g2, rwkv_k_k=rwkv_k_k,
             rwkv_k_a=rwkv_k_a, rwkv_r_k=rwkv_r_k.reshape(rwkv_r_k.shape[0], D_MODEL), rwkv_lnx_g=rwkv_lnx_g,
             rwkv_lnx_b=rwkv_lnx_b, rwkv_wo=rwkv_wo, s5_log_dt=s5_log_dt, s5_lam_re=s5_lam_re,
             s5_lam_im=s5_lam_im, s5_b_re=s5_b_re, s5_b_im=s5_b_im, s5_c_re=s5_c_re, s5_c_im=s5_c_im, s5_d=s5_d,
             s5_w_val=s5_w_val, s5_w_gate=s5_w_gate, moba_w_qkv=moba_w_qkv, moba_wo=moba_wo,
             ffn_w_in=ffn_w_in, ffn_conv_w=ffn_conv_w, ffn_conv_b=ffn_conv_b, ffn_w_down=ffn_w_down)
    B = x_prompt.shape[0]
    n_rwkv, n_s5 = state_rwkv_wkv.shape[0], state_s5_re.shape[0]
    zeros = lambda *s: jnp.zeros(s, F32)
    outs_p = _trunk('p', x_prompt, zeros(n_rwkv, B, RWKV_H, RWKV_N, RWKV_N), zeros(n_rwkv, B, D_MODEL),
                    zeros(n_s5, B, S5_G, S5_P), zeros(n_s5, B, S5_G, S5_P), zeros(DEPTH, B, CONV_W - 1, D_FF),
                    None, None, None, P)
    outs_s = _trunk('s', x_sample, state_rwkv_wkv, state_rwkv_shift, state_s5_re, state_s5_im, state_ffn_conv,
                    cache_k, cache_v, page_table, P)
    return (outs_p[0], outs_s[0]) + tuple(outs_p[1:]) + tuple(outs_s[1:])
```
